```python
import math
import jax
import jax.numpy as jnp
from jax import lax
import numpy as np

D_MODEL = 1024
BATCH = 8
SEQ = 4096
DEPTH = 2

HEAD_DIM = 64
BLOCK_Q = 128
NORM_EPS = 1e-6
NEG_INF = -1e30
ATTN_SCALE = HEAD_DIM ** -0.5

SWA_HEADS = 4
SWA_KV_HEADS = 2
SWA_GROUP = SWA_HEADS // SWA_KV_HEADS
SWA_WINDOW = 128
FOX_HEADS = 4
SB_HEADS = 4
DSA_HEADS = 4
IDX_HEADS = 4
IDX_DIM = 64
IDX_SCALE = IDX_DIM ** -0.5
DSA_TOPK_MAX = 256
T5_BUCKETS = 32
T5_MAX_DISTANCE = 128
T5_HEADS = SWA_HEADS + DSA_HEADS
N_BRANCHES = 4
BRANCH_WIDTH = 4 * HEAD_DIM
N_GROUPS = 4
EXPERTS_PER_GROUP = 8
N_EXPERTS = N_GROUPS * EXPERTS_PER_GROUP
EXPERT_FF = 512
TOP_K_IN_GROUP = 2
MOE_BLOCK = 128

IN_SPLITS = (
    SWA_HEADS * HEAD_DIM, SWA_KV_HEADS * HEAD_DIM, SWA_KV_HEADS * HEAD_DIM,
    FOX_HEADS * HEAD_DIM, FOX_HEADS * HEAD_DIM, FOX_HEADS * HEAD_DIM,
    FOX_HEADS,
    SB_HEADS * HEAD_DIM, SB_HEADS * HEAD_DIM, SB_HEADS * HEAD_DIM,
    DSA_HEADS * HEAD_DIM, HEAD_DIM, HEAD_DIM,
    IDX_HEADS * IDX_DIM, IDX_DIM, IDX_HEADS,
    N_BRANCHES * D_MODEL,
)
IN_COLS = sum(IN_SPLITS)

kernel_name = 'hybrid_gated_swa_fox_stickbreak_dsa_hmoe'


def rms_norm(t, g):
    tf = t.astype(jnp.float32)
    tf = tf * lax.rsqrt(jnp.mean(tf * tf, axis=-1, keepdims=True) + NORM_EPS)
    return (tf * g.astype(jnp.float32)).astype(t.dtype)


def t5_bucket(dist):
    n = jnp.maximum(dist, 0)
    max_exact = T5_BUCKETS // 2
    nf = jnp.maximum(n, 1).astype(jnp.float32)
    large = max_exact + (jnp.log(nf / max_exact) / math.log(T5_MAX_DISTANCE / max_exact)
                         * (T5_BUCKETS - max_exact)).astype(jnp.int32)
    large = jnp.minimum(large, T5_BUCKETS - 1)
    return jnp.where(n < max_exact, n, large)


def to_blocks(t):
    b, s = t.shape[:2]
    return jnp.moveaxis(t.reshape((b, s // BLOCK_Q, BLOCK_Q) + t.shape[2:]), 1, 0)


def from_blocks(o):
    nb, b = o.shape[:2]
    return jnp.moveaxis(o, 0, 1).reshape(b, nb * BLOCK_Q, -1)


def sliding_window_attention(q, k, v, sinks, bias_table):
    b, s_len = q.shape[:2]
    nb = s_len // BLOCK_Q
    qb = q.reshape(b, nb, BLOCK_Q, SWA_KV_HEADS, SWA_GROUP, HEAD_DIM)

    def band(t):
        tb = t.reshape(b, nb, BLOCK_Q, SWA_KV_HEADS, HEAD_DIM)
        prev = jnp.concatenate([jnp.zeros_like(tb[:, :1]), tb[:, :-1]], axis=1)
        return jnp.concatenate([prev, tb], axis=2)

    kw, vw = band(k), band(v)
    s = jnp.einsum('bnqkgd,bnskd->bnkgqs', qb, kw).astype(jnp.float32) * ATTN_SCALE
    qi = jnp.arange(BLOCK_Q)[:, None]
    kj = jnp.arange(2 * BLOCK_Q)[None, :]
    dist = BLOCK_Q + qi - kj
    bias = bias_table[t5_bucket(dist)].astype(jnp.float32)
    bias = jnp.transpose(bias, (2, 0, 1)).reshape(SWA_KV_HEADS, SWA_GROUP, BLOCK_Q, 2 * BLOCK_Q)
    kpos = (jnp.arange(nb)[:, None, None] - 1) * BLOCK_Q + kj[None]
    valid = (dist >= 0) & (dist < SWA_WINDOW) & (kpos >= 0)
    s = jnp.where(valid[None, :, None, None], s + bias, NEG_INF)
    sink = sinks.astype(jnp.float32).reshape(1, 1, SWA_KV_HEADS, SWA_GROUP, 1, 1)
    m = jnp.maximum(jnp.max(s, axis=-1, keepdims=True), sink)
    p = jnp.exp(s - m)
    denom = jnp.sum(p, axis=-1, keepdims=True) + jnp.exp(sink - m)
    p = (p / denom).astype(v.dtype)
    o = jnp.einsum('bnkgqs,bnskd->bnqkgd', p, vw)
    return o.reshape(b, s_len, SWA_HEADS * HEAD_DIM)


def forgetting_attention(q, k, v, log_f):
    s_len = q.shape[1]
    c = jnp.cumsum(log_f, axis=1)
    c_keys = jnp.swapaxes(c, 1, 2)
    kpos = jnp.arange(s_len)

    def block(args):
        i, qi, ci = args
        s = jnp.einsum('bqhd,bshd->bhqs', qi, k).astype(jnp.float32) * ATTN_SCALE
        s = s + jnp.swapaxes(ci, 1, 2)[..., None] - c_keys[:, :, None, :]
        qpos = i * BLOCK_Q + jnp.arange(BLOCK_Q)
        s = jnp.where(kpos[None, :] <= qpos[:, None], s, NEG_INF)
        p = jax.nn.softmax(s, axis=-1).astype(v.dtype)
        return jnp.einsum('bhqs,bshd->bqhd', p, v)

    nb = s_len // BLOCK_Q
    out = lax.map(block, (jnp.arange(nb), to_blocks(q), to_blocks(c)))
    return from_blocks(out)


def stick_breaking_attention(q, k, v):
    s_len = q.shape[1]
    kpos = jnp.arange(s_len)

    def block(args):
        i, qi = args
        z = jnp.einsum('bqhd,bshd->bhqs', qi, k).astype(jnp.float32) * ATTN_SCALE
        qpos = i * BLOCK_Q + jnp.arange(BLOCK_Q)
        strict = kpos[None, :] < qpos[:, None]
        log_beta = jax.nn.log_sigmoid(z)
        log_keep = jnp.where(strict, jax.nn.log_sigmoid(-z), 0.0)
        later = lax.cumsum(log_keep, axis=3, reverse=True) - log_keep
        a = jnp.where(strict, jnp.exp(log_beta + later), 0.0).astype(v.dtype)
        return jnp.einsum('bhqs,bshd->bqhd', a, v)

    nb = s_len // BLOCK_Q
    out = lax.map(block, (jnp.arange(nb), to_blocks(q)))
    return from_blocks(out)


def dsa_attention(q, k, v, q_idx, k_idx, w_idx, bias_table):
    s_len = k.shape[1]
    top_k = min(DSA_TOPK_MAX, s_len // 4)
    kpos = jnp.arange(s_len)
    gather = jax.vmap(lambda t, idx: t[idx])

    def block(args):
        i, qi, qii, wi = args
        qpos = i * BLOCK_Q + jnp.arange(BLOCK_Q)
        idx_logits = jnp.einsum('bqhd,bsd->bqhs', qii, k_idx).astype(jnp.float32) * IDX_SCALE
        score = jnp.einsum('bqh,bqhs->bqs', wi.astype(jnp.float32) * IDX_HEADS ** -0.5,
                           jax.nn.relu(idx_logits))
        score = jnp.where(kpos[None, None, :] <= qpos[None, :, None], score, -jnp.inf)
        _, idx = lax.top_k(score, top_k)
        kg = gather(k, idx)
        vg = gather(v, idx)
        s = jnp.einsum('bqhd,bqkd->bhqk', qi, kg).astype(jnp.float32) * ATTN_SCALE
        dist = qpos[None, :, None] - idx
        bias = jnp.moveaxis(bias_table[t5_bucket(dist)], -1, 1).astype(jnp.float32)
        s = jnp.where((dist >= 0)[:, None], s + bias, NEG_INF)
        p = jax.nn.softmax(s, axis=-1).astype(v.dtype)
        return jnp.einsum('bhqk,bqkd->bqhd', p, vg)

    nb = s_len // BLOCK_Q
    out = lax.map(block, (jnp.arange(nb), to_blocks(q), to_blocks(q_idx), to_blocks(w_idx)))
    return from_blocks(out)


def hierarchical_moe(h, w_rg, b_rg, w_re, b_re, w_up, w_down):
    n_tok, d = h.shape
    g_logits = (h @ w_rg).astype(jnp.float32) + b_rg.astype(jnp.float32)
    g_prob = jax.nn.softmax(g_logits, axis=-1)
    grp = jnp.argmax(g_logits, axis=-1).astype(jnp.int32)
    p_grp = jnp.take_along_axis(g_prob, grp[:, None], axis=1)[:, 0]
    e_logits = ((h @ w_re).astype(jnp.float32) + b_re.astype(jnp.float32)).reshape(
        n_tok, N_GROUPS, EXPERTS_PER_GROUP)
    e_in = jnp.take_along_axis(e_logits, grp[:, None, None], axis=1)[:, 0]
    top_p, top_e = lax.top_k(jax.nn.softmax(e_in, axis=-1), TOP_K_IN_GROUP)
    gate = p_grp[:, None] * top_p / jnp.sum(top_p, axis=-1, keepdims=True)
    expert = grp[:, None] * EXPERTS_PER_GROUP + top_e.astype(jnp.int32)

    n_assign = n_tok * TOP_K_IN_GROUP
    flat_e = expert.reshape(-1)
    flat_tok = jnp.repeat(jnp.arange(n_tok, dtype=jnp.int32), TOP_K_IN_GROUP)
    flat_gate = gate.reshape(-1)
    order = jnp.argsort(flat_e)
    se, stok, sgate = flat_e[order], flat_tok[order], flat_gate[order]
    counts = jnp.bincount(flat_e, length=N_EXPERTS).astype(jnp.int32)
    start = jnp.cumsum(counts) - counts
    padded = (counts + MOE_BLOCK - 1) // MOE_BLOCK * MOE_BLOCK
    pend = jnp.cumsum(padded)
    pstart = pend - padded
    pos = pstart[se] + (jnp.arange(n_assign, dtype=jnp.int32) - start[se])
    n_blocks = -(-n_assign // MOE_BLOCK) + N_EXPERTS
    slot_tok = jnp.full((n_blocks * MOE_BLOCK,), n_tok, jnp.int32).at[pos].set(stok)
    blk_expert = jnp.clip(jnp.searchsorted(pend, jnp.arange(n_blocks) * MOE_BLOCK, side='right'),
                          0, N_EXPERTS - 1)
    h_pad = jnp.concatenate([h, jnp.zeros((1, d), h.dtype)], axis=0)
    xb = h_pad[slot_tok].reshape(n_blocks, MOE_BLOCK, d)

    def expert_block(args):
        xi, e = args
        gu = xi @ w_up[e]
        g, u = jnp.split(gu, 2, axis=-1)
        return (jax.nn.silu(g) * u) @ w_down[e]

    yb = lax.map(expert_block, (xb, blk_expert)).reshape(n_blocks * MOE_BLOCK, d)
    contrib = yb[pos] * sgate[:, None].astype(yb.dtype)
    return jnp.zeros((n_tok, d), h.dtype).at[stok].add(contrib)


def setup_inputs(seed: int = 0) -> dict:
    key = jax.random.key(seed)
    ks = jax.random.split(key, 17)
    f32 = jnp.float32
    nrm = lambda k, shp: jax.random.normal(k, shp, f32)
    return {
        'x': nrm(ks[0], (BATCH, SEQ, D_MODEL)),
        'norm_mix_g': 1.0 + 0.02 * nrm(ks[1], (DEPTH, D_MODEL)),
        'w_in': nrm(ks[2], (DEPTH, D_MODEL, IN_COLS)) * D_MODEL ** -0.5,
        'forget_bias': jax.random.uniform(ks[3], (DEPTH, FOX_HEADS), f32, 1.0, 5.0),
        'attn_sinks': 0.5 * nrm(ks[4], (DEPTH, SWA_HEADS)),
        'qk_gain': 1.0 + 0.02 * nrm(ks[5], (DEPTH, 3, 2, HEAD_DIM)),
        'w_branch': nrm(ks[6], (DEPTH, N_BRANCHES, BRANCH_WIDTH, D_MODEL)) * BRANCH_WIDTH ** -0.5,
        'w_out': nrm(ks[7], (DEPTH, D_MODEL, D_MODEL)) * D_MODEL ** -0.5,
        't5_table': 0.5 * nrm(ks[8], (T5_BUCKETS, T5_HEADS)),
        'norm_ffn_g': 1.0 + 0.02 * nrm(ks[9], (DEPTH, D_MODEL)),
        'w_router_group': nrm(ks[10], (DEPTH, D_MODEL, N_GROUPS)) * D_MODEL ** -0.5,
        'b_router_group': 0.01 * nrm(ks[11], (DEPTH, N_GROUPS)),
        'w_router_expert': nrm(ks[12], (DEPTH, D_MODEL, N_EXPERTS)) * D_MODEL ** -0.5,
        'b_router_expert': 0.01 * nrm(ks[13], (DEPTH, N_EXPERTS)),
        'w_expert_up': nrm(ks[14], (DEPTH, N_EXPERTS, D_MODEL, 2 * EXPERT_FF)) * D_MODEL ** -0.5,
        'w_expert_down': nrm(ks[15], (DEPTH, N_EXPERTS, EXPERT_FF, D_MODEL)) * EXPERT_FF ** -0.5,
    }


def reference(x, norm_mix_g, w_in, forget_bias, attn_sinks, qk_gain, w_branch, w_out, t5_table,
              norm_ffn_g, w_router_group, b_router_group, w_router_expert, b_router_expert,
              w_expert_up, w_expert_down):
    b, s = x.shape[:2]
    offsets = np.cumsum(IN_SPLITS)[:-1].tolist()
    for layer in range(DEPTH):
        h = rms_norm(x, norm_mix_g[layer])
        proj = h @ w_in[layer]
        (a_q, a_k, a_v, f_q, f_k, f_v, f_gate, s_q, s_k, s_v,
         d_q, d_k, d_v, i_q, i_k, i_w, gate_cols) = jnp.split(proj, offsets, axis=-1)
        gq = qk_gain[layer]
        heads = lambda t, n: t.reshape(b, s, n, HEAD_DIM)

        o_swa = sliding_window_attention(
            rms_norm(heads(a_q, SWA_HEADS), gq[0, 0]),
            rms_norm(heads(a_k, SWA_KV_HEADS), gq[0, 1]),
            heads(a_v, SWA_KV_HEADS),
            attn_sinks[layer], t5_table[:, :SWA_HEADS])

        log_f = jax.nn.log_sigmoid(f_gate.astype(jnp.float32) + forget_bias[layer].astype(jnp.float32))
        o_fox = forgetting_attention(
            rms_norm(heads(f_q, FOX_HEADS), gq[1, 0]),
            rms_norm(heads(f_k, FOX_HEADS), gq[1, 1]),
            heads(f_v, FOX_HEADS), log_f)

        o_sb = stick_breaking_attention(heads(s_q, SB_HEADS), heads(s_k, SB_HEADS), heads(s_v, SB_HEADS))

        o_dsa = dsa_attention(
            rms_norm(heads(d_q, DSA_HEADS), gq[2, 0]),
            rms_norm(d_k, gq[2, 1]), d_v,
            i_q.reshape(b, s, IDX_HEADS, IDX_DIM), i_k, i_w,
            t5_table[:, SWA_HEADS:])

        g_swa, g_fox, g_sb, g_dsa = jnp.split(jax.nn.sigmoid(gate_cols), N_BRANCHES, axis=-1)
        wb = w_branch[layer]
        merged = (g_swa * (o_swa @ wb[0]) + g_fox * (o_fox @ wb[1])
                  + g_sb * (o_sb @ wb[2]) + g_dsa * (o_dsa @ wb[3]))
        x = x + merged @ w_out[layer]

        h2 = rms_norm(x, norm_ffn_g[layer]).reshape(b * s, D_MODEL)
        y = hierarchical_moe(h2, w_router_group[layer], b_router_group[layer],
                             w_router_expert[layer], b_router_expert[layer],
                             w_expert_up[layer], w_expert_down[layer])
        x = x + y.reshape(b, s, D_MODEL)
    return x
```

```python
import functools
import math

import jax
import jax.numpy as jnp
import numpy as np
from jax import lax
from jax.experimental import pallas as pl
from jax.experimental.pallas import tpu as pltpu

F32 = jnp.float32
BF16 = jnp.bfloat16
I32 = jnp.int32

HEAD_DIM = 64
LANES = 128
NORM_EPS = 1e-6
NEG_INF = -1e30
M_INIT = -1e29
ATTN_SCALE = HEAD_DIM ** -0.5
SWA_BLOCK = 128
IDX_SCALE = 64 ** -0.5
IDX_HEADS = 4
DSA_TOPK_MAX = 256
T5_BUCKETS = 32
T5_MAX_DISTANCE = 128
N_GROUPS = 4
EXPERTS_PER_GROUP = 8
N_EXPERTS = N_GROUPS * EXPERTS_PER_GROUP
EXPERT_FF = 512
SB_DEAD = -110.0
VMEM_LIMIT = 56 * 1024 * 1024

IN_SPLITS = (256, 128, 128, 256, 256, 256, 4, 256, 256, 256, 256, 64, 64, 256, 64, 4, 4096)

_SEG = dict(aq=(0, 256), ak=(256, 128), av=(384, 128), fq=(512, 256), fk=(768, 256), fv=(1024, 256),
            sq=(1280, 256), sk=(1536, 256), sv=(1792, 256), dq=(2048, 256), dkk=(2304, 128),
            dvv=(2432, 128), iq=(2560, 256), ikk=(2816, 128))
_W1_COLS = 2944
_SEG_ORDER = ("aq", "ak", "av", "fq", "fk", "fv", "sq", "sk", "sv", "dq", "dkk", "dvv", "iq", "ikk")


def _cparams(*sem):
    return pltpu.CompilerParams(dimension_semantics=sem, vmem_limit_bytes=VMEM_LIMIT)


def _rms(x, g):
    return x * lax.rsqrt(jnp.mean(x * x, axis=-1, keepdims=True) + NORM_EPS) * g


def _log_sigmoid(z):
    return jnp.minimum(z, 0.0) - jnp.log(1.0 + jnp.exp(-jnp.abs(z)))


def _dot_nt(a, b):
    return lax.dot_general(a, b, (((1,), (1,)), ((), ())), preferred_element_type=F32)


def _split_heads(qp):
    lo = lax.broadcasted_iota(I32, (1, LANES), 1) < HEAD_DIM
    zero = jnp.zeros_like(qp)
    return jnp.concatenate([jnp.where(lo, qp, zero), jnp.where(lo, zero, qp)], axis=0)


def _merge_heads(o, t):
    lo = lax.broadcasted_iota(I32, (1, LANES), 1) < HEAD_DIM
    return jnp.where(lo, o[:t], o[t:])


def _proj_kernel(x_ref, g_ref, w1_ref, wm_ref, gseg_ref, gains_ref, fb_ref, ltri_ref, *rest):
    outs = dict(zip(_SEG_ORDER, rest[:len(_SEG_ORDER)]))
    cm_ref = rest[len(_SEG_ORDER)]
    carry_ref = rest[len(_SEG_ORDER) + 1]

    @pl.when(pl.program_id(1) == 0)
    def _():
        carry_ref[...] = jnp.zeros_like(carry_ref)

    h = _rms(x_ref[0], g_ref[...])
    hb = h.astype(BF16)

    def seg(name):
        off, width = _SEG[name]
        return jnp.dot(hb, w1_ref[:, off:off + width], preferred_element_type=F32)

    def head_norm(t, row):
        width = t.shape[1]
        ssq = jnp.dot((t * t).astype(BF16), gseg_ref[:width, :width], preferred_element_type=F32)
        return t * lax.rsqrt(ssq * (1.0 / HEAD_DIM) + NORM_EPS) * gains_ref[row:row + 1, :width]

    normed = dict(aq=0, ak=1, fq=2, fk=3, dq=4, dkk=5)
    scaled = dict(sq=ATTN_SCALE, iq=IDX_SCALE)
    for name in _SEG_ORDER:
        t = seg(name)
        if name in normed:
            t = head_norm(t, normed[name])
        elif name in scaled:
            t = t * scaled[name]
        outs[name][0] = t.astype(BF16)

    misc = jnp.dot(h, wm_ref[...], precision=lax.Precision.HIGHEST, preferred_element_type=F32)
    lane = lax.broadcasted_iota(I32, misc.shape, 1)
    logf = jnp.where(lane < 4, _log_sigmoid(misc + fb_ref[...]), 0.0)
    c = jnp.dot(ltri_ref[...], logf, precision=lax.Precision.HIGHEST,
                preferred_element_type=F32) + carry_ref[0:1, :]
    tm = misc.shape[0]
    carry_ref[0:1, :] = c[tm - 1:tm, :]
    cm_ref[0] = jnp.where(lane < 4, c, misc)


def _proj(x, g, w1, wm, gseg, gains, fb, tm):
    b, s, d = x.shape
    ltri = jnp.tril(jnp.ones((tm, tm), F32))
    full = lambda shape: pl.BlockSpec(shape, lambda bi, si: (0,) * len(shape))
    out_shapes = [jax.ShapeDtypeStruct((b, s, _SEG[n][1]), BF16) for n in _SEG_ORDER]
    out_shapes.append(jax.ShapeDtypeStruct((b, s, LANES), F32))
    out_specs = [pl.BlockSpec((1, tm, _SEG[n][1]), lambda bi, si: (bi, si, 0)) for n in _SEG_ORDER]
    out_specs.append(pl.BlockSpec((1, tm, LANES), lambda bi, si: (bi, si, 0)))
    return pl.pallas_call(
        _proj_kernel,
        grid=(b, s // tm),
        in_specs=[pl.BlockSpec((1, tm, d), lambda bi, si: (bi, si, 0)),
                  full((1, d)), full(w1.shape), full(wm.shape), full(gseg.shape),
                  full(gains.shape), full(fb.shape), full((tm, tm))],
        out_specs=out_specs,
        out_shape=out_shapes,
        scratch_shapes=[pltpu.VMEM((8, LANES), F32)],
        compiler_params=_cparams("arbitrary", "arbitrary"),
        name="proj",
    )(x, g, w1, wm, gseg, gains, fb, ltri)


def _swa_kernel(q_ref, kp_ref, kc_ref, vp_ref, vc_ref, bias_ref, sink_ref, o_ref):
    i = pl.program_id(1)
    t = SWA_BLOCK
    q = q_ref[0]
    kcat = jnp.concatenate([kp_ref[0], kc_ref[0]], axis=0)
    vcat = jnp.concatenate([vp_ref[0], vc_ref[0]], axis=0)
    col = lax.broadcasted_iota(I32, (t, 2 * t), 1)
    no_prev = (col < t) & (i == 0)
    pairs = []
    for pair in range(2):
        qs = _split_heads(q[:, pair * LANES:(pair + 1) * LANES])
        s = _dot_nt(qs, kcat)
        ps = []
        for hh in range(2):
            head = pair + 2 * hh
            sh = s[hh * t:(hh + 1) * t] + bias_ref[head]
            sh = jnp.where(no_prev, NEG_INF, sh)
            sink = sink_ref[head:head + 1, 0:1]
            m = jnp.maximum(jnp.max(sh, axis=1, keepdims=True), sink)
            p = jnp.exp(sh - m)
            denom = jnp.sum(p, axis=1, keepdims=True) + jnp.exp(sink - m)
            ps.append(p / denom)
        o = jnp.dot(jnp.concatenate(ps, axis=0).astype(BF16), vcat, preferred_element_type=F32)
        pairs.append(_merge_heads(o, t))
    o_ref[0] = jnp.concatenate(pairs, axis=1).astype(BF16)


def _swa(aq, ak, av, bias, sinks):
    b, s, _ = aq.shape
    t = SWA_BLOCK
    cur = lambda bi, i: (bi, i, 0)
    prev = lambda bi, i: (bi, jnp.maximum(i - 1, 0), 0)
    return pl.pallas_call(
        _swa_kernel,
        grid=(b, s // t),
        in_specs=[pl.BlockSpec((1, t, 256), cur),
                  pl.BlockSpec((1, t, LANES), prev), pl.BlockSpec((1, t, LANES), cur),
                  pl.BlockSpec((1, t, LANES), prev), pl.BlockSpec((1, t, LANES), cur),
                  pl.BlockSpec(bias.shape, lambda bi, i: (0, 0, 0)),
                  pl.BlockSpec(sinks.shape, lambda bi, i: (0, 0))],
        out_specs=pl.BlockSpec((1, t, 256), cur),
        out_shape=jax.ShapeDtypeStruct((b, s, 256), BF16),
        compiler_params=_cparams("arbitrary", "arbitrary"),
        name="swa",
    )(aq, ak, ak, av, av, bias, sinks)


def _flash_update(carry, s_heads, vj):
    m, l, acc = carry
    s = jnp.concatenate(s_heads, axis=0)
    m_new = jnp.maximum(m, jnp.max(s, axis=1, keepdims=True))
    alpha = jnp.exp(m - m_new)
    p = jnp.exp(s - m_new)
    l = alpha * l + jnp.sum(p, axis=1, keepdims=True)
    acc = alpha * acc + jnp.dot(p.astype(BF16), vj, preferred_element_type=F32)
    return m_new, l, acc


def _fox_kernel(q_ref, k_ref, v_ref, ccol_ref, crow_ref, o_ref, *, tq, tk):
    i = pl.program_id(1)
    q = q_ref[0]
    ccol = ccol_ref[0]
    nfull = (i * tq) // tk
    row = lax.broadcasted_iota(I32, (tq, tk), 0)
    col = lax.broadcasted_iota(I32, (tq, tk), 1)
    pairs = []
    for pair in range(2):
        sl = slice(pair * LANES, (pair + 1) * LANES)
        qs = _split_heads(q[:, sl])
        heads = (2 * pair, 2 * pair + 1)

        def tile(j, carry, masked):
            start = pl.multiple_of(j * tk, tk)
            kj = k_ref[0, pl.ds(start, tk), sl]
            vj = v_ref[0, pl.ds(start, tk), sl]
            s = _dot_nt(qs, kj)
            s_heads = []
            for hh, head in enumerate(heads):
                d = ccol[:, head:head + 1] - crow_ref[0, head, pl.ds(j, 1), :]
                sh = s[hh * tq:(hh + 1) * tq] + d
                if masked:
                    sh = jnp.where(j * tk + col <= i * tq + row, sh, NEG_INF)
                s_heads.append(sh)
            return _flash_update(carry, s_heads, vj)

        init = (jnp.full((2 * tq, 1), M_INIT, F32), jnp.zeros((2 * tq, 1), F32),
                jnp.zeros((2 * tq, LANES), F32))
        carry = lax.fori_loop(0, nfull, lambda j, c: tile(j, c, False), init)
        _, l, acc = tile(nfull, carry, True)
        pairs.append(_merge_heads(acc / l, tq))
    o_ref[0] = jnp.concatenate(pairs, axis=1).astype(BF16)


def _fox(fq, fk, fv, ccol, crow, tq, tk):
    b, s, _ = fq.shape
    return pl.pallas_call(
        functools.partial(_fox_kernel, tq=tq, tk=tk),
        grid=(b, s // tq),
        in_specs=[pl.BlockSpec((1, tq, 256), lambda bi, i: (bi, i, 0)),
                  pl.BlockSpec((1, s, 256), lambda bi, i: (bi, 0, 0)),
                  pl.BlockSpec((1, s, 256), lambda bi, i: (bi, 0, 0)),
                  pl.BlockSpec((1, tq, LANES), lambda bi, i: (bi, i, 0)),
                  pl.BlockSpec((1, 4, s // tk, tk), lambda bi, i: (bi, 0, 0, 0))],
        out_specs=pl.BlockSpec((1, tq, 256), lambda bi, i: (bi, i, 0)),
        out_shape=jax.ShapeDtypeStruct((b, s, 256), BF16),
        compiler_params=_cparams("arbitrary", "arbitrary"),
        name="fox",
    )(fq, fk, fv, ccol, crow)


def _sb_kernel(q_ref, k_ref, v_ref, u_ref, o_ref, *, tq, tk):
    i = pl.program_id(1)
    q = q_ref[0]
    u = u_ref[...]
    nfull = (i * tq) // tk
    row = lax.broadcasted_iota(I32, (2 * tq, tk), 0)
    col = lax.broadcasted_iota(I32, (2 * tq, tk), 1)
    qrow = jnp.where(row >= tq, row - tq, row)
    pairs = []
    for pair in range(2):
        sl = slice(pair * LANES, (pair + 1) * LANES)
        qs = _split_heads(q[:, sl])

        def tile(j, r, acc, masked):
            start = pl.multiple_of(j * tk, tk)
            kj = k_ref[0, pl.ds(start, tk), sl]
            vj = v_ref[0, pl.ds(start, tk), sl]
            z = _dot_nt(qs, kj)
            sp = jnp.log(1.0 + jnp.exp(-jnp.abs(z)))
            log_beta = jnp.minimum(z, 0.0) - sp
            log_keep = jnp.minimum(-z, 0.0) - sp
            if masked:
                strict = j * tk + col < i * tq + qrow
                log_keep = jnp.where(strict, log_keep, 0.0)
            hi = log_keep.astype(BF16)
            lo = (log_keep - hi.astype(F32)).astype(BF16)
            later = (jnp.dot(hi, u, preferred_element_type=F32)
                     + jnp.dot(lo, u, preferred_element_type=F32))
            a = jnp.exp(log_beta + later + r)
            if masked:
                a = jnp.where(strict, a, 0.0)
            acc = acc + jnp.dot(a.astype(BF16), vj, preferred_element_type=F32)
            r = r + jnp.sum(log_keep, axis=1, keepdims=True)
            return r, acc

        r, acc = tile(nfull, jnp.zeros((2 * tq, 1), F32), jnp.zeros((2 * tq, LANES), F32), True)

        def cond(c):
            return (c[0] >= 0) & (c[1] > 0)

        def body(c):
            j, _, r, acc = c
            r, acc = tile(j, r, acc, False)
            return j - 1, (jnp.max(r) > SB_DEAD).astype(I32), r, acc

        _, _, _, acc = lax.while_loop(cond, body, (nfull - 1, (jnp.max(r) > SB_DEAD).astype(I32), r, acc))
        pairs.append(_merge_heads(acc, tq))
    o_ref[0] = jnp.concatenate(pairs, axis=1).astype(BF16)


def _sb(sq, sk, sv, tq, tk):
    b, s, _ = sq.shape
    u = (jnp.arange(tk)[:, None] > jnp.arange(tk)[None, :]).astype(BF16)
    return pl.pallas_call(
        functools.partial(_sb_kernel, tq=tq, tk=tk),
        grid=(b, s // tq),
        in_specs=[pl.BlockSpec((1, tq, 256), lambda bi, i: (bi, i, 0)),
                  pl.BlockSpec((1, s, 256), lambda bi, i: (bi, 0, 0)),
                  pl.BlockSpec((1, s, 256), lambda bi, i: (bi, 0, 0)),
                  pl.BlockSpec((tk, tk), lambda bi, i: (0, 0))],
        out_specs=pl.BlockSpec((1, tq, 256), lambda bi, i: (bi, i, 0)),
        out_shape=jax.ShapeDtypeStruct((b, s, 256), BF16),
        compiler_params=_cparams("arbitrary", "arbitrary"),
        name="stickbreak",
    )(sq, sk, sv, u)


INT_MIN = -2 ** 31


def _dsa_kernel(q_ref, kk_ref, vv_ref, iq_ref, ikk_ref, cm_ref, bias_ref, ustrict_ref, o_ref,
                key_ref, sel_ref, *, t, top_k):
    i = pl.program_id(1)
    row = lax.broadcasted_iota(I32, (t, t), 0)
    col = lax.broadcasted_iota(I32, (t, t), 1)
    causal = col <= row

    iq = iq_ref[0]
    iqs = jnp.concatenate([_split_heads(iq[:, :LANES]), _split_heads(iq[:, LANES:])], axis=0)
    cm = cm_ref[0]
    w = [cm[:, 4 + h:5 + h] * (IDX_HEADS ** -0.5) for h in range(IDX_HEADS)]

    def score_tile(j, masked):
        start = pl.multiple_of(j * t, t)
        logits = _dot_nt(iqs, ikk_ref[0, pl.ds(start, t), :])
        sc = w[0] * jnp.maximum(logits[0:t], 0.0)
        for h in range(1, IDX_HEADS):
            sc = sc + w[h] * jnp.maximum(logits[h * t:(h + 1) * t], 0.0)
        if masked:
            sc = jnp.where(causal, sc, -jnp.inf)
        bits = pltpu.bitcast(sc, I32)
        key = bits ^ ((bits >> 31) & 0x7FFFFFFF)
        key_ref[j] = jnp.where(key == -1, 0, key)

    def p1(j, c):
        score_tile(j, False)
        return c

    lax.fori_loop(0, i, p1, 0)
    score_tile(i, True)

    def count(pred):
        def cb(j, acc):
            return acc + pred(key_ref[j]).astype(F32)
        acc = lax.fori_loop(0, i + 1, cb, jnp.zeros((t, t), F32))
        return jnp.sum(acc, axis=1, keepdims=True)

    def bit_body(b, thr_u):
        cand_u = thr_u | (jnp.int32(1) << (31 - b))
        cand_s = cand_u ^ INT_MIN
        n = count(lambda k: k >= cand_s)
        return jnp.where(n >= top_k, cand_u, thr_u)

    thr_u = lax.fori_loop(0, 32, bit_body, jnp.zeros((t, 1), I32))
    thr = thr_u ^ INT_MIN
    n_gt = count(lambda k: k > thr)
    n_ge = count(lambda k: k >= thr)
    need = top_k - n_gt
    surplus = jnp.max(n_ge) > top_k

    def simple_pass():
        def sb(j, c):
            sel_ref[j] = jnp.where(key_ref[j] >= thr, 0.0, NEG_INF)
            return c
        lax.fori_loop(0, i + 1, sb, 0)

    def tie_pass():
        def tb(j, seen):
            k = key_ref[j]
            eq = k == thr
            before = jnp.dot(eq.astype(BF16), ustrict_ref[...], preferred_element_type=F32) + seen
            sel = (k > thr) | (eq & (before < need))
            sel_ref[j] = jnp.where(sel, 0.0, NEG_INF)
            return seen + jnp.sum(eq.astype(F32), axis=1, keepdims=True)
        lax.fori_loop(0, i + 1, tb, jnp.zeros((t, 1), F32))

    lax.cond(surplus, tie_pass, simple_pass)
    sel_ref[i] = jnp.where(causal, sel_ref[i], NEG_INF)

    q = q_ref[0]
    qs = jnp.concatenate([_split_heads(q[:, :LANES]), _split_heads(q[:, LANES:])], axis=0)

    def attn_tile(j, carry, near):
        start = pl.multiple_of(j * t, t)
        s = _dot_nt(qs, kk_ref[0, pl.ds(start, t), :])
        sel = sel_ref[j]
        s_heads = []
        for h in range(4):
            sh = s[h * t:(h + 1) * t] + sel
            if near is not None:
                sh = sh + bias_ref[h, :, near * t:(near + 1) * t]
            s_heads.append(sh)
        return _flash_update(carry, s_heads, vv_ref[0, pl.ds(start, t), :])

    init = (jnp.full((4 * t, 1), M_INIT, F32), jnp.zeros((4 * t, 1), F32), jnp.zeros((4 * t, LANES), F32))
    carry = lax.fori_loop(0, jnp.maximum(i - 1, 0), lambda j, c: attn_tile(j, c, None), init)
    carry = lax.cond(i > 0, lambda c: attn_tile(i - 1, c, 0), lambda c: c, carry)
    _, l, acc = attn_tile(i, carry, 1)
    o = acc / l
    o_ref[0] = jnp.concatenate([_merge_heads(o[:2 * t], t), _merge_heads(o[2 * t:], t)], axis=1).astype(BF16)


def _dsa(dq, dkk, dvv, iq, ikk, cm, bias, top_k):
    b, s, _ = dq.shape
    t = 128
    ustrict = (jnp.arange(t)[:, None] < jnp.arange(t)[None, :]).astype(BF16)
    blk = lambda w: pl.BlockSpec((1, t, w), lambda bi, i: (bi, i, 0))
    seq = lambda w: pl.BlockSpec((1, s, w), lambda bi, i: (bi, 0, 0))
    return pl.pallas_call(
        functools.partial(_dsa_kernel, t=t, top_k=top_k),
        grid=(b, s // t),
        in_specs=[blk(256), seq(LANES), seq(LANES), blk(256), seq(LANES), blk(LANES),
                  pl.BlockSpec(bias.shape, lambda bi, i: (0, 0, 0)),
                  pl.BlockSpec((t, t), lambda bi, i: (0, 0))],
        out_specs=blk(256),
        out_shape=jax.ShapeDtypeStruct((b, s, 256), BF16),
        scratch_shapes=[pltpu.VMEM((s // t, t, t), I32), pltpu.VMEM((s // t, t, t), F32)],
        compiler_params=_cparams("arbitrary", "arbitrary"),
        name="dsa",
    )(dq, dkk, dvv, iq, ikk, cm, bias, ustrict)


def _merge_kernel(x_ref, gm_ref, wg_ref, oa_ref, of_ref, os_ref, od_ref, wb_ref, wo_ref, gf_ref,
                  wr_ref, br_ref, xo_ref, h2_ref, route_ref):
    x = x_ref[...]
    hb = _rms(x, gm_ref[...]).astype(BF16)
    d = x.shape[1]
    merged = None
    for bi, o_ref in enumerate((oa_ref, of_ref, os_ref, od_ref)):
        gate = jax.nn.sigmoid(jnp.dot(hb, wg_ref[:, bi * d:(bi + 1) * d], preferred_element_type=F32))
        term = gate * jnp.dot(o_ref[...], wb_ref[bi], preferred_element_type=F32)
        merged = term if merged is None else merged + term
    xn = x + jnp.dot(merged.astype(BF16), wo_ref[...], preferred_element_type=F32)
    xo_ref[...] = xn
    h2 = _rms(xn, gf_ref[...])
    h2_ref[...] = h2

    logits = jnp.dot(h2, wr_ref[...], precision=lax.Precision.HIGHEST,
                     preferred_element_type=F32) + br_ref[...]
    lane = lax.broadcasted_iota(I32, logits.shape, 1).astype(F32)
    big = 1e9
    gl = jnp.where(lane < N_GROUPS, logits, -jnp.inf)
    gmax = jnp.max(gl, axis=1, keepdims=True)
    grp = jnp.min(jnp.where(gl == gmax, lane, big), axis=1, keepdims=True)
    p_grp = 1.0 / jnp.sum(jnp.exp(gl - gmax), axis=1, keepdims=True)
    first = N_GROUPS + grp * EXPERTS_PER_GROUP
    el = jnp.where((lane >= first) & (lane < first + EXPERTS_PER_GROUP), logits, -jnp.inf)
    l1 = jnp.max(el, axis=1, keepdims=True)
    i1 = jnp.min(jnp.where(el == l1, lane, big), axis=1, keepdims=True)
    el2 = jnp.where(lane == i1, -jnp.inf, el)
    l2 = jnp.max(el2, axis=1, keepdims=True)
    i2 = jnp.min(jnp.where(el2 == l2, lane, big), axis=1, keepdims=True)
    e2 = jnp.exp(l2 - l1)
    g1 = p_grp / (1.0 + e2)
    g2 = p_grp * e2 / (1.0 + e2)
    route = jnp.where(lane == 0, i1 - N_GROUPS,
                      jnp.where(lane == 1, i2 - N_GROUPS,
                                jnp.where(lane == 2, g1, jnp.where(lane == 3, g2, 0.0))))
    route_ref[...] = route


def _merge(x2, gm, wg, o_a, o_f, o_s, o_d, wb, wo, gf, wr, br, tm):
    n, d = x2.shape
    row = lambda w: pl.BlockSpec((tm, w), lambda i: (i, 0))
    full = lambda a: pl.BlockSpec(a.shape, lambda i: (0,) * a.ndim)
    return pl.pallas_call(
        _merge_kernel,
        grid=(n // tm,),
        in_specs=[row(d), full(gm), full(wg), row(256), row(256), row(256), row(256),
                  full(wb), full(wo), full(gf), full(wr), full(br)],
        out_specs=[row(d), row(d), row(LANES)],
        out_shape=[jax.ShapeDtypeStruct((n, d), F32), jax.ShapeDtypeStruct((n, d), F32),
                   jax.ShapeDtypeStruct((n, LANES), F32)],
        compiler_params=_cparams("arbitrary"),
        name="merge",
    )(x2, gm, wg, o_a, o_f, o_s, o_d, wb, wo, gf, wr, br)


def _rank_kernel(route_ref, ltri_ref, rank_ref, cnt_ref, carry_ref):
    @pl.when(pl.program_id(0) == 0)
    def _():
        carry_ref[...] = jnp.zeros_like(carry_ref)

    route = route_ref[...]
    lane = lax.broadcasted_iota(I32, route.shape, 1)
    e0 = route[:, 0:1].astype(I32)
    e1 = route[:, 1:2].astype(I32)
    oh0 = (lane == e0).astype(F32)
    oh1 = (lane == e1).astype(F32)
    both = oh0 + oh1
    before = jnp.dot(ltri_ref[...], both.astype(BF16), preferred_element_type=F32) + carry_ref[0:1, :]
    r0 = jnp.sum(oh0 * before, axis=1, keepdims=True)
    r1 = jnp.sum(oh1 * (before + oh0), axis=1, keepdims=True)
    rank_ref[...] = jnp.where(lane == 0, r0, jnp.where(lane == 1, r1, 0.0))
    total = carry_ref[0:1, :] + jnp.sum(both, axis=0, keepdims=True)
    carry_ref[0:1, :] = total
    cnt_ref[...] = jnp.broadcast_to(total, cnt_ref.shape)


def _rank(route, tm):
    n = route.shape[0]
    ltri = (jnp.arange(tm)[:, None] > jnp.arange(tm)[None, :]).astype(BF16)
    return pl.pallas_call(
        _rank_kernel,
        grid=(n // tm,),
        in_specs=[pl.BlockSpec((tm, LANES), lambda i: (i, 0)), pl.BlockSpec((tm, tm), lambda i: (0, 0))],
        out_specs=[pl.BlockSpec((tm, LANES), lambda i: (i, 0)), pl.BlockSpec((8, LANES), lambda i: (0, 0))],
        out_shape=[jax.ShapeDtypeStruct((n, LANES), F32), jax.ShapeDtypeStruct((8, LANES), F32)],
        scratch_shapes=[pltpu.VMEM((8, LANES), F32)],
        compiler_params=_cparams("arbitrary"),
        name="moe_rank",
    )(route, ltri)


def _expert_kernel(be_ref, nu_ref, tok_ref, h_hbm, wup_ref, wdn_ref, y_ref, xbuf, sem, *, te):
    b = pl.program_id(0)

    def row_copy(r):
        return pltpu.make_async_copy(h_hbm.at[pl.ds(tok_ref[0, 0, r], 1), :], xbuf.at[pl.ds(r, 1), :], sem)

    @pl.when(b < nu_ref[0])
    def _():
        def issue(r, c):
            row_copy(r).start()
            return c
        lax.fori_loop(0, te, issue, 0)

        def drain(r, c):
            row_copy(r).wait()
            return c
        lax.fori_loop(0, te, drain, 0)

        xb = xbuf[...].astype(BF16)
        gu = jnp.dot(xb, wup_ref[0], preferred_element_type=F32)
        g = gu[:, :EXPERT_FF]
        act = g * jax.nn.sigmoid(g) * gu[:, EXPERT_FF:]
        y_ref[...] = jnp.dot(act.astype(BF16), wdn_ref[0], preferred_element_type=F32)

    @pl.when(b >= nu_ref[0])
    def _():
        y_ref[...] = jnp.zeros_like(y_ref)


def _experts(blk_expert, n_used, slot_tok, h2, w_up, w_down, te):
    n_blocks = blk_expert.shape[0]
    d = h2.shape[1]
    grid_spec = pltpu.PrefetchScalarGridSpec(
        num_scalar_prefetch=2,
        grid=(n_blocks,),
        in_specs=[pl.BlockSpec((1, 1, te), lambda b, be, nu: (b, 0, 0), memory_space=pltpu.SMEM),
                  pl.BlockSpec(memory_space=pl.ANY),
                  pl.BlockSpec((1, d, 2 * EXPERT_FF), lambda b, be, nu: (be[b], 0, 0)),
                  pl.BlockSpec((1, EXPERT_FF, d), lambda b, be, nu: (be[b], 0, 0))],
        out_specs=pl.BlockSpec((te, d), lambda b, be, nu: (b, 0)),
        scratch_shapes=[pltpu.VMEM((te, d), F32), pltpu.SemaphoreType.DMA(())],
    )
    return pl.pallas_call(
        functools.partial(_expert_kernel, te=te),
        grid_spec=grid_spec,
        out_shape=jax.ShapeDtypeStruct((n_blocks * te, d), F32),
        compiler_params=_cparams("arbitrary"),
        name="moe_experts",
    )(blk_expert, n_used, slot_tok.reshape(n_blocks, 1, te), h2, w_up, w_down)


def _combine_kernel(pos_ref, x_ref, route_ref, y_hbm, o_ref, ybuf, sem, *, tc):
    def row_copy(r):
        return pltpu.make_async_copy(y_hbm.at[pl.ds(pos_ref[0, 0, r], 1), :], ybuf.at[pl.ds(r, 1), :], sem)

    def issue(r, c):
        row_copy(r).start()
        return c
    lax.fori_loop(0, 2 * tc, issue, 0)

    def drain(r, c):
        row_copy(r).wait()
        return c
    lax.fori_loop(0, 2 * tc, drain, 0)

    route = route_ref[...]
    o_ref[...] = x_ref[...] + route[:, 2:3] * ybuf[0:tc, :] + route[:, 3:4] * ybuf[tc:2 * tc, :]


def _combine(pos, x2, route, yb, tc):
    n, d = x2.shape
    nt = n // tc
    pos_t = jnp.transpose(pos.reshape(nt, tc, 2), (0, 2, 1)).reshape(nt, 1, 2 * tc)
    return pl.pallas_call(
        functools.partial(_combine_kernel, tc=tc),
        grid=(nt,),
        in_specs=[pl.BlockSpec((1, 1, 2 * tc), lambda i: (i, 0, 0), memory_space=pltpu.SMEM),
                  pl.BlockSpec((tc, d), lambda i: (i, 0)),
                  pl.BlockSpec((tc, LANES), lambda i: (i, 0)),
                  pl.BlockSpec(memory_space=pl.ANY)],
        out_specs=pl.BlockSpec((tc, d), lambda i: (i, 0)),
        out_shape=jax.ShapeDtypeStruct((n, d), F32),
        scratch_shapes=[pltpu.VMEM((2 * tc, d), F32), pltpu.SemaphoreType.DMA(())],
        compiler_params=_cparams("arbitrary"),
        name="moe_combine",
    )(pos_t, x2, route, yb)


def _t5_bucket(dist):
    n = jnp.maximum(dist, 0)
    max_exact = T5_BUCKETS // 2
    nf = jnp.maximum(n, 1).astype(F32)
    large = max_exact + (jnp.log(nf / max_exact) / math.log(T5_MAX_DISTANCE / max_exact)
                         * (T5_BUCKETS - max_exact)).astype(I32)
    large = jnp.minimum(large, T5_BUCKETS - 1)
    return jnp.where(n < max_exact, n, large)


def _bias_tiles(t5_table):
    t = SWA_BLOCK
    dist = t + jnp.arange(t)[:, None] - jnp.arange(2 * t)[None, :]
    tile = jnp.transpose(t5_table[_t5_bucket(dist)].astype(F32), (2, 0, 1))
    valid = (dist >= 0) & (dist < t)
    swa = jnp.where(valid[None], tile[:4], NEG_INF)
    dsa = tile[4:] - t5_table[T5_BUCKETS - 1, 4:].astype(F32)[:, None, None]
    return swa, dsa


def _layer_weights(w_in, qk_gain, forget_bias, w_branch):
    offs = np.concatenate([[0], np.cumsum(IN_SPLITS)]).tolist()
    part = lambda k: w_in[:, offs[k]:offs[k + 1]]
    dup = lambda w: jnp.concatenate([w, w], axis=1)
    aq = part(0).reshape(-1, 4, HEAD_DIM)[:, (0, 2, 1, 3)].reshape(-1, 256)
    cols = [aq, part(1), part(2), part(3), part(4), part(5), part(7), part(8), part(9),
            part(10), dup(part(11)), dup(part(12)), part(13), dup(part(14))]
    w1 = jnp.concatenate(cols, axis=1).astype(BF16)
    d = w_in.shape[0]
    wm = jnp.concatenate([part(6), part(15), jnp.zeros((d, LANES - 8), F32)], axis=1)
    wg = part(16).astype(BF16)
    tile = lambda g, reps, scale: jnp.pad(jnp.tile(g, reps) * scale, (0, 256 - reps * HEAD_DIM))
    gains = jnp.stack([tile(qk_gain[0, 0], 4, ATTN_SCALE), tile(qk_gain[0, 1], 2, 1.0),
                       tile(qk_gain[1, 0], 4, ATTN_SCALE), tile(qk_gain[1, 1], 4, 1.0),
                       tile(qk_gain[2, 0], 4, ATTN_SCALE), tile(qk_gain[2, 1], 2, 1.0),
                       jnp.zeros((256,), F32), jnp.zeros((256,), F32)]).astype(F32)
    fb = jnp.pad(forget_bias.astype(F32), (0, LANES - 4)).reshape(1, LANES)
    wb0 = w_branch[0].reshape(4, HEAD_DIM, -1)[(0, 2, 1, 3), :, :].reshape(256, -1)
    wb = jnp.stack([wb0, w_branch[1], w_branch[2], w_branch[3]]).astype(BF16)
    return w1, wm, wg, gains, fb, wb


def kernel(x, norm_mix_g, w_in, forget_bias, attn_sinks, qk_gain, w_branch, w_out, t5_table, norm_ffn_g,
           w_router_group, b_router_group, w_router_expert, b_router_expert, w_expert_up, w_expert_down):
    b, s, d = x.shape
    n = b * s
    depth = w_in.shape[0]
    top_k = min(DSA_TOPK_MAX, s // 4)
    tm_proj = min(512, s)
    fox_tq, fox_tk = min(256, s), min(512, s)
    sb_t = min(256, s)
    te = 256
    tc = 128

    gseg = (jnp.arange(256)[:, None] // HEAD_DIM == jnp.arange(256)[None, :] // HEAD_DIM).astype(BF16)
    bias_swa, bias_dsa = _bias_tiles(t5_table)
    n_blocks = -(-2 * n // te) + N_EXPERTS
    tok_ids = jnp.repeat(jnp.arange(n, dtype=I32), 2)

    for layer in range(depth):
        w1, wm, wg, gains, fb, wb = _layer_weights(w_in[layer], qk_gain[layer], forget_bias[layer],
                                                   w_branch[layer])
        sinks = jnp.broadcast_to(jnp.pad(attn_sinks[layer].astype(F32), (0, 4))[:, None], (8, LANES))
        (aq, ak, av, fq, fk, fv, sq, sk, sv, dq, dkk, dvv, iq, ikk, cm) = _proj(
            x, norm_mix_g[layer].reshape(1, d), w1, wm, gseg, gains, fb, tm_proj)

        o_swa = _swa(aq, ak, av, bias_swa, sinks)
        crow = jnp.transpose(cm[:, :, :4], (0, 2, 1)).reshape(b, 4, s // fox_tk, fox_tk)
        o_fox = _fox(fq, fk, fv, cm, crow, fox_tq, fox_tk)
        o_sb = _sb(sq, sk, sv, sb_t, sb_t)
        o_dsa = _dsa(dq, dkk, dvv, iq, ikk, cm, bias_dsa, top_k)

        wr = jnp.concatenate([w_router_group[layer], w_router_expert[layer],
                              jnp.zeros((d, LANES - N_GROUPS - N_EXPERTS), F32)], axis=1)
        br = jnp.concatenate([b_router_group[layer], b_router_expert[layer],
                              jnp.zeros((LANES - N_GROUPS - N_EXPERTS,), F32)]).reshape(1, LANES)
        x2, h2, route = _merge(
            x.reshape(n, d), norm_mix_g[layer].reshape(1, d), wg,
            o_swa.reshape(n, 256), o_fox.reshape(n, 256), o_sb.reshape(n, 256), o_dsa.reshape(n, 256),
            wb, w_out[layer].astype(BF16), norm_ffn_g[layer].reshape(1, d), wr, br, min(256, n))

        rank, cnt = _rank(route, min(512, n))
        counts = cnt[0, :N_EXPERTS].astype(I32)
        padded = (counts + te - 1) // te * te
        pend = jnp.cumsum(padded)
        pstart = pend - padded
        expert = route[:, :2].astype(I32)
        pos = pstart[expert] + rank[:, :2].astype(I32)
        slot_tok = jnp.zeros((n_blocks * te,), I32).at[pos.reshape(-1)].set(tok_ids)
        blk_expert = jnp.clip(jnp.searchsorted(pend, jnp.arange(n_blocks, dtype=I32) * te, side="right"),
                              0, N_EXPERTS - 1).astype(I32)
        n_used = (pend[-1:] // te).astype(I32)

        yb = _experts(blk_expert, n_used, slot_tok, h2, w_expert_up[layer].astype(BF16),
                      w_expert_down[layer].astype(BF16), te)
        x = _combine(pos, x2, route, yb, tc).reshape(b, s, d)
    return x
```

```python
import functools
import math

import jax
import jax.numpy as jnp
import numpy as np
from jax import lax
from jax.experimental import pallas as pl
from jax.experimental.pallas import tpu as pltpu

F32 = jnp.float32
BF16 = jnp.bfloat16
I32 = jnp.int32

HEAD_DIM = 64
LANES = 128
NORM_EPS = 1e-6
NEG_INF = -1e30
M_INIT = -1e29
ATTN_SCALE = HEAD_DIM ** -0.5
SWA_BLOCK = 128
IDX_SCALE = 64 ** -0.5
IDX_HEADS = 4
DSA_TOPK_MAX = 256
T5_BUCKETS = 32
T5_MAX_DISTANCE = 128
N_GROUPS = 4
EXPERTS_PER_GROUP = 8
N_EXPERTS = N_GROUPS * EXPERTS_PER_GROUP
EXPERT_FF = 512
SB_DEAD = -110.0
INT_MIN = -2 ** 31
VMEM_LIMIT = 56 * 1024 * 1024

IN_SPLITS = (256, 128, 128, 256, 256, 256, 4, 256, 256, 256, 256, 64, 64, 256, 64, 4, 4096)

_SEG = dict(aq=(0, 256), ak=(256, 128), av=(384, 128), fq=(512, 256), fk=(768, 256), fv=(1024, 256),
            sq=(1280, 256), sk=(1536, 256), sv=(1792, 256), dq=(2048, 256), dkk=(2304, 128),
            dvv=(2432, 128), iq=(2560, 256), ikk=(2816, 128))
_W1_COLS = 2944
_SEG_ORDER = ("aq", "ak", "av", "fq", "fk", "fv", "sq", "sk", "sv", "dq", "dkk", "dvv", "iq", "ikk")


def _cparams(*sem):
    return pltpu.CompilerParams(dimension_semantics=sem, vmem_limit_bytes=VMEM_LIMIT)


def _rms(x, g):
    return x * lax.rsqrt(jnp.mean(x * x, axis=-1, keepdims=True) + NORM_EPS) * g


def _log_sigmoid(z):
    return jnp.minimum(z, 0.0) - jnp.log(1.0 + jnp.exp(-jnp.abs(z)))


def _dot_nt(a, b):
    return lax.dot_general(a, b, (((1,), (1,)), ((), ())), preferred_element_type=F32)


def _split_heads(qp):
    lo = lax.broadcasted_iota(I32, (1, LANES), 1) < HEAD_DIM
    zero = jnp.zeros_like(qp)
    return jnp.concatenate([jnp.where(lo, qp, zero), jnp.where(lo, zero, qp)], axis=0)


def _merge_heads(o, t):
    lo = lax.broadcasted_iota(I32, (1, LANES), 1) < HEAD_DIM
    return jnp.where(lo, o[:t], o[t:])


def _proj_kernel(x_ref, g_ref, w1_ref, wm_ref, gseg_ref, gains_ref, fb_ref, ltri_ref, *rest):
    outs = dict(zip(_SEG_ORDER, rest[:len(_SEG_ORDER)]))
    cm_ref = rest[len(_SEG_ORDER)]
    carry_ref = rest[len(_SEG_ORDER) + 1]

    @pl.when(pl.program_id(1) == 0)
    def _():
        carry_ref[...] = jnp.zeros_like(carry_ref)

    h = _rms(x_ref[0], g_ref[...])
    hb = h.astype(BF16)

    def seg(name):
        off, width = _SEG[name]
        return jnp.dot(hb, w1_ref[:, off:off + width], preferred_element_type=F32)

    def head_norm(t, row):
        width = t.shape[1]
        ssq = jnp.dot((t * t).astype(BF16), gseg_ref[:width, :width], preferred_element_type=F32)
        return t * lax.rsqrt(ssq * (1.0 / HEAD_DIM) + NORM_EPS) * gains_ref[row:row + 1, :width]

    normed = dict(aq=0, ak=1, fq=2, fk=3, dq=4, dkk=5)
    scaled = dict(sq=ATTN_SCALE, iq=IDX_SCALE)
    for name in _SEG_ORDER:
        t = seg(name)
        if name in normed:
            t = head_norm(t, normed[name])
        elif name in scaled:
            t = t * scaled[name]
        outs[name][0] = t.astype(BF16)

    misc = jnp.dot(h, wm_ref[...], precision=lax.Precision.HIGHEST, preferred_element_type=F32)
    lane = lax.broadcasted_iota(I32, misc.shape, 1)
    logf = jnp.where(lane < 4, _log_sigmoid(misc + fb_ref[...]), 0.0)
    c = jnp.dot(ltri_ref[...], logf, precision=lax.Precision.HIGHEST,
                preferred_element_type=F32) + carry_ref[0:1, :]
    tm = misc.shape[0]
    carry_ref[0:1, :] = c[tm - 1:tm, :]
    cm_ref[0] = jnp.where(lane < 4, c, misc)


def _proj(x, g, w1, wm, gseg, gains, fb, tm):
    b, s, d = x.shape
    ltri = jnp.tril(jnp.ones((tm, tm), F32))
    full = lambda shape: pl.BlockSpec(shape, lambda bi, si: (0,) * len(shape))
    out_shapes = [jax.ShapeDtypeStruct((b, s, _SEG[n][1]), BF16) for n in _SEG_ORDER]
    out_shapes.append(jax.ShapeDtypeStruct((b, s, LANES), F32))
    out_specs = [pl.BlockSpec((1, tm, _SEG[n][1]), lambda bi, si: (bi, si, 0)) for n in _SEG_ORDER]
    out_specs.append(pl.BlockSpec((1, tm, LANES), lambda bi, si: (bi, si, 0)))
    return pl.pallas_call(
        _proj_kernel,
        grid=(b, s // tm),
        in_specs=[pl.BlockSpec((1, tm, d), lambda bi, si: (bi, si, 0)),
                  full((1, d)), full(w1.shape), full(wm.shape), full(gseg.shape),
                  full(gains.shape), full(fb.shape), full((tm, tm))],
        out_specs=out_specs,
        out_shape=out_shapes,
        scratch_shapes=[pltpu.VMEM((8, LANES), F32)],
        compiler_params=_cparams("arbitrary", "arbitrary"),
        name="proj",
    )(x, g, w1, wm, gseg, gains, fb, ltri)


def _swa_kernel(q_ref, kp_ref, kc_ref, vp_ref, vc_ref, bias_ref, sink_ref, o_ref):
    i = pl.program_id(1)
    t = SWA_BLOCK
    q = q_ref[0]
    kcat = jnp.concatenate([kp_ref[0], kc_ref[0]], axis=0)
    vcat = jnp.concatenate([vp_ref[0], vc_ref[0]], axis=0)
    col = lax.broadcasted_iota(I32, (t, 2 * t), 1)
    no_prev = (col < t) & (i == 0)
    pairs = []
    for pair in range(2):
        qs = _split_heads(q[:, pair * LANES:(pair + 1) * LANES])
        s = _dot_nt(qs, kcat)
        ps = []
        for hh in range(2):
            head = pair + 2 * hh
            sh = s[hh * t:(hh + 1) * t] + bias_ref[head]
            sh = jnp.where(no_prev, NEG_INF, sh)
            sink = sink_ref[head:head + 1, 0:1]
            m = jnp.maximum(jnp.max(sh, axis=1, keepdims=True), sink)
            p = jnp.exp(sh - m)
            denom = jnp.sum(p, axis=1, keepdims=True) + jnp.exp(sink - m)
            ps.append(p / denom)
        o = jnp.dot(jnp.concatenate(ps, axis=0).astype(BF16), vcat, preferred_element_type=F32)
        pairs.append(_merge_heads(o, t))
    o_ref[0] = jnp.concatenate(pairs, axis=1).astype(BF16)


def _swa(aq, ak, av, bias, sinks):
    b, s, _ = aq.shape
    t = SWA_BLOCK
    cur = lambda bi, i: (bi, i, 0)
    prev = lambda bi, i: (bi, jnp.maximum(i - 1, 0), 0)
    return pl.pallas_call(
        _swa_kernel,
        grid=(b, s // t),
        in_specs=[pl.BlockSpec((1, t, 256), cur),
                  pl.BlockSpec((1, t, LANES), prev), pl.BlockSpec((1, t, LANES), cur),
                  pl.BlockSpec((1, t, LANES), prev), pl.BlockSpec((1, t, LANES), cur),
                  pl.BlockSpec(bias.shape, lambda bi, i: (0, 0, 0)),
                  pl.BlockSpec(sinks.shape, lambda bi, i: (0, 0))],
        out_specs=pl.BlockSpec((1, t, 256), cur),
        out_shape=jax.ShapeDtypeStruct((b, s, 256), BF16),
        compiler_params=_cparams("arbitrary", "arbitrary"),
        name="swa",
    )(aq, ak, ak, av, av, bias, sinks)


def _flash_update(carry, s_heads, vj):
    m, l, acc = carry
    s = jnp.concatenate(s_heads, axis=0)
    m_new = jnp.maximum(m, jnp.max(s, axis=1, keepdims=True))
    alpha = jnp.exp(m - m_new)
    p = jnp.exp(s - m_new)
    l = alpha * l + jnp.sum(p, axis=1, keepdims=True)
    acc = alpha * acc + jnp.dot(p.astype(BF16), vj, preferred_element_type=F32)
    return m_new, l, acc


def _fox_kernel(q_ref, k_ref, v_ref, ccol_ref, crow_ref, o_ref, *, tq, tk):
    i = pl.program_id(1)
    q = q_ref[0]
    ccol = ccol_ref[0]
    nfull = (i * tq) // tk
    row = lax.broadcasted_iota(I32, (tq, tk), 0)
    col = lax.broadcasted_iota(I32, (tq, tk), 1)
    pairs = []
    for pair in range(2):
        sl = slice(pair * LANES, (pair + 1) * LANES)
        qs = _split_heads(q[:, sl])
        heads = (2 * pair, 2 * pair + 1)

        def tile(j, carry, masked):
            start = pl.multiple_of(j * tk, tk)
            kj = k_ref[0, pl.ds(start, tk), sl]
            vj = v_ref[0, pl.ds(start, tk), sl]
            s = _dot_nt(qs, kj)
            s_heads = []
            for hh, head in enumerate(heads):
                d = ccol[:, head:head + 1] - crow_ref[0, head, pl.ds(j, 1), :]
                sh = s[hh * tq:(hh + 1) * tq] + d
                if masked:
                    sh = jnp.where(j * tk + col <= i * tq + row, sh, NEG_INF)
                s_heads.append(sh)
            return _flash_update(carry, s_heads, vj)

        init = (jnp.full((2 * tq, 1), M_INIT, F32), jnp.zeros((2 * tq, 1), F32),
                jnp.zeros((2 * tq, LANES), F32))
        carry = lax.fori_loop(0, nfull, lambda j, c: tile(j, c, False), init)
        _, l, acc = tile(nfull, carry, True)
        pairs.append(_merge_heads(acc / l, tq))
    o_ref[0] = jnp.concatenate(pairs, axis=1).astype(BF16)


def _fox(fq, fk, fv, ccol, crow, tq, tk):
    b, s, _ = fq.shape
    return pl.pallas_call(
        functools.partial(_fox_kernel, tq=tq, tk=tk),
        grid=(b, s // tq),
        in_specs=[pl.BlockSpec((1, tq, 256), lambda bi, i: (bi, i, 0)),
                  pl.BlockSpec((1, s, 256), lambda bi, i: (bi, 0, 0)),
                  pl.BlockSpec((1, s, 256), lambda bi, i: (bi, 0, 0)),
                  pl.BlockSpec((1, tq, LANES), lambda bi, i: (bi, i, 0)),
                  pl.BlockSpec((1, 4, s // tk, tk), lambda bi, i: (bi, 0, 0, 0))],
        out_specs=pl.BlockSpec((1, tq, 256), lambda bi, i: (bi, i, 0)),
        out_shape=jax.ShapeDtypeStruct((b, s, 256), BF16),
        compiler_params=_cparams("arbitrary", "arbitrary"),
        name="fox",
    )(fq, fk, fv, ccol, crow)


def _sb_kernel(q_ref, k_ref, v_ref, u_ref, o_ref, *, tq, tk):
    i = pl.program_id(1)
    q = q_ref[0]
    u = u_ref[...]
    nfull = (i * tq) // tk
    row = lax.broadcasted_iota(I32, (2 * tq, tk), 0)
    col = lax.broadcasted_iota(I32, (2 * tq, tk), 1)
    qrow = jnp.where(row >= tq, row - tq, row)
    pairs = []
    for pair in range(2):
        sl = slice(pair * LANES, (pair + 1) * LANES)
        qs = _split_heads(q[:, sl])

        def tile(j, r, acc, masked):
            start = pl.multiple_of(j * tk, tk)
            kj = k_ref[0, pl.ds(start, tk), sl]
            vj = v_ref[0, pl.ds(start, tk), sl]
            z = _dot_nt(qs, kj)
            sp = jnp.log(1.0 + jnp.exp(-jnp.abs(z)))
            log_beta = jnp.minimum(z, 0.0) - sp
            log_keep = jnp.minimum(-z, 0.0) - sp
            if masked:
                strict = j * tk + col < i * tq + qrow
                log_keep = jnp.where(strict, log_keep, 0.0)
            hi = log_keep.astype(BF16)
            lo = (log_keep - hi.astype(F32)).astype(BF16)
            later = (jnp.dot(hi, u, preferred_element_type=F32)
                     + jnp.dot(lo, u, preferred_element_type=F32))
            a = jnp.exp(log_beta + later + r)
            if masked:
                a = jnp.where(strict, a, 0.0)
            acc = acc + jnp.dot(a.astype(BF16), vj, preferred_element_type=F32)
            r = r + jnp.sum(log_keep, axis=1, keepdims=True)
            return r, acc

        r, acc = tile(nfull, jnp.zeros((2 * tq, 1), F32), jnp.zeros((2 * tq, LANES), F32), True)

        def cond(c):
            return (c[0] >= 0) & (c[1] > 0)

        def body(c):
            j, _, r, acc = c
            r, acc = tile(j, r, acc, False)
            return j - 1, (jnp.max(r) > SB_DEAD).astype(I32), r, acc

        _, _, _, acc = lax.while_loop(cond, body, (nfull - 1, (jnp.max(r) > SB_DEAD).astype(I32), r, acc))
        pairs.append(_merge_heads(acc, tq))
    o_ref[0] = jnp.concatenate(pairs, axis=1).astype(BF16)


def _sb(sq, sk, sv, tq, tk):
    b, s, _ = sq.shape
    u = (jnp.arange(tk)[:, None] > jnp.arange(tk)[None, :]).astype(BF16)
    return pl.pallas_call(
        functools.partial(_sb_kernel, tq=tq, tk=tk),
        grid=(b, s // tq),
        in_specs=[pl.BlockSpec((1, tq, 256), lambda bi, i: (bi, i, 0)),
                  pl.BlockSpec((1, s, 256), lambda bi, i: (bi, 0, 0)),
                  pl.BlockSpec((1, s, 256), lambda bi, i: (bi, 0, 0)),
                  pl.BlockSpec((tk, tk), lambda bi, i: (0, 0))],
        out_specs=pl.BlockSpec((1, tq, 256), lambda bi, i: (bi, i, 0)),
        out_shape=jax.ShapeDtypeStruct((b, s, 256), BF16),
        compiler_params=_cparams("arbitrary", "arbitrary"),
        name="stickbreak",
    )(sq, sk, sv, u)


def _dsa_kernel(q_ref, kk_ref, vt_ref, iq_ref, ikk_ref, wt_ref, bias_ref, lstrict_ref, o_ref,
                key_ref, *, t, top_k):
    i = pl.program_id(1)
    group = 64
    causal = lax.broadcasted_iota(I32, (t, t), 0) <= lax.broadcasted_iota(I32, (t, t), 1)

    def head_stack(x):
        return jnp.concatenate([_split_heads(x[:, :LANES]), _split_heads(x[:, LANES:])], axis=0)

    def tile_rows(ref, j):
        return ref[0, pl.ds(pl.multiple_of(j * t, t), t), :]

    iqs = head_stack(iq_ref[0])
    wt = wt_ref[0]
    w = [wt[4 + h:5 + h, :] * (IDX_HEADS ** -0.5) for h in range(IDX_HEADS)]

    def score_tile(j, masked):
        lg = _dot_nt(tile_rows(ikk_ref, j), iqs)
        sc = w[0] * jnp.maximum(lg[:, 0:t], 0.0)
        for h in range(1, IDX_HEADS):
            sc = sc + w[h] * jnp.maximum(lg[:, h * t:(h + 1) * t], 0.0)
        bits = pltpu.bitcast(sc, I32)
        key = bits ^ ((bits >> 31) & 0x7FFFFFFF)
        key = jnp.where(key == -1, 0, key)
        if masked:
            key = jnp.where(causal, key, INT_MIN)
        key_ref[j] = key

    def p1(j, c):
        score_tile(j, False)
        return c

    lax.fori_loop(0, i, p1, 0)
    score_tile(i, True)

    def count(pred):
        def cb(j, acc):
            for g in range(t // group):
                acc = acc + jnp.where(pred(key_ref[j, g * group:(g + 1) * group, :]), 1.0, 0.0)
            return acc
        acc = lax.fori_loop(0, i + 1, cb, jnp.zeros((group, t), F32))
        return jnp.sum(acc, axis=0, keepdims=True)

    n_avail = (i * t + lax.broadcasted_iota(I32, (1, t), 1) + 1).astype(F32)
    done0 = jnp.where(n_avail <= top_k, 1.0, 0.0)

    def bis_cond(c):
        return (c[0] < 32) & (c[3] < 0.5)

    def bis_body(c):
        b, thr_u, done, _ = c
        cand_u = thr_u | (jnp.int32(1) << (31 - b))
        cand_s = cand_u ^ INT_MIN
        n = count(lambda k: k >= cand_s)
        take = (n >= top_k) & (done < 0.5)
        thr_u = jnp.where(take, cand_u, thr_u)
        done = jnp.where(take & (n == top_k), 1.0, done)
        return b + 1, thr_u, done, jnp.min(done)

    _, thr_u, _, _ = lax.while_loop(bis_cond, bis_body,
                                    (jnp.int32(0), jnp.zeros((1, t), I32), done0, jnp.min(done0)))
    thr = jnp.maximum(thr_u ^ INT_MIN, INT_MIN + 1)
    surplus = jnp.max(count(lambda k: k >= thr)) > top_k

    def tie_pass():
        need = top_k - count(lambda k: k > thr)

        def tb(j, seen):
            k = key_ref[j]
            eq = k == thr
            eqf = jnp.where(eq, 1.0, 0.0)
            before = jnp.dot(lstrict_ref[...], eqf.astype(BF16), preferred_element_type=F32) + seen
            sel = (k > thr) | (eq & (before < need))
            key_ref[j] = jnp.where(sel, 1, INT_MIN)
            return seen + jnp.sum(eqf, axis=0, keepdims=True)

        lax.fori_loop(0, i + 1, tb, jnp.zeros((1, t), F32))
        return jnp.zeros((1, t), I32)

    thr = lax.cond(surplus, tie_pass, lambda: thr)

    qs = head_stack(q_ref[0])

    def attn_tile(j, carry, bias_kind):
        m, l, acc = carry
        st = _dot_nt(tile_rows(kk_ref, j), qs)
        selb = jnp.where(key_ref[j] >= thr, 0.0, NEG_INF)
        parts = []
        for h in range(4):
            sh = st[:, h * t:(h + 1) * t] + selb
            if bias_kind is not None:
                sh = sh + bias_ref[bias_kind, h]
            parts.append(sh)
        st = jnp.concatenate(parts, axis=1)
        m_new = jnp.maximum(m, jnp.max(st, axis=0, keepdims=True))
        alpha = jnp.exp(m - m_new)
        p = jnp.exp(st - m_new)
        l = alpha * l + jnp.sum(p, axis=0, keepdims=True)
        acc = alpha * acc + jnp.dot(vt_ref[0, j], p.astype(BF16), preferred_element_type=F32)
        return m_new, l, acc

    init = (jnp.full((1, 4 * t), M_INIT, F32), jnp.zeros((1, 4 * t), F32), jnp.zeros((HEAD_DIM, 4 * t), F32))
    carry = lax.fori_loop(0, jnp.maximum(i - 1, 0), lambda j, c: attn_tile(j, c, None), init)
    carry = lax.cond(i > 0, lambda c: attn_tile(i - 1, c, 0), lambda c: c, carry)
    _, l, acc = attn_tile(i, carry, 1)
    ot = acc / l
    ot = jnp.concatenate([ot[:, h * t:(h + 1) * t] for h in range(4)], axis=0)
    o_ref[0] = ot.T.astype(BF16)


def _dsa_bias(t5_table, t):
    assert t + 1 >= T5_MAX_DISTANCE
    k = jnp.arange(t)[:, None]
    q = jnp.arange(t)[None, :]
    far = t5_table[T5_BUCKETS - 1, 4:].astype(F32)
    tiles = []
    for off in (t, 0):
        dist = off + q - k
        b = jnp.transpose(t5_table[_t5_bucket(dist)][..., 4:].astype(F32), (2, 0, 1)) - far[:, None, None]
        tiles.append(jnp.where((dist >= 0)[None], b, 0.0))
    return jnp.stack(tiles)


def _dsa(dq, dkk, dvv, iq, ikk, cm, bias, top_k, t):
    b, s, _ = dq.shape
    nt = s // t
    lstrict = (jnp.arange(t)[:, None] > jnp.arange(t)[None, :]).astype(BF16)
    vt = jnp.transpose(dvv[:, :, :HEAD_DIM].reshape(b, nt, t, HEAD_DIM), (0, 1, 3, 2))
    wt = jnp.transpose(cm[:, :, :8], (0, 2, 1))
    blk = lambda w: pl.BlockSpec((1, t, w), lambda bi, i: (bi, i, 0))
    seq = lambda w: pl.BlockSpec((1, s, w), lambda bi, i: (bi, 0, 0))
    return pl.pallas_call(
        functools.partial(_dsa_kernel, t=t, top_k=top_k),
        grid=(b, nt),
        in_specs=[blk(256), seq(LANES),
                  pl.BlockSpec((1, nt, HEAD_DIM, t), lambda bi, i: (bi, 0, 0, 0)),
                  blk(256), seq(LANES),
                  pl.BlockSpec((1, 8, t), lambda bi, i: (bi, 0, i)),
                  pl.BlockSpec(bias.shape, lambda bi, i: (0, 0, 0, 0)),
                  pl.BlockSpec((t, t), lambda bi, i: (0, 0))],
        out_specs=blk(256),
        out_shape=jax.ShapeDtypeStruct((b, s, 256), BF16),
        scratch_shapes=[pltpu.VMEM((nt, t, t), I32)],
        compiler_params=_cparams("arbitrary", "arbitrary"),
        name="dsa",
    )(dq, dkk, vt, iq, ikk, wt, bias, lstrict)


def _merge_kernel(x_ref, gm_ref, wg_ref, oa_ref, of_ref, os_ref, od_ref, wb_ref, wo_ref, gf_ref,
                  wr_ref, br_ref, xo_ref, h2_ref, route_ref):
    x = x_ref[...]
    hb = _rms(x, gm_ref[...]).astype(BF16)
    d = x.shape[1]
    merged = None
    for bi, o_ref in enumerate((oa_ref, of_ref, os_ref, od_ref)):
        gate = jax.nn.sigmoid(jnp.dot(hb, wg_ref[:, bi * d:(bi + 1) * d], preferred_element_type=F32))
        term = gate * jnp.dot(o_ref[...], wb_ref[bi], preferred_element_type=F32)
        merged = term if merged is None else merged + term
    xn = x + jnp.dot(merged.astype(BF16), wo_ref[...], preferred_element_type=F32)
    xo_ref[...] = xn
    h2 = _rms(xn, gf_ref[...])
    h2_ref[...] = h2

    logits = jnp.dot(h2, wr_ref[...], precision=lax.Precision.HIGHEST,
                     preferred_element_type=F32) + br_ref[...]
    lane = lax.broadcasted_iota(I32, logits.shape, 1).astype(F32)
    big = 1e9
    gl = jnp.where(lane < N_GROUPS, logits, -jnp.inf)
    gmax = jnp.max(gl, axis=1, keepdims=True)
    grp = jnp.min(jnp.where(gl == gmax, lane, big), axis=1, keepdims=True)
    p_grp = 1.0 / jnp.sum(jnp.exp(gl - gmax), axis=1, keepdims=True)
    first = N_GROUPS + grp * EXPERTS_PER_GROUP
    el = jnp.where((lane >= first) & (lane < first + EXPERTS_PER_GROUP), logits, -jnp.inf)
    l1 = jnp.max(el, axis=1, keepdims=True)
    i1 = jnp.min(jnp.where(el == l1, lane, big), axis=1, keepdims=True)
    el2 = jnp.where(lane == i1, -jnp.inf, el)
    l2 = jnp.max(el2, axis=1, keepdims=True)
    i2 = jnp.min(jnp.where(el2 == l2, lane, big), axis=1, keepdims=True)
    e2 = jnp.exp(l2 - l1)
    g1 = p_grp / (1.0 + e2)
    g2 = p_grp * e2 / (1.0 + e2)
    route = jnp.where(lane == 0, i1 - N_GROUPS,
                      jnp.where(lane == 1, i2 - N_GROUPS,
                                jnp.where(lane == 2, g1, jnp.where(lane == 3, g2, 0.0))))
    route_ref[...] = route


def _merge(x2, gm, wg, o_a, o_f, o_s, o_d, wb, wo, gf, wr, br, tm):
    n, d = x2.shape
    row = lambda w: pl.BlockSpec((tm, w), lambda i: (i, 0))
    full = lambda a: pl.BlockSpec(a.shape, lambda i: (0,) * a.ndim)
    return pl.pallas_call(
        _merge_kernel,
        grid=(n // tm,),
        in_specs=[row(d), full(gm), full(wg), row(256), row(256), row(256), row(256),
                  full(wb), full(wo), full(gf), full(wr), full(br)],
        out_specs=[row(d), row(d), row(LANES)],
        out_shape=[jax.ShapeDtypeStruct((n, d), F32), jax.ShapeDtypeStruct((n, d), F32),
                   jax.ShapeDtypeStruct((n, LANES), F32)],
        compiler_params=_cparams("arbitrary"),
        name="merge",
    )(x2, gm, wg, o_a, o_f, o_s, o_d, wb, wo, gf, wr, br)


def _rank_kernel(route_ref, ltri_ref, rank_ref, cnt_ref, carry_ref):
    @pl.when(pl.program_id(0) == 0)
    def _():
        carry_ref[...] = jnp.zeros_like(carry_ref)

    route = route_ref[...]
    lane = lax.broadcasted_iota(I32, route.shape, 1)
    e0 = route[:, 0:1].astype(I32)
    e1 = route[:, 1:2].astype(I32)
    oh0 = (lane == e0).astype(F32)
    oh1 = (lane == e1).astype(F32)
    both = oh0 + oh1
    before = jnp.dot(ltri_ref[...], both.astype(BF16), preferred_element_type=F32) + carry_ref[0:1, :]
    r0 = jnp.sum(oh0 * before, axis=1, keepdims=True)
    r1 = jnp.sum(oh1 * (before + oh0), axis=1, keepdims=True)
    rank_ref[...] = jnp.where(lane == 0, r0, jnp.where(lane == 1, r1, 0.0))
    total = carry_ref[0:1, :] + jnp.sum(both, axis=0, keepdims=True)
    carry_ref[0:1, :] = total
    cnt_ref[...] = jnp.broadcast_to(total, cnt_ref.shape)


def _rank(route, tm):
    n = route.shape[0]
    ltri = (jnp.arange(tm)[:, None] > jnp.arange(tm)[None, :]).astype(BF16)
    return pl.pallas_call(
        _rank_kernel,
        grid=(n // tm,),
        in_specs=[pl.BlockSpec((tm, LANES), lambda i: (i, 0)), pl.BlockSpec((tm, tm), lambda i: (0, 0))],
        out_specs=[pl.BlockSpec((tm, LANES), lambda i: (i, 0)), pl.BlockSpec((8, LANES), lambda i: (0, 0))],
        out_shape=[jax.ShapeDtypeStruct((n, LANES), F32), jax.ShapeDtypeStruct((8, LANES), F32)],
        scratch_shapes=[pltpu.VMEM((8, LANES), F32)],
        compiler_params=_cparams("arbitrary"),
        name="moe_rank",
    )(route, ltri)


def _expert_kernel(be_ref, nu_ref, tok_ref, h_hbm, wup_ref, wdn_ref, y_ref, xbuf, sem, *, te):
    b = pl.program_id(0)

    def row_copy(r):
        return pltpu.make_async_copy(h_hbm.at[pl.ds(tok_ref[0, 0, r], 1), :], xbuf.at[pl.ds(r, 1), :], sem)

    @pl.when(b < nu_ref[0])
    def _():
        def issue(r, c):
            row_copy(r).start()
            return c
        lax.fori_loop(0, te, issue, 0)

        def drain(r, c):
            row_copy(r).wait()
            return c
        lax.fori_loop(0, te, drain, 0)

        xb = xbuf[...].astype(BF16)
        gu = jnp.dot(xb, wup_ref[0], preferred_element_type=F32)
        g = gu[:, :EXPERT_FF]
        act = g * jax.nn.sigmoid(g) * gu[:, EXPERT_FF:]
        y_ref[...] = jnp.dot(act.astype(BF16), wdn_ref[0], preferred_element_type=F32)

    @pl.when(b >= nu_ref[0])
    def _():
        y_ref[...] = jnp.zeros_like(y_ref)


def _experts(blk_expert, n_used, slot_tok, h2, w_up, w_down, te):
    n_blocks = blk_expert.shape[0]
    d = h2.shape[1]
    grid_spec = pltpu.PrefetchScalarGridSpec(
        num_scalar_prefetch=2,
        grid=(n_blocks,),
        in_specs=[pl.BlockSpec((1, 1, te), lambda b, be, nu: (b, 0, 0), memory_space=pltpu.SMEM),
                  pl.BlockSpec(memory_space=pl.ANY),
                  pl.BlockSpec((1, d, 2 * EXPERT_FF), lambda b, be, nu: (be[b], 0, 0)),
                  pl.BlockSpec((1, EXPERT_FF, d), lambda b, be, nu: (be[b], 0, 0))],
        out_specs=pl.BlockSpec((te, d), lambda b, be, nu: (b, 0)),
        scratch_shapes=[pltpu.VMEM((te, d), F32), pltpu.SemaphoreType.DMA(())],
    )
    return pl.pallas_call(
        functools.partial(_expert_kernel, te=te),
        grid_spec=grid_spec,
        out_shape=jax.ShapeDtypeStruct((n_blocks * te, d), F32),
        compiler_params=_cparams("arbitrary"),
        name="moe_experts",
    )(blk_expert, n_used, slot_tok.reshape(n_blocks, 1, te), h2, w_up, w_down)


def _combine_kernel(pos_ref, x_ref, route_ref, y_hbm, o_ref, ybuf, sem, *, tc):
    def row_copy(r):
        return pltpu.make_async_copy(y_hbm.at[pl.ds(pos_ref[0, 0, r], 1), :], ybuf.at[pl.ds(r, 1), :], sem)

    def issue(r, c):
        row_copy(r).start()
        return c
    lax.fori_loop(0, 2 * tc, issue, 0)

    def drain(r, c):
        row_copy(r).wait()
        return c
    lax.fori_loop(0, 2 * tc, drain, 0)

    route = route_ref[...]
    o_ref[...] = x_ref[...] + route[:, 2:3] * ybuf[0:tc, :] + route[:, 3:4] * ybuf[tc:2 * tc, :]


def _combine(pos, x2, route, yb, tc):
    n, d = x2.shape
    nt = n // tc
    pos_t = jnp.transpose(pos.reshape(nt, tc, 2), (0, 2, 1)).reshape(nt, 1, 2 * tc)
    return pl.pallas_call(
        functools.partial(_combine_kernel, tc=tc),
        grid=(nt,),
        in_specs=[pl.BlockSpec((1, 1, 2 * tc), lambda i: (i, 0, 0), memory_space=pltpu.SMEM),
                  pl.BlockSpec((tc, d), lambda i: (i, 0)),
                  pl.BlockSpec((tc, LANES), lambda i: (i, 0)),
                  pl.BlockSpec(memory_space=pl.ANY)],
        out_specs=pl.BlockSpec((tc, d), lambda i: (i, 0)),
        out_shape=jax.ShapeDtypeStruct((n, d), F32),
        scratch_shapes=[pltpu.VMEM((2 * tc, d), F32), pltpu.SemaphoreType.DMA(())],
        compiler_params=_cparams("arbitrary"),
        name="moe_combine",
    )(pos_t, x2, route, yb)


def _t5_bucket(dist):
    n = jnp.maximum(dist, 0)
    max_exact = T5_BUCKETS // 2
    nf = jnp.maximum(n, 1).astype(F32)
    large = max_exact + (jnp.log(nf / max_exact) / math.log(T5_MAX_DISTANCE / max_exact)
                         * (T5_BUCKETS - max_exact)).astype(I32)
    large = jnp.minimum(large, T5_BUCKETS - 1)
    return jnp.where(n < max_exact, n, large)


def _swa_bias(t5_table):
    t = SWA_BLOCK
    dist = t + jnp.arange(t)[:, None] - jnp.arange(2 * t)[None, :]
    tile = jnp.transpose(t5_table[_t5_bucket(dist)][..., :4].astype(F32), (2, 0, 1))
    valid = (dist >= 0) & (dist < t)
    return jnp.where(valid[None], tile, NEG_INF)


def _layer_weights(w_in, qk_gain, forget_bias, w_branch):
    offs = np.concatenate([[0], np.cumsum(IN_SPLITS)]).tolist()
    part = lambda k: w_in[:, offs[k]:offs[k + 1]]
    dup = lambda w: jnp.concatenate([w, w], axis=1)
    aq = part(0).reshape(-1, 4, HEAD_DIM)[:, (0, 2, 1, 3)].reshape(-1, 256)
    cols = [aq, part(1), part(2), part(3), part(4), part(5), part(7), part(8), part(9),
            part(10), dup(part(11)), dup(part(12)), part(13), dup(part(14))]
    w1 = jnp.concatenate(cols, axis=1).astype(BF16)
    d = w_in.shape[0]
    wm = jnp.concatenate([part(6), part(15), jnp.zeros((d, LANES - 8), F32)], axis=1)
    wg = part(16).astype(BF16)
    tile = lambda g, reps, scale: jnp.pad(jnp.tile(g, reps) * scale, (0, 256 - reps * HEAD_DIM))
    gains = jnp.stack([tile(qk_gain[0, 0], 4, ATTN_SCALE), tile(qk_gain[0, 1], 2, 1.0),
                       tile(qk_gain[1, 0], 4, ATTN_SCALE), tile(qk_gain[1, 1], 4, 1.0),
                       tile(qk_gain[2, 0], 4, ATTN_SCALE), tile(qk_gain[2, 1], 2, 1.0),
                       jnp.zeros((256,), F32), jnp.zeros((256,), F32)]).astype(F32)
    fb = jnp.pad(forget_bias.astype(F32), (0, LANES - 4)).reshape(1, LANES)
    wb0 = w_branch[0].reshape(4, HEAD_DIM, -1)[(0, 2, 1, 3), :, :].reshape(256, -1)
    wb = jnp.stack([wb0, w_branch[1], w_branch[2], w_branch[3]]).astype(BF16)
    return w1, wm, wg, gains, fb, wb


def kernel(x, norm_mix_g, w_in, forget_bias, attn_sinks, qk_gain, w_branch, w_out, t5_table, norm_ffn_g,
           w_router_group, b_router_group, w_router_expert, b_router_expert, w_expert_up, w_expert_down):
    b, s, d = x.shape
    n = b * s
    depth = w_in.shape[0]
    top_k = min(DSA_TOPK_MAX, s // 4)
    tm_proj = min(512, s)
    fox_tq, fox_tk = min(256, s), min(512, s)
    sb_t = min(256, s)
    dsa_t = min(256, s)
    te = 256
    tc = 128

    gseg = (jnp.arange(256)[:, None] // HEAD_DIM == jnp.arange(256)[None, :] // HEAD_DIM).astype(BF16)
    bias_swa = _swa_bias(t5_table)
    bias_dsa = _dsa_bias(t5_table, dsa_t)
    n_blocks = -(-2 * n // te) + N_EXPERTS
    tok_ids = jnp.repeat(jnp.arange(n, dtype=I32), 2)

    for layer in range(depth):
        w1, wm, wg, gains, fb, wb = _layer_weights(w_in[layer], qk_gain[layer], forget_bias[layer],
                                                   w_branch[layer])
        sinks = jnp.broadcast_to(jnp.pad(attn_sinks[layer].astype(F32), (0, 4))[:, None], (8, LANES))
        (aq, ak, av, fq, fk, fv, sq, sk, sv, dq, dkk, dvv, iq, ikk, cm) = _proj(
            x, norm_mix_g[layer].reshape(1, d), w1, wm, gseg, gains, fb, tm_proj)

        o_swa = _swa(aq, ak, av, bias_swa, sinks)
        crow = jnp.transpose(cm[:, :, :4], (0, 2, 1)).reshape(b, 4, s // fox_tk, fox_tk)
        o_fox = _fox(fq, fk, fv, cm, crow, fox_tq, fox_tk)
        o_sb = _sb(sq, sk, sv, sb_t, sb_t)
        o_dsa = _dsa(dq, dkk, dvv, iq, ikk, cm, bias_dsa, top_k, dsa_t)

        wr = jnp.concatenate([w_router_group[layer], w_router_expert[layer],
                              jnp.zeros((d, LANES - N_GROUPS - N_EXPERTS), F32)], axis=1)
        br = jnp.concatenate([b_router_group[layer], b_router_expert[layer],
                              jnp.zeros((LANES - N_GROUPS - N_EXPERTS,), F32)]).reshape(1, LANES)
        x2, h2, route = _merge(
            x.reshape(n, d), norm_mix_g[layer].reshape(1, d), wg,
            o_swa.reshape(n, 256), o_fox.reshape(n, 256), o_sb.reshape(n, 256), o_dsa.reshape(n, 256),
            wb, w_out[layer].astype(BF16), norm_ffn_g[layer].reshape(1, d), wr, br, min(256, n))

        rank, cnt = _rank(route, min(512, n))
        counts = cnt[0, :N_EXPERTS].astype(I32)
        padded = (counts + te - 1) // te * te
        pend = jnp.cumsum(padded)
        pstart = pend - padded
        expert = route[:, :2].astype(I32)
        pos = pstart[expert] + rank[:, :2].astype(I32)
        slot_tok = jnp.zeros((n_blocks * te,), I32).at[pos.reshape(-1)].set(tok_ids)
        blk_expert = jnp.clip(jnp.searchsorted(pend, jnp.arange(n_blocks, dtype=I32) * te, side="right"),
                              0, N_EXPERTS - 1).astype(I32)
        n_used = (pend[-1:] // te).astype(I32)

        yb = _experts(blk_expert, n_used, slot_tok, h2, w_expert_up[layer].astype(BF16),
                      w_expert_down[layer].astype(BF16), te)
        x = _combine(pos, x2, route, yb, tc).reshape(b, s, d)
    return x
```

```python
import functools
import math

import jax
import jax.numpy as jnp
import numpy as np
from jax import lax
from jax.experimental import pallas as pl
from jax.experimental.pallas import tpu as pltpu

F32 = jnp.float32
BF16 = jnp.bfloat16
I32 = jnp.int32

HEAD_DIM = 64
LANES = 128
NORM_EPS = 1e-6
NEG_INF = -1e30
M_INIT = -1e29
ATTN_SCALE = HEAD_DIM ** -0.5
SWA_BLOCK = 128
IDX_SCALE = 64 ** -0.5
IDX_HEADS = 4
DSA_TOPK_MAX = 256
T5_BUCKETS = 32
T5_MAX_DISTANCE = 128
N_GROUPS = 4
EXPERTS_PER_GROUP = 8
N_EXPERTS = N_GROUPS * EXPERTS_PER_GROUP
EXPERT_FF = 512
SB_DEAD = -110.0
INT_MIN = -2 ** 31
VMEM_LIMIT = 56 * 1024 * 1024

IN_SPLITS = (256, 128, 128, 256, 256, 256, 4, 256, 256, 256, 256, 64, 64, 256, 64, 4, 4096)

_SEG = dict(aq=(0, 256), ak=(256, 128), av=(384, 128), fq=(512, 256), fk=(768, 256), fv=(1024, 256),
            sq=(1280, 256), sk=(1536, 256), sv=(1792, 256), dq=(2048, 256), dkk=(2304, 128),
            dvv=(2432, 128), iq=(2560, 256), ikk=(2816, 128))
_W1_COLS = 2944
_SEG_ORDER = ("aq", "ak", "av", "fq", "fk", "fv", "sq", "sk", "sv", "dq", "dkk", "dvv", "iq", "ikk")


def _cparams(*sem):
    return pltpu.CompilerParams(dimension_semantics=sem, vmem_limit_bytes=VMEM_LIMIT)


def _rms(x, g):
    return x * lax.rsqrt(jnp.mean(x * x, axis=-1, keepdims=True) + NORM_EPS) * g


def _log_sigmoid(z):
    return jnp.minimum(z, 0.0) - jnp.log(1.0 + jnp.exp(-jnp.abs(z)))


def _dot_nt(a, b):
    return lax.dot_general(a, b, (((1,), (1,)), ((), ())), preferred_element_type=F32)


def _split_heads(qp):
    lo = lax.broadcasted_iota(I32, (1, LANES), 1) < HEAD_DIM
    zero = jnp.zeros_like(qp)
    return jnp.concatenate([jnp.where(lo, qp, zero), jnp.where(lo, zero, qp)], axis=0)


def _merge_heads(o, t):
    lo = lax.broadcasted_iota(I32, (1, LANES), 1) < HEAD_DIM
    return jnp.where(lo, o[:t], o[t:])


def _proj_kernel(x_ref, g_ref, w1_ref, wm_ref, gseg_ref, gains_ref, fb_ref, ltri_ref, *rest):
    outs = dict(zip(_SEG_ORDER, rest[:len(_SEG_ORDER)]))
    cm_ref = rest[len(_SEG_ORDER)]
    carry_ref = rest[len(_SEG_ORDER) + 1]

    @pl.when(pl.program_id(1) == 0)
    def _():
        carry_ref[...] = jnp.zeros_like(carry_ref)

    h = _rms(x_ref[0], g_ref[...])
    hb = h.astype(BF16)

    def seg(name):
        off, width = _SEG[name]
        return jnp.dot(hb, w1_ref[:, off:off + width], preferred_element_type=F32)

    def head_norm(t, row):
        width = t.shape[1]
        ssq = jnp.dot((t * t).astype(BF16), gseg_ref[:width, :width], preferred_element_type=F32)
        return t * lax.rsqrt(ssq * (1.0 / HEAD_DIM) + NORM_EPS) * gains_ref[row:row + 1, :width]

    normed = dict(aq=0, ak=1, fq=2, fk=3, dq=4, dkk=5)
    scaled = dict(sq=ATTN_SCALE, iq=IDX_SCALE)
    for name in _SEG_ORDER:
        t = seg(name)
        if name in normed:
            t = head_norm(t, normed[name])
        elif name in scaled:
            t = t * scaled[name]
        outs[name][0] = t.astype(BF16)

    misc = jnp.dot(h, wm_ref[...], precision=lax.Precision.HIGHEST, preferred_element_type=F32)
    lane = lax.broadcasted_iota(I32, misc.shape, 1)
    logf = jnp.where(lane < 4, _log_sigmoid(misc + fb_ref[...]), 0.0)
    c = jnp.dot(ltri_ref[...], logf, precision=lax.Precision.HIGHEST,
                preferred_element_type=F32) + carry_ref[0:1, :]
    tm = misc.shape[0]
    carry_ref[0:1, :] = c[tm - 1:tm, :]
    cm_ref[0] = jnp.where(lane < 4, c, misc)


def _proj(x, g, w1, wm, gseg, gains, fb, tm):
    b, s, d = x.shape
    ltri = jnp.tril(jnp.ones((tm, tm), F32))
    full = lambda shape: pl.BlockSpec(shape, lambda bi, si: (0,) * len(shape))
    out_shapes = [jax.ShapeDtypeStruct((b, s, _SEG[n][1]), BF16) for n in _SEG_ORDER]
    out_shapes.append(jax.ShapeDtypeStruct((b, s, LANES), F32))
    out_specs = [pl.BlockSpec((1, tm, _SEG[n][1]), lambda bi, si: (bi, si, 0)) for n in _SEG_ORDER]
    out_specs.append(pl.BlockSpec((1, tm, LANES), lambda bi, si: (bi, si, 0)))
    return pl.pallas_call(
        _proj_kernel,
        grid=(b, s // tm),
        in_specs=[pl.BlockSpec((1, tm, d), lambda bi, si: (bi, si, 0)),
                  full((1, d)), full(w1.shape), full(wm.shape), full(gseg.shape),
                  full(gains.shape), full(fb.shape), full((tm, tm))],
        out_specs=out_specs,
        out_shape=out_shapes,
        scratch_shapes=[pltpu.VMEM((8, LANES), F32)],
        compiler_params=_cparams("arbitrary", "arbitrary"),
        name="proj",
    )(x, g, w1, wm, gseg, gains, fb, ltri)


def _swa_kernel(q_ref, kp_ref, kc_ref, vp_ref, vc_ref, bias_ref, sink_ref, o_ref):
    i = pl.program_id(1)
    t = SWA_BLOCK
    q = q_ref[0]
    kcat = jnp.concatenate([kp_ref[0], kc_ref[0]], axis=0)
    vcat = jnp.concatenate([vp_ref[0], vc_ref[0]], axis=0)
    col = lax.broadcasted_iota(I32, (t, 2 * t), 1)
    no_prev = (col < t) & (i == 0)
    pairs = []
    for pair in range(2):
        qs = _split_heads(q[:, pair * LANES:(pair + 1) * LANES])
        s = _dot_nt(qs, kcat)
        ps = []
        for hh in range(2):
            head = pair + 2 * hh
            sh = s[hh * t:(hh + 1) * t] + bias_ref[head]
            sh = jnp.where(no_prev, NEG_INF, sh)
            sink = sink_ref[head:head + 1, 0:1]
            m = jnp.maximum(jnp.max(sh, axis=1, keepdims=True), sink)
            p = jnp.exp(sh - m)
            denom = jnp.sum(p, axis=1, keepdims=True) + jnp.exp(sink - m)
            ps.append(p / denom)
        o = jnp.dot(jnp.concatenate(ps, axis=0).astype(BF16), vcat, preferred_element_type=F32)
        pairs.append(_merge_heads(o, t))
    o_ref[0] = jnp.concatenate(pairs, axis=1).astype(BF16)


def _swa(aq, ak, av, bias, sinks):
    b, s, _ = aq.shape
    t = SWA_BLOCK
    cur = lambda bi, i: (bi, i, 0)
    prev = lambda bi, i: (bi, jnp.maximum(i - 1, 0), 0)
    return pl.pallas_call(
        _swa_kernel,
        grid=(b, s // t),
        in_specs=[pl.BlockSpec((1, t, 256), cur),
                  pl.BlockSpec((1, t, LANES), prev), pl.BlockSpec((1, t, LANES), cur),
                  pl.BlockSpec((1, t, LANES), prev), pl.BlockSpec((1, t, LANES), cur),
                  pl.BlockSpec(bias.shape, lambda bi, i: (0, 0, 0)),
                  pl.BlockSpec(sinks.shape, lambda bi, i: (0, 0))],
        out_specs=pl.BlockSpec((1, t, 256), cur),
        out_shape=jax.ShapeDtypeStruct((b, s, 256), BF16),
        compiler_params=_cparams("arbitrary", "arbitrary"),
        name="swa",
    )(aq, ak, ak, av, av, bias, sinks)


def _fox_kernel(q_ref, k_ref, vt_ref, ct_ref, ccol_ref, o_ref, ckb_ref, *, t):
    i = pl.program_id(1)
    n_tiles = ckb_ref.shape[1] // t

    @pl.when(i == 0)
    def _():
        def fill(j, c):
            rows = pl.ds(pl.multiple_of(j * t, t), t)
            cc = ccol_ref[0, rows, :]
            for h in range(4):
                ckb_ref[h, rows, :] = jnp.broadcast_to(cc[:, h:h + 1], (t, LANES))
            return c
        lax.fori_loop(0, n_tiles, fill, 0)

    q = q_ref[0]
    ct = ct_ref[0]
    causal = lax.broadcasted_iota(I32, (t, t), 0) <= lax.broadcasted_iota(I32, (t, t), 1)
    outs = []
    for pair in range(2):
        sl = slice(pair * LANES, (pair + 1) * LANES)
        qs = _split_heads(q[:, sl])
        heads = (2 * pair, 2 * pair + 1)

        def tile(j, carry, masked):
            m, l, acc = carry
            rows = pl.ds(pl.multiple_of(j * t, t), t)
            st = _dot_nt(k_ref[0, rows, sl], qs)
            cols = []
            for hh, head in enumerate(heads):
                ck = ckb_ref[head, rows, :]
                for c in range(t // LANES):
                    cs = slice(c * LANES, (c + 1) * LANES)
                    sh = st[:, hh * t + c * LANES:hh * t + (c + 1) * LANES] + (ct[head:head + 1, cs] - ck)
                    if masked:
                        sh = jnp.where(causal[:, cs], sh, NEG_INF)
                    cols.append(sh)
            st = jnp.concatenate(cols, axis=1)
            m_new = jnp.maximum(m, jnp.max(st, axis=0, keepdims=True))
            alpha = jnp.exp(m - m_new)
            p = jnp.exp(st - m_new)
            l = alpha * l + jnp.sum(p, axis=0, keepdims=True)
            acc = alpha * acc + jnp.dot(vt_ref[0, j, sl, :], p.astype(BF16), preferred_element_type=F32)
            return m_new, l, acc

        init = (jnp.full((1, 2 * t), M_INIT, F32), jnp.zeros((1, 2 * t), F32), jnp.zeros((LANES, 2 * t), F32))
        carry = lax.fori_loop(0, i, lambda j, c: tile(j, c, False), init)
        _, l, acc = tile(i, carry, True)
        o = acc / l
        outs.append(jnp.concatenate([o[:HEAD_DIM, :t], o[HEAD_DIM:, t:]], axis=0))
    o_ref[0] = jnp.concatenate(outs, axis=0).T.astype(BF16)


def _fox(fq, fk, fv, ct, ccol, t):
    b, s, _ = fq.shape
    nt = s // t
    vt = jnp.transpose(fv.reshape(b, nt, t, 256), (0, 1, 3, 2))
    return pl.pallas_call(
        functools.partial(_fox_kernel, t=t),
        grid=(b, nt),
        in_specs=[pl.BlockSpec((1, t, 256), lambda bi, i: (bi, i, 0)),
                  pl.BlockSpec((1, s, 256), lambda bi, i: (bi, 0, 0)),
                  pl.BlockSpec((1, nt, 256, t), lambda bi, i: (bi, 0, 0, 0)),
                  pl.BlockSpec((1, 8, t), lambda bi, i: (bi, 0, i)),
                  pl.BlockSpec((1, s, LANES), lambda bi, i: (bi, 0, 0))],
        out_specs=pl.BlockSpec((1, t, 256), lambda bi, i: (bi, i, 0)),
        out_shape=jax.ShapeDtypeStruct((b, s, 256), BF16),
        scratch_shapes=[pltpu.VMEM((4, s, LANES), F32)],
        compiler_params=_cparams("arbitrary", "arbitrary"),
        name="fox",
    )(fq, fk, vt, ct, ccol)


def _sb_kernel(q_ref, k_ref, v_ref, u_ref, o_ref, *, tq, tk):
    i = pl.program_id(1)
    q = q_ref[0]
    u = u_ref[...]
    nfull = (i * tq) // tk
    row = lax.broadcasted_iota(I32, (2 * tq, tk), 0)
    col = lax.broadcasted_iota(I32, (2 * tq, tk), 1)
    qrow = jnp.where(row >= tq, row - tq, row)
    pairs = []
    for pair in range(2):
        sl = slice(pair * LANES, (pair + 1) * LANES)
        qs = _split_heads(q[:, sl])

        def tile(j, r, acc, masked):
            start = pl.multiple_of(j * tk, tk)
            kj = k_ref[0, pl.ds(start, tk), sl]
            vj = v_ref[0, pl.ds(start, tk), sl]
            z = _dot_nt(qs, kj)
            sp = jnp.log(1.0 + jnp.exp(-jnp.abs(z)))
            log_beta = jnp.minimum(z, 0.0) - sp
            log_keep = jnp.minimum(-z, 0.0) - sp
            if masked:
                strict = j * tk + col < i * tq + qrow
                log_keep = jnp.where(strict, log_keep, 0.0)
            hi = log_keep.astype(BF16)
            lo = (log_keep - hi.astype(F32)).astype(BF16)
            later = (jnp.dot(hi, u, preferred_element_type=F32)
                     + jnp.dot(lo, u, preferred_element_type=F32))
            a = jnp.exp(log_beta + later + r)
            if masked:
                a = jnp.where(strict, a, 0.0)
            acc = acc + jnp.dot(a.astype(BF16), vj, preferred_element_type=F32)
            r = r + jnp.sum(log_keep, axis=1, keepdims=True)
            return r, acc

        r, acc = tile(nfull, jnp.zeros((2 * tq, 1), F32), jnp.zeros((2 * tq, LANES), F32), True)

        def cond(c):
            return (c[0] >= 0) & (c[1] > 0)

        def body(c):
            j, _, r, acc = c
            r, acc = tile(j, r, acc, False)
            return j - 1, (jnp.max(r) > SB_DEAD).astype(I32), r, acc

        _, _, _, acc = lax.while_loop(cond, body, (nfull - 1, (jnp.max(r) > SB_DEAD).astype(I32), r, acc))
        pairs.append(_merge_heads(acc, tq))
    o_ref[0] = jnp.concatenate(pairs, axis=1).astype(BF16)


def _sb(sq, sk, sv, tq, tk):
    b, s, _ = sq.shape
    u = (jnp.arange(tk)[:, None] > jnp.arange(tk)[None, :]).astype(BF16)
    return pl.pallas_call(
        functools.partial(_sb_kernel, tq=tq, tk=tk),
        grid=(b, s // tq),
        in_specs=[pl.BlockSpec((1, tq, 256), lambda bi, i: (bi, i, 0)),
                  pl.BlockSpec((1, s, 256), lambda bi, i: (bi, 0, 0)),
                  pl.BlockSpec((1, s, 256), lambda bi, i: (bi, 0, 0)),
                  pl.BlockSpec((tk, tk), lambda bi, i: (0, 0))],
        out_specs=pl.BlockSpec((1, tq, 256), lambda bi, i: (bi, i, 0)),
        out_shape=jax.ShapeDtypeStruct((b, s, 256), BF16),
        compiler_params=_cparams("arbitrary", "arbitrary"),
        name="stickbreak",
    )(sq, sk, sv, u)


def _dsa_kernel(q_ref, kk_ref, vt_ref, iq_ref, ikk_ref, wt_ref, bias_ref, lstrict_ref, o_ref,
                key_ref, *, t, top_k):
    i = pl.program_id(1)
    group = 64
    causal = lax.broadcasted_iota(I32, (t, t), 0) <= lax.broadcasted_iota(I32, (t, t), 1)

    def head_stack(x):
        return jnp.concatenate([_split_heads(x[:, :LANES]), _split_heads(x[:, LANES:])], axis=0)

    def tile_rows(ref, j):
        return ref[0, pl.ds(pl.multiple_of(j * t, t), t), :]

    iqs = head_stack(iq_ref[0])
    wt = wt_ref[0]
    w = [wt[4 + h:5 + h, :] * (IDX_HEADS ** -0.5) for h in range(IDX_HEADS)]

    def score_tile(j, masked):
        lg = _dot_nt(tile_rows(ikk_ref, j), iqs)
        sc = w[0] * jnp.maximum(lg[:, 0:t], 0.0)
        for h in range(1, IDX_HEADS):
            sc = sc + w[h] * jnp.maximum(lg[:, h * t:(h + 1) * t], 0.0)
        bits = pltpu.bitcast(sc, I32)
        key = bits ^ ((bits >> 31) & 0x7FFFFFFF)
        key = jnp.where(key == -1, 0, key)
        if masked:
            key = jnp.where(causal, key, INT_MIN)
        key_ref[j] = key

    def p1(j, c):
        score_tile(j, False)
        return c

    lax.fori_loop(0, i, p1, 0)
    score_tile(i, True)

    def count(pred):
        def cb(j, acc):
            for g in range(t // group):
                acc = acc + jnp.where(pred(key_ref[j, g * group:(g + 1) * group, :]), 1.0, 0.0)
            return acc
        acc = lax.fori_loop(0, i + 1, cb, jnp.zeros((group, t), F32))
        return jnp.sum(acc, axis=0, keepdims=True)

    n_avail = (i * t + lax.broadcasted_iota(I32, (1, t), 1) + 1).astype(F32)
    done0 = jnp.where(n_avail <= top_k, 1.0, 0.0)

    def bis_cond(c):
        return (c[0] < 32) & (c[3] < 0.5)

    def bis_body(c):
        b, thr_u, done, _ = c
        cand_u = thr_u | (jnp.int32(1) << (31 - b))
        cand_s = cand_u ^ INT_MIN
        n = count(lambda k: k >= cand_s)
        take = (n >= top_k) & (done < 0.5)
        thr_u = jnp.where(take, cand_u, thr_u)
        done = jnp.where(take & (n == top_k), 1.0, done)
        return b + 1, thr_u, done, jnp.min(done)

    _, thr_u, _, _ = lax.while_loop(bis_cond, bis_body,
                                    (jnp.int32(0), jnp.zeros((1, t), I32), done0, jnp.min(done0)))
    thr = jnp.maximum(thr_u ^ INT_MIN, INT_MIN + 1)
    surplus = jnp.max(count(lambda k: k >= thr)) > top_k

    def tie_pass():
        need = top_k - count(lambda k: k > thr)

        def tb(j, seen):
            k = key_ref[j]
            eq = k == thr
            eqf = jnp.where(eq, 1.0, 0.0)
            before = jnp.dot(lstrict_ref[...], eqf.astype(BF16), preferred_element_type=F32) + seen
            sel = (k > thr) | (eq & (before < need))
            key_ref[j] = jnp.where(sel, 1, INT_MIN)
            return seen + jnp.sum(eqf, axis=0, keepdims=True)

        lax.fori_loop(0, i + 1, tb, jnp.zeros((1, t), F32))
        return jnp.zeros((1, t), I32)

    thr = lax.cond(surplus, tie_pass, lambda: thr)

    qs = head_stack(q_ref[0])

    def attn_tile(j, carry, bias_kind):
        m, l, acc = carry
        st = _dot_nt(tile_rows(kk_ref, j), qs)
        selb = jnp.where(key_ref[j] >= thr, 0.0, NEG_INF)
        parts = []
        for h in range(4):
            sh = st[:, h * t:(h + 1) * t] + selb
            if bias_kind is not None:
                sh = sh + bias_ref[bias_kind, h]
            parts.append(sh)
        st = jnp.concatenate(parts, axis=1)
        m_new = jnp.maximum(m, jnp.max(st, axis=0, keepdims=True))
        alpha = jnp.exp(m - m_new)
        p = jnp.exp(st - m_new)
        l = alpha * l + jnp.sum(p, axis=0, keepdims=True)
        acc = alpha * acc + jnp.dot(vt_ref[0, j], p.astype(BF16), preferred_element_type=F32)
        return m_new, l, acc

    init = (jnp.full((1, 4 * t), M_INIT, F32), jnp.zeros((1, 4 * t), F32), jnp.zeros((HEAD_DIM, 4 * t), F32))
    carry = lax.fori_loop(0, jnp.maximum(i - 1, 0), lambda j, c: attn_tile(j, c, None), init)
    carry = lax.cond(i > 0, lambda c: attn_tile(i - 1, c, 0), lambda c: c, carry)
    _, l, acc = attn_tile(i, carry, 1)
    ot = acc / l
    ot = jnp.concatenate([ot[:, h * t:(h + 1) * t] for h in range(4)], axis=0)
    o_ref[0] = ot.T.astype(BF16)


def _dsa_bias(t5_table, t):
    assert t + 1 >= T5_MAX_DISTANCE
    k = jnp.arange(t)[:, None]
    q = jnp.arange(t)[None, :]
    far = t5_table[T5_BUCKETS - 1, 4:].astype(F32)
    tiles = []
    for off in (t, 0):
        dist = off + q - k
        b = jnp.transpose(t5_table[_t5_bucket(dist)][..., 4:].astype(F32), (2, 0, 1)) - far[:, None, None]
        tiles.append(jnp.where((dist >= 0)[None], b, 0.0))
    return jnp.stack(tiles)


def _dsa(dq, dkk, dvv, iq, ikk, wt, bias, top_k, t):
    b, s, _ = dq.shape
    nt = s // t
    lstrict = (jnp.arange(t)[:, None] > jnp.arange(t)[None, :]).astype(BF16)
    vt = jnp.transpose(dvv[:, :, :HEAD_DIM].reshape(b, nt, t, HEAD_DIM), (0, 1, 3, 2))
    blk = lambda w: pl.BlockSpec((1, t, w), lambda bi, i: (bi, i, 0))
    seq = lambda w: pl.BlockSpec((1, s, w), lambda bi, i: (bi, 0, 0))
    return pl.pallas_call(
        functools.partial(_dsa_kernel, t=t, top_k=top_k),
        grid=(b, nt),
        in_specs=[blk(256), seq(LANES),
                  pl.BlockSpec((1, nt, HEAD_DIM, t), lambda bi, i: (bi, 0, 0, 0)),
                  blk(256), seq(LANES),
                  pl.BlockSpec((1, 8, t), lambda bi, i: (bi, 0, i)),
                  pl.BlockSpec(bias.shape, lambda bi, i: (0, 0, 0, 0)),
                  pl.BlockSpec((t, t), lambda bi, i: (0, 0))],
        out_specs=blk(256),
        out_shape=jax.ShapeDtypeStruct((b, s, 256), BF16),
        scratch_shapes=[pltpu.VMEM((nt, t, t), I32)],
        compiler_params=_cparams("arbitrary", "arbitrary"),
        name="dsa",
    )(dq, dkk, vt, iq, ikk, wt, bias, lstrict)


def _merge_kernel(x_ref, gm_ref, wg_ref, oa_ref, of_ref, os_ref, od_ref, wb_ref, wo_ref, gf_ref,
                  wr_ref, br_ref, xo_ref, h2_ref, route_ref):
    x = x_ref[...]
    hb = _rms(x, gm_ref[...]).astype(BF16)
    d = x.shape[1]
    merged = None
    for bi, o_ref in enumerate((oa_ref, of_ref, os_ref, od_ref)):
        gate = jax.nn.sigmoid(jnp.dot(hb, wg_ref[:, bi * d:(bi + 1) * d], preferred_element_type=F32))
        term = gate * jnp.dot(o_ref[...], wb_ref[bi], preferred_element_type=F32)
        merged = term if merged is None else merged + term
    xn = x + jnp.dot(merged.astype(BF16), wo_ref[...], preferred_element_type=F32)
    xo_ref[...] = xn
    h2 = _rms(xn, gf_ref[...])
    h2_ref[...] = h2

    logits = jnp.dot(h2, wr_ref[...], precision=lax.Precision.HIGHEST,
                     preferred_element_type=F32) + br_ref[...]
    lane = lax.broadcasted_iota(I32, logits.shape, 1).astype(F32)
    big = 1e9
    gl = jnp.where(lane < N_GROUPS, logits, -jnp.inf)
    gmax = jnp.max(gl, axis=1, keepdims=True)
    grp = jnp.min(jnp.where(gl == gmax, lane, big), axis=1, keepdims=True)
    p_grp = 1.0 / jnp.sum(jnp.exp(gl - gmax), axis=1, keepdims=True)
    first = N_GROUPS + grp * EXPERTS_PER_GROUP
    el = jnp.where((lane >= first) & (lane < first + EXPERTS_PER_GROUP), logits, -jnp.inf)
    l1 = jnp.max(el, axis=1, keepdims=True)
    i1 = jnp.min(jnp.where(el == l1, lane, big), axis=1, keepdims=True)
    el2 = jnp.where(lane == i1, -jnp.inf, el)
    l2 = jnp.max(el2, axis=1, keepdims=True)
    i2 = jnp.min(jnp.where(el2 == l2, lane, big), axis=1, keepdims=True)
    e2 = jnp.exp(l2 - l1)
    g1 = p_grp / (1.0 + e2)
    g2 = p_grp * e2 / (1.0 + e2)
    route = jnp.where(lane == 0, i1 - N_GROUPS,
                      jnp.where(lane == 1, i2 - N_GROUPS,
                                jnp.where(lane == 2, g1, jnp.where(lane == 3, g2, 0.0))))
    route_ref[...] = route


def _merge(x2, gm, wg, o_a, o_f, o_s, o_d, wb, wo, gf, wr, br, tm):
    n, d = x2.shape
    row = lambda w: pl.BlockSpec((tm, w), lambda i: (i, 0))
    full = lambda a: pl.BlockSpec(a.shape, lambda i: (0,) * a.ndim)
    return pl.pallas_call(
        _merge_kernel,
        grid=(n // tm,),
        in_specs=[row(d), full(gm), full(wg), row(256), row(256), row(256), row(256),
                  full(wb), full(wo), full(gf), full(wr), full(br)],
        out_specs=[row(d), row(d), row(LANES)],
        out_shape=[jax.ShapeDtypeStruct((n, d), F32), jax.ShapeDtypeStruct((n, d), F32),
                   jax.ShapeDtypeStruct((n, LANES), F32)],
        compiler_params=_cparams("arbitrary"),
        name="merge",
    )(x2, gm, wg, o_a, o_f, o_s, o_d, wb, wo, gf, wr, br)


def _rank_kernel(route_ref, ltri_ref, rank_ref, cnt_ref, carry_ref):
    @pl.when(pl.program_id(0) == 0)
    def _():
        carry_ref[...] = jnp.zeros_like(carry_ref)

    route = route_ref[...]
    lane = lax.broadcasted_iota(I32, route.shape, 1)
    e0 = route[:, 0:1].astype(I32)
    e1 = route[:, 1:2].astype(I32)
    oh0 = (lane == e0).astype(F32)
    oh1 = (lane == e1).astype(F32)
    both = oh0 + oh1
    before = jnp.dot(ltri_ref[...], both.astype(BF16), preferred_element_type=F32) + carry_ref[0:1, :]
    r0 = jnp.sum(oh0 * before, axis=1, keepdims=True)
    r1 = jnp.sum(oh1 * (before + oh0), axis=1, keepdims=True)
    rank_ref[...] = jnp.where(lane == 0, r0, jnp.where(lane == 1, r1, 0.0))
    total = carry_ref[0:1, :] + jnp.sum(both, axis=0, keepdims=True)
    carry_ref[0:1, :] = total
    cnt_ref[...] = jnp.broadcast_to(total, cnt_ref.shape)


def _rank(route, tm):
    n = route.shape[0]
    ltri = (jnp.arange(tm)[:, None] > jnp.arange(tm)[None, :]).astype(BF16)
    return pl.pallas_call(
        _rank_kernel,
        grid=(n // tm,),
        in_specs=[pl.BlockSpec((tm, LANES), lambda i: (i, 0)), pl.BlockSpec((tm, tm), lambda i: (0, 0))],
        out_specs=[pl.BlockSpec((tm, LANES), lambda i: (i, 0)), pl.BlockSpec((8, LANES), lambda i: (0, 0))],
        out_shape=[jax.ShapeDtypeStruct((n, LANES), F32), jax.ShapeDtypeStruct((8, LANES), F32)],
        scratch_shapes=[pltpu.VMEM((8, LANES), F32)],
        compiler_params=_cparams("arbitrary"),
        name="moe_rank",
    )(route, ltri)


def _expert_kernel(be_ref, nu_ref, tok_ref, tok_next_ref, h_hbm, wup_ref, wdn_ref, y_ref, xbuf, sem, *, te):
    b = pl.program_id(0)
    n_used = nu_ref[0]
    slot = b % 2

    def gather(tokens_ref, dst):
        def issue(r, c):
            pltpu.make_async_copy(h_hbm.at[pl.ds(tokens_ref[0, 0, r], 1), :],
                                  xbuf.at[dst, pl.ds(r, 1), :], sem.at[dst]).start()
            return c
        lax.fori_loop(0, te, issue, 0, unroll=8)

    @pl.when((b == 0) & (n_used > 0))
    def _():
        gather(tok_ref, 0)

    @pl.when(b + 1 < n_used)
    def _():
        gather(tok_next_ref, 1 - slot)

    @pl.when(b < n_used)
    def _():
        pltpu.make_async_copy(h_hbm.at[pl.ds(0, te), :], xbuf.at[slot], sem.at[slot]).wait()
        xb = xbuf[slot].astype(BF16)
        gu = jnp.dot(xb, wup_ref[0], preferred_element_type=F32)
        g = gu[:, :EXPERT_FF]
        act = g * jax.nn.sigmoid(g) * gu[:, EXPERT_FF:]
        y_ref[...] = jnp.dot(act.astype(BF16), wdn_ref[0], preferred_element_type=F32)

    @pl.when(b >= nu_ref[0])
    def _():
        y_ref[...] = jnp.zeros_like(y_ref)


def _experts(blk_expert, n_used, slot_tok, h2, w_up, w_down, te):
    n_blocks = blk_expert.shape[0]
    d = h2.shape[1]
    grid_spec = pltpu.PrefetchScalarGridSpec(
        num_scalar_prefetch=2,
        grid=(n_blocks,),
        in_specs=[pl.BlockSpec((1, 1, te), lambda b, be, nu: (b, 0, 0), memory_space=pltpu.SMEM),
                  pl.BlockSpec((1, 1, te), lambda b, be, nu: (jnp.minimum(b + 1, n_blocks - 1), 0, 0),
                               memory_space=pltpu.SMEM),
                  pl.BlockSpec(memory_space=pl.ANY),
                  pl.BlockSpec((1, d, 2 * EXPERT_FF), lambda b, be, nu: (be[b], 0, 0)),
                  pl.BlockSpec((1, EXPERT_FF, d), lambda b, be, nu: (be[b], 0, 0))],
        out_specs=pl.BlockSpec((te, d), lambda b, be, nu: (b, 0)),
        scratch_shapes=[pltpu.VMEM((2, te, d), F32), pltpu.SemaphoreType.DMA((2,))],
    )
    slots = slot_tok.reshape(n_blocks, 1, te)
    return pl.pallas_call(
        functools.partial(_expert_kernel, te=te),
        grid_spec=grid_spec,
        out_shape=jax.ShapeDtypeStruct((n_blocks * te, d), F32),
        compiler_params=_cparams("arbitrary"),
        name="moe_experts",
    )(blk_expert, n_used, slots, slots, h2, w_up, w_down)


def _combine_kernel(pos_ref, pos_next_ref, x_ref, route_ref, y_hbm, o_ref, ybuf, sem, *, tc):
    i = pl.program_id(0)
    slot = i % 2

    def gather(rows_ref, dst):
        def issue(r, c):
            pltpu.make_async_copy(y_hbm.at[pl.ds(rows_ref[0, 0, r], 1), :],
                                  ybuf.at[dst, pl.ds(r, 1), :], sem.at[dst]).start()
            return c
        lax.fori_loop(0, 2 * tc, issue, 0, unroll=8)

    @pl.when(i == 0)
    def _():
        gather(pos_ref, 0)

    @pl.when(i + 1 < pl.num_programs(0))
    def _():
        gather(pos_next_ref, 1 - slot)

    pltpu.make_async_copy(y_hbm.at[pl.ds(0, 2 * tc), :], ybuf.at[slot], sem.at[slot]).wait()
    route = route_ref[...]
    o_ref[...] = (x_ref[...] + route[:, 2:3] * ybuf[slot, 0:tc, :]
                  + route[:, 3:4] * ybuf[slot, tc:2 * tc, :])


def _combine(pos, x2, route, yb, tc):
    n, d = x2.shape
    nt = n // tc
    pos_t = jnp.transpose(pos.reshape(nt, tc, 2), (0, 2, 1)).reshape(nt, 1, 2 * tc)
    return pl.pallas_call(
        functools.partial(_combine_kernel, tc=tc),
        grid=(nt,),
        in_specs=[pl.BlockSpec((1, 1, 2 * tc), lambda i: (i, 0, 0), memory_space=pltpu.SMEM),
                  pl.BlockSpec((1, 1, 2 * tc), lambda i: (jnp.minimum(i + 1, nt - 1), 0, 0),
                               memory_space=pltpu.SMEM),
                  pl.BlockSpec((tc, d), lambda i: (i, 0)),
                  pl.BlockSpec((tc, LANES), lambda i: (i, 0)),
                  pl.BlockSpec(memory_space=pl.ANY)],
        out_specs=pl.BlockSpec((tc, d), lambda i: (i, 0)),
        out_shape=jax.ShapeDtypeStruct((n, d), F32),
        scratch_shapes=[pltpu.VMEM((2, 2 * tc, d), F32), pltpu.SemaphoreType.DMA((2,))],
        compiler_params=_cparams("arbitrary"),
        name="moe_combine",
    )(pos_t, pos_t, x2, route, yb)


def _t5_bucket(dist):
    n = jnp.maximum(dist, 0)
    max_exact = T5_BUCKETS // 2
    nf = jnp.maximum(n, 1).astype(F32)
    large = max_exact + (jnp.log(nf / max_exact) / math.log(T5_MAX_DISTANCE / max_exact)
                         * (T5_BUCKETS - max_exact)).astype(I32)
    large = jnp.minimum(large, T5_BUCKETS - 1)
    return jnp.where(n < max_exact, n, large)


def _swa_bias(t5_table):
    t = SWA_BLOCK
    dist = t + jnp.arange(t)[:, None] - jnp.arange(2 * t)[None, :]
    tile = jnp.transpose(t5_table[_t5_bucket(dist)][..., :4].astype(F32), (2, 0, 1))
    valid = (dist >= 0) & (dist < t)
    return jnp.where(valid[None], tile, NEG_INF)


def _layer_weights(w_in, qk_gain, forget_bias, w_branch):
    offs = np.concatenate([[0], np.cumsum(IN_SPLITS)]).tolist()
    part = lambda k: w_in[:, offs[k]:offs[k + 1]]
    dup = lambda w: jnp.concatenate([w, w], axis=1)
    aq = part(0).reshape(-1, 4, HEAD_DIM)[:, (0, 2, 1, 3)].reshape(-1, 256)
    cols = [aq, part(1), part(2), part(3), part(4), part(5), part(7), part(8), part(9),
            part(10), dup(part(11)), dup(part(12)), part(13), dup(part(14))]
    w1 = jnp.concatenate(cols, axis=1).astype(BF16)
    d = w_in.shape[0]
    wm = jnp.concatenate([part(6), part(15), jnp.zeros((d, LANES - 8), F32)], axis=1)
    wg = part(16).astype(BF16)
    tile = lambda g, reps, scale: jnp.pad(jnp.tile(g, reps) * scale, (0, 256 - reps * HEAD_DIM))
    gains = jnp.stack([tile(qk_gain[0, 0], 4, ATTN_SCALE), tile(qk_gain[0, 1], 2, 1.0),
                       tile(qk_gain[1, 0], 4, ATTN_SCALE), tile(qk_gain[1, 1], 4, 1.0),
                       tile(qk_gain[2, 0], 4, ATTN_SCALE), tile(qk_gain[2, 1], 2, 1.0),
                       jnp.zeros((256,), F32), jnp.zeros((256,), F32)]).astype(F32)
    fb = jnp.pad(forget_bias.astype(F32), (0, LANES - 4)).reshape(1, LANES)
    wb0 = w_branch[0].reshape(4, HEAD_DIM, -1)[(0, 2, 1, 3), :, :].reshape(256, -1)
    wb = jnp.stack([wb0, w_branch[1], w_branch[2], w_branch[3]]).astype(BF16)
    return w1, wm, wg, gains, fb, wb


def kernel(x, norm_mix_g, w_in, forget_bias, attn_sinks, qk_gain, w_branch, w_out, t5_table, norm_ffn_g,
           w_router_group, b_router_group, w_router_expert, b_router_expert, w_expert_up, w_expert_down):
    b, s, d = x.shape
    n = b * s
    depth = w_in.shape[0]
    top_k = min(DSA_TOPK_MAX, s // 4)
    tm_proj = min(512, s)
    fox_t = min(256, s)
    sb_t = min(256, s)
    dsa_t = min(256, s)
    te = 256
    tc = 128

    gseg = (jnp.arange(256)[:, None] // HEAD_DIM == jnp.arange(256)[None, :] // HEAD_DIM).astype(BF16)
    bias_swa = _swa_bias(t5_table)
    bias_dsa = _dsa_bias(t5_table, dsa_t)
    n_blocks = -(-2 * n // te) + N_EXPERTS
    tok_ids = jnp.repeat(jnp.arange(n, dtype=I32), 2)

    for layer in range(depth):
        w1, wm, wg, gains, fb, wb = _layer_weights(w_in[layer], qk_gain[layer], forget_bias[layer],
                                                   w_branch[layer])
        sinks = jnp.broadcast_to(jnp.pad(attn_sinks[layer].astype(F32), (0, 4))[:, None], (8, LANES))
        (aq, ak, av, fq, fk, fv, sq, sk, sv, dq, dkk, dvv, iq, ikk, cm) = _proj(
            x, norm_mix_g[layer].reshape(1, d), w1, wm, gseg, gains, fb, tm_proj)

        o_swa = _swa(aq, ak, av, bias_swa, sinks)
        cmt = jnp.transpose(cm[:, :, :8], (0, 2, 1))
        o_fox = _fox(fq, fk, fv, cmt, cm, fox_t)
        o_sb = _sb(sq, sk, sv, sb_t, sb_t)
        o_dsa = _dsa(dq, dkk, dvv, iq, ikk, cmt, bias_dsa, top_k, dsa_t)

        wr = jnp.concatenate([w_router_group[layer], w_router_expert[layer],
                              jnp.zeros((d, LANES - N_GROUPS - N_EXPERTS), F32)], axis=1)
        br = jnp.concatenate([b_router_group[layer], b_router_expert[layer],
                              jnp.zeros((LANES - N_GROUPS - N_EXPERTS,), F32)]).reshape(1, LANES)
        x2, h2, route = _merge(
            x.reshape(n, d), norm_mix_g[layer].reshape(1, d), wg,
            o_swa.reshape(n, 256), o_fox.reshape(n, 256), o_sb.reshape(n, 256), o_dsa.reshape(n, 256),
            wb, w_out[layer].astype(BF16), norm_ffn_g[layer].reshape(1, d), wr, br, min(256, n))

        rank, cnt = _rank(route, min(512, n))
        counts = cnt[0, :N_EXPERTS].astype(I32)
        padded = (counts + te - 1) // te * te
        pend = jnp.cumsum(padded)
        pstart = pend - padded
        expert = route[:, :2].astype(I32)
        own = expert[:, :, None] == jnp.arange(N_EXPERTS, dtype=I32)
        pos = jnp.sum(jnp.where(own, pstart, 0), axis=-1) + rank[:, :2].astype(I32)
        slot_tok = jnp.zeros((n_blocks * te,), I32).at[pos.reshape(-1)].set(tok_ids)
        blk_start = jnp.arange(n_blocks, dtype=I32)[:, None] * te
        blk_expert = jnp.minimum(jnp.sum((pend[None, :] <= blk_start).astype(I32), axis=1), N_EXPERTS - 1)
        n_used = (pend[-1:] // te).astype(I32)

        yb = _experts(blk_expert, n_used, slot_tok, h2, w_expert_up[layer].astype(BF16),
                      w_expert_down[layer].astype(BF16), te)
        x = _combine(pos, x2, route, yb, tc).reshape(b, s, d)
    return x
```

```python
import functools
import math

import jax
import jax.numpy as jnp
import numpy as np
from jax import lax
from jax.experimental import pallas as pl
from jax.experimental.pallas import tpu as pltpu

F32 = jnp.float32
BF16 = jnp.bfloat16
I32 = jnp.int32

HEAD_DIM = 64
LANES = 128
NORM_EPS = 1e-6
NEG_INF = -1e30
M_INIT = -1e29
ATTN_SCALE = HEAD_DIM ** -0.5
SWA_BLOCK = 128
IDX_SCALE = 64 ** -0.5
IDX_HEADS = 4
DSA_TOPK_MAX = 256
T5_BUCKETS = 32
T5_MAX_DISTANCE = 128
N_GROUPS = 4
EXPERTS_PER_GROUP = 8
N_EXPERTS = N_GROUPS * EXPERTS_PER_GROUP
EXPERT_FF = 512
SB_DEAD = -110.0
INT_MIN = -2 ** 31
VMEM_LIMIT = 56 * 1024 * 1024

IN_SPLITS = (256, 128, 128, 256, 256, 256, 4, 256, 256, 256, 256, 64, 64, 256, 64, 4, 4096)

_SEG = dict(aq=(0, 256), ak=(256, 128), av=(384, 128), fq=(512, 256), fk=(768, 256), fv=(1024, 256),
            sq=(1280, 256), sk=(1536, 256), sv=(1792, 256), dq=(2048, 256), dkk=(2304, 128),
            dvv=(2432, 128), iq=(2560, 256), ikk=(2816, 128))
_W1_COLS = 2944
_SEG_ORDER = ("aq", "ak", "av", "fq", "fk", "fv", "sq", "sk", "sv", "dq", "dkk", "dvv", "iq", "ikk")


def _cparams(*sem):
    return pltpu.CompilerParams(dimension_semantics=sem, vmem_limit_bytes=VMEM_LIMIT)


def _rms(x, g):
    return x * lax.rsqrt(jnp.mean(x * x, axis=-1, keepdims=True) + NORM_EPS) * g


def _log_sigmoid(z):
    return jnp.minimum(z, 0.0) - jnp.log(1.0 + jnp.exp(-jnp.abs(z)))


def _dot_nt(a, b):
    return lax.dot_general(a, b, (((1,), (1,)), ((), ())), preferred_element_type=F32)


def _split3(x):
    p1 = x.astype(BF16)
    r = x - p1.astype(F32)
    p2 = r.astype(BF16)
    return p1, p2, (r - p2.astype(F32)).astype(BF16)


def _hi_lo(w):
    hi = w.astype(BF16)
    return jnp.stack([hi, (w - hi.astype(F32)).astype(BF16)])


def _dot_x3(a, b_hi, b_lo):
    a_hi = a.astype(BF16)
    a_lo = (a - a_hi.astype(F32)).astype(BF16)
    return (jnp.dot(a_hi, b_hi, preferred_element_type=F32) + jnp.dot(a_lo, b_hi, preferred_element_type=F32)
            + jnp.dot(a_hi, b_lo, preferred_element_type=F32))


def _split_heads(qp):
    lo = lax.broadcasted_iota(I32, (1, LANES), 1) < HEAD_DIM
    zero = jnp.zeros_like(qp)
    return jnp.concatenate([jnp.where(lo, qp, zero), jnp.where(lo, zero, qp)], axis=0)


def _merge_heads(o, t):
    lo = lax.broadcasted_iota(I32, (1, LANES), 1) < HEAD_DIM
    return jnp.where(lo, o[:t], o[t:])


def _proj_kernel(x_ref, g_ref, w1_ref, wm_ref, gseg_ref, gains_ref, fb_ref, ltri_ref, *rest):
    outs = dict(zip(_SEG_ORDER, rest[:len(_SEG_ORDER)]))
    cm_ref = rest[len(_SEG_ORDER)]
    carry_ref = rest[len(_SEG_ORDER) + 1]

    @pl.when(pl.program_id(1) == 0)
    def _():
        carry_ref[...] = jnp.zeros_like(carry_ref)

    h = _rms(x_ref[0], g_ref[...])
    hb = h.astype(BF16)

    def seg(name):
        off, width = _SEG[name]
        return jnp.dot(hb, w1_ref[:, off:off + width], preferred_element_type=F32)

    def head_norm(t, row):
        width = t.shape[1]
        ssq = jnp.dot((t * t).astype(BF16), gseg_ref[:width, :width], preferred_element_type=F32)
        return t * lax.rsqrt(ssq * (1.0 / HEAD_DIM) + NORM_EPS) * gains_ref[row:row + 1, :width]

    normed = dict(aq=0, ak=1, fq=2, fk=3, dq=4, dkk=5)
    scaled = dict(sq=ATTN_SCALE, iq=IDX_SCALE)
    for name in _SEG_ORDER:
        t = seg(name)
        if name in normed:
            t = head_norm(t, normed[name])
        elif name in scaled:
            t = t * scaled[name]
        outs[name][0] = t.astype(BF16)

    misc = _dot_x3(h, wm_ref[0], wm_ref[1])
    lane = lax.broadcasted_iota(I32, misc.shape, 1)
    logf = jnp.where(lane < 4, _log_sigmoid(misc + fb_ref[...]), 0.0)
    ltri = ltri_ref[...]
    c = carry_ref[0:1, :]
    for piece in _split3(logf):
        c = c + jnp.dot(ltri, piece, preferred_element_type=F32)
    tm = misc.shape[0]
    carry_ref[0:1, :] = c[tm - 1:tm, :]
    cm_ref[0] = jnp.where(lane < 4, c, misc)


def _proj(x, g, w1, wm, gseg, gains, fb, tm):
    b, s, d = x.shape
    ltri = jnp.tril(jnp.ones((tm, tm), BF16))
    full = lambda shape: pl.BlockSpec(shape, lambda bi, si: (0,) * len(shape))
    out_shapes = [jax.ShapeDtypeStruct((b, s, _SEG[n][1]), BF16) for n in _SEG_ORDER]
    out_shapes.append(jax.ShapeDtypeStruct((b, s, LANES), F32))
    out_specs = [pl.BlockSpec((1, tm, _SEG[n][1]), lambda bi, si: (bi, si, 0)) for n in _SEG_ORDER]
    out_specs.append(pl.BlockSpec((1, tm, LANES), lambda bi, si: (bi, si, 0)))
    return pl.pallas_call(
        _proj_kernel,
        grid=(b, s // tm),
        in_specs=[pl.BlockSpec((1, tm, d), lambda bi, si: (bi, si, 0)),
                  full((1, d)), full(w1.shape), full(wm.shape), full(gseg.shape),
                  full(gains.shape), full(fb.shape), full((tm, tm))],
        out_specs=out_specs,
        out_shape=out_shapes,
        scratch_shapes=[pltpu.VMEM((8, LANES), F32)],
        compiler_params=_cparams("arbitrary", "arbitrary"),
        name="proj",
    )(x, g, w1, wm, gseg, gains, fb, ltri)


def _swa_kernel(q_ref, kp_ref, kc_ref, vp_ref, vc_ref, bias_ref, sink_ref, o_ref):
    i = pl.program_id(1)
    t = SWA_BLOCK
    q = q_ref[0]
    kcat = jnp.concatenate([kp_ref[0], kc_ref[0]], axis=0)
    vcat = jnp.concatenate([vp_ref[0], vc_ref[0]], axis=0)
    col = lax.broadcasted_iota(I32, (t, 2 * t), 1)
    no_prev = (col < t) & (i == 0)
    pairs = []
    for pair in range(2):
        qs = _split_heads(q[:, pair * LANES:(pair + 1) * LANES])
        s = _dot_nt(qs, kcat)
        ps = []
        for hh in range(2):
            head = pair + 2 * hh
            sh = s[hh * t:(hh + 1) * t] + bias_ref[head]
            sh = jnp.where(no_prev, NEG_INF, sh)
            sink = sink_ref[head:head + 1, 0:1]
            m = jnp.maximum(jnp.max(sh, axis=1, keepdims=True), sink)
            p = jnp.exp(sh - m)
            denom = jnp.sum(p, axis=1, keepdims=True) + jnp.exp(sink - m)
            ps.append(p / denom)
        o = jnp.dot(jnp.concatenate(ps, axis=0).astype(BF16), vcat, preferred_element_type=F32)
        pairs.append(_merge_heads(o, t))
    o_ref[0] = jnp.concatenate(pairs, axis=1).astype(BF16)


def _swa(aq, ak, av, bias, sinks):
    b, s, _ = aq.shape
    t = SWA_BLOCK
    cur = lambda bi, i: (bi, i, 0)
    prev = lambda bi, i: (bi, jnp.maximum(i - 1, 0), 0)
    return pl.pallas_call(
        _swa_kernel,
        grid=(b, s // t),
        in_specs=[pl.BlockSpec((1, t, 256), cur),
                  pl.BlockSpec((1, t, LANES), prev), pl.BlockSpec((1, t, LANES), cur),
                  pl.BlockSpec((1, t, LANES), prev), pl.BlockSpec((1, t, LANES), cur),
                  pl.BlockSpec(bias.shape, lambda bi, i: (0, 0, 0)),
                  pl.BlockSpec(sinks.shape, lambda bi, i: (0, 0))],
        out_specs=pl.BlockSpec((1, t, 256), cur),
        out_shape=jax.ShapeDtypeStruct((b, s, 256), BF16),
        compiler_params=_cparams("arbitrary", "arbitrary"),
        name="swa",
    )(aq, ak, ak, av, av, bias, sinks)


def _fox_kernel(q_ref, k_ref, vt_ref, ct_ref, ccol_ref, o_ref, ckb_ref, *, t, tk):
    i = pl.program_id(1)
    n_tiles = ckb_ref.shape[1] // t

    @pl.when(i == 0)
    def _():
        def fill(j, c):
            rows = pl.ds(pl.multiple_of(j * t, t), t)
            cc = ccol_ref[0, rows, :]
            for h in range(4):
                ckb_ref[h, rows, :] = jnp.broadcast_to(cc[:, h:h + 1], (t, LANES))
            return c
        lax.fori_loop(0, n_tiles, fill, 0)

    q = q_ref[0]
    ct = ct_ref[0]
    qs = [_split_heads(q[:, :LANES]), _split_heads(q[:, LANES:])]
    jd = (i * t) // tk
    valid = (lax.broadcasted_iota(I32, (tk, t), 0)
             <= lax.broadcasted_iota(I32, (tk, t), 1) + (i * t - jd * tk))

    def tile(j, carry, masked):
        m, l, accs = carry
        rows = pl.ds(pl.multiple_of(j * tk, tk), tk)
        cols = []
        for pair in range(2):
            st = _dot_nt(k_ref[0, rows, pair * LANES:(pair + 1) * LANES], qs[pair])
            for hh in range(2):
                head = 2 * pair + hh
                ck = ckb_ref[head, rows, :]
                for c in range(t // LANES):
                    cs = slice(c * LANES, (c + 1) * LANES)
                    sh = st[:, hh * t + c * LANES:hh * t + (c + 1) * LANES] + (ct[head:head + 1, cs] - ck)
                    if masked:
                        sh = jnp.where(valid[:, cs], sh, NEG_INF)
                    cols.append(sh)
        st = jnp.concatenate(cols, axis=1)
        m_new = jnp.maximum(m, jnp.max(st, axis=0, keepdims=True))
        alpha = jnp.exp(m - m_new)
        p = jnp.exp(st - m_new)
        l = alpha * l + jnp.sum(p, axis=0, keepdims=True)
        pb = p.astype(BF16)
        new = []
        for pair in range(2):
            lanes = slice(pair * 2 * t, (pair + 1) * 2 * t)
            pv = jnp.dot(vt_ref[0, j, pair * LANES:(pair + 1) * LANES, :], pb[:, lanes],
                         preferred_element_type=F32)
            new.append(alpha[:, lanes] * accs[pair] + pv)
        return m_new, l, tuple(new)

    init = (jnp.full((1, 4 * t), M_INIT, F32), jnp.zeros((1, 4 * t), F32),
            (jnp.zeros((LANES, 2 * t), F32), jnp.zeros((LANES, 2 * t), F32)))
    carry = lax.fori_loop(0, jd, lambda j, c: tile(j, c, False), init)
    _, l, accs = tile(jd, carry, True)
    outs = []
    for pair in range(2):
        o = accs[pair] / l[:, pair * 2 * t:(pair + 1) * 2 * t]
        outs.append(jnp.concatenate([o[:HEAD_DIM, :t], o[HEAD_DIM:, t:]], axis=0))
    o_ref[0] = jnp.concatenate(outs, axis=0).T.astype(BF16)


def _fox(fq, fk, fv, ct, ccol, t, tk):
    b, s, _ = fq.shape
    nt = s // t
    vt = jnp.transpose(fv.reshape(b, s // tk, tk, 256), (0, 1, 3, 2))
    return pl.pallas_call(
        functools.partial(_fox_kernel, t=t, tk=tk),
        grid=(b, nt),
        in_specs=[pl.BlockSpec((1, t, 256), lambda bi, i: (bi, i, 0)),
                  pl.BlockSpec((1, s, 256), lambda bi, i: (bi, 0, 0)),
                  pl.BlockSpec((1, s // tk, 256, tk), lambda bi, i: (bi, 0, 0, 0)),
                  pl.BlockSpec((1, 8, t), lambda bi, i: (bi, 0, i)),
                  pl.BlockSpec((1, s, LANES), lambda bi, i: (bi, 0, 0))],
        out_specs=pl.BlockSpec((1, t, 256), lambda bi, i: (bi, i, 0)),
        out_shape=jax.ShapeDtypeStruct((b, s, 256), BF16),
        scratch_shapes=[pltpu.VMEM((4, s, LANES), F32)],
        compiler_params=_cparams("arbitrary", "arbitrary"),
        name="fox",
    )(fq, fk, vt, ct, ccol)


def _sb_kernel(q_ref, k_ref, v_ref, u_ref, o_ref, *, tq, tk):
    i = pl.program_id(1)
    q = q_ref[0]
    u = u_ref[...]
    nfull = (i * tq) // tk
    row = lax.broadcasted_iota(I32, (2 * tq, tk), 0)
    col = lax.broadcasted_iota(I32, (2 * tq, tk), 1)
    qrow = jnp.where(row >= tq, row - tq, row)
    pairs = []
    for pair in range(2):
        sl = slice(pair * LANES, (pair + 1) * LANES)
        qs = _split_heads(q[:, sl])

        def tile(j, r, acc, masked):
            start = pl.multiple_of(j * tk, tk)
            kj = k_ref[0, pl.ds(start, tk), sl]
            vj = v_ref[0, pl.ds(start, tk), sl]
            z = _dot_nt(qs, kj)
            sp = jnp.log(1.0 + jnp.exp(-jnp.abs(z)))
            log_beta = jnp.minimum(z, 0.0) - sp
            log_keep = jnp.minimum(-z, 0.0) - sp
            if masked:
                strict = j * tk + col < i * tq + qrow
                log_keep = jnp.where(strict, log_keep, 0.0)
            hi = log_keep.astype(BF16)
            lo = (log_keep - hi.astype(F32)).astype(BF16)
            later = (jnp.dot(hi, u, preferred_element_type=F32)
                     + jnp.dot(lo, u, preferred_element_type=F32))
            a = jnp.exp(log_beta + later + r)
            if masked:
                a = jnp.where(strict, a, 0.0)
            acc = acc + jnp.dot(a.astype(BF16), vj, preferred_element_type=F32)
            r = r + jnp.sum(log_keep, axis=1, keepdims=True)
            return r, acc

        r, acc = tile(nfull, jnp.zeros((2 * tq, 1), F32), jnp.zeros((2 * tq, LANES), F32), True)

        def cond(c):
            return (c[0] >= 0) & (c[1] > 0)

        def body(c):
            j, _, r, acc = c
            r, acc = tile(j, r, acc, False)
            return j - 1, (jnp.max(r) > SB_DEAD).astype(I32), r, acc

        _, _, _, acc = lax.while_loop(cond, body, (nfull - 1, (jnp.max(r) > SB_DEAD).astype(I32), r, acc))
        pairs.append(_merge_heads(acc, tq))
    o_ref[0] = jnp.concatenate(pairs, axis=1).astype(BF16)


def _sb(sq, sk, sv, tq, tk):
    b, s, _ = sq.shape
    u = (jnp.arange(tk)[:, None] > jnp.arange(tk)[None, :]).astype(BF16)
    return pl.pallas_call(
        functools.partial(_sb_kernel, tq=tq, tk=tk),
        grid=(b, s // tq),
        in_specs=[pl.BlockSpec((1, tq, 256), lambda bi, i: (bi, i, 0)),
                  pl.BlockSpec((1, s, 256), lambda bi, i: (bi, 0, 0)),
                  pl.BlockSpec((1, s, 256), lambda bi, i: (bi, 0, 0)),
                  pl.BlockSpec((tk, tk), lambda bi, i: (0, 0))],
        out_specs=pl.BlockSpec((1, tq, 256), lambda bi, i: (bi, i, 0)),
        out_shape=jax.ShapeDtypeStruct((b, s, 256), BF16),
        compiler_params=_cparams("arbitrary", "arbitrary"),
        name="stickbreak",
    )(sq, sk, sv, u)


def _dsa_kernel(q_ref, kk_ref, vt_ref, iq_ref, ikk_ref, wt_ref, bias_ref, lstrict_ref, o_ref,
                key_ref, *, t, top_k):
    i = pl.program_id(1)
    group = 64
    causal = lax.broadcasted_iota(I32, (t, t), 0) <= lax.broadcasted_iota(I32, (t, t), 1)

    def head_stack(x):
        return jnp.concatenate([_split_heads(x[:, :LANES]), _split_heads(x[:, LANES:])], axis=0)

    def tile_rows(ref, j):
        return ref[0, pl.ds(pl.multiple_of(j * t, t), t), :]

    iqs = head_stack(iq_ref[0])
    wt = wt_ref[0]
    w = [wt[4 + h:5 + h, :] * (IDX_HEADS ** -0.5) for h in range(IDX_HEADS)]

    def score_tile(j, masked):
        lg = _dot_nt(tile_rows(ikk_ref, j), iqs)
        sc = w[0] * jnp.maximum(lg[:, 0:t], 0.0)
        for h in range(1, IDX_HEADS):
            sc = sc + w[h] * jnp.maximum(lg[:, h * t:(h + 1) * t], 0.0)
        bits = pltpu.bitcast(sc, I32)
        key = bits ^ ((bits >> 31) & 0x7FFFFFFF)
        key = jnp.where(key == -1, 0, key)
        if masked:
            key = jnp.where(causal, key, INT_MIN)
        key_ref[j] = key

    def p1(j, c):
        score_tile(j, False)
        return c

    lax.fori_loop(0, i, p1, 0)
    score_tile(i, True)

    def count(pred):
        def cb(j, acc):
            for g in range(t // group):
                acc = acc + jnp.where(pred(key_ref[j, g * group:(g + 1) * group, :]), 1.0, 0.0)
            return acc
        acc = lax.fori_loop(0, i + 1, cb, jnp.zeros((group, t), F32))
        return jnp.sum(acc, axis=0, keepdims=True)

    def bis_body(b, thr_u):
        cand_u = thr_u | (jnp.int32(1) << (31 - b))
        n = count(lambda k: k >= (cand_u ^ INT_MIN))
        return jnp.where(n >= top_k, cand_u, thr_u)

    thr_u = lax.fori_loop(0, 32, bis_body, jnp.zeros((1, t), I32))
    thr = jnp.maximum(thr_u ^ INT_MIN, INT_MIN + 1)
    surplus = jnp.max(count(lambda k: k >= thr)) > top_k

    def tie_pass():
        need = top_k - count(lambda k: k > thr)

        def tb(j, seen):
            k = key_ref[j]
            eq = k == thr
            eqf = jnp.where(eq, 1.0, 0.0)
            before = jnp.dot(lstrict_ref[...], eqf.astype(BF16), preferred_element_type=F32) + seen
            sel = (k > thr) | (eq & (before < need))
            key_ref[j] = jnp.where(sel, 1, INT_MIN)
            return seen + jnp.sum(eqf, axis=0, keepdims=True)

        lax.fori_loop(0, i + 1, tb, jnp.zeros((1, t), F32))
        return jnp.zeros((1, t), I32)

    thr = lax.cond(surplus, tie_pass, lambda: thr)

    qs = head_stack(q_ref[0])

    def attn_tile(j, carry, bias_kind):
        m, l, acc = carry
        st = _dot_nt(tile_rows(kk_ref, j), qs)
        selb = jnp.where(key_ref[j] >= thr, 0.0, NEG_INF)
        parts = []
        for h in range(4):
            sh = st[:, h * t:(h + 1) * t] + selb
            if bias_kind is not None:
                sh = sh + bias_ref[bias_kind, h]
            parts.append(sh)
        st = jnp.concatenate(parts, axis=1)
        m_new = jnp.maximum(m, jnp.max(st, axis=0, keepdims=True))
        alpha = jnp.exp(m - m_new)
        p = jnp.exp(st - m_new)
        l = alpha * l + jnp.sum(p, axis=0, keepdims=True)
        acc = alpha * acc + jnp.dot(vt_ref[0, j], p.astype(BF16), preferred_element_type=F32)
        return m_new, l, acc

    init = (jnp.full((1, 4 * t), M_INIT, F32), jnp.zeros((1, 4 * t), F32), jnp.zeros((HEAD_DIM, 4 * t), F32))
    carry = lax.fori_loop(0, jnp.maximum(i - 1, 0), lambda j, c: attn_tile(j, c, None), init)
    carry = lax.cond(i > 0, lambda c: attn_tile(i - 1, c, 0), lambda c: c, carry)
    _, l, acc = attn_tile(i, carry, 1)
    ot = acc / l
    ot = jnp.concatenate([ot[:, h * t:(h + 1) * t] for h in range(4)], axis=0)
    o_ref[0] = ot.T.astype(BF16)


def _dsa_bias(t5_table, t):
    assert t + 1 >= T5_MAX_DISTANCE
    k = jnp.arange(t)[:, None]
    q = jnp.arange(t)[None, :]
    far = t5_table[T5_BUCKETS - 1, 4:].astype(F32)
    tiles = []
    for off in (t, 0):
        dist = off + q - k
        b = jnp.transpose(t5_table[_t5_bucket(dist)][..., 4:].astype(F32), (2, 0, 1)) - far[:, None, None]
        tiles.append(jnp.where((dist >= 0)[None], b, 0.0))
    return jnp.stack(tiles)


def _dsa(dq, dkk, dvv, iq, ikk, wt, bias, top_k, t):
    b, s, _ = dq.shape
    nt = s // t
    lstrict = (jnp.arange(t)[:, None] > jnp.arange(t)[None, :]).astype(BF16)
    vt = jnp.transpose(dvv[:, :, :HEAD_DIM].reshape(b, nt, t, HEAD_DIM), (0, 1, 3, 2))
    blk = lambda w: pl.BlockSpec((1, t, w), lambda bi, i: (bi, i, 0))
    seq = lambda w: pl.BlockSpec((1, s, w), lambda bi, i: (bi, 0, 0))
    return pl.pallas_call(
        functools.partial(_dsa_kernel, t=t, top_k=top_k),
        grid=(b, nt),
        in_specs=[blk(256), seq(LANES),
                  pl.BlockSpec((1, nt, HEAD_DIM, t), lambda bi, i: (bi, 0, 0, 0)),
                  blk(256), seq(LANES),
                  pl.BlockSpec((1, 8, t), lambda bi, i: (bi, 0, i)),
                  pl.BlockSpec(bias.shape, lambda bi, i: (0, 0, 0, 0)),
                  pl.BlockSpec((t, t), lambda bi, i: (0, 0))],
        out_specs=blk(256),
        out_shape=jax.ShapeDtypeStruct((b, s, 256), BF16),
        scratch_shapes=[pltpu.VMEM((nt, t, t), I32)],
        compiler_params=_cparams("arbitrary", "arbitrary"),
        name="dsa",
    )(dq, dkk, vt, iq, ikk, wt, bias, lstrict)


def _merge_kernel(x_ref, gm_ref, wg_ref, oa_ref, of_ref, os_ref, od_ref, wb_ref, wo_ref, gf_ref,
                  wr_ref, br_ref, xo_ref, h2_ref, route_ref):
    x = x_ref[...]
    hb = _rms(x, gm_ref[...]).astype(BF16)
    d = x.shape[1]
    merged = None
    for bi, o_ref in enumerate((oa_ref, of_ref, os_ref, od_ref)):
        gate = jax.nn.sigmoid(jnp.dot(hb, wg_ref[:, bi * d:(bi + 1) * d], preferred_element_type=F32))
        term = gate * jnp.dot(o_ref[...], wb_ref[bi], preferred_element_type=F32)
        merged = term if merged is None else merged + term
    xn = x + jnp.dot(merged.astype(BF16), wo_ref[...], preferred_element_type=F32)
    xo_ref[...] = xn
    h2 = _rms(xn, gf_ref[...])
    h2_ref[...] = h2

    logits = _dot_x3(h2, wr_ref[0], wr_ref[1]) + br_ref[...]
    lane = lax.broadcasted_iota(I32, logits.shape, 1).astype(F32)
    big = 1e9
    gl = jnp.where(lane < N_GROUPS, logits, -jnp.inf)
    gmax = jnp.max(gl, axis=1, keepdims=True)
    grp = jnp.min(jnp.where(gl == gmax, lane, big), axis=1, keepdims=True)
    p_grp = 1.0 / jnp.sum(jnp.exp(gl - gmax), axis=1, keepdims=True)
    first = N_GROUPS + grp * EXPERTS_PER_GROUP
    el = jnp.where((lane >= first) & (lane < first + EXPERTS_PER_GROUP), logits, -jnp.inf)
    l1 = jnp.max(el, axis=1, keepdims=True)
    i1 = jnp.min(jnp.where(el == l1, lane, big), axis=1, keepdims=True)
    el2 = jnp.where(lane == i1, -jnp.inf, el)
    l2 = jnp.max(el2, axis=1, keepdims=True)
    i2 = jnp.min(jnp.where(el2 == l2, lane, big), axis=1, keepdims=True)
    e2 = jnp.exp(l2 - l1)
    g1 = p_grp / (1.0 + e2)
    g2 = p_grp * e2 / (1.0 + e2)
    route = jnp.where(lane == 0, i1 - N_GROUPS,
                      jnp.where(lane == 1, i2 - N_GROUPS,
                                jnp.where(lane == 2, g1, jnp.where(lane == 3, g2, 0.0))))
    route_ref[...] = route


def _merge(x2, gm, wg, o_a, o_f, o_s, o_d, wb, wo, gf, wr, br, tm):
    n, d = x2.shape
    row = lambda w: pl.BlockSpec((tm, w), lambda i: (i, 0))
    full = lambda a: pl.BlockSpec(a.shape, lambda i: (0,) * a.ndim)
    return pl.pallas_call(
        _merge_kernel,
        grid=(n // tm,),
        in_specs=[row(d), full(gm), full(wg), row(256), row(256), row(256), row(256),
                  full(wb), full(wo), full(gf), full(wr), full(br)],
        out_specs=[row(d), row(d), row(LANES)],
        out_shape=[jax.ShapeDtypeStruct((n, d), F32), jax.ShapeDtypeStruct((n, d), F32),
                   jax.ShapeDtypeStruct((n, LANES), F32)],
        compiler_params=_cparams("arbitrary"),
        name="merge",
    )(x2, gm, wg, o_a, o_f, o_s, o_d, wb, wo, gf, wr, br)


def _rank_kernel(route_ref, ltri_ref, rank_ref, cnt_ref, carry_ref):
    @pl.when(pl.program_id(0) == 0)
    def _():
        carry_ref[...] = jnp.zeros_like(carry_ref)

    route = route_ref[...]
    lane = lax.broadcasted_iota(I32, route.shape, 1)
    e0 = route[:, 0:1].astype(I32)
    e1 = route[:, 1:2].astype(I32)
    oh0 = (lane == e0).astype(F32)
    oh1 = (lane == e1).astype(F32)
    both = oh0 + oh1
    before = jnp.dot(ltri_ref[...], both.astype(BF16), preferred_element_type=F32) + carry_ref[0:1, :]
    r0 = jnp.sum(oh0 * before, axis=1, keepdims=True)
    r1 = jnp.sum(oh1 * (before + oh0), axis=1, keepdims=True)
    rank_ref[...] = jnp.where(lane == 0, r0, jnp.where(lane == 1, r1, 0.0))
    total = carry_ref[0:1, :] + jnp.sum(both, axis=0, keepdims=True)
    carry_ref[0:1, :] = total
    cnt_ref[...] = jnp.broadcast_to(total, cnt_ref.shape)


def _rank(route, tm):
    n = route.shape[0]
    ltri = (jnp.arange(tm)[:, None] > jnp.arange(tm)[None, :]).astype(BF16)
    return pl.pallas_call(
        _rank_kernel,
        grid=(n // tm,),
        in_specs=[pl.BlockSpec((tm, LANES), lambda i: (i, 0)), pl.BlockSpec((tm, tm), lambda i: (0, 0))],
        out_specs=[pl.BlockSpec((tm, LANES), lambda i: (i, 0)), pl.BlockSpec((8, LANES), lambda i: (0, 0))],
        out_shape=[jax.ShapeDtypeStruct((n, LANES), F32), jax.ShapeDtypeStruct((8, LANES), F32)],
        scratch_shapes=[pltpu.VMEM((8, LANES), F32)],
        compiler_params=_cparams("arbitrary"),
        name="moe_rank",
    )(route, ltri)


def _expert_kernel(be_ref, nu_ref, tok_ref, tok_next_ref, h_hbm, wup_ref, wdn_ref, y_ref, xbuf, sem, *, te):
    b = pl.program_id(0)
    n_used = nu_ref[0]
    slot = b % 2

    def start_row(tokens_ref, dst, r):
        pltpu.make_async_copy(h_hbm.at[pl.ds(tokens_ref[0, 0, r], 1), :],
                              xbuf.at[dst, pl.ds(r, 1), :], sem.at[dst]).start()

    def block(prefetch_next):
        pltpu.make_async_copy(h_hbm.at[pl.ds(0, te), :], xbuf.at[slot], sem.at[slot]).wait()
        xb = xbuf[slot].astype(BF16)
        if prefetch_next:
            for r in range(te):
                start_row(tok_next_ref, 1 - slot, r)
        gu = jnp.dot(xb, wup_ref[0], preferred_element_type=F32)
        g = gu[:, :EXPERT_FF]
        act = g * jax.nn.sigmoid(g) * gu[:, EXPERT_FF:]
        y_ref[...] = jnp.dot(act.astype(BF16), wdn_ref[0], preferred_element_type=F32)

    @pl.when((b == 0) & (n_used > 0))
    def _():
        def issue(r, c):
            start_row(tok_ref, 0, r)
            return c
        lax.fori_loop(0, te, issue, 0, unroll=8)

    @pl.when(b + 1 < n_used)
    def _():
        block(True)

    @pl.when(b + 1 == n_used)
    def _():
        block(False)

    @pl.when(b >= n_used)
    def _():
        y_ref[...] = jnp.zeros_like(y_ref)


def _experts(blk_expert, n_used, slot_tok, h2, w_up, w_down, te):
    n_blocks = blk_expert.shape[0]
    d = h2.shape[1]
    grid_spec = pltpu.PrefetchScalarGridSpec(
        num_scalar_prefetch=2,
        grid=(n_blocks,),
        in_specs=[pl.BlockSpec((1, 1, te), lambda b, be, nu: (b, 0, 0), memory_space=pltpu.SMEM),
                  pl.BlockSpec((1, 1, te), lambda b, be, nu: (jnp.minimum(b + 1, n_blocks - 1), 0, 0),
                               memory_space=pltpu.SMEM),
                  pl.BlockSpec(memory_space=pl.ANY),
                  pl.BlockSpec((1, d, 2 * EXPERT_FF), lambda b, be, nu: (be[b], 0, 0)),
                  pl.BlockSpec((1, EXPERT_FF, d), lambda b, be, nu: (be[b], 0, 0))],
        out_specs=pl.BlockSpec((te, d), lambda b, be, nu: (b, 0)),
        scratch_shapes=[pltpu.VMEM((2, te, d), F32), pltpu.SemaphoreType.DMA((2,))],
    )
    slots = slot_tok.reshape(n_blocks, 1, te)
    return pl.pallas_call(
        functools.partial(_expert_kernel, te=te),
        grid_spec=grid_spec,
        out_shape=jax.ShapeDtypeStruct((n_blocks * te, d), F32),
        compiler_params=_cparams("arbitrary"),
        name="moe_experts",
    )(blk_expert, n_used, slots, slots, h2, w_up, w_down)


def _combine_kernel(pos_ref, pos_next_ref, x_ref, route_ref, y_hbm, o_ref, ybuf, sem, *, tc):
    i = pl.program_id(0)
    slot = i % 2

    def gather(rows_ref, dst):
        def issue(r, c):
            pltpu.make_async_copy(y_hbm.at[pl.ds(rows_ref[0, 0, r], 1), :],
                                  ybuf.at[dst, pl.ds(r, 1), :], sem.at[dst]).start()
            return c
        lax.fori_loop(0, 2 * tc, issue, 0, unroll=8)

    @pl.when(i == 0)
    def _():
        gather(pos_ref, 0)

    @pl.when(i + 1 < pl.num_programs(0))
    def _():
        gather(pos_next_ref, 1 - slot)

    pltpu.make_async_copy(y_hbm.at[pl.ds(0, 2 * tc), :], ybuf.at[slot], sem.at[slot]).wait()
    route = route_ref[...]
    o_ref[...] = (x_ref[...] + route[:, 2:3] * ybuf[slot, 0:tc, :]
                  + route[:, 3:4] * ybuf[slot, tc:2 * tc, :])


def _combine(pos, x2, route, yb, tc):
    n, d = x2.shape
    nt = n // tc
    pos_t = jnp.transpose(pos.reshape(nt, tc, 2), (0, 2, 1)).reshape(nt, 1, 2 * tc)
    return pl.pallas_call(
        functools.partial(_combine_kernel, tc=tc),
        grid=(nt,),
        in_specs=[pl.BlockSpec((1, 1, 2 * tc), lambda i: (i, 0, 0), memory_space=pltpu.SMEM),
                  pl.BlockSpec((1, 1, 2 * tc), lambda i: (jnp.minimum(i + 1, nt - 1), 0, 0),
                               memory_space=pltpu.SMEM),
                  pl.BlockSpec((tc, d), lambda i: (i, 0)),
                  pl.BlockSpec((tc, LANES), lambda i: (i, 0)),
                  pl.BlockSpec(memory_space=pl.ANY)],
        out_specs=pl.BlockSpec((tc, d), lambda i: (i, 0)),
        out_shape=jax.ShapeDtypeStruct((n, d), F32),
        scratch_shapes=[pltpu.VMEM((2, 2 * tc, d), F32), pltpu.SemaphoreType.DMA((2,))],
        compiler_params=_cparams("arbitrary"),
        name="moe_combine",
    )(pos_t, pos_t, x2, route, yb)


def _t5_bucket(dist):
    n = jnp.maximum(dist, 0)
    max_exact = T5_BUCKETS // 2
    nf = jnp.maximum(n, 1).astype(F32)
    large = max_exact + (jnp.log(nf / max_exact) / math.log(T5_MAX_DISTANCE / max_exact)
                         * (T5_BUCKETS - max_exact)).astype(I32)
    large = jnp.minimum(large, T5_BUCKETS - 1)
    return jnp.where(n < max_exact, n, large)


def _swa_bias(t5_table):
    t = SWA_BLOCK
    dist = t + jnp.arange(t)[:, None] - jnp.arange(2 * t)[None, :]
    tile = jnp.transpose(t5_table[_t5_bucket(dist)][..., :4].astype(F32), (2, 0, 1))
    valid = (dist >= 0) & (dist < t)
    return jnp.where(valid[None], tile, NEG_INF)


def _layer_weights(w_in, qk_gain, forget_bias, w_branch):
    offs = np.concatenate([[0], np.cumsum(IN_SPLITS)]).tolist()
    part = lambda k: w_in[:, offs[k]:offs[k + 1]]
    dup = lambda w: jnp.concatenate([w, w], axis=1)
    aq = part(0).reshape(-1, 4, HEAD_DIM)[:, (0, 2, 1, 3)].reshape(-1, 256)
    cols = [aq, part(1), part(2), part(3), part(4), part(5), part(7), part(8), part(9),
            part(10), dup(part(11)), dup(part(12)), part(13), dup(part(14))]
    w1 = jnp.concatenate(cols, axis=1).astype(BF16)
    d = w_in.shape[0]
    wm = _hi_lo(jnp.concatenate([part(6), part(15), jnp.zeros((d, LANES - 8), F32)], axis=1))
    wg = part(16).astype(BF16)
    tile = lambda g, reps, scale: jnp.pad(jnp.tile(g, reps) * scale, (0, 256 - reps * HEAD_DIM))
    gains = jnp.stack([tile(qk_gain[0, 0], 4, ATTN_SCALE), tile(qk_gain[0, 1], 2, 1.0),
                       tile(qk_gain[1, 0], 4, ATTN_SCALE), tile(qk_gain[1, 1], 4, 1.0),
                       tile(qk_gain[2, 0], 4, ATTN_SCALE), tile(qk_gain[2, 1], 2, 1.0),
                       jnp.zeros((256,), F32), jnp.zeros((256,), F32)]).astype(F32)
    fb = jnp.pad(forget_bias.astype(F32), (0, LANES - 4)).reshape(1, LANES)
    wb0 = w_branch[0].reshape(4, HEAD_DIM, -1)[(0, 2, 1, 3), :, :].reshape(256, -1)
    wb = jnp.stack([wb0, w_branch[1], w_branch[2], w_branch[3]]).astype(BF16)
    return w1, wm, wg, gains, fb, wb


def kernel(x, norm_mix_g, w_in, forget_bias, attn_sinks, qk_gain, w_branch, w_out, t5_table, norm_ffn_g,
           w_router_group, b_router_group, w_router_expert, b_router_expert, w_expert_up, w_expert_down):
    b, s, d = x.shape
    n = b * s
    depth = w_in.shape[0]
    top_k = min(DSA_TOPK_MAX, s // 4)
    tm_proj = min(512, s)
    fox_t, fox_tk = min(256, s), min(512, s)
    sb_t = min(256, s)
    dsa_t = min(256, s)
    te = 256
    tc = 128

    gseg = (jnp.arange(256)[:, None] // HEAD_DIM == jnp.arange(256)[None, :] // HEAD_DIM).astype(BF16)
    bias_swa = _swa_bias(t5_table)
    bias_dsa = _dsa_bias(t5_table, dsa_t)
    n_blocks = -(-2 * n // te) + N_EXPERTS
    tok_ids = jnp.repeat(jnp.arange(n, dtype=I32), 2)

    for layer in range(depth):
        w1, wm, wg, gains, fb, wb = _layer_weights(w_in[layer], qk_gain[layer], forget_bias[layer],
                                                   w_branch[layer])
        sinks = jnp.broadcast_to(jnp.pad(attn_sinks[layer].astype(F32), (0, 4))[:, None], (8, LANES))
        (aq, ak, av, fq, fk, fv, sq, sk, sv, dq, dkk, dvv, iq, ikk, cm) = _proj(
            x, norm_mix_g[layer].reshape(1, d), w1, wm, gseg, gains, fb, tm_proj)

        o_swa = _swa(aq, ak, av, bias_swa, sinks)
        cmt = jnp.transpose(cm[:, :, :8], (0, 2, 1))
        o_fox = _fox(fq, fk, fv, cmt, cm, fox_t, fox_tk)
        o_sb = _sb(sq, sk, sv, sb_t, sb_t)
        o_dsa = _dsa(dq, dkk, dvv, iq, ikk, cmt, bias_dsa, top_k, dsa_t)

        wr = _hi_lo(jnp.concatenate([w_router_group[layer], w_router_expert[layer],
                                     jnp.zeros((d, LANES - N_GROUPS - N_EXPERTS), F32)], axis=1))
        br = jnp.concatenate([b_router_group[layer], b_router_expert[layer],
                              jnp.zeros((LANES - N_GROUPS - N_EXPERTS,), F32)]).reshape(1, LANES)
        x2, h2, route = _merge(
            x.reshape(n, d), norm_mix_g[layer].reshape(1, d), wg,
            o_swa.reshape(n, 256), o_fox.reshape(n, 256), o_sb.reshape(n, 256), o_dsa.reshape(n, 256),
            wb, w_out[layer].astype(BF16), norm_ffn_g[layer].reshape(1, d), wr, br, min(256, n))

        rank, cnt = _rank(route, min(512, n))
        counts = cnt[0, :N_EXPERTS].astype(I32)
        padded = (counts + te - 1) // te * te
        pend = jnp.cumsum(padded)
        pstart = pend - padded
        expert = route[:, :2].astype(I32)
        own = expert[:, :, None] == jnp.arange(N_EXPERTS, dtype=I32)
        pos = jnp.sum(jnp.where(own, pstart, 0), axis=-1) + rank[:, :2].astype(I32)
        slot_tok = jnp.zeros((n_blocks * te,), I32).at[pos.reshape(-1)].set(tok_ids)
        blk_start = jnp.arange(n_blocks, dtype=I32)[:, None] * te
        blk_expert = jnp.minimum(jnp.sum((pend[None, :] <= blk_start).astype(I32), axis=1), N_EXPERTS - 1)
        n_used = (pend[-1:] // te).astype(I32)

        yb = _experts(blk_expert, n_used, slot_tok, h2, w_expert_up[layer].astype(BF16),
                      w_expert_down[layer].astype(BF16), te)
        x = _combine(pos, x2, route, yb, tc).reshape(b, s, d)
    return x
```

```python
import functools
import math

import jax
import jax.numpy as jnp
import numpy as np
from jax import lax
from jax.experimental import pallas as pl
from jax.experimental.pallas import tpu as pltpu

F32 = jnp.float32
BF16 = jnp.bfloat16
I32 = jnp.int32

HEAD_DIM = 64
LANES = 128
NORM_EPS = 1e-6
NEG_INF = -1e30
M_INIT = -1e29
ATTN_SCALE = HEAD_DIM ** -0.5
SWA_BLOCK = 128
IDX_SCALE = 64 ** -0.5
IDX_HEADS = 4
DSA_TOPK_MAX = 256
T5_BUCKETS = 32
T5_MAX_DISTANCE = 128
N_GROUPS = 4
EXPERTS_PER_GROUP = 8
N_EXPERTS = N_GROUPS * EXPERTS_PER_GROUP
EXPERT_FF = 512
SB_DEAD = -110.0
INT_MIN = -2 ** 31
VMEM_LIMIT = 56 * 1024 * 1024

IN_SPLITS = (256, 128, 128, 256, 256, 256, 4, 256, 256, 256, 256, 64, 64, 256, 64, 4, 4096)

_SEG = dict(aq=(0, 256), ak=(256, 128), av=(384, 128), fq=(512, 256), fk=(768, 256), fv=(1024, 256),
            sq=(1280, 256), sk=(1536, 256), sv=(1792, 256), dq=(2048, 256), dkk=(2304, 128),
            dvv=(2432, 128), iq=(2560, 256), ikk=(2816, 128))
_W1_COLS = 2944
_SEG_ORDER = ("aq", "ak", "av", "fq", "fk", "fv", "sq", "sk", "sv", "dq", "dkk", "dvv", "iq", "ikk")


def _cparams(*sem):
    return pltpu.CompilerParams(dimension_semantics=sem, vmem_limit_bytes=VMEM_LIMIT)


def _rms(x, g):
    return x * lax.rsqrt(jnp.mean(x * x, axis=-1, keepdims=True) + NORM_EPS) * g


def _log_sigmoid(z):
    return jnp.minimum(z, 0.0) - jnp.log(1.0 + jnp.exp(-jnp.abs(z)))


def _dot_nt(a, b):
    return lax.dot_general(a, b, (((1,), (1,)), ((), ())), preferred_element_type=F32)


def _split3(x):
    p1 = x.astype(BF16)
    r = x - p1.astype(F32)
    p2 = r.astype(BF16)
    return p1, p2, (r - p2.astype(F32)).astype(BF16)


def _hi_lo(w):
    hi = w.astype(BF16)
    return jnp.stack([hi, (w - hi.astype(F32)).astype(BF16)])


def _dot_x3(a, b_hi, b_lo):
    a_hi = a.astype(BF16)
    a_lo = (a - a_hi.astype(F32)).astype(BF16)
    return (jnp.dot(a_hi, b_hi, preferred_element_type=F32) + jnp.dot(a_lo, b_hi, preferred_element_type=F32)
            + jnp.dot(a_hi, b_lo, preferred_element_type=F32))


def _split_heads(qp):
    lo = lax.broadcasted_iota(I32, (1, LANES), 1) < HEAD_DIM
    zero = jnp.zeros_like(qp)
    return jnp.concatenate([jnp.where(lo, qp, zero), jnp.where(lo, zero, qp)], axis=0)


def _merge_heads(o, t):
    lo = lax.broadcasted_iota(I32, (1, LANES), 1) < HEAD_DIM
    return jnp.where(lo, o[:t], o[t:])


def _proj_kernel(x_ref, g_ref, w1_ref, wm_ref, gseg_ref, gains_ref, fb_ref, ltri_ref, *rest):
    outs = dict(zip(_SEG_ORDER, rest[:len(_SEG_ORDER)]))
    cm_ref = rest[len(_SEG_ORDER)]
    carry_ref = rest[len(_SEG_ORDER) + 1]

    @pl.when(pl.program_id(1) == 0)
    def _():
        carry_ref[...] = jnp.zeros_like(carry_ref)

    h = _rms(x_ref[0], g_ref[...])
    hb = h.astype(BF16)

    def seg(name):
        off, width = _SEG[name]
        return jnp.dot(hb, w1_ref[:, off:off + width], preferred_element_type=F32)

    def head_norm(t, row):
        width = t.shape[1]
        ssq = jnp.dot((t * t).astype(BF16), gseg_ref[:width, :width], preferred_element_type=F32)
        return t * lax.rsqrt(ssq * (1.0 / HEAD_DIM) + NORM_EPS) * gains_ref[row:row + 1, :width]

    normed = dict(aq=0, ak=1, fq=2, fk=3, dq=4, dkk=5)
    scaled = dict(sq=ATTN_SCALE, iq=IDX_SCALE)
    for name in _SEG_ORDER:
        t = seg(name)
        if name in normed:
            t = head_norm(t, normed[name])
        elif name in scaled:
            t = t * scaled[name]
        outs[name][0] = t.astype(BF16)

    misc = _dot_x3(h, wm_ref[0], wm_ref[1])
    lane = lax.broadcasted_iota(I32, misc.shape, 1)
    logf = jnp.where(lane < 4, _log_sigmoid(misc + fb_ref[...]), 0.0)
    ltri = ltri_ref[...]
    c = carry_ref[0:1, :]
    for piece in _split3(logf):
        c = c + jnp.dot(ltri, piece, preferred_element_type=F32)
    tm = misc.shape[0]
    carry_ref[0:1, :] = c[tm - 1:tm, :]
    cm_ref[0] = jnp.where(lane < 4, c, misc)


def _proj(x, g, w1, wm, gseg, gains, fb, tm):
    b, s, d = x.shape
    ltri = jnp.tril(jnp.ones((tm, tm), BF16))
    full = lambda shape: pl.BlockSpec(shape, lambda bi, si: (0,) * len(shape))
    out_shapes = [jax.ShapeDtypeStruct((b, s, _SEG[n][1]), BF16) for n in _SEG_ORDER]
    out_shapes.append(jax.ShapeDtypeStruct((b, s, LANES), F32))
    out_specs = [pl.BlockSpec((1, tm, _SEG[n][1]), lambda bi, si: (bi, si, 0)) for n in _SEG_ORDER]
    out_specs.append(pl.BlockSpec((1, tm, LANES), lambda bi, si: (bi, si, 0)))
    return pl.pallas_call(
        _proj_kernel,
        grid=(b, s // tm),
        in_specs=[pl.BlockSpec((1, tm, d), lambda bi, si: (bi, si, 0)),
                  full((1, d)), full(w1.shape), full(wm.shape), full(gseg.shape),
                  full(gains.shape), full(fb.shape), full((tm, tm))],
        out_specs=out_specs,
        out_shape=out_shapes,
        scratch_shapes=[pltpu.VMEM((8, LANES), F32)],
        compiler_params=_cparams("arbitrary", "arbitrary"),
        name="proj",
    )(x, g, w1, wm, gseg, gains, fb, ltri)


def _swa_kernel(q_ref, kp_ref, kc_ref, vp_ref, vc_ref, bias_ref, sink_ref, o_ref):
    i = pl.program_id(1)
    t = SWA_BLOCK
    q = q_ref[0]
    kcat = jnp.concatenate([kp_ref[0], kc_ref[0]], axis=0)
    vcat = jnp.concatenate([vp_ref[0], vc_ref[0]], axis=0)
    col = lax.broadcasted_iota(I32, (t, 2 * t), 1)
    no_prev = (col < t) & (i == 0)
    pairs = []
    for pair in range(2):
        qs = _split_heads(q[:, pair * LANES:(pair + 1) * LANES])
        s = _dot_nt(qs, kcat)
        ps = []
        for hh in range(2):
            head = pair + 2 * hh
            sh = s[hh * t:(hh + 1) * t] + bias_ref[head]
            sh = jnp.where(no_prev, NEG_INF, sh)
            sink = sink_ref[head:head + 1, 0:1]
            m = jnp.maximum(jnp.max(sh, axis=1, keepdims=True), sink)
            p = jnp.exp(sh - m)
            denom = jnp.sum(p, axis=1, keepdims=True) + jnp.exp(sink - m)
            ps.append(p / denom)
        o = jnp.dot(jnp.concatenate(ps, axis=0).astype(BF16), vcat, preferred_element_type=F32)
        pairs.append(_merge_heads(o, t))
    o_ref[0] = jnp.concatenate(pairs, axis=1).astype(BF16)


def _swa(aq, ak, av, bias, sinks):
    b, s, _ = aq.shape
    t = SWA_BLOCK
    cur = lambda bi, i: (bi, i, 0)
    prev = lambda bi, i: (bi, jnp.maximum(i - 1, 0), 0)
    return pl.pallas_call(
        _swa_kernel,
        grid=(b, s // t),
        in_specs=[pl.BlockSpec((1, t, 256), cur),
                  pl.BlockSpec((1, t, LANES), prev), pl.BlockSpec((1, t, LANES), cur),
                  pl.BlockSpec((1, t, LANES), prev), pl.BlockSpec((1, t, LANES), cur),
                  pl.BlockSpec(bias.shape, lambda bi, i: (0, 0, 0)),
                  pl.BlockSpec(sinks.shape, lambda bi, i: (0, 0))],
        out_specs=pl.BlockSpec((1, t, 256), cur),
        out_shape=jax.ShapeDtypeStruct((b, s, 256), BF16),
        compiler_params=_cparams("arbitrary", "arbitrary"),
        name="swa",
    )(aq, ak, ak, av, av, bias, sinks)


def _fox_kernel(q_ref, k_ref, vt_ref, ct_ref, ccol_ref, o_ref, ckb_ref, *, t, tk):
    i = pl.program_id(1)
    n_tiles = ckb_ref.shape[1] // t

    @pl.when(i == 0)
    def _():
        def fill(j, c):
            rows = pl.ds(pl.multiple_of(j * t, t), t)
            cc = ccol_ref[0, rows, :]
            for h in range(4):
                ckb_ref[h, rows, :] = jnp.broadcast_to(cc[:, h:h + 1], (t, LANES))
            return c
        lax.fori_loop(0, n_tiles, fill, 0)

    q = q_ref[0]
    ct = ct_ref[0]
    qs = [_split_heads(q[:, :LANES]), _split_heads(q[:, LANES:])]
    jd = (i * t) // tk
    valid = (lax.broadcasted_iota(I32, (tk, t), 0)
             <= lax.broadcasted_iota(I32, (tk, t), 1) + (i * t - jd * tk))

    def tile(j, carry, masked):
        m, l, accs = carry
        rows = pl.ds(pl.multiple_of(j * tk, tk), tk)
        cols = []
        for pair in range(2):
            st = _dot_nt(k_ref[0, rows, pair * LANES:(pair + 1) * LANES], qs[pair])
            for hh in range(2):
                head = 2 * pair + hh
                ck = ckb_ref[head, rows, :]
                for c in range(t // LANES):
                    cs = slice(c * LANES, (c + 1) * LANES)
                    sh = st[:, hh * t + c * LANES:hh * t + (c + 1) * LANES] + (ct[head:head + 1, cs] - ck)
                    if masked:
                        sh = jnp.where(valid[:, cs], sh, NEG_INF)
                    cols.append(sh)
        st = jnp.concatenate(cols, axis=1)
        m_new = jnp.maximum(m, jnp.max(st, axis=0, keepdims=True))
        alpha = jnp.exp(m - m_new)
        p = jnp.exp(st - m_new)
        l = alpha * l + jnp.sum(p, axis=0, keepdims=True)
        pb = p.astype(BF16)
        new = []
        for pair in range(2):
            lanes = slice(pair * 2 * t, (pair + 1) * 2 * t)
            pv = jnp.dot(vt_ref[0, j, pair * LANES:(pair + 1) * LANES, :], pb[:, lanes],
                         preferred_element_type=F32)
            new.append(alpha[:, lanes] * accs[pair] + pv)
        return m_new, l, tuple(new)

    init = (jnp.full((1, 4 * t), M_INIT, F32), jnp.zeros((1, 4 * t), F32),
            (jnp.zeros((LANES, 2 * t), F32), jnp.zeros((LANES, 2 * t), F32)))
    carry = lax.fori_loop(0, jd, lambda j, c: tile(j, c, False), init)
    _, l, accs = tile(jd, carry, True)
    outs = []
    for pair in range(2):
        o = accs[pair] / l[:, pair * 2 * t:(pair + 1) * 2 * t]
        outs.append(jnp.concatenate([o[:HEAD_DIM, :t], o[HEAD_DIM:, t:]], axis=0))
    o_ref[0] = jnp.concatenate(outs, axis=0).T.astype(BF16)


def _fox(fq, fk, fv, ct, ccol, t, tk):
    b, s, _ = fq.shape
    nt = s // t
    vt = jnp.transpose(fv.reshape(b, s // tk, tk, 256), (0, 1, 3, 2))
    return pl.pallas_call(
        functools.partial(_fox_kernel, t=t, tk=tk),
        grid=(b, nt),
        in_specs=[pl.BlockSpec((1, t, 256), lambda bi, i: (bi, i, 0)),
                  pl.BlockSpec((1, s, 256), lambda bi, i: (bi, 0, 0)),
                  pl.BlockSpec((1, s // tk, 256, tk), lambda bi, i: (bi, 0, 0, 0)),
                  pl.BlockSpec((1, 8, t), lambda bi, i: (bi, 0, i)),
                  pl.BlockSpec((1, s, LANES), lambda bi, i: (bi, 0, 0))],
        out_specs=pl.BlockSpec((1, t, 256), lambda bi, i: (bi, i, 0)),
        out_shape=jax.ShapeDtypeStruct((b, s, 256), BF16),
        scratch_shapes=[pltpu.VMEM((4, s, LANES), F32)],
        compiler_params=_cparams("arbitrary", "arbitrary"),
        name="fox",
    )(fq, fk, vt, ct, ccol)


def _sb_kernel(q_ref, k_ref, v_ref, u_ref, o_ref, *, tq, tk):
    i = pl.program_id(1)
    q = q_ref[0]
    u = u_ref[...]
    nfull = (i * tq) // tk
    row = lax.broadcasted_iota(I32, (2 * tq, tk), 0)
    col = lax.broadcasted_iota(I32, (2 * tq, tk), 1)
    qrow = jnp.where(row >= tq, row - tq, row)
    pairs = []
    for pair in range(2):
        sl = slice(pair * LANES, (pair + 1) * LANES)
        qs = _split_heads(q[:, sl])

        def tile(j, r, acc, masked):
            start = pl.multiple_of(j * tk, tk)
            kj = k_ref[0, pl.ds(start, tk), sl]
            vj = v_ref[0, pl.ds(start, tk), sl]
            z = _dot_nt(qs, kj)
            sp = jnp.log(1.0 + jnp.exp(-jnp.abs(z)))
            log_beta = jnp.minimum(z, 0.0) - sp
            log_keep = jnp.minimum(-z, 0.0) - sp
            if masked:
                strict = j * tk + col < i * tq + qrow
                log_keep = jnp.where(strict, log_keep, 0.0)
            hi = log_keep.astype(BF16)
            lo = (log_keep - hi.astype(F32)).astype(BF16)
            later = (jnp.dot(hi, u, preferred_element_type=F32)
                     + jnp.dot(lo, u, preferred_element_type=F32))
            a = jnp.exp(log_beta + later + r)
            if masked:
                a = jnp.where(strict, a, 0.0)
            acc = acc + jnp.dot(a.astype(BF16), vj, preferred_element_type=F32)
            r = r + jnp.sum(log_keep, axis=1, keepdims=True)
            return r, acc

        r, acc = tile(nfull, jnp.zeros((2 * tq, 1), F32), jnp.zeros((2 * tq, LANES), F32), True)

        def cond(c):
            return (c[0] >= 0) & (c[1] > 0)

        def body(c):
            j, _, r, acc = c
            r, acc = tile(j, r, acc, False)
            return j - 1, (jnp.max(r) > SB_DEAD).astype(I32), r, acc

        _, _, _, acc = lax.while_loop(cond, body, (nfull - 1, (jnp.max(r) > SB_DEAD).astype(I32), r, acc))
        pairs.append(_merge_heads(acc, tq))
    o_ref[0] = jnp.concatenate(pairs, axis=1).astype(BF16)


def _sb(sq, sk, sv, tq, tk):
    b, s, _ = sq.shape
    u = (jnp.arange(tk)[:, None] > jnp.arange(tk)[None, :]).astype(BF16)
    return pl.pallas_call(
        functools.partial(_sb_kernel, tq=tq, tk=tk),
        grid=(b, s // tq),
        in_specs=[pl.BlockSpec((1, tq, 256), lambda bi, i: (bi, i, 0)),
                  pl.BlockSpec((1, s, 256), lambda bi, i: (bi, 0, 0)),
                  pl.BlockSpec((1, s, 256), lambda bi, i: (bi, 0, 0)),
                  pl.BlockSpec((tk, tk), lambda bi, i: (0, 0))],
        out_specs=pl.BlockSpec((1, tq, 256), lambda bi, i: (bi, i, 0)),
        out_shape=jax.ShapeDtypeStruct((b, s, 256), BF16),
        compiler_params=_cparams("arbitrary", "arbitrary"),
        name="stickbreak",
    )(sq, sk, sv, u)


def _dsa_kernel(q_ref, kk_ref, vt_ref, iq_ref, ikk_ref, wt_ref, bias_ref, lstrict_ref, o_ref,
                key_ref, *, t, top_k):
    i = pl.program_id(1)
    group = 64
    causal = lax.broadcasted_iota(I32, (t, t), 0) <= lax.broadcasted_iota(I32, (t, t), 1)

    def head_stack(x):
        return jnp.concatenate([_split_heads(x[:, :LANES]), _split_heads(x[:, LANES:])], axis=0)

    def key_rows(ref, j0, n):
        return ref[0, pl.ds(pl.multiple_of(j0 * t, t), n * t), :]

    iqs = head_stack(iq_ref[0])
    wt = wt_ref[0]
    w = [wt[4 + h:5 + h, :] * (IDX_HEADS ** -0.5) for h in range(IDX_HEADS)]

    def score_tiles(j0, n, masked):
        lg = _dot_nt(key_rows(ikk_ref, j0, n), iqs)
        sc = w[0] * jnp.maximum(lg[:, 0:t], 0.0)
        for h in range(1, IDX_HEADS):
            sc = sc + w[h] * jnp.maximum(lg[:, h * t:(h + 1) * t], 0.0)
        bits = pltpu.bitcast(sc, I32)
        key = bits ^ ((bits >> 31) & 0x7FFFFFFF)
        key = jnp.where(key == -1, 0, key)
        if masked:
            key = jnp.where(causal, key, INT_MIN)
        for u in range(n):
            key_ref[j0 + u] = key[u * t:(u + 1) * t]

    def p1(p, c):
        score_tiles(2 * p, 2, False)
        return c

    lax.fori_loop(0, i // 2, p1, 0)

    @pl.when(i % 2 == 1)
    def _():
        score_tiles(i - 1, 1, False)

    score_tiles(i, 1, True)

    def count(pred):
        def cb(j, acc):
            for g in range(t // group):
                acc = acc + jnp.where(pred(key_ref[j, g * group:(g + 1) * group, :]), 1.0, 0.0)
            return acc
        acc = lax.fori_loop(0, i + 1, cb, jnp.zeros((group, t), F32))
        return jnp.sum(acc, axis=0, keepdims=True)

    def bis_body(b, thr_u):
        cand_u = thr_u | (jnp.int32(1) << (31 - b))
        n = count(lambda k: k >= (cand_u ^ INT_MIN))
        return jnp.where(n >= top_k, cand_u, thr_u)

    thr_u = lax.fori_loop(0, 32, bis_body, jnp.zeros((1, t), I32))
    thr = jnp.maximum(thr_u ^ INT_MIN, INT_MIN + 1)
    surplus = jnp.max(count(lambda k: k >= thr)) > top_k

    def tie_pass():
        need = top_k - count(lambda k: k > thr)

        def tb(j, seen):
            k = key_ref[j]
            eq = k == thr
            eqf = jnp.where(eq, 1.0, 0.0)
            before = jnp.dot(lstrict_ref[...], eqf.astype(BF16), preferred_element_type=F32) + seen
            sel = (k > thr) | (eq & (before < need))
            key_ref[j] = jnp.where(sel, 1, INT_MIN)
            return seen + jnp.sum(eqf, axis=0, keepdims=True)

        lax.fori_loop(0, i + 1, tb, jnp.zeros((1, t), F32))
        return jnp.zeros((1, t), I32)

    thr = lax.cond(surplus, tie_pass, lambda: thr)

    qs = head_stack(q_ref[0])

    def attn_tiles(j0, n, carry, bias_kind):
        m, l, acc = carry
        st = _dot_nt(key_rows(kk_ref, j0, n), qs)
        selb = jnp.concatenate([jnp.where(key_ref[j0 + u] >= thr, 0.0, NEG_INF) for u in range(n)], axis=0)
        parts = []
        for h in range(4):
            sh = st[:, h * t:(h + 1) * t] + selb
            if bias_kind is not None:
                sh = sh + bias_ref[bias_kind, h]
            parts.append(sh)
        st = jnp.concatenate(parts, axis=1)
        m_new = jnp.maximum(m, jnp.max(st, axis=0, keepdims=True))
        alpha = jnp.exp(m - m_new)
        p = jnp.exp(st - m_new)
        l = alpha * l + jnp.sum(p, axis=0, keepdims=True)
        vt = jnp.concatenate([vt_ref[0, j0 + u] for u in range(n)], axis=1)
        acc = alpha * acc + jnp.dot(vt, p.astype(BF16), preferred_element_type=F32)
        return m_new, l, acc

    init = (jnp.full((1, 4 * t), M_INIT, F32), jnp.zeros((1, 4 * t), F32), jnp.zeros((HEAD_DIM, 4 * t), F32))
    carry = lax.fori_loop(0, jnp.maximum(i - 1, 0) // 2, lambda p, c: attn_tiles(2 * p, 2, c, None), init)
    carry = lax.cond((i >= 2) & (i % 2 == 0), lambda c: attn_tiles(i - 2, 1, c, None), lambda c: c, carry)
    carry = lax.cond(i > 0, lambda c: attn_tiles(i - 1, 1, c, 0), lambda c: c, carry)
    _, l, acc = attn_tiles(i, 1, carry, 1)
    ot = acc / l
    ot = jnp.concatenate([ot[:, h * t:(h + 1) * t] for h in range(4)], axis=0)
    o_ref[0] = ot.T.astype(BF16)


def _dsa_bias(t5_table, t):
    assert t + 1 >= T5_MAX_DISTANCE
    k = jnp.arange(t)[:, None]
    q = jnp.arange(t)[None, :]
    far = t5_table[T5_BUCKETS - 1, 4:].astype(F32)
    tiles = []
    for off in (t, 0):
        dist = off + q - k
        b = jnp.transpose(_t5_lookup(t5_table, dist)[..., 4:], (2, 0, 1)) - far[:, None, None]
        tiles.append(jnp.where((dist >= 0)[None], b, 0.0))
    return jnp.stack(tiles)


def _dsa(dq, dkk, dvv, iq, ikk, wt, bias, top_k, t):
    b, s, _ = dq.shape
    nt = s // t
    lstrict = (jnp.arange(t)[:, None] > jnp.arange(t)[None, :]).astype(BF16)
    vt = jnp.transpose(dvv[:, :, :HEAD_DIM].reshape(b, nt, t, HEAD_DIM), (0, 1, 3, 2))
    blk = lambda w: pl.BlockSpec((1, t, w), lambda bi, i: (bi, i, 0))
    seq = lambda w: pl.BlockSpec((1, s, w), lambda bi, i: (bi, 0, 0))
    return pl.pallas_call(
        functools.partial(_dsa_kernel, t=t, top_k=top_k),
        grid=(b, nt),
        in_specs=[blk(256), seq(LANES),
                  pl.BlockSpec((1, nt, HEAD_DIM, t), lambda bi, i: (bi, 0, 0, 0)),
                  blk(256), seq(LANES),
                  pl.BlockSpec((1, 8, t), lambda bi, i: (bi, 0, i)),
                  pl.BlockSpec(bias.shape, lambda bi, i: (0, 0, 0, 0)),
                  pl.BlockSpec((t, t), lambda bi, i: (0, 0))],
        out_specs=blk(256),
        out_shape=jax.ShapeDtypeStruct((b, s, 256), BF16),
        scratch_shapes=[pltpu.VMEM((nt, t, t), I32)],
        compiler_params=_cparams("arbitrary", "arbitrary"),
        name="dsa",
    )(dq, dkk, vt, iq, ikk, wt, bias, lstrict)


def _merge_kernel(x_ref, gm_ref, wg_ref, oa_ref, of_ref, os_ref, od_ref, wb_ref, wo_ref, gf_ref,
                  wr_ref, br_ref, xo_ref, h2_ref, route_ref):
    x = x_ref[...]
    hb = _rms(x, gm_ref[...]).astype(BF16)
    d = x.shape[1]
    merged = None
    for bi, o_ref in enumerate((oa_ref, of_ref, os_ref, od_ref)):
        gate = jax.nn.sigmoid(jnp.dot(hb, wg_ref[:, bi * d:(bi + 1) * d], preferred_element_type=F32))
        term = gate * jnp.dot(o_ref[...], wb_ref[bi], preferred_element_type=F32)
        merged = term if merged is None else merged + term
    xn = x + jnp.dot(merged.astype(BF16), wo_ref[...], preferred_element_type=F32)
    xo_ref[...] = xn
    h2 = _rms(xn, gf_ref[...])
    h2_ref[...] = h2

    logits = _dot_x3(h2, wr_ref[0], wr_ref[1]) + br_ref[...]
    lane = lax.broadcasted_iota(I32, logits.shape, 1).astype(F32)
    big = 1e9
    gl = jnp.where(lane < N_GROUPS, logits, -jnp.inf)
    gmax = jnp.max(gl, axis=1, keepdims=True)
    grp = jnp.min(jnp.where(gl == gmax, lane, big), axis=1, keepdims=True)
    p_grp = 1.0 / jnp.sum(jnp.exp(gl - gmax), axis=1, keepdims=True)
    first = N_GROUPS + grp * EXPERTS_PER_GROUP
    el = jnp.where((lane >= first) & (lane < first + EXPERTS_PER_GROUP), logits, -jnp.inf)
    l1 = jnp.max(el, axis=1, keepdims=True)
    i1 = jnp.min(jnp.where(el == l1, lane, big), axis=1, keepdims=True)
    el2 = jnp.where(lane == i1, -jnp.inf, el)
    l2 = jnp.max(el2, axis=1, keepdims=True)
    i2 = jnp.min(jnp.where(el2 == l2, lane, big), axis=1, keepdims=True)
    e2 = jnp.exp(l2 - l1)
    g1 = p_grp / (1.0 + e2)
    g2 = p_grp * e2 / (1.0 + e2)
    route = jnp.where(lane == 0, i1 - N_GROUPS,
                      jnp.where(lane == 1, i2 - N_GROUPS,
                                jnp.where(lane == 2, g1, jnp.where(lane == 3, g2, 0.0))))
    route_ref[...] = route


def _merge(x2, gm, wg, o_a, o_f, o_s, o_d, wb, wo, gf, wr, br, tm):
    n, d = x2.shape
    row = lambda w: pl.BlockSpec((tm, w), lambda i: (i, 0))
    full = lambda a: pl.BlockSpec(a.shape, lambda i: (0,) * a.ndim)
    return pl.pallas_call(
        _merge_kernel,
        grid=(n // tm,),
        in_specs=[row(d), full(gm), full(wg), row(256), row(256), row(256), row(256),
                  full(wb), full(wo), full(gf), full(wr), full(br)],
        out_specs=[row(d), row(d), row(LANES)],
        out_shape=[jax.ShapeDtypeStruct((n, d), F32), jax.ShapeDtypeStruct((n, d), F32),
                   jax.ShapeDtypeStruct((n, LANES), F32)],
        compiler_params=_cparams("arbitrary"),
        name="merge",
    )(x2, gm, wg, o_a, o_f, o_s, o_d, wb, wo, gf, wr, br)


def _rank_kernel(route_ref, ltri_ref, rank_ref, cnt_ref, carry_ref):
    @pl.when(pl.program_id(0) == 0)
    def _():
        carry_ref[...] = jnp.zeros_like(carry_ref)

    route = route_ref[...]
    lane = lax.broadcasted_iota(I32, route.shape, 1)
    e0 = route[:, 0:1].astype(I32)
    e1 = route[:, 1:2].astype(I32)
    oh0 = (lane == e0).astype(F32)
    oh1 = (lane == e1).astype(F32)
    both = oh0 + oh1
    before = jnp.dot(ltri_ref[...], both.astype(BF16), preferred_element_type=F32) + carry_ref[0:1, :]
    r0 = jnp.sum(oh0 * before, axis=1, keepdims=True)
    r1 = jnp.sum(oh1 * (before + oh0), axis=1, keepdims=True)
    rank_ref[...] = jnp.where(lane == 0, r0, jnp.where(lane == 1, r1, 0.0))
    total = carry_ref[0:1, :] + jnp.sum(both, axis=0, keepdims=True)
    carry_ref[0:1, :] = total
    cnt_ref[...] = jnp.broadcast_to(total, cnt_ref.shape)


def _rank(route, tm):
    n = route.shape[0]
    ltri = (jnp.arange(tm)[:, None] > jnp.arange(tm)[None, :]).astype(BF16)
    return pl.pallas_call(
        _rank_kernel,
        grid=(n // tm,),
        in_specs=[pl.BlockSpec((tm, LANES), lambda i: (i, 0)), pl.BlockSpec((tm, tm), lambda i: (0, 0))],
        out_specs=[pl.BlockSpec((tm, LANES), lambda i: (i, 0)), pl.BlockSpec((8, LANES), lambda i: (0, 0))],
        out_shape=[jax.ShapeDtypeStruct((n, LANES), F32), jax.ShapeDtypeStruct((8, LANES), F32)],
        scratch_shapes=[pltpu.VMEM((8, LANES), F32)],
        compiler_params=_cparams("arbitrary"),
        name="moe_rank",
    )(route, ltri)


def _expert_kernel(be_ref, nu_ref, tok_ref, tok1_ref, tok2_ref, h_hbm, wup_ref, wdn_ref, y_ref, xbuf, sem, *, te):
    b = pl.program_id(0)
    n_used = nu_ref[0]
    slot = b % 3

    def start_row(tokens_ref, dst, r):
        pltpu.make_async_copy(h_hbm.at[pl.ds(tokens_ref[0, 0, r], 1), :],
                              xbuf.at[dst, pl.ds(r, 1), :], sem.at[dst]).start()

    def start_block(tokens_ref, dst):
        def issue(r, c):
            start_row(tokens_ref, dst, r)
            return c
        lax.fori_loop(0, te, issue, 0, unroll=8)

    def block(prefetch):
        pltpu.make_async_copy(h_hbm.at[pl.ds(0, te), :], xbuf.at[slot], sem.at[slot]).wait()
        xb = xbuf[slot].astype(BF16)
        if prefetch:
            dst = (b + 2) % 3
            for r in range(te):
                start_row(tok2_ref, dst, r)
        gu = jnp.dot(xb, wup_ref[0], preferred_element_type=F32)
        g = gu[:, :EXPERT_FF]
        act = g * jax.nn.sigmoid(g) * gu[:, EXPERT_FF:]
        y_ref[...] = jnp.dot(act.astype(BF16), wdn_ref[0], preferred_element_type=F32)

    @pl.when((b == 0) & (n_used > 0))
    def _():
        start_block(tok_ref, 0)

    @pl.when((b == 0) & (n_used > 1))
    def _():
        start_block(tok1_ref, 1)

    @pl.when(b + 2 < n_used)
    def _():
        block(True)

    @pl.when((b < n_used) & (b + 2 >= n_used))
    def _():
        block(False)

    @pl.when(b >= n_used)
    def _():
        y_ref[...] = jnp.zeros_like(y_ref)


def _experts(blk_expert, n_used, slot_tok, h2, w_up, w_down, te):
    n_blocks = blk_expert.shape[0]
    d = h2.shape[1]
    ahead = lambda k: pl.BlockSpec((1, 1, te), lambda b, be, nu: (jnp.minimum(b + k, n_blocks - 1), 0, 0),
                                   memory_space=pltpu.SMEM)
    grid_spec = pltpu.PrefetchScalarGridSpec(
        num_scalar_prefetch=2,
        grid=(n_blocks,),
        in_specs=[ahead(0), ahead(1), ahead(2),
                  pl.BlockSpec(memory_space=pl.ANY),
                  pl.BlockSpec((1, d, 2 * EXPERT_FF), lambda b, be, nu: (be[b], 0, 0)),
                  pl.BlockSpec((1, EXPERT_FF, d), lambda b, be, nu: (be[b], 0, 0))],
        out_specs=pl.BlockSpec((te, d), lambda b, be, nu: (b, 0)),
        scratch_shapes=[pltpu.VMEM((3, te, d), F32), pltpu.SemaphoreType.DMA((3,))],
    )
    slots = slot_tok.reshape(n_blocks, 1, te)
    return pl.pallas_call(
        functools.partial(_expert_kernel, te=te),
        grid_spec=grid_spec,
        out_shape=jax.ShapeDtypeStruct((n_blocks * te, d), F32),
        compiler_params=_cparams("arbitrary"),
        name="moe_experts",
    )(blk_expert, n_used, slots, slots, slots, h2, w_up, w_down)


def _combine_kernel(pos_ref, pos_next_ref, x_ref, route_ref, y_hbm, o_ref, ybuf, sem, *, tc):
    i = pl.program_id(0)
    slot = i % 2

    def gather(rows_ref, dst):
        def issue(r, c):
            pltpu.make_async_copy(y_hbm.at[pl.ds(rows_ref[0, 0, r], 1), :],
                                  ybuf.at[dst, pl.ds(r, 1), :], sem.at[dst]).start()
            return c
        lax.fori_loop(0, 2 * tc, issue, 0, unroll=8)

    @pl.when(i == 0)
    def _():
        gather(pos_ref, 0)

    @pl.when(i + 1 < pl.num_programs(0))
    def _():
        gather(pos_next_ref, 1 - slot)

    pltpu.make_async_copy(y_hbm.at[pl.ds(0, 2 * tc), :], ybuf.at[slot], sem.at[slot]).wait()
    route = route_ref[...]
    o_ref[...] = (x_ref[...] + route[:, 2:3] * ybuf[slot, 0:tc, :]
                  + route[:, 3:4] * ybuf[slot, tc:2 * tc, :])


def _combine(pos, x2, route, yb, tc):
    n, d = x2.shape
    nt = n // tc
    pos_t = jnp.transpose(pos.reshape(nt, tc, 2), (0, 2, 1)).reshape(nt, 1, 2 * tc)
    return pl.pallas_call(
        functools.partial(_combine_kernel, tc=tc),
        grid=(nt,),
        in_specs=[pl.BlockSpec((1, 1, 2 * tc), lambda i: (i, 0, 0), memory_space=pltpu.SMEM),
                  pl.BlockSpec((1, 1, 2 * tc), lambda i: (jnp.minimum(i + 1, nt - 1), 0, 0),
                               memory_space=pltpu.SMEM),
                  pl.BlockSpec((tc, d), lambda i: (i, 0)),
                  pl.BlockSpec((tc, LANES), lambda i: (i, 0)),
                  pl.BlockSpec(memory_space=pl.ANY)],
        out_specs=pl.BlockSpec((tc, d), lambda i: (i, 0)),
        out_shape=jax.ShapeDtypeStruct((n, d), F32),
        scratch_shapes=[pltpu.VMEM((2, 2 * tc, d), F32), pltpu.SemaphoreType.DMA((2,))],
        compiler_params=_cparams("arbitrary"),
        name="moe_combine",
    )(pos_t, pos_t, x2, route, yb)


def _swap_mid_heads(w, axis):
    h = jnp.split(w, 4, axis=axis)
    return jnp.concatenate([h[0], h[2], h[1], h[3]], axis=axis)


def _t5_lookup(t5_table, dist):
    onehot = (_t5_bucket(dist)[..., None] == jnp.arange(T5_BUCKETS)).astype(F32)
    return jnp.einsum("...b,bh->...h", onehot, t5_table.astype(F32), precision=lax.Precision.HIGHEST)


def _t5_bucket(dist):
    n = jnp.maximum(dist, 0)
    max_exact = T5_BUCKETS // 2
    nf = jnp.maximum(n, 1).astype(F32)
    large = max_exact + (jnp.log(nf / max_exact) / math.log(T5_MAX_DISTANCE / max_exact)
                         * (T5_BUCKETS - max_exact)).astype(I32)
    large = jnp.minimum(large, T5_BUCKETS - 1)
    return jnp.where(n < max_exact, n, large)


def _swa_bias(t5_table):
    t = SWA_BLOCK
    dist = t + jnp.arange(t)[:, None] - jnp.arange(2 * t)[None, :]
    tile = jnp.transpose(_t5_lookup(t5_table, dist)[..., :4], (2, 0, 1))
    valid = (dist >= 0) & (dist < t)
    return jnp.where(valid[None], tile, NEG_INF)


def _layer_weights(w_in, qk_gain, forget_bias, w_branch):
    offs = np.concatenate([[0], np.cumsum(IN_SPLITS)]).tolist()
    part = lambda k: w_in[:, offs[k]:offs[k + 1]]
    dup = lambda w: jnp.concatenate([w, w], axis=1)
    aq = _swap_mid_heads(part(0), 1)
    cols = [aq, part(1), part(2), part(3), part(4), part(5), part(7), part(8), part(9),
            part(10), dup(part(11)), dup(part(12)), part(13), dup(part(14))]
    w1 = jnp.concatenate(cols, axis=1).astype(BF16)
    d = w_in.shape[0]
    wm = _hi_lo(jnp.concatenate([part(6), part(15), jnp.zeros((d, LANES - 8), F32)], axis=1))
    wg = part(16).astype(BF16)
    tile = lambda g, reps, scale: jnp.pad(jnp.tile(g, reps) * scale, (0, 256 - reps * HEAD_DIM))
    gains = jnp.stack([tile(qk_gain[0, 0], 4, ATTN_SCALE), tile(qk_gain[0, 1], 2, 1.0),
                       tile(qk_gain[1, 0], 4, ATTN_SCALE), tile(qk_gain[1, 1], 4, 1.0),
                       tile(qk_gain[2, 0], 4, ATTN_SCALE), tile(qk_gain[2, 1], 2, 1.0),
                       jnp.zeros((256,), F32), jnp.zeros((256,), F32)]).astype(F32)
    fb = jnp.pad(forget_bias.astype(F32), (0, LANES - 4)).reshape(1, LANES)
    wb = jnp.stack([_swap_mid_heads(w_branch[0], 0), w_branch[1], w_branch[2], w_branch[3]]).astype(BF16)
    return w1, wm, wg, gains, fb, wb


def kernel(x, norm_mix_g, w_in, forget_bias, attn_sinks, qk_gain, w_branch, w_out, t5_table, norm_ffn_g,
           w_router_group, b_router_group, w_router_expert, b_router_expert, w_expert_up, w_expert_down):
    b, s, d = x.shape
    n = b * s
    depth = w_in.shape[0]
    top_k = min(DSA_TOPK_MAX, s // 4)
    tm_proj = min(512, s)
    fox_t, fox_tk = min(256, s), min(512, s)
    sb_t = min(256, s)
    dsa_t = min(256, s)
    te = 256
    tc = 128

    gseg = (jnp.arange(256)[:, None] // HEAD_DIM == jnp.arange(256)[None, :] // HEAD_DIM).astype(BF16)
    bias_swa = _swa_bias(t5_table)
    bias_dsa = _dsa_bias(t5_table, dsa_t)
    n_blocks = -(-2 * n // te) + N_EXPERTS
    tok_ids = jnp.repeat(jnp.arange(n, dtype=I32), 2)

    for layer in range(depth):
        w1, wm, wg, gains, fb, wb = _layer_weights(w_in[layer], qk_gain[layer], forget_bias[layer],
                                                   w_branch[layer])
        sinks = jnp.broadcast_to(jnp.pad(attn_sinks[layer].astype(F32), (0, 4))[:, None], (8, LANES))
        (aq, ak, av, fq, fk, fv, sq, sk, sv, dq, dkk, dvv, iq, ikk, cm) = _proj(
            x, norm_mix_g[layer].reshape(1, d), w1, wm, gseg, gains, fb, tm_proj)

        o_swa = _swa(aq, ak, av, bias_swa, sinks)
        cmt = jnp.transpose(cm[:, :, :8], (0, 2, 1))
        o_fox = _fox(fq, fk, fv, cmt, cm, fox_t, fox_tk)
        o_sb = _sb(sq, sk, sv, sb_t, sb_t)
        o_dsa = _dsa(dq, dkk, dvv, iq, ikk, cmt, bias_dsa, top_k, dsa_t)

        wr = _hi_lo(jnp.concatenate([w_router_group[layer], w_router_expert[layer],
                                     jnp.zeros((d, LANES - N_GROUPS - N_EXPERTS), F32)], axis=1))
        br = jnp.concatenate([b_router_group[layer], b_router_expert[layer],
                              jnp.zeros((LANES - N_GROUPS - N_EXPERTS,), F32)]).reshape(1, LANES)
        x2, h2, route = _merge(
            x.reshape(n, d), norm_mix_g[layer].reshape(1, d), wg,
            o_swa.reshape(n, 256), o_fox.reshape(n, 256), o_sb.reshape(n, 256), o_dsa.reshape(n, 256),
            wb, w_out[layer].astype(BF16), norm_ffn_g[layer].reshape(1, d), wr, br, min(256, n))

        rank, cnt = _rank(route, min(512, n))
        counts = cnt[0, :N_EXPERTS].astype(I32)
        padded = (counts + te - 1) // te * te
        pend = jnp.cumsum(padded)
        pstart = pend - padded
        expert = route[:, :2].astype(I32)
        own = expert[:, :, None] == jnp.arange(N_EXPERTS, dtype=I32)
        pos = jnp.sum(jnp.where(own, pstart, 0), axis=-1) + rank[:, :2].astype(I32)
        slot_tok = jnp.zeros((n_blocks * te,), I32).at[pos.reshape(-1)].set(tok_ids)
        blk_start = jnp.arange(n_blocks, dtype=I32)[:, None] * te
        blk_expert = jnp.minimum(jnp.sum((pend[None, :] <= blk_start).astype(I32), axis=1), N_EXPERTS - 1)
        n_used = (pend[-1:] // te).astype(I32)

        yb = _experts(blk_expert, n_used, slot_tok, h2, w_expert_up[layer].astype(BF16),
                      w_expert_down[layer].astype(BF16), te)
        x = _combine(pos, x2, route, yb, tc).reshape(b, s, d)
    return x
```

```python
import functools
import math

import jax
import jax.numpy as jnp
import numpy as np
from jax import lax
from jax.experimental import pallas as pl
from jax.experimental.pallas import tpu as pltpu

F32 = jnp.float32
BF16 = jnp.bfloat16
I32 = jnp.int32

HEAD_DIM = 64
LANES = 128
NORM_EPS = 1e-6
NEG_INF = -1e30
M_INIT = -1e29
ATTN_SCALE = HEAD_DIM ** -0.5
SWA_BLOCK = 128
IDX_SCALE = 64 ** -0.5
IDX_HEADS = 4
DSA_TOPK_MAX = 256
T5_BUCKETS = 32
T5_MAX_DISTANCE = 128
N_GROUPS = 4
EXPERTS_PER_GROUP = 8
N_EXPERTS = N_GROUPS * EXPERTS_PER_GROUP
EXPERT_FF = 512
SB_DEAD = -110.0
INT_MIN = -2 ** 31
VMEM_LIMIT = 56 * 1024 * 1024

IN_SPLITS = (256, 128, 128, 256, 256, 256, 4, 256, 256, 256, 256, 64, 64, 256, 64, 4, 4096)

_SEG = dict(aq=(0, 256), ak=(256, 128), av=(384, 128), fq=(512, 256), fk=(768, 256), fv=(1024, 256),
            sq=(1280, 256), sk=(1536, 256), sv=(1792, 256), dq=(2048, 256), dkk=(2304, 128),
            dvv=(2432, 128), iq=(2560, 256), ikk=(2816, 128))
_W1_COLS = 2944
_SEG_ORDER = ("aq", "ak", "av", "fq", "fk", "fv", "sq", "sk", "sv", "dq", "dkk", "dvv", "iq", "ikk")


def _cparams(*sem):
    return pltpu.CompilerParams(dimension_semantics=sem, vmem_limit_bytes=VMEM_LIMIT)


def _rms(x, g):
    return x * lax.rsqrt(jnp.mean(x * x, axis=-1, keepdims=True) + NORM_EPS) * g


def _log_sigmoid(z):
    return jnp.minimum(z, 0.0) - jnp.log(1.0 + jnp.exp(-jnp.abs(z)))


def _dot_nt(a, b):
    return lax.dot_general(a, b, (((1,), (1,)), ((), ())), preferred_element_type=F32)


def _split3(x):
    p1 = x.astype(BF16)
    r = x - p1.astype(F32)
    p2 = r.astype(BF16)
    return p1, p2, (r - p2.astype(F32)).astype(BF16)


def _hi_lo(w):
    hi = w.astype(BF16)
    return jnp.stack([hi, (w - hi.astype(F32)).astype(BF16)])


def _dot_x3(a, b_hi, b_lo):
    a_hi = a.astype(BF16)
    a_lo = (a - a_hi.astype(F32)).astype(BF16)
    return (jnp.dot(a_hi, b_hi, preferred_element_type=F32) + jnp.dot(a_lo, b_hi, preferred_element_type=F32)
            + jnp.dot(a_hi, b_lo, preferred_element_type=F32))


def _split_heads(qp):
    lo = lax.broadcasted_iota(I32, (1, LANES), 1) < HEAD_DIM
    zero = jnp.zeros_like(qp)
    return jnp.concatenate([jnp.where(lo, qp, zero), jnp.where(lo, zero, qp)], axis=0)


def _merge_heads(o, t):
    lo = lax.broadcasted_iota(I32, (1, LANES), 1) < HEAD_DIM
    return jnp.where(lo, o[:t], o[t:])


def _proj_kernel(x_ref, g_ref, w1_ref, wm_ref, gseg_ref, gains_ref, fb_ref, ltri_ref, *rest):
    outs = dict(zip(_SEG_ORDER, rest[:len(_SEG_ORDER)]))
    cm_ref = rest[len(_SEG_ORDER)]
    carry_ref = rest[len(_SEG_ORDER) + 1]

    @pl.when(pl.program_id(1) == 0)
    def _():
        carry_ref[...] = jnp.zeros_like(carry_ref)

    h = _rms(x_ref[0], g_ref[...])
    hb = h.astype(BF16)

    def seg(name):
        off, width = _SEG[name]
        return jnp.dot(hb, w1_ref[:, off:off + width], preferred_element_type=F32)

    def head_norm(t, row):
        width = t.shape[1]
        ssq = jnp.dot((t * t).astype(BF16), gseg_ref[:width, :width], preferred_element_type=F32)
        return t * lax.rsqrt(ssq * (1.0 / HEAD_DIM) + NORM_EPS) * gains_ref[row:row + 1, :width]

    normed = dict(aq=0, ak=1, fq=2, fk=3, dq=4, dkk=5)
    scaled = dict(sq=ATTN_SCALE, iq=IDX_SCALE)
    for name in _SEG_ORDER:
        t = seg(name)
        if name in normed:
            t = head_norm(t, normed[name])
        elif name in scaled:
            t = t * scaled[name]
        outs[name][0] = t.astype(BF16)

    misc = _dot_x3(h, wm_ref[0], wm_ref[1])
    lane = lax.broadcasted_iota(I32, misc.shape, 1)
    logf = jnp.where(lane < 4, _log_sigmoid(misc + fb_ref[...]), 0.0)
    ltri = ltri_ref[...]
    c = carry_ref[0:1, :]
    for piece in _split3(logf):
        c = c + jnp.dot(ltri, piece, preferred_element_type=F32)
    tm = misc.shape[0]
    carry_ref[0:1, :] = c[tm - 1:tm, :]
    cm_ref[0] = jnp.where(lane < 4, c, misc)


def _proj(x, g, w1, wm, gseg, gains, fb, tm):
    b, s, d = x.shape
    ltri = jnp.tril(jnp.ones((tm, tm), BF16))
    full = lambda shape: pl.BlockSpec(shape, lambda bi, si: (0,) * len(shape))
    out_shapes = [jax.ShapeDtypeStruct((b, s, _SEG[n][1]), BF16) for n in _SEG_ORDER]
    out_shapes.append(jax.ShapeDtypeStruct((b, s, LANES), F32))
    out_specs = [pl.BlockSpec((1, tm, _SEG[n][1]), lambda bi, si: (bi, si, 0)) for n in _SEG_ORDER]
    out_specs.append(pl.BlockSpec((1, tm, LANES), lambda bi, si: (bi, si, 0)))
    return pl.pallas_call(
        _proj_kernel,
        grid=(b, s // tm),
        in_specs=[pl.BlockSpec((1, tm, d), lambda bi, si: (bi, si, 0)),
                  full((1, d)), full(w1.shape), full(wm.shape), full(gseg.shape),
                  full(gains.shape), full(fb.shape), full((tm, tm))],
        out_specs=out_specs,
        out_shape=out_shapes,
        scratch_shapes=[pltpu.VMEM((8, LANES), F32)],
        compiler_params=_cparams("arbitrary", "arbitrary"),
        name="proj",
    )(x, g, w1, wm, gseg, gains, fb, ltri)


def _swa_kernel(q_ref, kp_ref, kc_ref, vp_ref, vc_ref, bias_ref, sink_ref, o_ref):
    i = pl.program_id(1)
    t = SWA_BLOCK
    q = q_ref[0]
    kcat = jnp.concatenate([kp_ref[0], kc_ref[0]], axis=0)
    vcat = jnp.concatenate([vp_ref[0], vc_ref[0]], axis=0)
    col = lax.broadcasted_iota(I32, (t, 2 * t), 1)
    no_prev = (col < t) & (i == 0)
    pairs = []
    for pair in range(2):
        qs = _split_heads(q[:, pair * LANES:(pair + 1) * LANES])
        s = _dot_nt(qs, kcat)
        ps = []
        for hh in range(2):
            head = pair + 2 * hh
            sh = s[hh * t:(hh + 1) * t] + bias_ref[head]
            sh = jnp.where(no_prev, NEG_INF, sh)
            sink = sink_ref[head:head + 1, 0:1]
            m = jnp.maximum(jnp.max(sh, axis=1, keepdims=True), sink)
            p = jnp.exp(sh - m)
            denom = jnp.sum(p, axis=1, keepdims=True) + jnp.exp(sink - m)
            ps.append(p / denom)
        o = jnp.dot(jnp.concatenate(ps, axis=0).astype(BF16), vcat, preferred_element_type=F32)
        pairs.append(_merge_heads(o, t))
    o_ref[0] = jnp.concatenate(pairs, axis=1).astype(BF16)


def _swa(aq, ak, av, bias, sinks):
    b, s, _ = aq.shape
    t = SWA_BLOCK
    cur = lambda bi, i: (bi, i, 0)
    prev = lambda bi, i: (bi, jnp.maximum(i - 1, 0), 0)
    return pl.pallas_call(
        _swa_kernel,
        grid=(b, s // t),
        in_specs=[pl.BlockSpec((1, t, 256), cur),
                  pl.BlockSpec((1, t, LANES), prev), pl.BlockSpec((1, t, LANES), cur),
                  pl.BlockSpec((1, t, LANES), prev), pl.BlockSpec((1, t, LANES), cur),
                  pl.BlockSpec(bias.shape, lambda bi, i: (0, 0, 0)),
                  pl.BlockSpec(sinks.shape, lambda bi, i: (0, 0))],
        out_specs=pl.BlockSpec((1, t, 256), cur),
        out_shape=jax.ShapeDtypeStruct((b, s, 256), BF16),
        compiler_params=_cparams("arbitrary", "arbitrary"),
        name="swa",
    )(aq, ak, ak, av, av, bias, sinks)


def _fox_kernel(q_ref, k_ref, vt_ref, ct_ref, ccol_ref, o_ref, ckb_ref, *, t, tk):
    i = pl.program_id(1)
    n_tiles = ckb_ref.shape[1] // t

    @pl.when(i == 0)
    def _():
        def fill(j, c):
            rows = pl.ds(pl.multiple_of(j * t, t), t)
            cc = ccol_ref[0, rows, :]
            for h in range(4):
                ckb_ref[h, rows, :] = jnp.broadcast_to(cc[:, h:h + 1], (t, LANES))
            return c
        lax.fori_loop(0, n_tiles, fill, 0)

    q = q_ref[0]
    ct = ct_ref[0]
    qs = [_split_heads(q[:, :LANES]), _split_heads(q[:, LANES:])]
    jd = (i * t) // tk
    valid = (lax.broadcasted_iota(I32, (tk, t), 0)
             <= lax.broadcasted_iota(I32, (tk, t), 1) + (i * t - jd * tk))

    def tile(j, carry, masked):
        m, l, accs = carry
        rows = pl.ds(pl.multiple_of(j * tk, tk), tk)
        cols = []
        for pair in range(2):
            st = _dot_nt(k_ref[0, rows, pair * LANES:(pair + 1) * LANES], qs[pair])
            for hh in range(2):
                head = 2 * pair + hh
                ck = ckb_ref[head, rows, :]
                for c in range(t // LANES):
                    cs = slice(c * LANES, (c + 1) * LANES)
                    sh = st[:, hh * t + c * LANES:hh * t + (c + 1) * LANES] + (ct[head:head + 1, cs] - ck)
                    if masked:
                        sh = jnp.where(valid[:, cs], sh, NEG_INF)
                    cols.append(sh)
        st = jnp.concatenate(cols, axis=1)
        m_new = jnp.maximum(m, jnp.max(st, axis=0, keepdims=True))
        alpha = jnp.exp(m - m_new)
        p = jnp.exp(st - m_new)
        l = alpha * l + jnp.sum(p, axis=0, keepdims=True)
        pb = p.astype(BF16)
        new = []
        for pair in range(2):
            lanes = slice(pair * 2 * t, (pair + 1) * 2 * t)
            pv = jnp.dot(vt_ref[0, j, pair * LANES:(pair + 1) * LANES, :], pb[:, lanes],
                         preferred_element_type=F32)
            new.append(alpha[:, lanes] * accs[pair] + pv)
        return m_new, l, tuple(new)

    init = (jnp.full((1, 4 * t), M_INIT, F32), jnp.zeros((1, 4 * t), F32),
            (jnp.zeros((LANES, 2 * t), F32), jnp.zeros((LANES, 2 * t), F32)))
    carry = lax.fori_loop(0, jd, lambda j, c: tile(j, c, False), init)
    _, l, accs = tile(jd, carry, True)
    outs = []
    for pair in range(2):
        o = accs[pair] / l[:, pair * 2 * t:(pair + 1) * 2 * t]
        outs.append(jnp.concatenate([o[:HEAD_DIM, :t], o[HEAD_DIM:, t:]], axis=0))
    o_ref[0] = jnp.concatenate(outs, axis=0).T.astype(BF16)


def _fox(fq, fk, fv, ct, ccol, t, tk):
    b, s, _ = fq.shape
    nt = s // t
    vt = jnp.transpose(fv.reshape(b, s // tk, tk, 256), (0, 1, 3, 2))
    return pl.pallas_call(
        functools.partial(_fox_kernel, t=t, tk=tk),
        grid=(b, nt),
        in_specs=[pl.BlockSpec((1, t, 256), lambda bi, i: (bi, i, 0)),
                  pl.BlockSpec((1, s, 256), lambda bi, i: (bi, 0, 0)),
                  pl.BlockSpec((1, s // tk, 256, tk), lambda bi, i: (bi, 0, 0, 0)),
                  pl.BlockSpec((1, 8, t), lambda bi, i: (bi, 0, i)),
                  pl.BlockSpec((1, s, LANES), lambda bi, i: (bi, 0, 0))],
        out_specs=pl.BlockSpec((1, t, 256), lambda bi, i: (bi, i, 0)),
        out_shape=jax.ShapeDtypeStruct((b, s, 256), BF16),
        scratch_shapes=[pltpu.VMEM((4, s, LANES), F32)],
        compiler_params=_cparams("arbitrary", "arbitrary"),
        name="fox",
    )(fq, fk, vt, ct, ccol)


def _sb_kernel(q_ref, k_ref, v_ref, u_ref, o_ref, *, tq, tk):
    i = pl.program_id(1)
    q = q_ref[0]
    u = u_ref[...]
    nfull = (i * tq) // tk
    row = lax.broadcasted_iota(I32, (2 * tq, tk), 0)
    col = lax.broadcasted_iota(I32, (2 * tq, tk), 1)
    qrow = jnp.where(row >= tq, row - tq, row)
    pairs = []
    for pair in range(2):
        sl = slice(pair * LANES, (pair + 1) * LANES)
        qs = _split_heads(q[:, sl])

        def tile(j, r, acc, masked):
            start = pl.multiple_of(j * tk, tk)
            kj = k_ref[0, pl.ds(start, tk), sl]
            vj = v_ref[0, pl.ds(start, tk), sl]
            z = _dot_nt(qs, kj)
            sp = jnp.log(1.0 + jnp.exp(-jnp.abs(z)))
            log_beta = jnp.minimum(z, 0.0) - sp
            log_keep = jnp.minimum(-z, 0.0) - sp
            if masked:
                strict = j * tk + col < i * tq + qrow
                log_keep = jnp.where(strict, log_keep, 0.0)
            hi = log_keep.astype(BF16)
            lo = (log_keep - hi.astype(F32)).astype(BF16)
            later = (jnp.dot(hi, u, preferred_element_type=F32)
                     + jnp.dot(lo, u, preferred_element_type=F32))
            a = jnp.exp(log_beta + later + r)
            if masked:
                a = jnp.where(strict, a, 0.0)
            acc = acc + jnp.dot(a.astype(BF16), vj, preferred_element_type=F32)
            r = r + jnp.sum(log_keep, axis=1, keepdims=True)
            return r, acc

        r, acc = tile(nfull, jnp.zeros((2 * tq, 1), F32), jnp.zeros((2 * tq, LANES), F32), True)

        def cond(c):
            return (c[0] >= 0) & (c[1] > 0)

        def body(c):
            j, _, r, acc = c
            r, acc = tile(j, r, acc, False)
            return j - 1, (jnp.max(r) > SB_DEAD).astype(I32), r, acc

        _, _, _, acc = lax.while_loop(cond, body, (nfull - 1, (jnp.max(r) > SB_DEAD).astype(I32), r, acc))
        pairs.append(_merge_heads(acc, tq))
    o_ref[0] = jnp.concatenate(pairs, axis=1).astype(BF16)


def _sb(sq, sk, sv, tq, tk):
    b, s, _ = sq.shape
    u = (jnp.arange(tk)[:, None] > jnp.arange(tk)[None, :]).astype(BF16)
    return pl.pallas_call(
        functools.partial(_sb_kernel, tq=tq, tk=tk),
        grid=(b, s // tq),
        in_specs=[pl.BlockSpec((1, tq, 256), lambda bi, i: (bi, i, 0)),
                  pl.BlockSpec((1, s, 256), lambda bi, i: (bi, 0, 0)),
                  pl.BlockSpec((1, s, 256), lambda bi, i: (bi, 0, 0)),
                  pl.BlockSpec((tk, tk), lambda bi, i: (0, 0))],
        out_specs=pl.BlockSpec((1, tq, 256), lambda bi, i: (bi, i, 0)),
        out_shape=jax.ShapeDtypeStruct((b, s, 256), BF16),
        compiler_params=_cparams("arbitrary", "arbitrary"),
        name="stickbreak",
    )(sq, sk, sv, u)


def _bit_planes(words):
    words = list(words)
    j, m = 16, 0x0000FFFF
    while j:
        k = 0
        while k < 32:
            tt = (words[k] ^ lax.shift_right_logical(words[k + j], jnp.full_like(words[k + j], j))) & m
            words[k] = words[k] ^ tt
            words[k + j] = words[k + j] ^ (tt << j)
            k = (k + j + 1) & ~j
        j >>= 1
        m = (m ^ (m << j)) & 0xFFFFFFFF
    return words


def _dsa_kernel(q_ref, kk_ref, vt_ref, iq_ref, ikk_ref, wt_ref, bias_ref, lstrict_ref, o_ref,
                key_ref, plane_ref, *, t, top_k):
    i = pl.program_id(1)
    assert t == 8 * 32

    @pl.when(i == 0)
    def _():
        plane_ref[...] = jnp.zeros_like(plane_ref)
    causal = lax.broadcasted_iota(I32, (t, t), 0) <= lax.broadcasted_iota(I32, (t, t), 1)

    def head_stack(x):
        return jnp.concatenate([_split_heads(x[:, :LANES]), _split_heads(x[:, LANES:])], axis=0)

    def key_rows(ref, j0, n):
        return ref[0, pl.ds(pl.multiple_of(j0 * t, t), n * t), :]

    iqs = head_stack(iq_ref[0])
    wt = wt_ref[0]
    w = [wt[4 + h:5 + h, :] * (IDX_HEADS ** -0.5) for h in range(IDX_HEADS)]

    def score_tiles(j0, n, masked):
        lg = _dot_nt(key_rows(ikk_ref, j0, n), iqs)
        sc = w[0] * jnp.maximum(lg[:, 0:t], 0.0)
        for h in range(1, IDX_HEADS):
            sc = sc + w[h] * jnp.maximum(lg[:, h * t:(h + 1) * t], 0.0)
        bits = pltpu.bitcast(sc, I32)
        key = bits ^ ((bits >> 31) & 0x7FFFFFFF)
        key = jnp.where(key == -1, 0, key)
        if masked:
            key = jnp.where(causal, key, INT_MIN)
        for u in range(n):
            key_u = key[u * t:(u + 1) * t]
            key_ref[j0 + u] = key_u
            for p, plane in enumerate(_bit_planes([key_u[8 * g:8 * g + 8, :] for g in range(32)])):
                plane_ref[p, j0 + u] = plane

    def p1(p, c):
        score_tiles(2 * p, 2, False)
        return c

    lax.fori_loop(0, i // 2, p1, 0)

    @pl.when(i % 2 == 1)
    def _():
        score_tiles(i - 1, 1, False)

    score_tiles(i, 1, True)

    def popcount_rows(words):
        per_tile = jnp.sum(lax.population_count(words), axis=0)
        return jnp.sum(per_tile.astype(F32), axis=0, keepdims=True)

    def bis_body(p, c):
        alive, n_gt, thr_u = c
        plane = plane_ref[p] ^ jnp.where(p == 0, -1, 0)
        ones = alive & plane
        cnt = popcount_rows(ones)
        take = n_gt + cnt >= top_k
        alive = jnp.where(take, ones, alive ^ ones)
        n_gt = jnp.where(take, n_gt, n_gt + cnt)
        thr_u = jnp.where(take, thr_u | (jnp.int32(1) << (31 - p)), thr_u)
        return alive, n_gt, thr_u

    n_tiles = key_ref.shape[0]
    alive0 = jnp.where(lax.broadcasted_iota(I32, (n_tiles, 8, t), 0) <= i, -1, 0)
    alive, n_gt, thr_u = lax.fori_loop(
        0, 32, bis_body, (alive0, jnp.zeros((1, t), F32), jnp.zeros((1, t), I32)))
    thr = jnp.maximum(thr_u ^ INT_MIN, INT_MIN + 1)
    n_avail = (i * t + lax.broadcasted_iota(I32, (1, t), 1) + 1).astype(F32)
    n_ge = jnp.where(n_avail > top_k, n_gt + popcount_rows(alive), 0.0)
    surplus = jnp.max(n_ge) > top_k

    def tie_pass():
        need = top_k - n_gt

        def tb(j, seen):
            k = key_ref[j]
            eq = k == thr
            eqf = jnp.where(eq, 1.0, 0.0)
            before = jnp.dot(lstrict_ref[...], eqf.astype(BF16), preferred_element_type=F32) + seen
            sel = (k > thr) | (eq & (before < need))
            key_ref[j] = jnp.where(sel, 1, INT_MIN)
            return seen + jnp.sum(eqf, axis=0, keepdims=True)

        lax.fori_loop(0, i + 1, tb, jnp.zeros((1, t), F32))
        return jnp.zeros((1, t), I32)

    thr = lax.cond(surplus, tie_pass, lambda: thr)

    qs = head_stack(q_ref[0])

    def attn_tiles(j0, n, carry, bias_kind):
        m, l, acc = carry
        st = _dot_nt(key_rows(kk_ref, j0, n), qs)
        selb = jnp.concatenate([jnp.where(key_ref[j0 + u] >= thr, 0.0, NEG_INF) for u in range(n)], axis=0)
        parts = []
        for h in range(4):
            sh = st[:, h * t:(h + 1) * t] + selb
            if bias_kind is not None:
                sh = sh + bias_ref[bias_kind, h]
            parts.append(sh)
        st = jnp.concatenate(parts, axis=1)
        m_new = jnp.maximum(m, jnp.max(st, axis=0, keepdims=True))
        alpha = jnp.exp(m - m_new)
        p = jnp.exp(st - m_new)
        l = alpha * l + jnp.sum(p, axis=0, keepdims=True)
        vt = jnp.concatenate([vt_ref[0, j0 + u] for u in range(n)], axis=1)
        acc = alpha * acc + jnp.dot(vt, p.astype(BF16), preferred_element_type=F32)
        return m_new, l, acc

    init = (jnp.full((1, 4 * t), M_INIT, F32), jnp.zeros((1, 4 * t), F32), jnp.zeros((HEAD_DIM, 4 * t), F32))
    carry = lax.fori_loop(0, jnp.maximum(i - 1, 0) // 2, lambda p, c: attn_tiles(2 * p, 2, c, None), init)
    carry = lax.cond((i >= 2) & (i % 2 == 0), lambda c: attn_tiles(i - 2, 1, c, None), lambda c: c, carry)
    carry = lax.cond(i > 0, lambda c: attn_tiles(i - 1, 1, c, 0), lambda c: c, carry)
    _, l, acc = attn_tiles(i, 1, carry, 1)
    ot = acc / l
    ot = jnp.concatenate([ot[:, h * t:(h + 1) * t] for h in range(4)], axis=0)
    o_ref[0] = ot.T.astype(BF16)


def _dsa_bias(t5_table, t):
    assert t + 1 >= T5_MAX_DISTANCE
    k = jnp.arange(t)[:, None]
    q = jnp.arange(t)[None, :]
    far = t5_table[T5_BUCKETS - 1, 4:].astype(F32)
    tiles = []
    for off in (t, 0):
        dist = off + q - k
        b = jnp.transpose(_t5_lookup(t5_table, dist)[..., 4:], (2, 0, 1)) - far[:, None, None]
        tiles.append(jnp.where((dist >= 0)[None], b, 0.0))
    return jnp.stack(tiles)


def _dsa(dq, dkk, dvv, iq, ikk, wt, bias, top_k, t):
    b, s, _ = dq.shape
    nt = s // t
    lstrict = (jnp.arange(t)[:, None] > jnp.arange(t)[None, :]).astype(BF16)
    vt = jnp.transpose(dvv[:, :, :HEAD_DIM].reshape(b, nt, t, HEAD_DIM), (0, 1, 3, 2))
    blk = lambda w: pl.BlockSpec((1, t, w), lambda bi, i: (bi, i, 0))
    seq = lambda w: pl.BlockSpec((1, s, w), lambda bi, i: (bi, 0, 0))
    return pl.pallas_call(
        functools.partial(_dsa_kernel, t=t, top_k=top_k),
        grid=(b, nt),
        in_specs=[blk(256), seq(LANES),
                  pl.BlockSpec((1, nt, HEAD_DIM, t), lambda bi, i: (bi, 0, 0, 0)),
                  blk(256), seq(LANES),
                  pl.BlockSpec((1, 8, t), lambda bi, i: (bi, 0, i)),
                  pl.BlockSpec(bias.shape, lambda bi, i: (0, 0, 0, 0)),
                  pl.BlockSpec((t, t), lambda bi, i: (0, 0))],
        out_specs=blk(256),
        out_shape=jax.ShapeDtypeStruct((b, s, 256), BF16),
        scratch_shapes=[pltpu.VMEM((nt, t, t), I32), pltpu.VMEM((32, nt, 8, t), I32)],
        compiler_params=_cparams("arbitrary", "arbitrary"),
        name="dsa",
    )(dq, dkk, vt, iq, ikk, wt, bias, lstrict)


def _merge_kernel(x_ref, gm_ref, wg_ref, oa_ref, of_ref, os_ref, od_ref, wb_ref, wo_ref, gf_ref,
                  wr_ref, br_ref, xo_ref, h2_ref, route_ref):
    x = x_ref[...]
    hb = _rms(x, gm_ref[...]).astype(BF16)
    d = x.shape[1]
    merged = None
    for bi, o_ref in enumerate((oa_ref, of_ref, os_ref, od_ref)):
        gate = jax.nn.sigmoid(jnp.dot(hb, wg_ref[:, bi * d:(bi + 1) * d], preferred_element_type=F32))
        term = gate * jnp.dot(o_ref[...], wb_ref[bi], preferred_element_type=F32)
        merged = term if merged is None else merged + term
    xn = x + jnp.dot(merged.astype(BF16), wo_ref[...], preferred_element_type=F32)
    xo_ref[...] = xn
    h2 = _rms(xn, gf_ref[...])
    h2_ref[...] = h2

    logits = _dot_x3(h2, wr_ref[0], wr_ref[1]) + br_ref[...]
    lane = lax.broadcasted_iota(I32, logits.shape, 1).astype(F32)
    big = 1e9
    gl = jnp.where(lane < N_GROUPS, logits, -jnp.inf)
    gmax = jnp.max(gl, axis=1, keepdims=True)
    grp = jnp.min(jnp.where(gl == gmax, lane, big), axis=1, keepdims=True)
    p_grp = 1.0 / jnp.sum(jnp.exp(gl - gmax), axis=1, keepdims=True)
    first = N_GROUPS + grp * EXPERTS_PER_GROUP
    el = jnp.where((lane >= first) & (lane < first + EXPERTS_PER_GROUP), logits, -jnp.inf)
    l1 = jnp.max(el, axis=1, keepdims=True)
    i1 = jnp.min(jnp.where(el == l1, lane, big), axis=1, keepdims=True)
    el2 = jnp.where(lane == i1, -jnp.inf, el)
    l2 = jnp.max(el2, axis=1, keepdims=True)
    i2 = jnp.min(jnp.where(el2 == l2, lane, big), axis=1, keepdims=True)
    e2 = jnp.exp(l2 - l1)
    g1 = p_grp / (1.0 + e2)
    g2 = p_grp * e2 / (1.0 + e2)
    route = jnp.where(lane == 0, i1 - N_GROUPS,
                      jnp.where(lane == 1, i2 - N_GROUPS,
                                jnp.where(lane == 2, g1, jnp.where(lane == 3, g2, 0.0))))
    route_ref[...] = route


def _merge(x2, gm, wg, o_a, o_f, o_s, o_d, wb, wo, gf, wr, br, tm):
    n, d = x2.shape
    row = lambda w: pl.BlockSpec((tm, w), lambda i: (i, 0))
    full = lambda a: pl.BlockSpec(a.shape, lambda i: (0,) * a.ndim)
    return pl.pallas_call(
        _merge_kernel,
        grid=(n // tm,),
        in_specs=[row(d), full(gm), full(wg), row(256), row(256), row(256), row(256),
                  full(wb), full(wo), full(gf), full(wr), full(br)],
        out_specs=[row(d), row(d), row(LANES)],
        out_shape=[jax.ShapeDtypeStruct((n, d), F32), jax.ShapeDtypeStruct((n, d), F32),
                   jax.ShapeDtypeStruct((n, LANES), F32)],
        compiler_params=_cparams("arbitrary"),
        name="merge",
    )(x2, gm, wg, o_a, o_f, o_s, o_d, wb, wo, gf, wr, br)


def _rank_kernel(route_ref, ltri_ref, rank_ref, cnt_ref, carry_ref):
    @pl.when(pl.program_id(0) == 0)
    def _():
        carry_ref[...] = jnp.zeros_like(carry_ref)

    route = route_ref[...]
    lane = lax.broadcasted_iota(I32, route.shape, 1)
    e0 = route[:, 0:1].astype(I32)
    e1 = route[:, 1:2].astype(I32)
    oh0 = (lane == e0).astype(F32)
    oh1 = (lane == e1).astype(F32)
    both = oh0 + oh1
    before = jnp.dot(ltri_ref[...], both.astype(BF16), preferred_element_type=F32) + carry_ref[0:1, :]
    r0 = jnp.sum(oh0 * before, axis=1, keepdims=True)
    r1 = jnp.sum(oh1 * (before + oh0), axis=1, keepdims=True)
    rank_ref[...] = jnp.where(lane == 0, r0, jnp.where(lane == 1, r1, 0.0))
    total = carry_ref[0:1, :] + jnp.sum(both, axis=0, keepdims=True)
    carry_ref[0:1, :] = total
    cnt_ref[...] = jnp.broadcast_to(total, cnt_ref.shape)


def _rank(route, tm):
    n = route.shape[0]
    ltri = (jnp.arange(tm)[:, None] > jnp.arange(tm)[None, :]).astype(BF16)
    return pl.pallas_call(
        _rank_kernel,
        grid=(n // tm,),
        in_specs=[pl.BlockSpec((tm, LANES), lambda i: (i, 0)), pl.BlockSpec((tm, tm), lambda i: (0, 0))],
        out_specs=[pl.BlockSpec((tm, LANES), lambda i: (i, 0)), pl.BlockSpec((8, LANES), lambda i: (0, 0))],
        out_shape=[jax.ShapeDtypeStruct((n, LANES), F32), jax.ShapeDtypeStruct((8, LANES), F32)],
        scratch_shapes=[pltpu.VMEM((8, LANES), F32)],
        compiler_params=_cparams("arbitrary"),
        name="moe_rank",
    )(route, ltri)


def _expert_kernel(be_ref, nu_ref, tok_ref, tok1_ref, tok2_ref, h_hbm, wup_ref, wdn_ref, y_ref, xbuf, sem, *, te):
    b = pl.program_id(0)
    n_used = nu_ref[0]
    slot = b % 3

    def start_row(tokens_ref, dst, r):
        pltpu.make_async_copy(h_hbm.at[pl.ds(tokens_ref[0, 0, r], 1), :],
                              xbuf.at[dst, pl.ds(r, 1), :], sem.at[dst]).start()

    def start_block(tokens_ref, dst):
        def issue(r, c):
            start_row(tokens_ref, dst, r)
            return c
        lax.fori_loop(0, te, issue, 0, unroll=8)

    def block(prefetch):
        pltpu.make_async_copy(h_hbm.at[pl.ds(0, te), :], xbuf.at[slot], sem.at[slot]).wait()
        xb = xbuf[slot].astype(BF16)
        if prefetch:
            dst = (b + 2) % 3
            for r in range(te):
                start_row(tok2_ref, dst, r)
        gu = jnp.dot(xb, wup_ref[0], preferred_element_type=F32)
        g = gu[:, :EXPERT_FF]
        act = g * jax.nn.sigmoid(g) * gu[:, EXPERT_FF:]
        y_ref[...] = jnp.dot(act.astype(BF16), wdn_ref[0], preferred_element_type=F32)

    @pl.when((b == 0) & (n_used > 0))
    def _():
        start_block(tok_ref, 0)

    @pl.when((b == 0) & (n_used > 1))
    def _():
        start_block(tok1_ref, 1)

    @pl.when(b + 2 < n_used)
    def _():
        block(True)

    @pl.when((b < n_used) & (b + 2 >= n_used))
    def _():
        block(False)

    @pl.when(b >= n_used)
    def _():
        y_ref[...] = jnp.zeros_like(y_ref)


def _experts(blk_expert, n_used, slot_tok, h2, w_up, w_down, te):
    n_blocks = blk_expert.shape[0]
    d = h2.shape[1]
    ahead = lambda k: pl.BlockSpec((1, 1, te), lambda b, be, nu: (jnp.minimum(b + k, n_blocks - 1), 0, 0),
                                   memory_space=pltpu.SMEM)
    grid_spec = pltpu.PrefetchScalarGridSpec(
        num_scalar_prefetch=2,
        grid=(n_blocks,),
        in_specs=[ahead(0), ahead(1), ahead(2),
                  pl.BlockSpec(memory_space=pl.ANY),
                  pl.BlockSpec((1, d, 2 * EXPERT_FF), lambda b, be, nu: (be[b], 0, 0)),
                  pl.BlockSpec((1, EXPERT_FF, d), lambda b, be, nu: (be[b], 0, 0))],
        out_specs=pl.BlockSpec((te, d), lambda b, be, nu: (b, 0)),
        scratch_shapes=[pltpu.VMEM((3, te, d), F32), pltpu.SemaphoreType.DMA((3,))],
    )
    slots = slot_tok.reshape(n_blocks, 1, te)
    return pl.pallas_call(
        functools.partial(_expert_kernel, te=te),
        grid_spec=grid_spec,
        out_shape=jax.ShapeDtypeStruct((n_blocks * te, d), F32),
        compiler_params=_cparams("arbitrary"),
        name="moe_experts",
    )(blk_expert, n_used, slots, slots, slots, h2, w_up, w_down)


def _combine_kernel(pos_ref, pos_next_ref, x_ref, route_ref, y_hbm, o_ref, ybuf, sem, *, tc):
    i = pl.program_id(0)
    slot = i % 2

    def gather(rows_ref, dst):
        def issue(r, c):
            pltpu.make_async_copy(y_hbm.at[pl.ds(rows_ref[0, 0, r], 1), :],
                                  ybuf.at[dst, pl.ds(r, 1), :], sem.at[dst]).start()
            return c
        lax.fori_loop(0, 2 * tc, issue, 0, unroll=8)

    @pl.when(i == 0)
    def _():
        gather(pos_ref, 0)

    @pl.when(i + 1 < pl.num_programs(0))
    def _():
        gather(pos_next_ref, 1 - slot)

    pltpu.make_async_copy(y_hbm.at[pl.ds(0, 2 * tc), :], ybuf.at[slot], sem.at[slot]).wait()
    route = route_ref[...]
    o_ref[...] = (x_ref[...] + route[:, 2:3] * ybuf[slot, 0:tc, :]
                  + route[:, 3:4] * ybuf[slot, tc:2 * tc, :])


def _combine(pos, x2, route, yb, tc):
    n, d = x2.shape
    nt = n // tc
    pos_t = jnp.transpose(pos.reshape(nt, tc, 2), (0, 2, 1)).reshape(nt, 1, 2 * tc)
    return pl.pallas_call(
        functools.partial(_combine_kernel, tc=tc),
        grid=(nt,),
        in_specs=[pl.BlockSpec((1, 1, 2 * tc), lambda i: (i, 0, 0), memory_space=pltpu.SMEM),
                  pl.BlockSpec((1, 1, 2 * tc), lambda i: (jnp.minimum(i + 1, nt - 1), 0, 0),
                               memory_space=pltpu.SMEM),
                  pl.BlockSpec((tc, d), lambda i: (i, 0)),
                  pl.BlockSpec((tc, LANES), lambda i: (i, 0)),
                  pl.BlockSpec(memory_space=pl.ANY)],
        out_specs=pl.BlockSpec((tc, d), lambda i: (i, 0)),
        out_shape=jax.ShapeDtypeStruct((n, d), F32),
        scratch_shapes=[pltpu.VMEM((2, 2 * tc, d), F32), pltpu.SemaphoreType.DMA((2,))],
        compiler_params=_cparams("arbitrary"),
        name="moe_combine",
    )(pos_t, pos_t, x2, route, yb)


def _swap_mid_heads(w, axis):
    h = jnp.split(w, 4, axis=axis)
    return jnp.concatenate([h[0], h[2], h[1], h[3]], axis=axis)


def _t5_lookup(t5_table, dist):
    onehot = (_t5_bucket(dist)[..., None] == jnp.arange(T5_BUCKETS)).astype(F32)
    return jnp.einsum("...b,bh->...h", onehot, t5_table.astype(F32), precision=lax.Precision.HIGHEST)


def _t5_bucket(dist):
    n = jnp.maximum(dist, 0)
    max_exact = T5_BUCKETS // 2
    nf = jnp.maximum(n, 1).astype(F32)
    large = max_exact + (jnp.log(nf / max_exact) / math.log(T5_MAX_DISTANCE / max_exact)
                         * (T5_BUCKETS - max_exact)).astype(I32)
    large = jnp.minimum(large, T5_BUCKETS - 1)
    return jnp.where(n < max_exact, n, large)


def _swa_bias(t5_table):
    t = SWA_BLOCK
    dist = t + jnp.arange(t)[:, None] - jnp.arange(2 * t)[None, :]
    tile = jnp.transpose(_t5_lookup(t5_table, dist)[..., :4], (2, 0, 1))
    valid = (dist >= 0) & (dist < t)
    return jnp.where(valid[None], tile, NEG_INF)


def _layer_weights(w_in, qk_gain, forget_bias, w_branch):
    offs = np.concatenate([[0], np.cumsum(IN_SPLITS)]).tolist()
    part = lambda k: w_in[:, offs[k]:offs[k + 1]]
    dup = lambda w: jnp.concatenate([w, w], axis=1)
    aq = _swap_mid_heads(part(0), 1)
    cols = [aq, part(1), part(2), part(3), part(4), part(5), part(7), part(8), part(9),
            part(10), dup(part(11)), dup(part(12)), part(13), dup(part(14))]
    w1 = jnp.concatenate(cols, axis=1).astype(BF16)
    d = w_in.shape[0]
    wm = _hi_lo(jnp.concatenate([part(6), part(15), jnp.zeros((d, LANES - 8), F32)], axis=1))
    wg = part(16).astype(BF16)
    tile = lambda g, reps, scale: jnp.pad(jnp.tile(g, reps) * scale, (0, 256 - reps * HEAD_DIM))
    gains = jnp.stack([tile(qk_gain[0, 0], 4, ATTN_SCALE), tile(qk_gain[0, 1], 2, 1.0),
                       tile(qk_gain[1, 0], 4, ATTN_SCALE), tile(qk_gain[1, 1], 4, 1.0),
                       tile(qk_gain[2, 0], 4, ATTN_SCALE), tile(qk_gain[2, 1], 2, 1.0),
                       jnp.zeros((256,), F32), jnp.zeros((256,), F32)]).astype(F32)
    fb = jnp.pad(forget_bias.astype(F32), (0, LANES - 4)).reshape(1, LANES)
    wb = jnp.stack([_swap_mid_heads(w_branch[0], 0), w_branch[1], w_branch[2], w_branch[3]]).astype(BF16)
    return w1, wm, wg, gains, fb, wb


def kernel(x, norm_mix_g, w_in, forget_bias, attn_sinks, qk_gain, w_branch, w_out, t5_table, norm_ffn_g,
           w_router_group, b_router_group, w_router_expert, b_router_expert, w_expert_up, w_expert_down):
    b, s, d = x.shape
    n = b * s
    depth = w_in.shape[0]
    top_k = min(DSA_TOPK_MAX, s // 4)
    tm_proj = min(512, s)
    fox_t, fox_tk = min(256, s), min(512, s)
    sb_t = min(256, s)
    dsa_t = min(256, s)
    te = 256
    tc = 128

    gseg = (jnp.arange(256)[:, None] // HEAD_DIM == jnp.arange(256)[None, :] // HEAD_DIM).astype(BF16)
    bias_swa = _swa_bias(t5_table)
    bias_dsa = _dsa_bias(t5_table, dsa_t)
    n_blocks = -(-2 * n // te) + N_EXPERTS
    tok_ids = jnp.repeat(jnp.arange(n, dtype=I32), 2)

    for layer in range(depth):
        w1, wm, wg, gains, fb, wb = _layer_weights(w_in[layer], qk_gain[layer], forget_bias[layer],
                                                   w_branch[layer])
        sinks = jnp.broadcast_to(jnp.pad(attn_sinks[layer].astype(F32), (0, 4))[:, None], (8, LANES))
        (aq, ak, av, fq, fk, fv, sq, sk, sv, dq, dkk, dvv, iq, ikk, cm) = _proj(
            x, norm_mix_g[layer].reshape(1, d), w1, wm, gseg, gains, fb, tm_proj)

        o_swa = _swa(aq, ak, av, bias_swa, sinks)
        cmt = jnp.transpose(cm[:, :, :8], (0, 2, 1))
        o_fox = _fox(fq, fk, fv, cmt, cm, fox_t, fox_tk)
        o_sb = _sb(sq, sk, sv, sb_t, sb_t)
        o_dsa = _dsa(dq, dkk, dvv, iq, ikk, cmt, bias_dsa, top_k, dsa_t)

        wr = _hi_lo(jnp.concatenate([w_router_group[layer], w_router_expert[layer],
                                     jnp.zeros((d, LANES - N_GROUPS - N_EXPERTS), F32)], axis=1))
        br = jnp.concatenate([b_router_group[layer], b_router_expert[layer],
                              jnp.zeros((LANES - N_GROUPS - N_EXPERTS,), F32)]).reshape(1, LANES)
        x2, h2, route = _merge(
            x.reshape(n, d), norm_mix_g[layer].reshape(1, d), wg,
            o_swa.reshape(n, 256), o_fox.reshape(n, 256), o_sb.reshape(n, 256), o_dsa.reshape(n, 256),
            wb, w_out[layer].astype(BF16), norm_ffn_g[layer].reshape(1, d), wr, br, min(256, n))

        rank, cnt = _rank(route, min(512, n))
        counts = cnt[0, :N_EXPERTS].astype(I32)
        padded = (counts + te - 1) // te * te
        pend = jnp.cumsum(padded)
        pstart = pend - padded
        expert = route[:, :2].astype(I32)
        own = expert[:, :, None] == jnp.arange(N_EXPERTS, dtype=I32)
        pos = jnp.sum(jnp.where(own, pstart, 0), axis=-1) + rank[:, :2].astype(I32)
        slot_tok = jnp.zeros((n_blocks * te,), I32).at[pos.reshape(-1)].set(tok_ids)
        blk_start = jnp.arange(n_blocks, dtype=I32)[:, None] * te
        blk_expert = jnp.minimum(jnp.sum((pend[None, :] <= blk_start).astype(I32), axis=1), N_EXPERTS - 1)
        n_used = (pend[-1:] // te).astype(I32)

        yb = _experts(blk_expert, n_used, slot_tok, h2, w_expert_up[layer].astype(BF16),
                      w_expert_down[layer].astype(BF16), te)
        x = _combine(pos, x2, route, yb, tc).reshape(b, s, d)
    return x
```

```python
import functools
import math

import jax
import jax.numpy as jnp
import numpy as np
from jax import lax
from jax.experimental import pallas as pl
from jax.experimental.pallas import tpu as pltpu

F32 = jnp.float32
BF16 = jnp.bfloat16
I32 = jnp.int32

HEAD_DIM = 64
LANES = 128
NORM_EPS = 1e-6
NEG_INF = -1e30
M_INIT = -1e29
ATTN_SCALE = HEAD_DIM ** -0.5
SWA_BLOCK = 128
IDX_SCALE = 64 ** -0.5
IDX_HEADS = 4
DSA_TOPK_MAX = 256
T5_BUCKETS = 32
T5_MAX_DISTANCE = 128
N_GROUPS = 4
EXPERTS_PER_GROUP = 8
N_EXPERTS = N_GROUPS * EXPERTS_PER_GROUP
EXPERT_FF = 512
SB_DEAD = -110.0
INT_MIN = -2 ** 31
VMEM_LIMIT = 56 * 1024 * 1024

IN_SPLITS = (256, 128, 128, 256, 256, 256, 4, 256, 256, 256, 256, 64, 64, 256, 64, 4, 4096)

_SEG = dict(aq=(0, 256), ak=(256, 128), av=(384, 128), fq=(512, 256), fk=(768, 256), fv=(1024, 256),
            sq=(1280, 256), sk=(1536, 256), sv=(1792, 256), dq=(2048, 256), dkk=(2304, 128),
            dvv=(2432, 128), iq=(2560, 256), ikk=(2816, 128))
_W1_COLS = 2944
_SEG_ORDER = ("aq", "ak", "av", "fq", "fk", "fv", "sq", "sk", "sv", "dq", "dkk", "dvv", "iq", "ikk")


def _cparams(*sem):
    return pltpu.CompilerParams(dimension_semantics=sem, vmem_limit_bytes=VMEM_LIMIT)


def _rms(x, g):
    return x * lax.rsqrt(jnp.mean(x * x, axis=-1, keepdims=True) + NORM_EPS) * g


def _log_sigmoid(z):
    return jnp.minimum(z, 0.0) - jnp.log(1.0 + jnp.exp(-jnp.abs(z)))


def _dot_nt(a, b):
    return lax.dot_general(a, b, (((1,), (1,)), ((), ())), preferred_element_type=F32)


def _split3(x):
    p1 = x.astype(BF16)
    r = x - p1.astype(F32)
    p2 = r.astype(BF16)
    return p1, p2, (r - p2.astype(F32)).astype(BF16)


def _hi_lo(w):
    hi = w.astype(BF16)
    return jnp.stack([hi, (w - hi.astype(F32)).astype(BF16)])


def _dot_x3(a, b_hi, b_lo):
    a_hi = a.astype(BF16)
    a_lo = (a - a_hi.astype(F32)).astype(BF16)
    return (jnp.dot(a_hi, b_hi, preferred_element_type=F32) + jnp.dot(a_lo, b_hi, preferred_element_type=F32)
            + jnp.dot(a_hi, b_lo, preferred_element_type=F32))


def _split_heads(qp):
    lo = lax.broadcasted_iota(I32, (1, LANES), 1) < HEAD_DIM
    zero = jnp.zeros_like(qp)
    return jnp.concatenate([jnp.where(lo, qp, zero), jnp.where(lo, zero, qp)], axis=0)


def _merge_heads(o, t):
    lo = lax.broadcasted_iota(I32, (1, LANES), 1) < HEAD_DIM
    return jnp.where(lo, o[:t], o[t:])


def _proj_kernel(x_ref, g_ref, w1_ref, wm_ref, gseg_ref, gains_ref, fb_ref, ltri_ref, *rest):
    outs = dict(zip(_SEG_ORDER, rest[:len(_SEG_ORDER)]))
    cm_ref = rest[len(_SEG_ORDER)]
    carry_ref = rest[len(_SEG_ORDER) + 1]

    @pl.when(pl.program_id(1) == 0)
    def _():
        carry_ref[...] = jnp.zeros_like(carry_ref)

    h = _rms(x_ref[0], g_ref[...])
    hb = h.astype(BF16)

    def seg(name):
        off, width = _SEG[name]
        return jnp.dot(hb, w1_ref[:, off:off + width], preferred_element_type=F32)

    def head_norm(t, row):
        width = t.shape[1]
        ssq = jnp.dot((t * t).astype(BF16), gseg_ref[:width, :width], preferred_element_type=F32)
        return t * lax.rsqrt(ssq * (1.0 / HEAD_DIM) + NORM_EPS) * gains_ref[row:row + 1, :width]

    normed = dict(aq=0, ak=1, fq=2, fk=3, dq=4, dkk=5)
    scaled = dict(sq=ATTN_SCALE, iq=IDX_SCALE)
    for name in _SEG_ORDER:
        t = seg(name)
        if name in normed:
            t = head_norm(t, normed[name])
        elif name in scaled:
            t = t * scaled[name]
        outs[name][0] = t.astype(BF16)

    misc = _dot_x3(h, wm_ref[0], wm_ref[1])
    lane = lax.broadcasted_iota(I32, misc.shape, 1)
    logf = jnp.where(lane < 4, _log_sigmoid(misc + fb_ref[...]), 0.0)
    ltri = ltri_ref[...]
    c = carry_ref[0:1, :]
    for piece in _split3(logf):
        c = c + jnp.dot(ltri, piece, preferred_element_type=F32)
    tm = misc.shape[0]
    carry_ref[0:1, :] = c[tm - 1:tm, :]
    cm_ref[0] = jnp.where(lane < 4, c, misc)


def _proj(x, g, w1, wm, gseg, gains, fb, tm):
    b, s, d = x.shape
    ltri = jnp.tril(jnp.ones((tm, tm), BF16))
    full = lambda shape: pl.BlockSpec(shape, lambda bi, si: (0,) * len(shape))
    out_shapes = [jax.ShapeDtypeStruct((b, s, _SEG[n][1]), BF16) for n in _SEG_ORDER]
    out_shapes.append(jax.ShapeDtypeStruct((b, s, LANES), F32))
    out_specs = [pl.BlockSpec((1, tm, _SEG[n][1]), lambda bi, si: (bi, si, 0)) for n in _SEG_ORDER]
    out_specs.append(pl.BlockSpec((1, tm, LANES), lambda bi, si: (bi, si, 0)))
    return pl.pallas_call(
        _proj_kernel,
        grid=(b, s // tm),
        in_specs=[pl.BlockSpec((1, tm, d), lambda bi, si: (bi, si, 0)),
                  full((1, d)), full(w1.shape), full(wm.shape), full(gseg.shape),
                  full(gains.shape), full(fb.shape), full((tm, tm))],
        out_specs=out_specs,
        out_shape=out_shapes,
        scratch_shapes=[pltpu.VMEM((8, LANES), F32)],
        compiler_params=_cparams("arbitrary", "arbitrary"),
        name="proj",
    )(x, g, w1, wm, gseg, gains, fb, ltri)


def _swa_kernel(q_ref, kp_ref, kc_ref, vp_ref, vc_ref, bias_ref, sink_ref, o_ref):
    i = pl.program_id(1)
    t = SWA_BLOCK
    q = q_ref[0]
    kcat = jnp.concatenate([kp_ref[0], kc_ref[0]], axis=0)
    vcat = jnp.concatenate([vp_ref[0], vc_ref[0]], axis=0)
    col = lax.broadcasted_iota(I32, (t, 2 * t), 1)
    no_prev = (col < t) & (i == 0)
    pairs = []
    for pair in range(2):
        qs = _split_heads(q[:, pair * LANES:(pair + 1) * LANES])
        s = _dot_nt(qs, kcat)
        ps = []
        for hh in range(2):
            head = pair + 2 * hh
            sh = s[hh * t:(hh + 1) * t] + bias_ref[head]
            sh = jnp.where(no_prev, NEG_INF, sh)
            sink = sink_ref[head:head + 1, 0:1]
            m = jnp.maximum(jnp.max(sh, axis=1, keepdims=True), sink)
            p = jnp.exp(sh - m)
            denom = jnp.sum(p, axis=1, keepdims=True) + jnp.exp(sink - m)
            ps.append(p / denom)
        o = jnp.dot(jnp.concatenate(ps, axis=0).astype(BF16), vcat, preferred_element_type=F32)
        pairs.append(_merge_heads(o, t))
    o_ref[0] = jnp.concatenate(pairs, axis=1).astype(BF16)


def _swa(aq, ak, av, bias, sinks):
    b, s, _ = aq.shape
    t = SWA_BLOCK
    cur = lambda bi, i: (bi, i, 0)
    prev = lambda bi, i: (bi, jnp.maximum(i - 1, 0), 0)
    return pl.pallas_call(
        _swa_kernel,
        grid=(b, s // t),
        in_specs=[pl.BlockSpec((1, t, 256), cur),
                  pl.BlockSpec((1, t, LANES), prev), pl.BlockSpec((1, t, LANES), cur),
                  pl.BlockSpec((1, t, LANES), prev), pl.BlockSpec((1, t, LANES), cur),
                  pl.BlockSpec(bias.shape, lambda bi, i: (0, 0, 0)),
                  pl.BlockSpec(sinks.shape, lambda bi, i: (0, 0))],
        out_specs=pl.BlockSpec((1, t, 256), cur),
        out_shape=jax.ShapeDtypeStruct((b, s, 256), BF16),
        compiler_params=_cparams("arbitrary", "arbitrary"),
        name="swa",
    )(aq, ak, ak, av, av, bias, sinks)


def _fox_kernel(q_ref, k_ref, vt_ref, ct_ref, ccol_ref, o_ref, ckb_ref, *, t, tk):
    i = pl.program_id(1)
    n_tiles = ckb_ref.shape[1] // t

    @pl.when(i == 0)
    def _():
        def fill(j, c):
            rows = pl.ds(pl.multiple_of(j * t, t), t)
            cc = ccol_ref[0, rows, :]
            for h in range(4):
                ckb_ref[h, rows, :] = jnp.broadcast_to(cc[:, h:h + 1], (t, LANES))
            return c
        lax.fori_loop(0, n_tiles, fill, 0)

    q = q_ref[0]
    ct = ct_ref[0]
    qs = [_split_heads(q[:, :LANES]), _split_heads(q[:, LANES:])]
    jd = (i * t) // tk
    valid = (lax.broadcasted_iota(I32, (tk, t), 0)
             <= lax.broadcasted_iota(I32, (tk, t), 1) + (i * t - jd * tk))

    def tile(j, carry, masked):
        m, l, accs = carry
        rows = pl.ds(pl.multiple_of(j * tk, tk), tk)
        cols = []
        for pair in range(2):
            st = _dot_nt(k_ref[0, rows, pair * LANES:(pair + 1) * LANES], qs[pair])
            for hh in range(2):
                head = 2 * pair + hh
                ck = ckb_ref[head, rows, :]
                for c in range(t // LANES):
                    cs = slice(c * LANES, (c + 1) * LANES)
                    sh = st[:, hh * t + c * LANES:hh * t + (c + 1) * LANES] + (ct[head:head + 1, cs] - ck)
                    if masked:
                        sh = jnp.where(valid[:, cs], sh, NEG_INF)
                    cols.append(sh)
        st = jnp.concatenate(cols, axis=1)
        m_new = jnp.maximum(m, jnp.max(st, axis=0, keepdims=True))
        alpha = jnp.exp(m - m_new)
        p = jnp.exp(st - m_new)
        l = alpha * l + jnp.sum(p, axis=0, keepdims=True)
        pb = p.astype(BF16)
        new = []
        for pair in range(2):
            lanes = slice(pair * 2 * t, (pair + 1) * 2 * t)
            pv = jnp.dot(vt_ref[0, j, pair * LANES:(pair + 1) * LANES, :], pb[:, lanes],
                         preferred_element_type=F32)
            new.append(alpha[:, lanes] * accs[pair] + pv)
        return m_new, l, tuple(new)

    init = (jnp.full((1, 4 * t), M_INIT, F32), jnp.zeros((1, 4 * t), F32),
            (jnp.zeros((LANES, 2 * t), F32), jnp.zeros((LANES, 2 * t), F32)))
    carry = lax.fori_loop(0, jd, lambda j, c: tile(j, c, False), init)
    _, l, accs = tile(jd, carry, True)
    outs = []
    for pair in range(2):
        o = accs[pair] / l[:, pair * 2 * t:(pair + 1) * 2 * t]
        outs.append(jnp.concatenate([o[:HEAD_DIM, :t], o[HEAD_DIM:, t:]], axis=0))
    o_ref[0] = jnp.concatenate(outs, axis=0).T.astype(BF16)


def _fox(fq, fk, fv, ct, ccol, t, tk):
    b, s, _ = fq.shape
    nt = s // t
    vt = jnp.transpose(fv.reshape(b, s // tk, tk, 256), (0, 1, 3, 2))
    return pl.pallas_call(
        functools.partial(_fox_kernel, t=t, tk=tk),
        grid=(b, nt),
        in_specs=[pl.BlockSpec((1, t, 256), lambda bi, i: (bi, i, 0)),
                  pl.BlockSpec((1, s, 256), lambda bi, i: (bi, 0, 0)),
                  pl.BlockSpec((1, s // tk, 256, tk), lambda bi, i: (bi, 0, 0, 0)),
                  pl.BlockSpec((1, 8, t), lambda bi, i: (bi, 0, i)),
                  pl.BlockSpec((1, s, LANES), lambda bi, i: (bi, 0, 0))],
        out_specs=pl.BlockSpec((1, t, 256), lambda bi, i: (bi, i, 0)),
        out_shape=jax.ShapeDtypeStruct((b, s, 256), BF16),
        scratch_shapes=[pltpu.VMEM((4, s, LANES), F32)],
        compiler_params=_cparams("arbitrary", "arbitrary"),
        name="fox",
    )(fq, fk, vt, ct, ccol)


def _sb_kernel(q_ref, k_ref, vt_ref, lgt_ref, o_ref, *, t):
    i = pl.program_id(1)
    q = q_ref[0]
    lgt = lgt_ref[...]
    qs = [_split_heads(q[:, :LANES]), _split_heads(q[:, LANES:])]
    key_i = lax.broadcasted_iota(I32, (t, 4 * t), 0)
    query_i = lax.broadcasted_iota(I32, (t, 4 * t), 1) & (t - 1)
    strict = key_i < query_i

    def tile(j, r, accs, masked):
        rows = pl.ds(pl.multiple_of(j * t, t), t)
        z = jnp.concatenate([_dot_nt(k_ref[0, rows, p * LANES:(p + 1) * LANES], qs[p]) for p in range(2)],
                            axis=1)
        sp = jnp.log(1.0 + jnp.exp(-jnp.abs(z)))
        log_beta = jnp.minimum(z, 0.0) - sp
        log_keep = jnp.minimum(-z, 0.0) - sp
        if masked:
            log_keep = jnp.where(strict, log_keep, 0.0)
        hi = log_keep.astype(BF16)
        lo = (log_keep - hi.astype(F32)).astype(BF16)
        later = (jnp.dot(lgt, hi, preferred_element_type=F32)
                 + jnp.dot(lgt, lo, preferred_element_type=F32))
        a = jnp.exp(log_beta + later + r)
        if masked:
            a = jnp.where(strict, a, 0.0)
        ab = a.astype(BF16)
        new = tuple(accs[p] + jnp.dot(vt_ref[0, j, p * LANES:(p + 1) * LANES, :],
                                      ab[:, p * 2 * t:(p + 1) * 2 * t], preferred_element_type=F32)
                    for p in range(2))
        return r + jnp.sum(log_keep, axis=0, keepdims=True), new

    zero_acc = jnp.zeros((LANES, 2 * t), F32)
    r, accs = tile(i, jnp.zeros((1, 4 * t), F32), (zero_acc, zero_acc), True)

    def cond(c):
        return (c[0] >= 0) & (c[1] > 0)

    def body(c):
        j, _, r, accs = c
        r, accs = tile(j, r, accs, False)
        return j - 1, (jnp.max(r) > SB_DEAD).astype(I32), r, accs

    _, _, _, accs = lax.while_loop(cond, body, (i - 1, (jnp.max(r) > SB_DEAD).astype(I32), r, accs))
    outs = [jnp.concatenate([accs[p][:HEAD_DIM, :t], accs[p][HEAD_DIM:, t:]], axis=0) for p in range(2)]
    o_ref[0] = jnp.concatenate(outs, axis=0).T.astype(BF16)


def _sb(sq, sk, sv, t):
    b, s, _ = sq.shape
    nt = s // t
    lgt = (jnp.arange(t)[:, None] < jnp.arange(t)[None, :]).astype(BF16)
    vt = jnp.transpose(sv.reshape(b, nt, t, 256), (0, 1, 3, 2))
    return pl.pallas_call(
        functools.partial(_sb_kernel, t=t),
        grid=(b, nt),
        in_specs=[pl.BlockSpec((1, t, 256), lambda bi, i: (bi, i, 0)),
                  pl.BlockSpec((1, s, 256), lambda bi, i: (bi, 0, 0)),
                  pl.BlockSpec((1, nt, 256, t), lambda bi, i: (bi, 0, 0, 0)),
                  pl.BlockSpec((t, t), lambda bi, i: (0, 0))],
        out_specs=pl.BlockSpec((1, t, 256), lambda bi, i: (bi, i, 0)),
        out_shape=jax.ShapeDtypeStruct((b, s, 256), BF16),
        compiler_params=_cparams("arbitrary", "arbitrary"),
        name="stickbreak",
    )(sq, sk, vt, lgt)


def _bit_planes(words):
    words = list(words)
    j, m = 16, 0x0000FFFF
    while j:
        k = 0
        while k < 32:
            tt = (words[k] ^ lax.shift_right_logical(words[k + j], jnp.full_like(words[k + j], j))) & m
            words[k] = words[k] ^ tt
            words[k + j] = words[k + j] ^ (tt << j)
            k = (k + j + 1) & ~j
        j >>= 1
        m = (m ^ (m << j)) & 0xFFFFFFFF
    return words


def _dsa_kernel(q_ref, kk_ref, vt_ref, iq_ref, ikk_ref, wt_ref, bias_ref, lstrict_ref, o_ref,
                key_ref, plane_ref, *, t, top_k):
    i = pl.program_id(1)
    assert t == 8 * 32

    @pl.when(i == 0)
    def _():
        plane_ref[...] = jnp.zeros_like(plane_ref)
    causal = lax.broadcasted_iota(I32, (t, t), 0) <= lax.broadcasted_iota(I32, (t, t), 1)

    def head_stack(x):
        return jnp.concatenate([_split_heads(x[:, :LANES]), _split_heads(x[:, LANES:])], axis=0)

    def key_rows(ref, j0, n):
        return ref[0, pl.ds(pl.multiple_of(j0 * t, t), n * t), :]

    iqs = head_stack(iq_ref[0])
    wt = wt_ref[0]
    w = [wt[4 + h:5 + h, :] * (IDX_HEADS ** -0.5) for h in range(IDX_HEADS)]

    def score_tiles(j0, n, masked):
        lg = _dot_nt(key_rows(ikk_ref, j0, n), iqs)
        sc = w[0] * jnp.maximum(lg[:, 0:t], 0.0)
        for h in range(1, IDX_HEADS):
            sc = sc + w[h] * jnp.maximum(lg[:, h * t:(h + 1) * t], 0.0)
        bits = pltpu.bitcast(sc, I32)
        key = bits ^ ((bits >> 31) & 0x7FFFFFFF)
        key = jnp.where(key == -1, 0, key)
        if masked:
            key = jnp.where(causal, key, INT_MIN)
        for u in range(n):
            key_u = key[u * t:(u + 1) * t]
            key_ref[j0 + u] = key_u
            for p, plane in enumerate(_bit_planes([key_u[8 * g:8 * g + 8, :] for g in range(32)])):
                plane_ref[p, j0 + u] = plane

    def p1(p, c):
        score_tiles(2 * p, 2, False)
        return c

    lax.fori_loop(0, i // 2, p1, 0)

    @pl.when(i % 2 == 1)
    def _():
        score_tiles(i - 1, 1, False)

    score_tiles(i, 1, True)

    def popcount_rows(words):
        per_tile = jnp.sum(lax.population_count(words), axis=0)
        return jnp.sum(per_tile.astype(F32), axis=0, keepdims=True)

    def bis_body(p, c):
        alive, n_gt, thr_u = c
        plane = plane_ref[p] ^ jnp.where(p == 0, -1, 0)
        ones = alive & plane
        cnt = popcount_rows(ones)
        take = n_gt + cnt >= top_k
        alive = jnp.where(take, ones, alive ^ ones)
        n_gt = jnp.where(take, n_gt, n_gt + cnt)
        thr_u = jnp.where(take, thr_u | (jnp.int32(1) << (31 - p)), thr_u)
        return alive, n_gt, thr_u

    n_tiles = key_ref.shape[0]
    alive0 = jnp.where(lax.broadcasted_iota(I32, (n_tiles, 8, t), 0) <= i, -1, 0)
    alive, n_gt, thr_u = lax.fori_loop(
        0, 32, bis_body, (alive0, jnp.zeros((1, t), F32), jnp.zeros((1, t), I32)))
    thr = jnp.maximum(thr_u ^ INT_MIN, INT_MIN + 1)
    n_avail = (i * t + lax.broadcasted_iota(I32, (1, t), 1) + 1).astype(F32)
    n_ge = jnp.where(n_avail > top_k, n_gt + popcount_rows(alive), 0.0)
    surplus = jnp.max(n_ge) > top_k

    def tie_pass():
        need = top_k - n_gt

        def tb(j, seen):
            k = key_ref[j]
            eq = k == thr
            eqf = jnp.where(eq, 1.0, 0.0)
            before = jnp.dot(lstrict_ref[...], eqf.astype(BF16), preferred_element_type=F32) + seen
            sel = (k > thr) | (eq & (before < need))
            key_ref[j] = jnp.where(sel, 1, INT_MIN)
            return seen + jnp.sum(eqf, axis=0, keepdims=True)

        lax.fori_loop(0, i + 1, tb, jnp.zeros((1, t), F32))
        return jnp.zeros((1, t), I32)

    thr = lax.cond(surplus, tie_pass, lambda: thr)

    qs = head_stack(q_ref[0])

    def attn_tiles(j0, n, carry, bias_kind):
        m, l, acc = carry
        st = _dot_nt(key_rows(kk_ref, j0, n), qs)
        selb = jnp.concatenate([jnp.where(key_ref[j0 + u] >= thr, 0.0, NEG_INF) for u in range(n)], axis=0)
        parts = []
        for h in range(4):
            sh = st[:, h * t:(h + 1) * t] + selb
            if bias_kind is not None:
                sh = sh + bias_ref[bias_kind, h]
            parts.append(sh)
        st = jnp.concatenate(parts, axis=1)
        m_new = jnp.maximum(m, jnp.max(st, axis=0, keepdims=True))
        alpha = jnp.exp(m - m_new)
        p = jnp.exp(st - m_new)
        l = alpha * l + jnp.sum(p, axis=0, keepdims=True)
        vt = jnp.concatenate([vt_ref[0, j0 + u] for u in range(n)], axis=1)
        acc = alpha * acc + jnp.dot(vt, p.astype(BF16), preferred_element_type=F32)
        return m_new, l, acc

    init = (jnp.full((1, 4 * t), M_INIT, F32), jnp.zeros((1, 4 * t), F32), jnp.zeros((HEAD_DIM, 4 * t), F32))
    carry = lax.fori_loop(0, jnp.maximum(i - 1, 0) // 2, lambda p, c: attn_tiles(2 * p, 2, c, None), init)
    carry = lax.cond((i >= 2) & (i % 2 == 0), lambda c: attn_tiles(i - 2, 1, c, None), lambda c: c, carry)
    carry = lax.cond(i > 0, lambda c: attn_tiles(i - 1, 1, c, 0), lambda c: c, carry)
    _, l, acc = attn_tiles(i, 1, carry, 1)
    ot = acc / l
    ot = jnp.concatenate([ot[:, h * t:(h + 1) * t] for h in range(4)], axis=0)
    o_ref[0] = ot.T.astype(BF16)


def _dsa_bias(t5_table, t):
    assert t + 1 >= T5_MAX_DISTANCE
    k = jnp.arange(t)[:, None]
    q = jnp.arange(t)[None, :]
    far = t5_table[T5_BUCKETS - 1, 4:].astype(F32)
    tiles = []
    for off in (t, 0):
        dist = off + q - k
        b = jnp.transpose(_t5_lookup(t5_table, dist)[..., 4:], (2, 0, 1)) - far[:, None, None]
        tiles.append(jnp.where((dist >= 0)[None], b, 0.0))
    return jnp.stack(tiles)


def _dsa(dq, dkk, dvv, iq, ikk, wt, bias, top_k, t):
    b, s, _ = dq.shape
    nt = s // t
    lstrict = (jnp.arange(t)[:, None] > jnp.arange(t)[None, :]).astype(BF16)
    vt = jnp.transpose(dvv[:, :, :HEAD_DIM].reshape(b, nt, t, HEAD_DIM), (0, 1, 3, 2))
    blk = lambda w: pl.BlockSpec((1, t, w), lambda bi, i: (bi, i, 0))
    seq = lambda w: pl.BlockSpec((1, s, w), lambda bi, i: (bi, 0, 0))
    return pl.pallas_call(
        functools.partial(_dsa_kernel, t=t, top_k=top_k),
        grid=(b, nt),
        in_specs=[blk(256), seq(LANES),
                  pl.BlockSpec((1, nt, HEAD_DIM, t), lambda bi, i: (bi, 0, 0, 0)),
                  blk(256), seq(LANES),
                  pl.BlockSpec((1, 8, t), lambda bi, i: (bi, 0, i)),
                  pl.BlockSpec(bias.shape, lambda bi, i: (0, 0, 0, 0)),
                  pl.BlockSpec((t, t), lambda bi, i: (0, 0))],
        out_specs=blk(256),
        out_shape=jax.ShapeDtypeStruct((b, s, 256), BF16),
        scratch_shapes=[pltpu.VMEM((nt, t, t), I32), pltpu.VMEM((32, nt, 8, t), I32)],
        compiler_params=_cparams("arbitrary", "arbitrary"),
        name="dsa",
    )(dq, dkk, vt, iq, ikk, wt, bias, lstrict)


def _merge_kernel(x_ref, gm_ref, wg_ref, oa_ref, of_ref, os_ref, od_ref, wb_ref, wo_ref, gf_ref,
                  wr_ref, br_ref, xo_ref, h2_ref, route_ref):
    x = x_ref[...]
    hb = _rms(x, gm_ref[...]).astype(BF16)
    d = x.shape[1]
    merged = None
    for bi, o_ref in enumerate((oa_ref, of_ref, os_ref, od_ref)):
        gate = jax.nn.sigmoid(jnp.dot(hb, wg_ref[:, bi * d:(bi + 1) * d], preferred_element_type=F32))
        term = gate * jnp.dot(o_ref[...], wb_ref[bi], preferred_element_type=F32)
        merged = term if merged is None else merged + term
    xn = x + jnp.dot(merged.astype(BF16), wo_ref[...], preferred_element_type=F32)
    xo_ref[...] = xn
    h2 = _rms(xn, gf_ref[...])
    h2_ref[...] = h2

    logits = _dot_x3(h2, wr_ref[0], wr_ref[1]) + br_ref[...]
    lane = lax.broadcasted_iota(I32, logits.shape, 1).astype(F32)
    big = 1e9
    gl = jnp.where(lane < N_GROUPS, logits, -jnp.inf)
    gmax = jnp.max(gl, axis=1, keepdims=True)
    grp = jnp.min(jnp.where(gl == gmax, lane, big), axis=1, keepdims=True)
    p_grp = 1.0 / jnp.sum(jnp.exp(gl - gmax), axis=1, keepdims=True)
    first = N_GROUPS + grp * EXPERTS_PER_GROUP
    el = jnp.where((lane >= first) & (lane < first + EXPERTS_PER_GROUP), logits, -jnp.inf)
    l1 = jnp.max(el, axis=1, keepdims=True)
    i1 = jnp.min(jnp.where(el == l1, lane, big), axis=1, keepdims=True)
    el2 = jnp.where(lane == i1, -jnp.inf, el)
    l2 = jnp.max(el2, axis=1, keepdims=True)
    i2 = jnp.min(jnp.where(el2 == l2, lane, big), axis=1, keepdims=True)
    e2 = jnp.exp(l2 - l1)
    g1 = p_grp / (1.0 + e2)
    g2 = p_grp * e2 / (1.0 + e2)
    route = jnp.where(lane == 0, i1 - N_GROUPS,
                      jnp.where(lane == 1, i2 - N_GROUPS,
                                jnp.where(lane == 2, g1, jnp.where(lane == 3, g2, 0.0))))
    route_ref[...] = route


def _merge(x2, gm, wg, o_a, o_f, o_s, o_d, wb, wo, gf, wr, br, tm):
    n, d = x2.shape
    row = lambda w: pl.BlockSpec((tm, w), lambda i: (i, 0))
    full = lambda a: pl.BlockSpec(a.shape, lambda i: (0,) * a.ndim)
    return pl.pallas_call(
        _merge_kernel,
        grid=(n // tm,),
        in_specs=[row(d), full(gm), full(wg), row(256), row(256), row(256), row(256),
                  full(wb), full(wo), full(gf), full(wr), full(br)],
        out_specs=[row(d), row(d), row(LANES)],
        out_shape=[jax.ShapeDtypeStruct((n, d), F32), jax.ShapeDtypeStruct((n, d), F32),
                   jax.ShapeDtypeStruct((n, LANES), F32)],
        compiler_params=_cparams("arbitrary"),
        name="merge",
    )(x2, gm, wg, o_a, o_f, o_s, o_d, wb, wo, gf, wr, br)


def _rank_kernel(route_ref, ltri_ref, rank_ref, cnt_ref, carry_ref):
    @pl.when(pl.program_id(0) == 0)
    def _():
        carry_ref[...] = jnp.zeros_like(carry_ref)

    route = route_ref[...]
    lane = lax.broadcasted_iota(I32, route.shape, 1)
    e0 = route[:, 0:1].astype(I32)
    e1 = route[:, 1:2].astype(I32)
    oh0 = (lane == e0).astype(F32)
    oh1 = (lane == e1).astype(F32)
    both = oh0 + oh1
    before = jnp.dot(ltri_ref[...], both.astype(BF16), preferred_element_type=F32) + carry_ref[0:1, :]
    r0 = jnp.sum(oh0 * before, axis=1, keepdims=True)
    r1 = jnp.sum(oh1 * (before + oh0), axis=1, keepdims=True)
    rank_ref[...] = jnp.where(lane == 0, r0, jnp.where(lane == 1, r1, 0.0))
    total = carry_ref[0:1, :] + jnp.sum(both, axis=0, keepdims=True)
    carry_ref[0:1, :] = total
    cnt_ref[...] = jnp.broadcast_to(total, cnt_ref.shape)


def _rank(route, tm):
    n = route.shape[0]
    ltri = (jnp.arange(tm)[:, None] > jnp.arange(tm)[None, :]).astype(BF16)
    return pl.pallas_call(
        _rank_kernel,
        grid=(n // tm,),
        in_specs=[pl.BlockSpec((tm, LANES), lambda i: (i, 0)), pl.BlockSpec((tm, tm), lambda i: (0, 0))],
        out_specs=[pl.BlockSpec((tm, LANES), lambda i: (i, 0)), pl.BlockSpec((8, LANES), lambda i: (0, 0))],
        out_shape=[jax.ShapeDtypeStruct((n, LANES), F32), jax.ShapeDtypeStruct((8, LANES), F32)],
        scratch_shapes=[pltpu.VMEM((8, LANES), F32)],
        compiler_params=_cparams("arbitrary"),
        name="moe_rank",
    )(route, ltri)


def _expert_kernel(be_ref, nu_ref, tok_ref, tok1_ref, tok2_ref, h_hbm, wup_ref, wdn_ref, y_ref,
                   xbuf, sem, wup_b, wdn_b, *, te):
    b = pl.program_id(0)
    n_used = nu_ref[0]
    slot = b % 3

    def start_row(tokens_ref, dst, r):
        pltpu.make_async_copy(h_hbm.at[pl.ds(tokens_ref[0, 0, r], 1), :],
                              xbuf.at[dst, pl.ds(r, 1), :], sem.at[dst]).start()

    def start_block(tokens_ref, dst):
        def issue(r, c):
            start_row(tokens_ref, dst, r)
            return c
        lax.fori_loop(0, te, issue, 0, unroll=8)

    def block(prefetch):
        pltpu.make_async_copy(h_hbm.at[pl.ds(0, te), :], xbuf.at[slot], sem.at[slot]).wait()
        xb = xbuf[slot].astype(BF16)
        if prefetch:
            dst = (b + 2) % 3
            for r in range(te):
                start_row(tok2_ref, dst, r)
        gu = jnp.dot(xb, wup_b[...], preferred_element_type=F32)
        g = gu[:, :EXPERT_FF]
        act = g * jax.nn.sigmoid(g) * gu[:, EXPERT_FF:]
        y_ref[...] = jnp.dot(act.astype(BF16), wdn_b[...], preferred_element_type=F32)

    @pl.when((b < n_used) & ((b == 0) | (be_ref[b] != be_ref[jnp.maximum(b - 1, 0)])))
    def _():
        wup_b[...] = wup_ref[0].astype(BF16)
        wdn_b[...] = wdn_ref[0].astype(BF16)

    @pl.when((b == 0) & (n_used > 0))
    def _():
        start_block(tok_ref, 0)

    @pl.when((b == 0) & (n_used > 1))
    def _():
        start_block(tok1_ref, 1)

    @pl.when(b + 2 < n_used)
    def _():
        block(True)

    @pl.when((b < n_used) & (b + 2 >= n_used))
    def _():
        block(False)

    @pl.when(b >= n_used)
    def _():
        y_ref[...] = jnp.zeros_like(y_ref)


def _experts(blk_expert, n_used, slot_tok, h2, w_up, w_down, te):
    n_blocks = blk_expert.shape[0]
    d = h2.shape[1]
    ahead = lambda k: pl.BlockSpec((1, 1, te), lambda b, be, nu: (jnp.minimum(b + k, n_blocks - 1), 0, 0),
                                   memory_space=pltpu.SMEM)
    grid_spec = pltpu.PrefetchScalarGridSpec(
        num_scalar_prefetch=2,
        grid=(n_blocks,),
        in_specs=[ahead(0), ahead(1), ahead(2),
                  pl.BlockSpec(memory_space=pl.ANY),
                  pl.BlockSpec((1, d, 2 * EXPERT_FF), lambda b, be, nu: (be[b], 0, 0)),
                  pl.BlockSpec((1, EXPERT_FF, d), lambda b, be, nu: (be[b], 0, 0))],
        out_specs=pl.BlockSpec((te, d), lambda b, be, nu: (b, 0)),
        scratch_shapes=[pltpu.VMEM((3, te, d), F32), pltpu.SemaphoreType.DMA((3,)),
                        pltpu.VMEM((d, 2 * EXPERT_FF), BF16), pltpu.VMEM((EXPERT_FF, d), BF16)],
    )
    slots = slot_tok.reshape(n_blocks, 1, te)
    return pl.pallas_call(
        functools.partial(_expert_kernel, te=te),
        grid_spec=grid_spec,
        out_shape=jax.ShapeDtypeStruct((n_blocks * te, d), F32),
        compiler_params=_cparams("arbitrary"),
        name="moe_experts",
    )(blk_expert, n_used, slots, slots, slots, h2, w_up, w_down)


def _combine_kernel(pos_ref, pos_next_ref, x_ref, route_ref, y_hbm, o_ref, ybuf, sem, *, tc):
    i = pl.program_id(0)
    slot = i % 2

    def gather(rows_ref, dst):
        def issue(r, c):
            pltpu.make_async_copy(y_hbm.at[pl.ds(rows_ref[0, 0, r], 1), :],
                                  ybuf.at[dst, pl.ds(r, 1), :], sem.at[dst]).start()
            return c
        lax.fori_loop(0, 2 * tc, issue, 0, unroll=8)

    @pl.when(i == 0)
    def _():
        gather(pos_ref, 0)

    @pl.when(i + 1 < pl.num_programs(0))
    def _():
        gather(pos_next_ref, 1 - slot)

    pltpu.make_async_copy(y_hbm.at[pl.ds(0, 2 * tc), :], ybuf.at[slot], sem.at[slot]).wait()
    route = route_ref[...]
    o_ref[...] = (x_ref[...] + route[:, 2:3] * ybuf[slot, 0:tc, :]
                  + route[:, 3:4] * ybuf[slot, tc:2 * tc, :])


def _combine(pos, x2, route, yb, tc):
    n, d = x2.shape
    nt = n // tc
    pos_t = jnp.transpose(pos.reshape(nt, tc, 2), (0, 2, 1)).reshape(nt, 1, 2 * tc)
    return pl.pallas_call(
        functools.partial(_combine_kernel, tc=tc),
        grid=(nt,),
        in_specs=[pl.BlockSpec((1, 1, 2 * tc), lambda i: (i, 0, 0), memory_space=pltpu.SMEM),
                  pl.BlockSpec((1, 1, 2 * tc), lambda i: (jnp.minimum(i + 1, nt - 1), 0, 0),
                               memory_space=pltpu.SMEM),
                  pl.BlockSpec((tc, d), lambda i: (i, 0)),
                  pl.BlockSpec((tc, LANES), lambda i: (i, 0)),
                  pl.BlockSpec(memory_space=pl.ANY)],
        out_specs=pl.BlockSpec((tc, d), lambda i: (i, 0)),
        out_shape=jax.ShapeDtypeStruct((n, d), F32),
        scratch_shapes=[pltpu.VMEM((2, 2 * tc, d), F32), pltpu.SemaphoreType.DMA((2,))],
        compiler_params=_cparams("arbitrary"),
        name="moe_combine",
    )(pos_t, pos_t, x2, route, yb)


def _swap_mid_heads(w, axis):
    h = jnp.split(w, 4, axis=axis)
    return jnp.concatenate([h[0], h[2], h[1], h[3]], axis=axis)


def _t5_lookup(t5_table, dist):
    onehot = (_t5_bucket(dist)[..., None] == jnp.arange(T5_BUCKETS)).astype(F32)
    return jnp.einsum("...b,bh->...h", onehot, t5_table.astype(F32), precision=lax.Precision.HIGHEST)


def _t5_bucket(dist):
    n = jnp.maximum(dist, 0)
    max_exact = T5_BUCKETS // 2
    nf = jnp.maximum(n, 1).astype(F32)
    large = max_exact + (jnp.log(nf / max_exact) / math.log(T5_MAX_DISTANCE / max_exact)
                         * (T5_BUCKETS - max_exact)).astype(I32)
    large = jnp.minimum(large, T5_BUCKETS - 1)
    return jnp.where(n < max_exact, n, large)


def _swa_bias(t5_table):
    t = SWA_BLOCK
    dist = t + jnp.arange(t)[:, None] - jnp.arange(2 * t)[None, :]
    tile = jnp.transpose(_t5_lookup(t5_table, dist)[..., :4], (2, 0, 1))
    valid = (dist >= 0) & (dist < t)
    return jnp.where(valid[None], tile, NEG_INF)


def _layer_weights(w_in, qk_gain, forget_bias, w_branch):
    offs = np.concatenate([[0], np.cumsum(IN_SPLITS)]).tolist()
    part = lambda k: w_in[:, offs[k]:offs[k + 1]]
    dup = lambda w: jnp.concatenate([w, w], axis=1)
    aq = _swap_mid_heads(part(0), 1)
    cols = [aq, part(1), part(2), part(3), part(4), part(5), part(7), part(8), part(9),
            part(10), dup(part(11)), dup(part(12)), part(13), dup(part(14))]
    w1 = jnp.concatenate(cols, axis=1).astype(BF16)
    d = w_in.shape[0]
    wm = _hi_lo(jnp.concatenate([part(6), part(15), jnp.zeros((d, LANES - 8), F32)], axis=1))
    wg = part(16).astype(BF16)
    tile = lambda g, reps, scale: jnp.pad(jnp.tile(g, reps) * scale, (0, 256 - reps * HEAD_DIM))
    gains = jnp.stack([tile(qk_gain[0, 0], 4, ATTN_SCALE), tile(qk_gain[0, 1], 2, 1.0),
                       tile(qk_gain[1, 0], 4, ATTN_SCALE), tile(qk_gain[1, 1], 4, 1.0),
                       tile(qk_gain[2, 0], 4, ATTN_SCALE), tile(qk_gain[2, 1], 2, 1.0),
                       jnp.zeros((256,), F32), jnp.zeros((256,), F32)]).astype(F32)
    fb = jnp.pad(forget_bias.astype(F32), (0, LANES - 4)).reshape(1, LANES)
    wb = jnp.stack([_swap_mid_heads(w_branch[0], 0), w_branch[1], w_branch[2], w_branch[3]]).astype(BF16)
    return w1, wm, wg, gains, fb, wb


def kernel(x, norm_mix_g, w_in, forget_bias, attn_sinks, qk_gain, w_branch, w_out, t5_table, norm_ffn_g,
           w_router_group, b_router_group, w_router_expert, b_router_expert, w_expert_up, w_expert_down):
    b, s, d = x.shape
    n = b * s
    depth = w_in.shape[0]
    top_k = min(DSA_TOPK_MAX, s // 4)
    tm_proj = min(512, s)
    fox_t, fox_tk = min(256, s), min(512, s)
    sb_t = min(256, s)
    dsa_t = min(256, s)
    te = 256
    tc = 128

    gseg = (jnp.arange(256)[:, None] // HEAD_DIM == jnp.arange(256)[None, :] // HEAD_DIM).astype(BF16)
    bias_swa = _swa_bias(t5_table)
    bias_dsa = _dsa_bias(t5_table, dsa_t)
    n_blocks = -(-2 * n // te) + N_EXPERTS
    tok_ids = jnp.repeat(jnp.arange(n, dtype=I32), 2)

    for layer in range(depth):
        w1, wm, wg, gains, fb, wb = _layer_weights(w_in[layer], qk_gain[layer], forget_bias[layer],
                                                   w_branch[layer])
        sinks = jnp.broadcast_to(jnp.pad(attn_sinks[layer].astype(F32), (0, 4))[:, None], (8, LANES))
        (aq, ak, av, fq, fk, fv, sq, sk, sv, dq, dkk, dvv, iq, ikk, cm) = _proj(
            x, norm_mix_g[layer].reshape(1, d), w1, wm, gseg, gains, fb, tm_proj)

        o_swa = _swa(aq, ak, av, bias_swa, sinks)
        cmt = jnp.transpose(cm[:, :, :8], (0, 2, 1))
        o_fox = _fox(fq, fk, fv, cmt, cm, fox_t, fox_tk)
        o_sb = _sb(sq, sk, sv, sb_t)
        o_dsa = _dsa(dq, dkk, dvv, iq, ikk, cmt, bias_dsa, top_k, dsa_t)

        wr = _hi_lo(jnp.concatenate([w_router_group[layer], w_router_expert[layer],
                                     jnp.zeros((d, LANES - N_GROUPS - N_EXPERTS), F32)], axis=1))
        br = jnp.concatenate([b_router_group[layer], b_router_expert[layer],
                              jnp.zeros((LANES - N_GROUPS - N_EXPERTS,), F32)]).reshape(1, LANES)
        x2, h2, route = _merge(
            x.reshape(n, d), norm_mix_g[layer].reshape(1, d), wg,
            o_swa.reshape(n, 256), o_fox.reshape(n, 256), o_sb.reshape(n, 256), o_dsa.reshape(n, 256),
            wb, w_out[layer].astype(BF16), norm_ffn_g[layer].reshape(1, d), wr, br, min(256, n))

        rank, cnt = _rank(route, min(512, n))
        counts = cnt[0, :N_EXPERTS].astype(I32)
        padded = (counts + te - 1) // te * te
        pend = jnp.cumsum(padded)
        pstart = pend - padded
        expert = route[:, :2].astype(I32)
        own = expert[:, :, None] == jnp.arange(N_EXPERTS, dtype=I32)
        pos = jnp.sum(jnp.where(own, pstart, 0), axis=-1) + rank[:, :2].astype(I32)
        slot_tok = jnp.zeros((n_blocks * te,), I32).at[pos.reshape(-1)].set(tok_ids)
        blk_start = jnp.arange(n_blocks, dtype=I32)[:, None] * te
        blk_expert = jnp.minimum(jnp.sum((pend[None, :] <= blk_start).astype(I32), axis=1), N_EXPERTS - 1)
        n_used = (pend[-1:] // te).astype(I32)

        yb = _experts(blk_expert, n_used, slot_tok, h2, w_expert_up[layer], w_expert_down[layer], te)
        x = _combine(pos, x2, route, yb, tc).reshape(b, s, d)
    return x
```

```python
import functools
import math

import jax
import jax.numpy as jnp
import numpy as np
from jax import lax
from jax.experimental import pallas as pl
from jax.experimental.pallas import tpu as pltpu

F32 = jnp.float32
BF16 = jnp.bfloat16
I32 = jnp.int32

HEAD_DIM = 64
LANES = 128
NORM_EPS = 1e-6
NEG_INF = -1e30
M_INIT = -1e29
ATTN_SCALE = HEAD_DIM ** -0.5
LOG2E = math.log2(math.e)
SWA_BLOCK = 128
IDX_SCALE = 64 ** -0.5
IDX_HEADS = 4
DSA_TOPK_MAX = 256
T5_BUCKETS = 32
T5_MAX_DISTANCE = 128
N_GROUPS = 4
EXPERTS_PER_GROUP = 8
N_EXPERTS = N_GROUPS * EXPERTS_PER_GROUP
EXPERT_FF = 512
SB_DEAD = -110.0
INT_MIN = -2 ** 31
VMEM_LIMIT = 56 * 1024 * 1024

IN_SPLITS = (256, 128, 128, 256, 256, 256, 4, 256, 256, 256, 256, 64, 64, 256, 64, 4, 4096)

_SEG = dict(aq=(0, 256), ak=(256, 128), av=(384, 128), fq=(512, 256), fk=(768, 256), fv=(1024, 256),
            sq=(1280, 256), sk=(1536, 256), sv=(1792, 256), dq=(2048, 256), dkk=(2304, 128),
            dvv=(2432, 128), iq=(2560, 256), ikk=(2816, 128))
_W1_COLS = 2944
_SEG_ORDER = ("aq", "ak", "av", "fq", "fk", "fv", "sq", "sk", "sv", "dq", "dkk", "dvv", "iq", "ikk")


def _cparams(*sem):
    return pltpu.CompilerParams(dimension_semantics=sem, vmem_limit_bytes=VMEM_LIMIT)


def _rms(x, g):
    return x * lax.rsqrt(jnp.mean(x * x, axis=-1, keepdims=True) + NORM_EPS) * g


def _log_sigmoid(z):
    return jnp.minimum(z, 0.0) - jnp.log(1.0 + jnp.exp(-jnp.abs(z)))


def _dot_nt(a, b):
    return lax.dot_general(a, b, (((1,), (1,)), ((), ())), preferred_element_type=F32)


def _split3(x):
    p1 = x.astype(BF16)
    r = x - p1.astype(F32)
    p2 = r.astype(BF16)
    return p1, p2, (r - p2.astype(F32)).astype(BF16)


def _hi_lo(w):
    hi = w.astype(BF16)
    return jnp.stack([hi, (w - hi.astype(F32)).astype(BF16)])


def _dot_x3(a, b_hi, b_lo):
    a_hi = a.astype(BF16)
    a_lo = (a - a_hi.astype(F32)).astype(BF16)
    return (jnp.dot(a_hi, b_hi, preferred_element_type=F32) + jnp.dot(a_lo, b_hi, preferred_element_type=F32)
            + jnp.dot(a_hi, b_lo, preferred_element_type=F32))


def _split_heads(qp):
    lo = lax.broadcasted_iota(I32, (1, LANES), 1) < HEAD_DIM
    zero = jnp.zeros_like(qp)
    return jnp.concatenate([jnp.where(lo, qp, zero), jnp.where(lo, zero, qp)], axis=0)


def _merge_heads(o, t):
    lo = lax.broadcasted_iota(I32, (1, LANES), 1) < HEAD_DIM
    return jnp.where(lo, o[:t], o[t:])


def _proj_kernel(x_ref, g_ref, w1_ref, wm_ref, gseg_ref, gains_ref, fb_ref, ltri_ref, *rest):
    outs = dict(zip(_SEG_ORDER, rest[:len(_SEG_ORDER)]))
    cm_ref = rest[len(_SEG_ORDER)]
    carry_ref = rest[len(_SEG_ORDER) + 1]

    @pl.when(pl.program_id(1) == 0)
    def _():
        carry_ref[...] = jnp.zeros_like(carry_ref)

    h = _rms(x_ref[0], g_ref[...])
    hb = h.astype(BF16)

    def seg(name):
        off, width = _SEG[name]
        return jnp.dot(hb, w1_ref[:, off:off + width], preferred_element_type=F32)

    def head_norm(t, row):
        width = t.shape[1]
        ssq = jnp.dot((t * t).astype(BF16), gseg_ref[:width, :width], preferred_element_type=F32)
        return t * lax.rsqrt(ssq * (1.0 / HEAD_DIM) + NORM_EPS) * gains_ref[row:row + 1, :width]

    normed = dict(aq=0, ak=1, fq=2, fk=3, dq=4, dkk=5)
    scaled = dict(sq=ATTN_SCALE, iq=IDX_SCALE)
    for name in _SEG_ORDER:
        t = seg(name)
        if name in normed:
            t = head_norm(t, normed[name])
        elif name in scaled:
            t = t * scaled[name]
        outs[name][0] = t.astype(BF16)

    misc = _dot_x3(h, wm_ref[0], wm_ref[1])
    lane = lax.broadcasted_iota(I32, misc.shape, 1)
    logf = jnp.where(lane < 4, _log_sigmoid(misc + fb_ref[...]), 0.0)
    ltri = ltri_ref[...]
    c = carry_ref[0:1, :]
    for piece in _split3(logf):
        c = c + jnp.dot(ltri, piece, preferred_element_type=F32)
    tm = misc.shape[0]
    carry_ref[0:1, :] = c[tm - 1:tm, :]
    cm_ref[0] = jnp.where(lane < 4, c, misc)


def _proj(x, g, w1, wm, gseg, gains, fb, tm):
    b, s, d = x.shape
    ltri = jnp.tril(jnp.ones((tm, tm), BF16))
    full = lambda shape: pl.BlockSpec(shape, lambda bi, si: (0,) * len(shape))
    out_shapes = [jax.ShapeDtypeStruct((b, s, _SEG[n][1]), BF16) for n in _SEG_ORDER]
    out_shapes.append(jax.ShapeDtypeStruct((b, s, LANES), F32))
    out_specs = [pl.BlockSpec((1, tm, _SEG[n][1]), lambda bi, si: (bi, si, 0)) for n in _SEG_ORDER]
    out_specs.append(pl.BlockSpec((1, tm, LANES), lambda bi, si: (bi, si, 0)))
    return pl.pallas_call(
        _proj_kernel,
        grid=(b, s // tm),
        in_specs=[pl.BlockSpec((1, tm, d), lambda bi, si: (bi, si, 0)),
                  full((1, d)), full(w1.shape), full(wm.shape), full(gseg.shape),
                  full(gains.shape), full(fb.shape), full((tm, tm))],
        out_specs=out_specs,
        out_shape=out_shapes,
        scratch_shapes=[pltpu.VMEM((8, LANES), F32)],
        compiler_params=_cparams("arbitrary", "arbitrary"),
        name="proj",
    )(x, g, w1, wm, gseg, gains, fb, ltri)


def _swa_kernel(q_ref, kp_ref, kc_ref, vp_ref, vc_ref, bias_ref, sink_ref, o_ref, *, nsub):
    i = pl.program_id(1)
    t = SWA_BLOCK
    col = lax.broadcasted_iota(I32, (t, 2 * t), 1)
    no_prev = (col < t) & (i == 0)
    for u in range(nsub):
        rows = slice(u * t, (u + 1) * t)
        q = q_ref[0, rows, :]
        if u == 0:
            kcat = jnp.concatenate([kp_ref[0], kc_ref[0, rows, :]], axis=0)
            vcat = jnp.concatenate([vp_ref[0], vc_ref[0, rows, :]], axis=0)
        else:
            kcat = kc_ref[0, (u - 1) * t:(u + 1) * t, :]
            vcat = vc_ref[0, (u - 1) * t:(u + 1) * t, :]
        pairs = []
        for pair in range(2):
            qs = _split_heads(q[:, pair * LANES:(pair + 1) * LANES])
            s = _dot_nt(qs, kcat)
            ps = []
            for hh in range(2):
                head = pair + 2 * hh
                sh = s[hh * t:(hh + 1) * t] + bias_ref[head]
                if u == 0:
                    sh = jnp.where(no_prev, NEG_INF, sh)
                sink = sink_ref[head:head + 1, 0:1]
                m = jnp.maximum(jnp.max(sh, axis=1, keepdims=True), sink)
                p = jnp.exp(sh - m)
                denom = jnp.sum(p, axis=1, keepdims=True) + jnp.exp(sink - m)
                ps.append(p / denom)
            o = jnp.dot(jnp.concatenate(ps, axis=0).astype(BF16), vcat, preferred_element_type=F32)
            pairs.append(_merge_heads(o, t))
        o_ref[0, rows, :] = jnp.concatenate(pairs, axis=1).astype(BF16)


def _swa(aq, ak, av, bias, sinks, nsub):
    b, s, _ = aq.shape
    t = SWA_BLOCK
    cur = lambda bi, i: (bi, i, 0)
    prev = lambda bi, i: (bi, jnp.maximum(i * nsub - 1, 0), 0)
    return pl.pallas_call(
        functools.partial(_swa_kernel, nsub=nsub),
        grid=(b, s // (t * nsub)),
        in_specs=[pl.BlockSpec((1, t * nsub, 256), cur),
                  pl.BlockSpec((1, t, LANES), prev), pl.BlockSpec((1, t * nsub, LANES), cur),
                  pl.BlockSpec((1, t, LANES), prev), pl.BlockSpec((1, t * nsub, LANES), cur),
                  pl.BlockSpec(bias.shape, lambda bi, i: (0, 0, 0)),
                  pl.BlockSpec(sinks.shape, lambda bi, i: (0, 0))],
        out_specs=pl.BlockSpec((1, t * nsub, 256), cur),
        out_shape=jax.ShapeDtypeStruct((b, s, 256), BF16),
        compiler_params=_cparams("arbitrary", "arbitrary"),
        name="swa",
    )(aq, ak, ak, av, av, bias, sinks)


def _fox_kernel(q_ref, k_ref, vt_ref, ct_ref, ccol_ref, o_ref, ckb_ref, *, t, tk):
    i = pl.program_id(1)
    n_tiles = ckb_ref.shape[1] // t

    @pl.when(i == 0)
    def _():
        def fill(j, c):
            rows = pl.ds(pl.multiple_of(j * t, t), t)
            cc = ccol_ref[0, rows, :] * LOG2E
            for h in range(4):
                ckb_ref[h, rows, :] = jnp.broadcast_to(cc[:, h:h + 1], (t, LANES))
            return c
        lax.fori_loop(0, n_tiles, fill, 0)

    q = q_ref[0]
    ct = ct_ref[0] * LOG2E
    qs = [_split_heads(q[:, :LANES]), _split_heads(q[:, LANES:])]
    jd = (i * t) // tk
    valid = (lax.broadcasted_iota(I32, (tk, t), 0)
             <= lax.broadcasted_iota(I32, (tk, t), 1) + (i * t - jd * tk))

    def tile(j, carry, masked):
        m, l, accs = carry
        rows = pl.ds(pl.multiple_of(j * tk, tk), tk)
        cols = []
        for pair in range(2):
            st = _dot_nt(k_ref[0, rows, pair * LANES:(pair + 1) * LANES], qs[pair])
            for hh in range(2):
                head = 2 * pair + hh
                ck = ckb_ref[head, rows, :]
                for c in range(t // LANES):
                    cs = slice(c * LANES, (c + 1) * LANES)
                    sh = st[:, hh * t + c * LANES:hh * t + (c + 1) * LANES] + (ct[head:head + 1, cs] - ck)
                    if masked:
                        sh = jnp.where(valid[:, cs], sh, NEG_INF)
                    cols.append(sh)
        st = jnp.concatenate(cols, axis=1)
        m_new = jnp.maximum(m, jnp.max(st, axis=0, keepdims=True))
        alpha = jnp.exp2(m - m_new)
        p = jnp.exp2(st - m_new)
        l = alpha * l + jnp.sum(p, axis=0, keepdims=True)
        pb = p.astype(BF16)
        new = []
        for pair in range(2):
            lanes = slice(pair * 2 * t, (pair + 1) * 2 * t)
            pv = jnp.dot(vt_ref[0, j, pair * LANES:(pair + 1) * LANES, :], pb[:, lanes],
                         preferred_element_type=F32)
            new.append(alpha[:, lanes] * accs[pair] + pv)
        return m_new, l, tuple(new)

    init = (jnp.full((1, 4 * t), M_INIT, F32), jnp.zeros((1, 4 * t), F32),
            (jnp.zeros((LANES, 2 * t), F32), jnp.zeros((LANES, 2 * t), F32)))
    carry = lax.fori_loop(0, jd, lambda j, c: tile(j, c, False), init)
    _, l, accs = tile(jd, carry, True)
    outs = []
    for pair in range(2):
        o = accs[pair] / l[:, pair * 2 * t:(pair + 1) * 2 * t]
        outs.append(jnp.concatenate([o[:HEAD_DIM, :t], o[HEAD_DIM:, t:]], axis=0))
    o_ref[0] = jnp.concatenate(outs, axis=0).T.astype(BF16)


def _fox(fq, fk, fv, ct, ccol, t, tk):
    b, s, _ = fq.shape
    nt = s // t
    vt = jnp.transpose(fv.reshape(b, s // tk, tk, 256), (0, 1, 3, 2))
    return pl.pallas_call(
        functools.partial(_fox_kernel, t=t, tk=tk),
        grid=(b, nt),
        in_specs=[pl.BlockSpec((1, t, 256), lambda bi, i: (bi, i, 0)),
                  pl.BlockSpec((1, s, 256), lambda bi, i: (bi, 0, 0)),
                  pl.BlockSpec((1, s // tk, 256, tk), lambda bi, i: (bi, 0, 0, 0)),
                  pl.BlockSpec((1, 8, t), lambda bi, i: (bi, 0, i)),
                  pl.BlockSpec((1, s, LANES), lambda bi, i: (bi, 0, 0))],
        out_specs=pl.BlockSpec((1, t, 256), lambda bi, i: (bi, i, 0)),
        out_shape=jax.ShapeDtypeStruct((b, s, 256), BF16),
        scratch_shapes=[pltpu.VMEM((4, s, LANES), F32)],
        compiler_params=_cparams("arbitrary", "arbitrary"),
        name="fox",
    )(fq, fk, vt, ct, ccol)


def _sb_kernel(q_ref, k_ref, vt_ref, lgt_ref, o_ref, *, t):
    i = pl.program_id(1)
    q = q_ref[0]
    lgt = lgt_ref[...]
    qs = [_split_heads(q[:, :LANES]), _split_heads(q[:, LANES:])]
    key_i = lax.broadcasted_iota(I32, (t, 4 * t), 0)
    query_i = lax.broadcasted_iota(I32, (t, 4 * t), 1) & (t - 1)
    strict = key_i < query_i

    def tile(j, r, accs, masked):
        rows = pl.ds(pl.multiple_of(j * t, t), t)
        z = jnp.concatenate([_dot_nt(k_ref[0, rows, p * LANES:(p + 1) * LANES], qs[p]) for p in range(2)],
                            axis=1)
        sp = jnp.log(1.0 + jnp.exp(-jnp.abs(z)))
        log_beta = jnp.minimum(z, 0.0) - sp
        log_keep = jnp.minimum(-z, 0.0) - sp
        if masked:
            log_keep = jnp.where(strict, log_keep, 0.0)
        hi = log_keep.astype(BF16)
        lo = (log_keep - hi.astype(F32)).astype(BF16)
        later = (jnp.dot(lgt, hi, preferred_element_type=F32)
                 + jnp.dot(lgt, lo, preferred_element_type=F32))
        a = jnp.exp(log_beta + later + r)
        if masked:
            a = jnp.where(strict, a, 0.0)
        ab = a.astype(BF16)
        new = tuple(accs[p] + jnp.dot(vt_ref[0, j, p * LANES:(p + 1) * LANES, :],
                                      ab[:, p * 2 * t:(p + 1) * 2 * t], preferred_element_type=F32)
                    for p in range(2))
        return r + jnp.sum(log_keep, axis=0, keepdims=True), new

    zero_acc = jnp.zeros((LANES, 2 * t), F32)
    r, accs = tile(i, jnp.zeros((1, 4 * t), F32), (zero_acc, zero_acc), True)

    def cond(c):
        return (c[0] >= 0) & (c[1] > 0)

    def body(c):
        j, _, r, accs = c
        r, accs = tile(j, r, accs, False)
        return j - 1, (jnp.max(r) > SB_DEAD).astype(I32), r, accs

    _, _, _, accs = lax.while_loop(cond, body, (i - 1, (jnp.max(r) > SB_DEAD).astype(I32), r, accs))
    outs = [jnp.concatenate([accs[p][:HEAD_DIM, :t], accs[p][HEAD_DIM:, t:]], axis=0) for p in range(2)]
    o_ref[0] = jnp.concatenate(outs, axis=0).T.astype(BF16)


def _sb(sq, sk, sv, t):
    b, s, _ = sq.shape
    nt = s // t
    lgt = (jnp.arange(t)[:, None] < jnp.arange(t)[None, :]).astype(BF16)
    vt = jnp.transpose(sv.reshape(b, nt, t, 256), (0, 1, 3, 2))
    return pl.pallas_call(
        functools.partial(_sb_kernel, t=t),
        grid=(b, nt),
        in_specs=[pl.BlockSpec((1, t, 256), lambda bi, i: (bi, i, 0)),
                  pl.BlockSpec((1, s, 256), lambda bi, i: (bi, 0, 0)),
                  pl.BlockSpec((1, nt, 256, t), lambda bi, i: (bi, 0, 0, 0)),
                  pl.BlockSpec((t, t), lambda bi, i: (0, 0))],
        out_specs=pl.BlockSpec((1, t, 256), lambda bi, i: (bi, i, 0)),
        out_shape=jax.ShapeDtypeStruct((b, s, 256), BF16),
        compiler_params=_cparams("arbitrary", "arbitrary"),
        name="stickbreak",
    )(sq, sk, vt, lgt)


def _bit_planes(words):
    words = list(words)
    j, m = 16, 0x0000FFFF
    while j:
        k = 0
        while k < 32:
            tt = (words[k] ^ lax.shift_right_logical(words[k + j], jnp.full_like(words[k + j], j))) & m
            words[k] = words[k] ^ tt
            words[k + j] = words[k + j] ^ (tt << j)
            k = (k + j + 1) & ~j
        j >>= 1
        m = (m ^ (m << j)) & 0xFFFFFFFF
    return words


def _dsa_kernel(q_ref, kk_ref, vt_ref, iq_ref, ikk_ref, wt_ref, bias_ref, lstrict_ref, o_ref,
                key_ref, plane_ref, *, t, top_k):
    i = pl.program_id(1)
    assert t == 8 * 32

    @pl.when(i == 0)
    def _():
        plane_ref[...] = jnp.zeros_like(plane_ref)
    causal = lax.broadcasted_iota(I32, (t, t), 0) <= lax.broadcasted_iota(I32, (t, t), 1)

    def head_stack(x):
        return jnp.concatenate([_split_heads(x[:, :LANES]), _split_heads(x[:, LANES:])], axis=0)

    def key_rows(ref, j0, n):
        return ref[0, pl.ds(pl.multiple_of(j0 * t, t), n * t), :]

    iqs = head_stack(iq_ref[0])
    wt = wt_ref[0]
    w = [wt[4 + h:5 + h, :] * (IDX_HEADS ** -0.5) for h in range(IDX_HEADS)]

    def score_tiles(j0, n, masked):
        lg = _dot_nt(key_rows(ikk_ref, j0, n), iqs)
        sc = w[0] * jnp.maximum(lg[:, 0:t], 0.0)
        for h in range(1, IDX_HEADS):
            sc = sc + w[h] * jnp.maximum(lg[:, h * t:(h + 1) * t], 0.0)
        bits = pltpu.bitcast(sc, I32)
        key = bits ^ ((bits >> 31) & 0x7FFFFFFF)
        key = jnp.where(key == -1, 0, key)
        if masked:
            key = jnp.where(causal, key, INT_MIN)
        for u in range(n):
            key_u = key[u * t:(u + 1) * t]
            key_ref[j0 + u] = key_u
            for p, plane in enumerate(_bit_planes([key_u[8 * g:8 * g + 8, :] for g in range(32)])):
                plane_ref[p, j0 + u] = plane

    def p1(p, c):
        score_tiles(2 * p, 2, False)
        return c

    lax.fori_loop(0, i // 2, p1, 0)

    @pl.when(i % 2 == 1)
    def _():
        score_tiles(i - 1, 1, False)

    score_tiles(i, 1, True)

    def popcount_rows(words):
        per_tile = jnp.sum(lax.population_count(words), axis=0)
        return jnp.sum(per_tile.astype(F32), axis=0, keepdims=True)

    def bis_body(p, c):
        alive, n_gt, thr_u = c
        plane = plane_ref[p] ^ jnp.where(p == 0, -1, 0)
        ones = alive & plane
        cnt = popcount_rows(ones)
        take = n_gt + cnt >= top_k
        alive = jnp.where(take, ones, alive ^ ones)
        n_gt = jnp.where(take, n_gt, n_gt + cnt)
        thr_u = jnp.where(take, thr_u | (jnp.int32(1) << (31 - p)), thr_u)
        return alive, n_gt, thr_u

    n_tiles = key_ref.shape[0]
    alive0 = jnp.where(lax.broadcasted_iota(I32, (n_tiles, 8, t), 0) <= i, -1, 0)
    alive, n_gt, thr_u = lax.fori_loop(
        0, 32, bis_body, (alive0, jnp.zeros((1, t), F32), jnp.zeros((1, t), I32)))
    thr = jnp.maximum(thr_u ^ INT_MIN, INT_MIN + 1)
    n_avail = (i * t + lax.broadcasted_iota(I32, (1, t), 1) + 1).astype(F32)
    n_ge = jnp.where(n_avail > top_k, n_gt + popcount_rows(alive), 0.0)
    surplus = jnp.max(n_ge) > top_k

    def tie_pass():
        need = top_k - n_gt

        def tb(j, seen):
            k = key_ref[j]
            eq = k == thr
            eqf = jnp.where(eq, 1.0, 0.0)
            before = jnp.dot(lstrict_ref[...], eqf.astype(BF16), preferred_element_type=F32) + seen
            sel = (k > thr) | (eq & (before < need))
            key_ref[j] = jnp.where(sel, 1, INT_MIN)
            return seen + jnp.sum(eqf, axis=0, keepdims=True)

        lax.fori_loop(0, i + 1, tb, jnp.zeros((1, t), F32))
        return jnp.zeros((1, t), I32)

    thr = lax.cond(surplus, tie_pass, lambda: thr)

    qs = head_stack(q_ref[0])

    def attn_tiles(j0, n, carry, bias_kind):
        m, l, acc = carry
        st = _dot_nt(key_rows(kk_ref, j0, n), qs)
        selb = jnp.concatenate([jnp.where(key_ref[j0 + u] >= thr, 0.0, NEG_INF) for u in range(n)], axis=0)
        parts = []
        for h in range(4):
            sh = st[:, h * t:(h + 1) * t] + selb
            if bias_kind is not None:
                sh = sh + bias_ref[bias_kind, h]
            parts.append(sh)
        st = jnp.concatenate(parts, axis=1)
        m_new = jnp.maximum(m, jnp.max(st, axis=0, keepdims=True))
        alpha = jnp.exp2(m - m_new)
        p = jnp.exp2(st - m_new)
        l = alpha * l + jnp.sum(p, axis=0, keepdims=True)
        vt = jnp.concatenate([vt_ref[0, j0 + u] for u in range(n)], axis=1)
        acc = alpha * acc + jnp.dot(vt, p.astype(BF16), preferred_element_type=F32)
        return m_new, l, acc

    init = (jnp.full((1, 4 * t), M_INIT, F32), jnp.zeros((1, 4 * t), F32), jnp.zeros((HEAD_DIM, 4 * t), F32))
    carry = lax.fori_loop(0, jnp.maximum(i - 1, 0) // 2, lambda p, c: attn_tiles(2 * p, 2, c, None), init)
    carry = lax.cond((i >= 2) & (i % 2 == 0), lambda c: attn_tiles(i - 2, 1, c, None), lambda c: c, carry)
    carry = lax.cond(i > 0, lambda c: attn_tiles(i - 1, 1, c, 0), lambda c: c, carry)
    _, l, acc = attn_tiles(i, 1, carry, 1)
    ot = acc / l
    ot = jnp.concatenate([ot[:, h * t:(h + 1) * t] for h in range(4)], axis=0)
    o_ref[0] = ot.T.astype(BF16)


def _dsa_bias(t5_table, t):
    assert t + 1 >= T5_MAX_DISTANCE
    k = jnp.arange(t)[:, None]
    q = jnp.arange(t)[None, :]
    far = t5_table[T5_BUCKETS - 1, 4:].astype(F32)
    tiles = []
    for off in (t, 0):
        dist = off + q - k
        b = jnp.transpose(_t5_lookup(t5_table, dist)[..., 4:], (2, 0, 1)) - far[:, None, None]
        tiles.append(jnp.where((dist >= 0)[None], b, 0.0))
    return jnp.stack(tiles)


def _dsa(dq, dkk, dvv, iq, ikk, wt, bias, top_k, t):
    b, s, _ = dq.shape
    nt = s // t
    lstrict = (jnp.arange(t)[:, None] > jnp.arange(t)[None, :]).astype(BF16)
    vt = jnp.transpose(dvv[:, :, :HEAD_DIM].reshape(b, nt, t, HEAD_DIM), (0, 1, 3, 2))
    blk = lambda w: pl.BlockSpec((1, t, w), lambda bi, i: (bi, i, 0))
    seq = lambda w: pl.BlockSpec((1, s, w), lambda bi, i: (bi, 0, 0))
    return pl.pallas_call(
        functools.partial(_dsa_kernel, t=t, top_k=top_k),
        grid=(b, nt),
        in_specs=[blk(256), seq(LANES),
                  pl.BlockSpec((1, nt, HEAD_DIM, t), lambda bi, i: (bi, 0, 0, 0)),
                  blk(256), seq(LANES),
                  pl.BlockSpec((1, 8, t), lambda bi, i: (bi, 0, i)),
                  pl.BlockSpec(bias.shape, lambda bi, i: (0, 0, 0, 0)),
                  pl.BlockSpec((t, t), lambda bi, i: (0, 0))],
        out_specs=blk(256),
        out_shape=jax.ShapeDtypeStruct((b, s, 256), BF16),
        scratch_shapes=[pltpu.VMEM((nt, t, t), I32), pltpu.VMEM((32, nt, 8, t), I32)],
        compiler_params=_cparams("arbitrary", "arbitrary"),
        name="dsa",
    )(dq, dkk, vt, iq, ikk, wt, bias, lstrict)


def _merge_kernel(x_ref, gm_ref, wg_ref, oa_ref, of_ref, os_ref, od_ref, wb_ref, wo_ref, gf_ref,
                  wr_ref, br_ref, xo_ref, h2_ref, route_ref):
    x = x_ref[...]
    hb = _rms(x, gm_ref[...]).astype(BF16)
    d = x.shape[1]
    merged = None
    for bi, o_ref in enumerate((oa_ref, of_ref, os_ref, od_ref)):
        gate = jax.nn.sigmoid(jnp.dot(hb, wg_ref[:, bi * d:(bi + 1) * d], preferred_element_type=F32))
        term = gate * jnp.dot(o_ref[...], wb_ref[bi], preferred_element_type=F32)
        merged = term if merged is None else merged + term
    xn = x + jnp.dot(merged.astype(BF16), wo_ref[...], preferred_element_type=F32)
    xo_ref[...] = xn
    h2 = _rms(xn, gf_ref[...])
    h2_ref[...] = h2

    logits = _dot_x3(h2, wr_ref[0], wr_ref[1]) + br_ref[...]
    lane = lax.broadcasted_iota(I32, logits.shape, 1).astype(F32)
    big = 1e9
    gl = jnp.where(lane < N_GROUPS, logits, -jnp.inf)
    gmax = jnp.max(gl, axis=1, keepdims=True)
    grp = jnp.min(jnp.where(gl == gmax, lane, big), axis=1, keepdims=True)
    p_grp = 1.0 / jnp.sum(jnp.exp(gl - gmax), axis=1, keepdims=True)
    first = N_GROUPS + grp * EXPERTS_PER_GROUP
    el = jnp.where((lane >= first) & (lane < first + EXPERTS_PER_GROUP), logits, -jnp.inf)
    l1 = jnp.max(el, axis=1, keepdims=True)
    i1 = jnp.min(jnp.where(el == l1, lane, big), axis=1, keepdims=True)
    el2 = jnp.where(lane == i1, -jnp.inf, el)
    l2 = jnp.max(el2, axis=1, keepdims=True)
    i2 = jnp.min(jnp.where(el2 == l2, lane, big), axis=1, keepdims=True)
    e2 = jnp.exp(l2 - l1)
    g1 = p_grp / (1.0 + e2)
    g2 = p_grp * e2 / (1.0 + e2)
    route = jnp.where(lane == 0, i1 - N_GROUPS,
                      jnp.where(lane == 1, i2 - N_GROUPS,
                                jnp.where(lane == 2, g1, jnp.where(lane == 3, g2, 0.0))))
    route_ref[...] = route


def _merge(x2, gm, wg, o_a, o_f, o_s, o_d, wb, wo, gf, wr, br, tm):
    n, d = x2.shape
    row = lambda w: pl.BlockSpec((tm, w), lambda i: (i, 0))
    full = lambda a: pl.BlockSpec(a.shape, lambda i: (0,) * a.ndim, pipeline_mode=pl.Buffered(1))
    return pl.pallas_call(
        _merge_kernel,
        grid=(n // tm,),
        in_specs=[row(d), full(gm), full(wg), row(256), row(256), row(256), row(256),
                  full(wb), full(wo), full(gf), full(wr), full(br)],
        out_specs=[row(d), row(d), row(LANES)],
        out_shape=[jax.ShapeDtypeStruct((n, d), F32), jax.ShapeDtypeStruct((n, d), F32),
                   jax.ShapeDtypeStruct((n, LANES), F32)],
        compiler_params=_cparams("arbitrary"),
        name="merge",
    )(x2, gm, wg, o_a, o_f, o_s, o_d, wb, wo, gf, wr, br)


def _rank_kernel(route_ref, ltri_ref, rank_ref, cnt_ref, carry_ref):
    @pl.when(pl.program_id(0) == 0)
    def _():
        carry_ref[...] = jnp.zeros_like(carry_ref)

    route = route_ref[...]
    lane = lax.broadcasted_iota(I32, route.shape, 1)
    e0 = route[:, 0:1].astype(I32)
    e1 = route[:, 1:2].astype(I32)
    oh0 = (lane == e0).astype(F32)
    oh1 = (lane == e1).astype(F32)
    both = oh0 + oh1
    before = jnp.dot(ltri_ref[...], both.astype(BF16), preferred_element_type=F32) + carry_ref[0:1, :]
    r0 = jnp.sum(oh0 * before, axis=1, keepdims=True)
    r1 = jnp.sum(oh1 * (before + oh0), axis=1, keepdims=True)
    rank_ref[...] = jnp.where(lane == 0, r0, jnp.where(lane == 1, r1, 0.0))
    total = carry_ref[0:1, :] + jnp.sum(both, axis=0, keepdims=True)
    carry_ref[0:1, :] = total
    cnt_ref[...] = jnp.broadcast_to(total, cnt_ref.shape)


def _rank(route, tm):
    n = route.shape[0]
    ltri = (jnp.arange(tm)[:, None] > jnp.arange(tm)[None, :]).astype(BF16)
    return pl.pallas_call(
        _rank_kernel,
        grid=(n // tm,),
        in_specs=[pl.BlockSpec((tm, LANES), lambda i: (i, 0)), pl.BlockSpec((tm, tm), lambda i: (0, 0))],
        out_specs=[pl.BlockSpec((tm, LANES), lambda i: (i, 0)), pl.BlockSpec((8, LANES), lambda i: (0, 0))],
        out_shape=[jax.ShapeDtypeStruct((n, LANES), F32), jax.ShapeDtypeStruct((8, LANES), F32)],
        scratch_shapes=[pltpu.VMEM((8, LANES), F32)],
        compiler_params=_cparams("arbitrary"),
        name="moe_rank",
    )(route, ltri)


def _expert_kernel(be_ref, nu_ref, tok_ref, tok1_ref, tok2_ref, h_hbm, wup_ref, wdn_ref, y_ref,
                   xbuf, sem, wup_b, wdn_b, *, te):
    b = pl.program_id(0)
    n_used = nu_ref[0]
    slot = b % 3

    def start_row(tokens_ref, dst, r):
        pltpu.make_async_copy(h_hbm.at[pl.ds(tokens_ref[0, 0, r], 1), :],
                              xbuf.at[dst, pl.ds(r, 1), :], sem.at[dst]).start()

    def start_block(tokens_ref, dst):
        def issue(r, c):
            start_row(tokens_ref, dst, r)
            return c
        lax.fori_loop(0, te, issue, 0, unroll=8)

    def block(prefetch):
        pltpu.make_async_copy(h_hbm.at[pl.ds(0, te), :], xbuf.at[slot], sem.at[slot]).wait()
        xb = xbuf[slot].astype(BF16)
        if prefetch:
            dst = (b + 2) % 3
            for r in range(te):
                start_row(tok2_ref, dst, r)
        gu = jnp.dot(xb, wup_b[...], preferred_element_type=F32)
        g = gu[:, :EXPERT_FF]
        act = g * jax.nn.sigmoid(g) * gu[:, EXPERT_FF:]
        y_ref[...] = jnp.dot(act.astype(BF16), wdn_b[...], preferred_element_type=F32)

    @pl.when((b < n_used) & ((b == 0) | (be_ref[b] != be_ref[jnp.maximum(b - 1, 0)])))
    def _():
        wup_b[...] = wup_ref[0, 0].astype(BF16)
        wdn_b[...] = wdn_ref[0, 0].astype(BF16)

    @pl.when((b == 0) & (n_used > 0))
    def _():
        start_block(tok_ref, 0)

    @pl.when((b == 0) & (n_used > 1))
    def _():
        start_block(tok1_ref, 1)

    @pl.when(b + 2 < n_used)
    def _():
        block(True)

    @pl.when((b < n_used) & (b + 2 >= n_used))
    def _():
        block(False)

    @pl.when(b >= n_used)
    def _():
        y_ref[...] = jnp.zeros_like(y_ref)


def _experts(blk_expert, n_used, slot_tok, h2, w_up, w_down, layer, te):
    n_blocks = blk_expert.shape[0]
    d = h2.shape[1]
    ahead = lambda k: pl.BlockSpec((1, 1, te), lambda b, be, nu: (jnp.minimum(b + k, n_blocks - 1), 0, 0),
                                   memory_space=pltpu.SMEM)
    grid_spec = pltpu.PrefetchScalarGridSpec(
        num_scalar_prefetch=2,
        grid=(n_blocks,),
        in_specs=[ahead(0), ahead(1), ahead(2),
                  pl.BlockSpec(memory_space=pl.ANY),
                  pl.BlockSpec((1, 1, d, 2 * EXPERT_FF), lambda b, be, nu: (layer, be[b], 0, 0)),
                  pl.BlockSpec((1, 1, EXPERT_FF, d), lambda b, be, nu: (layer, be[b], 0, 0))],
        out_specs=pl.BlockSpec((te, d), lambda b, be, nu: (b, 0)),
        scratch_shapes=[pltpu.VMEM((3, te, d), F32), pltpu.SemaphoreType.DMA((3,)),
                        pltpu.VMEM((d, 2 * EXPERT_FF), BF16), pltpu.VMEM((EXPERT_FF, d), BF16)],
    )
    slots = slot_tok.reshape(n_blocks, 1, te)
    return pl.pallas_call(
        functools.partial(_expert_kernel, te=te),
        grid_spec=grid_spec,
        out_shape=jax.ShapeDtypeStruct((n_blocks * te, d), F32),
        compiler_params=_cparams("arbitrary"),
        name="moe_experts",
    )(blk_expert, n_used, slots, slots, slots, h2, w_up, w_down)


def _combine_kernel(pos_ref, pos_next_ref, x_ref, route_ref, y_hbm, o_ref, ybuf, sem, *, tc):
    i = pl.program_id(0)
    slot = i % 2

    def gather(rows_ref, dst):
        def issue(r, c):
            pltpu.make_async_copy(y_hbm.at[pl.ds(rows_ref[0, 0, r], 1), :],
                                  ybuf.at[dst, pl.ds(r, 1), :], sem.at[dst]).start()
            return c
        lax.fori_loop(0, 2 * tc, issue, 0, unroll=8)

    @pl.when(i == 0)
    def _():
        gather(pos_ref, 0)

    @pl.when(i + 1 < pl.num_programs(0))
    def _():
        gather(pos_next_ref, 1 - slot)

    pltpu.make_async_copy(y_hbm.at[pl.ds(0, 2 * tc), :], ybuf.at[slot], sem.at[slot]).wait()
    route = route_ref[...]
    o_ref[...] = (x_ref[...] + route[:, 2:3] * ybuf[slot, 0:tc, :]
                  + route[:, 3:4] * ybuf[slot, tc:2 * tc, :])


def _combine(pos, x2, route, yb, tc):
    n, d = x2.shape
    nt = n // tc
    pos_t = jnp.transpose(pos.reshape(nt, tc, 2), (0, 2, 1)).reshape(nt, 1, 2 * tc)
    return pl.pallas_call(
        functools.partial(_combine_kernel, tc=tc),
        grid=(nt,),
        in_specs=[pl.BlockSpec((1, 1, 2 * tc), lambda i: (i, 0, 0), memory_space=pltpu.SMEM),
                  pl.BlockSpec((1, 1, 2 * tc), lambda i: (jnp.minimum(i + 1, nt - 1), 0, 0),
                               memory_space=pltpu.SMEM),
                  pl.BlockSpec((tc, d), lambda i: (i, 0)),
                  pl.BlockSpec((tc, LANES), lambda i: (i, 0)),
                  pl.BlockSpec(memory_space=pl.ANY)],
        out_specs=pl.BlockSpec((tc, d), lambda i: (i, 0)),
        out_shape=jax.ShapeDtypeStruct((n, d), F32),
        scratch_shapes=[pltpu.VMEM((2, 2 * tc, d), F32), pltpu.SemaphoreType.DMA((2,))],
        compiler_params=_cparams("arbitrary"),
        name="moe_combine",
    )(pos_t, pos_t, x2, route, yb)


def _swap_mid_heads(w, axis):
    h = jnp.split(w, 4, axis=axis)
    return jnp.concatenate([h[0], h[2], h[1], h[3]], axis=axis)


def _t5_lookup(t5_table, dist):
    onehot = (_t5_bucket(dist)[..., None] == jnp.arange(T5_BUCKETS)).astype(F32)
    return jnp.einsum("...b,bh->...h", onehot, t5_table.astype(F32), precision=lax.Precision.HIGHEST)


def _t5_bucket(dist):
    n = jnp.maximum(dist, 0)
    max_exact = T5_BUCKETS // 2
    nf = jnp.maximum(n, 1).astype(F32)
    large = max_exact + (jnp.log(nf / max_exact) / math.log(T5_MAX_DISTANCE / max_exact)
                         * (T5_BUCKETS - max_exact)).astype(I32)
    large = jnp.minimum(large, T5_BUCKETS - 1)
    return jnp.where(n < max_exact, n, large)


def _swa_bias(t5_table):
    t = SWA_BLOCK
    dist = t + jnp.arange(t)[:, None] - jnp.arange(2 * t)[None, :]
    tile = jnp.transpose(_t5_lookup(t5_table, dist)[..., :4], (2, 0, 1))
    valid = (dist >= 0) & (dist < t)
    return jnp.where(valid[None], tile, NEG_INF)


def _layer_weights(w_in, qk_gain, forget_bias, w_branch):
    offs = np.concatenate([[0], np.cumsum(IN_SPLITS)]).tolist()
    part = lambda k: w_in[:, offs[k]:offs[k + 1]]
    dup = lambda w: jnp.concatenate([w, w], axis=1)
    aq = _swap_mid_heads(part(0), 1)
    cols = [aq, part(1), part(2), part(3), part(4), part(5), part(7), part(8), part(9),
            part(10), dup(part(11)), dup(part(12)), part(13), dup(part(14))]
    w1 = jnp.concatenate(cols, axis=1).astype(BF16)
    d = w_in.shape[0]
    wm = _hi_lo(jnp.concatenate([part(6), part(15), jnp.zeros((d, LANES - 8), F32)], axis=1))
    wg = part(16).astype(BF16)
    tile = lambda g, reps, scale: jnp.pad(jnp.tile(g, reps) * scale, (0, 256 - reps * HEAD_DIM))
    gains = jnp.stack([tile(qk_gain[0, 0], 4, ATTN_SCALE), tile(qk_gain[0, 1], 2, 1.0),
                       tile(qk_gain[1, 0], 4, ATTN_SCALE * LOG2E), tile(qk_gain[1, 1], 4, 1.0),
                       tile(qk_gain[2, 0], 4, ATTN_SCALE * LOG2E), tile(qk_gain[2, 1], 2, 1.0),
                       jnp.zeros((256,), F32), jnp.zeros((256,), F32)]).astype(F32)
    fb = jnp.pad(forget_bias.astype(F32), (0, LANES - 4)).reshape(1, LANES)
    wb = jnp.stack([_swap_mid_heads(w_branch[0], 0), w_branch[1], w_branch[2], w_branch[3]]).astype(BF16)
    return w1, wm, wg, gains, fb, wb


def kernel(x, norm_mix_g, w_in, forget_bias, attn_sinks, qk_gain, w_branch, w_out, t5_table, norm_ffn_g,
           w_router_group, b_router_group, w_router_expert, b_router_expert, w_expert_up, w_expert_down):
    b, s, d = x.shape
    n = b * s
    depth = w_in.shape[0]
    top_k = min(DSA_TOPK_MAX, s // 4)
    tm_proj = min(512, s)
    fox_t, fox_tk = min(256, s), min(512, s)
    sb_t = min(256, s)
    dsa_t = min(256, s)
    te = 256
    tc = 128

    gseg = (jnp.arange(256)[:, None] // HEAD_DIM == jnp.arange(256)[None, :] // HEAD_DIM).astype(BF16)
    bias_swa = _swa_bias(t5_table)
    bias_dsa = _dsa_bias(t5_table, dsa_t) * LOG2E
    n_blocks = -(-2 * n // te) + N_EXPERTS
    tok_ids = jnp.repeat(jnp.arange(n, dtype=I32), 2)

    for layer in range(depth):
        w1, wm, wg, gains, fb, wb = _layer_weights(w_in[layer], qk_gain[layer], forget_bias[layer],
                                                   w_branch[layer])
        sinks = jnp.broadcast_to(jnp.pad(attn_sinks[layer].astype(F32), (0, 4))[:, None], (8, LANES))
        (aq, ak, av, fq, fk, fv, sq, sk, sv, dq, dkk, dvv, iq, ikk, cm) = _proj(
            x, norm_mix_g[layer].reshape(1, d), w1, wm, gseg, gains, fb, tm_proj)

        o_swa = _swa(aq, ak, av, bias_swa, sinks, min(4, s // SWA_BLOCK))
        cmt = jnp.transpose(cm[:, :, :8], (0, 2, 1))
        o_fox = _fox(fq, fk, fv, cmt, cm, fox_t, fox_tk)
        o_sb = _sb(sq, sk, sv, sb_t)
        o_dsa = _dsa(dq, dkk, dvv, iq, ikk, cmt, bias_dsa, top_k, dsa_t)

        wr = _hi_lo(jnp.concatenate([w_router_group[layer], w_router_expert[layer],
                                     jnp.zeros((d, LANES - N_GROUPS - N_EXPERTS), F32)], axis=1))
        br = jnp.concatenate([b_router_group[layer], b_router_expert[layer],
                              jnp.zeros((LANES - N_GROUPS - N_EXPERTS,), F32)]).reshape(1, LANES)
        x2, h2, route = _merge(
            x.reshape(n, d), norm_mix_g[layer].reshape(1, d), wg,
            o_swa.reshape(n, 256), o_fox.reshape(n, 256), o_sb.reshape(n, 256), o_dsa.reshape(n, 256),
            wb, w_out[layer].astype(BF16), norm_ffn_g[layer].reshape(1, d), wr, br, min(512, n))

        rank, cnt = _rank(route, min(512, n))
        counts = cnt[0, :N_EXPERTS].astype(I32)
        padded = (counts + te - 1) // te * te
        pend = jnp.cumsum(padded)
        pstart = pend - padded
        expert = route[:, :2].astype(I32)
        own = expert[:, :, None] == jnp.arange(N_EXPERTS, dtype=I32)
        pos = jnp.sum(jnp.where(own, pstart, 0), axis=-1) + rank[:, :2].astype(I32)
        slot_tok = jnp.zeros((n_blocks * te,), I32).at[pos.reshape(-1)].set(tok_ids)
        blk_start = jnp.arange(n_blocks, dtype=I32)[:, None] * te
        blk_expert = jnp.minimum(jnp.sum((pend[None, :] <= blk_start).astype(I32), axis=1), N_EXPERTS - 1)
        n_used = (pend[-1:] // te).astype(I32)

        yb = _experts(blk_expert, n_used, slot_tok, h2, w_expert_up, w_expert_down, layer, te)
        x = _combine(pos, x2, route, yb, tc).reshape(b, s, d)
    return x
```

```python
import functools
import math

import jax
import jax.numpy as jnp
import numpy as np
from jax import lax
from jax.experimental import pallas as pl
from jax.experimental.pallas import tpu as pltpu

F32 = jnp.float32
BF16 = jnp.bfloat16
I32 = jnp.int32

HEAD_DIM = 64
LANES = 128
NORM_EPS = 1e-6
NEG_INF = -1e30
M_INIT = -1e29
ATTN_SCALE = HEAD_DIM ** -0.5
LOG2E = math.log2(math.e)
SWA_BLOCK = 128
IDX_SCALE = 64 ** -0.5
IDX_HEADS = 4
DSA_TOPK_MAX = 256
T5_BUCKETS = 32
T5_MAX_DISTANCE = 128
N_GROUPS = 4
EXPERTS_PER_GROUP = 8
N_EXPERTS = N_GROUPS * EXPERTS_PER_GROUP
EXPERT_FF = 512
SB_DEAD = -110.0
INT_MIN = -2 ** 31
VMEM_LIMIT = 56 * 1024 * 1024

IN_SPLITS = (256, 128, 128, 256, 256, 256, 4, 256, 256, 256, 256, 64, 64, 256, 64, 4, 4096)

_SEG = dict(aq=(0, 256), ak=(256, 128), av=(384, 128), fq=(512, 256), fk=(768, 256), fv=(1024, 256),
            sq=(1280, 256), sk=(1536, 256), sv=(1792, 256), dq=(2048, 256), dkk=(2304, 128),
            dvv=(2432, 128), iq=(2560, 256), ikk=(2816, 128))
_W1_COLS = 2944
_SEG_ORDER = ("aq", "ak", "av", "fq", "fk", "fv", "sq", "sk", "sv", "dq", "dkk", "dvv", "iq", "ikk")


def _cparams(*sem):
    return pltpu.CompilerParams(dimension_semantics=sem, vmem_limit_bytes=VMEM_LIMIT)


def _rms(x, g):
    return x * lax.rsqrt(jnp.mean(x * x, axis=-1, keepdims=True) + NORM_EPS) * g


def _log_sigmoid(z):
    return jnp.minimum(z, 0.0) - jnp.log(1.0 + jnp.exp(-jnp.abs(z)))


def _dot_nt(a, b):
    return lax.dot_general(a, b, (((1,), (1,)), ((), ())), preferred_element_type=F32)


def _split3(x):
    p1 = x.astype(BF16)
    r = x - p1.astype(F32)
    p2 = r.astype(BF16)
    return p1, p2, (r - p2.astype(F32)).astype(BF16)


def _hi_lo(w):
    hi = w.astype(BF16)
    return jnp.stack([hi, (w - hi.astype(F32)).astype(BF16)])


def _dot_x3(a, b_hi, b_lo):
    a_hi = a.astype(BF16)
    a_lo = (a - a_hi.astype(F32)).astype(BF16)
    return (jnp.dot(a_hi, b_hi, preferred_element_type=F32) + jnp.dot(a_lo, b_hi, preferred_element_type=F32)
            + jnp.dot(a_hi, b_lo, preferred_element_type=F32))


def _split_heads(qp):
    lo = lax.broadcasted_iota(I32, (1, LANES), 1) < HEAD_DIM
    zero = jnp.zeros_like(qp)
    return jnp.concatenate([jnp.where(lo, qp, zero), jnp.where(lo, zero, qp)], axis=0)


def _merge_heads(o, t):
    lo = lax.broadcasted_iota(I32, (1, LANES), 1) < HEAD_DIM
    return jnp.where(lo, o[:t], o[t:])


def _proj_kernel(x_ref, g_ref, w1_ref, wm_ref, gseg_ref, gains_ref, fb_ref, ltri_ref, *rest):
    outs = dict(zip(_SEG_ORDER, rest[:len(_SEG_ORDER)]))
    cm_ref = rest[len(_SEG_ORDER)]
    carry_ref = rest[len(_SEG_ORDER) + 1]

    @pl.when(pl.program_id(1) == 0)
    def _():
        carry_ref[...] = jnp.zeros_like(carry_ref)

    h = _rms(x_ref[0], g_ref[...])
    hb = h.astype(BF16)

    def seg(name):
        off, width = _SEG[name]
        return jnp.dot(hb, w1_ref[:, off:off + width], preferred_element_type=F32)

    def head_norm(t, row):
        width = t.shape[1]
        ssq = jnp.dot((t * t).astype(BF16), gseg_ref[:width, :width], preferred_element_type=F32)
        return t * lax.rsqrt(ssq * (1.0 / HEAD_DIM) + NORM_EPS) * gains_ref[row:row + 1, :width]

    normed = dict(aq=0, ak=1, fq=2, fk=3, dq=4, dkk=5)
    scaled = dict(sq=ATTN_SCALE, iq=IDX_SCALE)
    for name in _SEG_ORDER:
        t = seg(name)
        if name in normed:
            t = head_norm(t, normed[name])
        elif name in scaled:
            t = t * scaled[name]
        outs[name][0] = t.astype(BF16)

    misc = _dot_x3(h, wm_ref[0], wm_ref[1])
    lane = lax.broadcasted_iota(I32, misc.shape, 1)
    logf = jnp.where(lane < 4, _log_sigmoid(misc + fb_ref[...]), 0.0)
    ltri = ltri_ref[...]
    c = carry_ref[0:1, :]
    for piece in _split3(logf):
        c = c + jnp.dot(ltri, piece, preferred_element_type=F32)
    tm = misc.shape[0]
    carry_ref[0:1, :] = c[tm - 1:tm, :]
    cm_ref[0] = jnp.where(lane < 4, c, misc)


def _proj(x, g, w1, wm, gseg, gains, fb, tm):
    b, s, d = x.shape
    ltri = jnp.tril(jnp.ones((tm, tm), BF16))
    full = lambda shape: pl.BlockSpec(shape, lambda bi, si: (0,) * len(shape))
    out_shapes = [jax.ShapeDtypeStruct((b, s, _SEG[n][1]), BF16) for n in _SEG_ORDER]
    out_shapes.append(jax.ShapeDtypeStruct((b, s, LANES), F32))
    out_specs = [pl.BlockSpec((1, tm, _SEG[n][1]), lambda bi, si: (bi, si, 0)) for n in _SEG_ORDER]
    out_specs.append(pl.BlockSpec((1, tm, LANES), lambda bi, si: (bi, si, 0)))
    return pl.pallas_call(
        _proj_kernel,
        grid=(b, s // tm),
        in_specs=[pl.BlockSpec((1, tm, d), lambda bi, si: (bi, si, 0)),
                  full((1, d)), full(w1.shape), full(wm.shape), full(gseg.shape),
                  full(gains.shape), full(fb.shape), full((tm, tm))],
        out_specs=out_specs,
        out_shape=out_shapes,
        scratch_shapes=[pltpu.VMEM((8, LANES), F32)],
        compiler_params=_cparams("arbitrary", "arbitrary"),
        name="proj",
    )(x, g, w1, wm, gseg, gains, fb, ltri)


def _swa_kernel(q_ref, kp_ref, kc_ref, vtp_ref, vtc_ref, bias_ref, sink_ref, o_ref, *, nsub):
    i = pl.program_id(1)
    t = SWA_BLOCK
    no_prev = (lax.broadcasted_iota(I32, (2 * t, 4 * t), 0) < t) & (i == 0)
    sink = sink_ref[0:1, :]
    for u in range(nsub):
        rows = slice(u * t, (u + 1) * t)
        q = q_ref[0, rows, :]
        qs = jnp.concatenate([_split_heads(q[:, :LANES]), _split_heads(q[:, LANES:])], axis=0)
        if u == 0:
            kcat = jnp.concatenate([kp_ref[0], kc_ref[0, rows, :]], axis=0)
            vt = jnp.concatenate([vtp_ref[0], vtc_ref[0, :, rows]], axis=1)
        else:
            kcat = kc_ref[0, (u - 1) * t:(u + 1) * t, :]
            vt = vtc_ref[0, :, (u - 1) * t:(u + 1) * t]
        st = _dot_nt(kcat, qs) + bias_ref[...]
        if u == 0:
            st = jnp.where(no_prev, NEG_INF, st)
        m = jnp.maximum(jnp.max(st, axis=0, keepdims=True), sink)
        p = jnp.exp(st - m)
        denom = jnp.sum(p, axis=0, keepdims=True) + jnp.exp(sink - m)
        ot = jnp.dot(vt, p.astype(BF16), preferred_element_type=F32) / denom
        ot = jnp.concatenate([ot[:HEAD_DIM, 0:t], ot[HEAD_DIM:, t:2 * t],
                              ot[:HEAD_DIM, 2 * t:3 * t], ot[HEAD_DIM:, 3 * t:]], axis=0)
        o_ref[0, rows, :] = ot.T.astype(BF16)


def _swa(aq, ak, av, bias, sinks, nsub):
    b, s, _ = aq.shape
    t = SWA_BLOCK
    avt = jnp.transpose(av, (0, 2, 1))
    cur = lambda bi, i: (bi, i, 0)
    prev = lambda bi, i: (bi, jnp.maximum(i * nsub - 1, 0), 0)
    return pl.pallas_call(
        functools.partial(_swa_kernel, nsub=nsub),
        grid=(b, s // (t * nsub)),
        in_specs=[pl.BlockSpec((1, t * nsub, 256), cur),
                  pl.BlockSpec((1, t, LANES), prev), pl.BlockSpec((1, t * nsub, LANES), cur),
                  pl.BlockSpec((1, LANES, t), lambda bi, i: (bi, 0, jnp.maximum(i * nsub - 1, 0))),
                  pl.BlockSpec((1, LANES, t * nsub), lambda bi, i: (bi, 0, i)),
                  pl.BlockSpec(bias.shape, lambda bi, i: (0, 0)),
                  pl.BlockSpec(sinks.shape, lambda bi, i: (0, 0))],
        out_specs=pl.BlockSpec((1, t * nsub, 256), cur),
        out_shape=jax.ShapeDtypeStruct((b, s, 256), BF16),
        compiler_params=_cparams("arbitrary", "arbitrary"),
        name="swa",
    )(aq, ak, ak, avt, avt, bias, sinks)


def _fox_kernel(q_ref, k_ref, vt_ref, ct_ref, ccol_ref, o_ref, ckb_ref, *, t, tk):
    i = pl.program_id(1)
    n_tiles = ckb_ref.shape[1] // t

    @pl.when(i == 0)
    def _():
        def fill(j, c):
            rows = pl.ds(pl.multiple_of(j * t, t), t)
            cc = ccol_ref[0, rows, :] * LOG2E
            for h in range(4):
                ckb_ref[h, rows, :] = jnp.broadcast_to(cc[:, h:h + 1], (t, LANES))
            return c
        lax.fori_loop(0, n_tiles, fill, 0)

    q = q_ref[0]
    ct = ct_ref[0] * LOG2E
    qs = [_split_heads(q[:, :LANES]), _split_heads(q[:, LANES:])]
    jd = (i * t) // tk
    valid = (lax.broadcasted_iota(I32, (tk, t), 0)
             <= lax.broadcasted_iota(I32, (tk, t), 1) + (i * t - jd * tk))

    def tile(j, carry, masked):
        m, l, accs = carry
        rows = pl.ds(pl.multiple_of(j * tk, tk), tk)
        cols = []
        for pair in range(2):
            st = _dot_nt(k_ref[0, rows, pair * LANES:(pair + 1) * LANES], qs[pair])
            for hh in range(2):
                head = 2 * pair + hh
                ck = ckb_ref[head, rows, :]
                for c in range(t // LANES):
                    cs = slice(c * LANES, (c + 1) * LANES)
                    sh = st[:, hh * t + c * LANES:hh * t + (c + 1) * LANES] + (ct[head:head + 1, cs] - ck)
                    if masked:
                        sh = jnp.where(valid[:, cs], sh, NEG_INF)
                    cols.append(sh)
        st = jnp.concatenate(cols, axis=1)
        m_new = jnp.maximum(m, jnp.max(st, axis=0, keepdims=True))
        alpha = jnp.exp2(m - m_new)
        p = jnp.exp2(st - m_new)
        l = alpha * l + jnp.sum(p, axis=0, keepdims=True)
        pb = p.astype(BF16)
        new = []
        for pair in range(2):
            lanes = slice(pair * 2 * t, (pair + 1) * 2 * t)
            pv = jnp.dot(vt_ref[0, j, pair * LANES:(pair + 1) * LANES, :], pb[:, lanes],
                         preferred_element_type=F32)
            new.append(alpha[:, lanes] * accs[pair] + pv)
        return m_new, l, tuple(new)

    init = (jnp.full((1, 4 * t), M_INIT, F32), jnp.zeros((1, 4 * t), F32),
            (jnp.zeros((LANES, 2 * t), F32), jnp.zeros((LANES, 2 * t), F32)))
    carry = lax.fori_loop(0, jd, lambda j, c: tile(j, c, False), init)
    _, l, accs = tile(jd, carry, True)
    outs = []
    for pair in range(2):
        o = accs[pair] / l[:, pair * 2 * t:(pair + 1) * 2 * t]
        outs.append(jnp.concatenate([o[:HEAD_DIM, :t], o[HEAD_DIM:, t:]], axis=0))
    o_ref[0] = jnp.concatenate(outs, axis=0).T.astype(BF16)


def _fox(fq, fk, fv, ct, ccol, t, tk):
    b, s, _ = fq.shape
    nt = s // t
    vt = jnp.transpose(fv.reshape(b, s // tk, tk, 256), (0, 1, 3, 2))
    return pl.pallas_call(
        functools.partial(_fox_kernel, t=t, tk=tk),
        grid=(b, nt),
        in_specs=[pl.BlockSpec((1, t, 256), lambda bi, i: (bi, i, 0)),
                  pl.BlockSpec((1, s, 256), lambda bi, i: (bi, 0, 0)),
                  pl.BlockSpec((1, s // tk, 256, tk), lambda bi, i: (bi, 0, 0, 0)),
                  pl.BlockSpec((1, 8, t), lambda bi, i: (bi, 0, i)),
                  pl.BlockSpec((1, s, LANES), lambda bi, i: (bi, 0, 0))],
        out_specs=pl.BlockSpec((1, t, 256), lambda bi, i: (bi, i, 0)),
        out_shape=jax.ShapeDtypeStruct((b, s, 256), BF16),
        scratch_shapes=[pltpu.VMEM((4, s, LANES), F32)],
        compiler_params=_cparams("arbitrary", "arbitrary"),
        name="fox",
    )(fq, fk, vt, ct, ccol)


def _sb_kernel(q_ref, k_ref, vt_ref, lgt_ref, o_ref, *, t):
    i = pl.program_id(1)
    q = q_ref[0]
    lgt = lgt_ref[...]
    qs = [_split_heads(q[:, :LANES]), _split_heads(q[:, LANES:])]
    key_i = lax.broadcasted_iota(I32, (t, 4 * t), 0)
    query_i = lax.broadcasted_iota(I32, (t, 4 * t), 1) & (t - 1)
    strict = key_i < query_i

    def tile(j, r, accs, masked):
        rows = pl.ds(pl.multiple_of(j * t, t), t)
        z = jnp.concatenate([_dot_nt(k_ref[0, rows, p * LANES:(p + 1) * LANES], qs[p]) for p in range(2)],
                            axis=1)
        sp = jnp.log(1.0 + jnp.exp(-jnp.abs(z)))
        log_beta = jnp.minimum(z, 0.0) - sp
        log_keep = jnp.minimum(-z, 0.0) - sp
        if masked:
            log_keep = jnp.where(strict, log_keep, 0.0)
        hi = log_keep.astype(BF16)
        lo = (log_keep - hi.astype(F32)).astype(BF16)
        later = (jnp.dot(lgt, hi, preferred_element_type=F32)
                 + jnp.dot(lgt, lo, preferred_element_type=F32))
        a = jnp.exp(log_beta + later + r)
        if masked:
            a = jnp.where(strict, a, 0.0)
        ab = a.astype(BF16)
        new = tuple(accs[p] + jnp.dot(vt_ref[0, j, p * LANES:(p + 1) * LANES, :],
                                      ab[:, p * 2 * t:(p + 1) * 2 * t], preferred_element_type=F32)
                    for p in range(2))
        return r + jnp.sum(log_keep, axis=0, keepdims=True), new

    zero_acc = jnp.zeros((LANES, 2 * t), F32)
    r, accs = tile(i, jnp.zeros((1, 4 * t), F32), (zero_acc, zero_acc), True)

    def cond(c):
        return (c[0] >= 0) & (c[1] > 0)

    def body(c):
        j, _, r, accs = c
        r, accs = tile(j, r, accs, False)
        return j - 1, (jnp.max(r) > SB_DEAD).astype(I32), r, accs

    _, _, _, accs = lax.while_loop(cond, body, (i - 1, (jnp.max(r) > SB_DEAD).astype(I32), r, accs))
    outs = [jnp.concatenate([accs[p][:HEAD_DIM, :t], accs[p][HEAD_DIM:, t:]], axis=0) for p in range(2)]
    o_ref[0] = jnp.concatenate(outs, axis=0).T.astype(BF16)


def _sb(sq, sk, sv, t):
    b, s, _ = sq.shape
    nt = s // t
    lgt = (jnp.arange(t)[:, None] < jnp.arange(t)[None, :]).astype(BF16)
    vt = jnp.transpose(sv.reshape(b, nt, t, 256), (0, 1, 3, 2))
    return pl.pallas_call(
        functools.partial(_sb_kernel, t=t),
        grid=(b, nt),
        in_specs=[pl.BlockSpec((1, t, 256), lambda bi, i: (bi, i, 0)),
                  pl.BlockSpec((1, s, 256), lambda bi, i: (bi, 0, 0)),
                  pl.BlockSpec((1, nt, 256, t), lambda bi, i: (bi, 0, 0, 0)),
                  pl.BlockSpec((t, t), lambda bi, i: (0, 0))],
        out_specs=pl.BlockSpec((1, t, 256), lambda bi, i: (bi, i, 0)),
        out_shape=jax.ShapeDtypeStruct((b, s, 256), BF16),
        compiler_params=_cparams("arbitrary", "arbitrary"),
        name="stickbreak",
    )(sq, sk, vt, lgt)


def _bit_planes(words):
    words = list(words)
    j, m = 16, 0x0000FFFF
    while j:
        k = 0
        while k < 32:
            tt = (words[k] ^ lax.shift_right_logical(words[k + j], jnp.full_like(words[k + j], j))) & m
            words[k] = words[k] ^ tt
            words[k + j] = words[k + j] ^ (tt << j)
            k = (k + j + 1) & ~j
        j >>= 1
        m = (m ^ (m << j)) & 0xFFFFFFFF
    return words


def _dsa_kernel(q_ref, kk_ref, vt_ref, iq_ref, ikk_ref, wt_ref, bias_ref, lstrict_ref, o_ref,
                key_ref, plane_ref, *, t, top_k):
    i = pl.program_id(1)
    assert t == 8 * 32

    @pl.when(i == 0)
    def _():
        plane_ref[...] = jnp.zeros_like(plane_ref)
    causal = lax.broadcasted_iota(I32, (t, t), 0) <= lax.broadcasted_iota(I32, (t, t), 1)

    def head_stack(x):
        return jnp.concatenate([_split_heads(x[:, :LANES]), _split_heads(x[:, LANES:])], axis=0)

    def key_rows(ref, j0, n):
        return ref[0, pl.ds(pl.multiple_of(j0 * t, t), n * t), :]

    iqs = head_stack(iq_ref[0])
    wt = wt_ref[0]
    w = [wt[4 + h:5 + h, :] * (IDX_HEADS ** -0.5) for h in range(IDX_HEADS)]

    def score_tiles(j0, n, masked):
        lg = _dot_nt(key_rows(ikk_ref, j0, n), iqs)
        sc = w[0] * jnp.maximum(lg[:, 0:t], 0.0)
        for h in range(1, IDX_HEADS):
            sc = sc + w[h] * jnp.maximum(lg[:, h * t:(h + 1) * t], 0.0)
        bits = pltpu.bitcast(sc, I32)
        key = bits ^ ((bits >> 31) & 0x7FFFFFFF)
        key = jnp.where(key == -1, 0, key)
        if masked:
            key = jnp.where(causal, key, INT_MIN)
        for u in range(n):
            key_u = key[u * t:(u + 1) * t]
            key_ref[j0 + u] = key_u
            for p, plane in enumerate(_bit_planes([key_u[8 * g:8 * g + 8, :] for g in range(32)])):
                plane_ref[p, j0 + u] = plane

    def p1(p, c):
        score_tiles(2 * p, 2, False)
        return c

    lax.fori_loop(0, i // 2, p1, 0)

    @pl.when(i % 2 == 1)
    def _():
        score_tiles(i - 1, 1, False)

    score_tiles(i, 1, True)

    def popcount_rows(words):
        per_tile = jnp.sum(lax.population_count(words), axis=0)
        return jnp.sum(per_tile.astype(F32), axis=0, keepdims=True)

    def bis_body(p, c):
        alive, n_gt, thr_u = c
        plane = plane_ref[p] ^ jnp.where(p == 0, -1, 0)
        ones = alive & plane
        cnt = popcount_rows(ones)
        take = n_gt + cnt >= top_k
        alive = jnp.where(take, ones, alive ^ ones)
        n_gt = jnp.where(take, n_gt, n_gt + cnt)
        thr_u = jnp.where(take, thr_u | (jnp.int32(1) << (31 - p)), thr_u)
        return alive, n_gt, thr_u

    n_tiles = key_ref.shape[0]
    alive0 = jnp.where(lax.broadcasted_iota(I32, (n_tiles, 8, t), 0) <= i, -1, 0)
    alive, n_gt, thr_u = lax.fori_loop(
        0, 32, bis_body, (alive0, jnp.zeros((1, t), F32), jnp.zeros((1, t), I32)))
    thr = jnp.maximum(thr_u ^ INT_MIN, INT_MIN + 1)
    n_avail = (i * t + lax.broadcasted_iota(I32, (1, t), 1) + 1).astype(F32)
    n_ge = jnp.where(n_avail > top_k, n_gt + popcount_rows(alive), 0.0)
    surplus = jnp.max(n_ge) > top_k

    def tie_pass():
        need = top_k - n_gt

        def tb(j, seen):
            k = key_ref[j]
            eq = k == thr
            eqf = jnp.where(eq, 1.0, 0.0)
            before = jnp.dot(lstrict_ref[...], eqf.astype(BF16), preferred_element_type=F32) + seen
            sel = (k > thr) | (eq & (before < need))
            key_ref[j] = jnp.where(sel, 1, INT_MIN)
            return seen + jnp.sum(eqf, axis=0, keepdims=True)

        lax.fori_loop(0, i + 1, tb, jnp.zeros((1, t), F32))
        return jnp.zeros((1, t), I32)

    thr = lax.cond(surplus, tie_pass, lambda: thr)

    qs = head_stack(q_ref[0])

    def attn_tiles(j0, n, carry, bias_kind):
        m, l, acc = carry
        st = _dot_nt(key_rows(kk_ref, j0, n), qs)
        selb = jnp.concatenate([jnp.where(key_ref[j0 + u] >= thr, 0.0, NEG_INF) for u in range(n)], axis=0)
        parts = []
        for h in range(4):
            sh = st[:, h * t:(h + 1) * t] + selb
            if bias_kind is not None:
                sh = sh + bias_ref[bias_kind, h]
            parts.append(sh)
        st = jnp.concatenate(parts, axis=1)
        m_new = jnp.maximum(m, jnp.max(st, axis=0, keepdims=True))
        alpha = jnp.exp2(m - m_new)
        p = jnp.exp2(st - m_new)
        l = alpha * l + jnp.sum(p, axis=0, keepdims=True)
        vt = jnp.concatenate([vt_ref[0, j0 + u] for u in range(n)], axis=1)
        acc = alpha * acc + jnp.dot(vt, p.astype(BF16), preferred_element_type=F32)
        return m_new, l, acc

    init = (jnp.full((1, 4 * t), M_INIT, F32), jnp.zeros((1, 4 * t), F32), jnp.zeros((HEAD_DIM, 4 * t), F32))
    carry = lax.fori_loop(0, jnp.maximum(i - 1, 0) // 2, lambda p, c: attn_tiles(2 * p, 2, c, None), init)
    carry = lax.cond((i >= 2) & (i % 2 == 0), lambda c: attn_tiles(i - 2, 1, c, None), lambda c: c, carry)
    carry = lax.cond(i > 0, lambda c: attn_tiles(i - 1, 1, c, 0), lambda c: c, carry)
    _, l, acc = attn_tiles(i, 1, carry, 1)
    ot = acc / l
    ot = jnp.concatenate([ot[:, h * t:(h + 1) * t] for h in range(4)], axis=0)
    o_ref[0] = ot.T.astype(BF16)


def _dsa_bias(t5_table, t):
    assert t + 1 >= T5_MAX_DISTANCE
    k = jnp.arange(t)[:, None]
    q = jnp.arange(t)[None, :]
    far = t5_table[T5_BUCKETS - 1, 4:].astype(F32)
    tiles = []
    for off in (t, 0):
        dist = off + q - k
        b = jnp.transpose(_t5_lookup(t5_table, dist)[..., 4:], (2, 0, 1)) - far[:, None, None]
        tiles.append(jnp.where((dist >= 0)[None], b, 0.0))
    return jnp.stack(tiles)


def _dsa(dq, dkk, dvv, iq, ikk, wt, bias, top_k, t):
    b, s, _ = dq.shape
    nt = s // t
    lstrict = (jnp.arange(t)[:, None] > jnp.arange(t)[None, :]).astype(BF16)
    vt = jnp.transpose(dvv[:, :, :HEAD_DIM].reshape(b, nt, t, HEAD_DIM), (0, 1, 3, 2))
    blk = lambda w: pl.BlockSpec((1, t, w), lambda bi, i: (bi, i, 0))
    seq = lambda w: pl.BlockSpec((1, s, w), lambda bi, i: (bi, 0, 0))
    return pl.pallas_call(
        functools.partial(_dsa_kernel, t=t, top_k=top_k),
        grid=(b, nt),
        in_specs=[blk(256), seq(LANES),
                  pl.BlockSpec((1, nt, HEAD_DIM, t), lambda bi, i: (bi, 0, 0, 0)),
                  blk(256), seq(LANES),
                  pl.BlockSpec((1, 8, t), lambda bi, i: (bi, 0, i)),
                  pl.BlockSpec(bias.shape, lambda bi, i: (0, 0, 0, 0)),
                  pl.BlockSpec((t, t), lambda bi, i: (0, 0))],
        out_specs=blk(256),
        out_shape=jax.ShapeDtypeStruct((b, s, 256), BF16),
        scratch_shapes=[pltpu.VMEM((nt, t, t), I32), pltpu.VMEM((32, nt, 8, t), I32)],
        compiler_params=_cparams("arbitrary", "arbitrary"),
        name="dsa",
    )(dq, dkk, vt, iq, ikk, wt, bias, lstrict)


def _merge_kernel(x_ref, gm_ref, wg_ref, oa_ref, of_ref, os_ref, od_ref, wb_ref, wo_ref, gf_ref,
                  wr_ref, br_ref, xo_ref, h2_ref, route_ref):
    x = x_ref[...]
    hb = _rms(x, gm_ref[...]).astype(BF16)
    d = x.shape[1]
    merged = None
    for bi, o_ref in enumerate((oa_ref, of_ref, os_ref, od_ref)):
        gate = jax.nn.sigmoid(jnp.dot(hb, wg_ref[:, bi * d:(bi + 1) * d], preferred_element_type=F32))
        term = gate * jnp.dot(o_ref[...], wb_ref[bi], preferred_element_type=F32)
        merged = term if merged is None else merged + term
    xn = x + jnp.dot(merged.astype(BF16), wo_ref[...], preferred_element_type=F32)
    xo_ref[...] = xn
    h2 = _rms(xn, gf_ref[...])
    h2_ref[...] = h2

    logits = _dot_x3(h2, wr_ref[0], wr_ref[1]) + br_ref[...]
    lane = lax.broadcasted_iota(I32, logits.shape, 1).astype(F32)
    big = 1e9
    gl = jnp.where(lane < N_GROUPS, logits, -jnp.inf)
    gmax = jnp.max(gl, axis=1, keepdims=True)
    grp = jnp.min(jnp.where(gl == gmax, lane, big), axis=1, keepdims=True)
    p_grp = 1.0 / jnp.sum(jnp.exp(gl - gmax), axis=1, keepdims=True)
    first = N_GROUPS + grp * EXPERTS_PER_GROUP
    el = jnp.where((lane >= first) & (lane < first + EXPERTS_PER_GROUP), logits, -jnp.inf)
    l1 = jnp.max(el, axis=1, keepdims=True)
    i1 = jnp.min(jnp.where(el == l1, lane, big), axis=1, keepdims=True)
    el2 = jnp.where(lane == i1, -jnp.inf, el)
    l2 = jnp.max(el2, axis=1, keepdims=True)
    i2 = jnp.min(jnp.where(el2 == l2, lane, big), axis=1, keepdims=True)
    e2 = jnp.exp(l2 - l1)
    g1 = p_grp / (1.0 + e2)
    g2 = p_grp * e2 / (1.0 + e2)
    route = jnp.where(lane == 0, i1 - N_GROUPS,
                      jnp.where(lane == 1, i2 - N_GROUPS,
                                jnp.where(lane == 2, g1, jnp.where(lane == 3, g2, 0.0))))
    route_ref[...] = route


def _merge(x2, gm, wg, o_a, o_f, o_s, o_d, wb, wo, gf, wr, br, tm):
    n, d = x2.shape
    row = lambda w: pl.BlockSpec((tm, w), lambda i: (i, 0))
    full = lambda a: pl.BlockSpec(a.shape, lambda i: (0,) * a.ndim, pipeline_mode=pl.Buffered(1))
    return pl.pallas_call(
        _merge_kernel,
        grid=(n // tm,),
        in_specs=[row(d), full(gm), full(wg), row(256), row(256), row(256), row(256),
                  full(wb), full(wo), full(gf), full(wr), full(br)],
        out_specs=[row(d), row(d), row(LANES)],
        out_shape=[jax.ShapeDtypeStruct((n, d), F32), jax.ShapeDtypeStruct((n, d), F32),
                   jax.ShapeDtypeStruct((n, LANES), F32)],
        compiler_params=_cparams("arbitrary"),
        name="merge",
    )(x2, gm, wg, o_a, o_f, o_s, o_d, wb, wo, gf, wr, br)


def _rank_kernel(route_ref, ltri_ref, rank_ref, cnt_ref, carry_ref):
    @pl.when(pl.program_id(0) == 0)
    def _():
        carry_ref[...] = jnp.zeros_like(carry_ref)

    route = route_ref[...]
    lane = lax.broadcasted_iota(I32, route.shape, 1)
    e0 = route[:, 0:1].astype(I32)
    e1 = route[:, 1:2].astype(I32)
    oh0 = (lane == e0).astype(F32)
    oh1 = (lane == e1).astype(F32)
    both = oh0 + oh1
    before = jnp.dot(ltri_ref[...], both.astype(BF16), preferred_element_type=F32) + carry_ref[0:1, :]
    r0 = jnp.sum(oh0 * before, axis=1, keepdims=True)
    r1 = jnp.sum(oh1 * (before + oh0), axis=1, keepdims=True)
    rank_ref[...] = jnp.where(lane == 0, r0, jnp.where(lane == 1, r1, 0.0))
    total = carry_ref[0:1, :] + jnp.sum(both, axis=0, keepdims=True)
    carry_ref[0:1, :] = total
    cnt_ref[...] = jnp.broadcast_to(total, cnt_ref.shape)


def _rank(route, tm):
    n = route.shape[0]
    ltri = (jnp.arange(tm)[:, None] > jnp.arange(tm)[None, :]).astype(BF16)
    return pl.pallas_call(
        _rank_kernel,
        grid=(n // tm,),
        in_specs=[pl.BlockSpec((tm, LANES), lambda i: (i, 0)), pl.BlockSpec((tm, tm), lambda i: (0, 0))],
        out_specs=[pl.BlockSpec((tm, LANES), lambda i: (i, 0)), pl.BlockSpec((8, LANES), lambda i: (0, 0))],
        out_shape=[jax.ShapeDtypeStruct((n, LANES), F32), jax.ShapeDtypeStruct((8, LANES), F32)],
        scratch_shapes=[pltpu.VMEM((8, LANES), F32)],
        compiler_params=_cparams("arbitrary"),
        name="moe_rank",
    )(route, ltri)


def _expert_kernel(be_ref, nu_ref, tok_ref, tok1_ref, tok2_ref, h_hbm, wup_ref, wdn_ref, y_ref,
                   xbuf, sem, wup_b, wdn_b, *, te):
    b = pl.program_id(0)
    n_used = nu_ref[0]
    slot = b % 3

    def start_row(tokens_ref, dst, r):
        pltpu.make_async_copy(h_hbm.at[pl.ds(tokens_ref[0, 0, r], 1), :],
                              xbuf.at[dst, pl.ds(r, 1), :], sem.at[dst]).start()

    def start_block(tokens_ref, dst):
        def issue(r, c):
            start_row(tokens_ref, dst, r)
            return c
        lax.fori_loop(0, te, issue, 0, unroll=8)

    def block(prefetch):
        pltpu.make_async_copy(h_hbm.at[pl.ds(0, te), :], xbuf.at[slot], sem.at[slot]).wait()
        xb = xbuf[slot].astype(BF16)
        if prefetch:
            dst = (b + 2) % 3
            for r in range(te):
                start_row(tok2_ref, dst, r)
        gu = jnp.dot(xb, wup_b[...], preferred_element_type=F32)
        g = gu[:, :EXPERT_FF]
        act = g * jax.nn.sigmoid(g) * gu[:, EXPERT_FF:]
        y_ref[...] = jnp.dot(act.astype(BF16), wdn_b[...], preferred_element_type=F32)

    @pl.when((b < n_used) & ((b == 0) | (be_ref[b] != be_ref[jnp.maximum(b - 1, 0)])))
    def _():
        wup_b[...] = wup_ref[0, 0].astype(BF16)
        wdn_b[...] = wdn_ref[0, 0].astype(BF16)

    @pl.when((b == 0) & (n_used > 0))
    def _():
        start_block(tok_ref, 0)

    @pl.when((b == 0) & (n_used > 1))
    def _():
        start_block(tok1_ref, 1)

    @pl.when(b + 2 < n_used)
    def _():
        block(True)

    @pl.when((b < n_used) & (b + 2 >= n_used))
    def _():
        block(False)

    @pl.when(b >= n_used)
    def _():
        y_ref[...] = jnp.zeros_like(y_ref)


def _experts(blk_expert, n_used, slot_tok, h2, w_up, w_down, layer, te):
    n_blocks = blk_expert.shape[0]
    d = h2.shape[1]
    ahead = lambda k: pl.BlockSpec((1, 1, te), lambda b, be, nu: (jnp.minimum(b + k, n_blocks - 1), 0, 0),
                                   memory_space=pltpu.SMEM)
    grid_spec = pltpu.PrefetchScalarGridSpec(
        num_scalar_prefetch=2,
        grid=(n_blocks,),
        in_specs=[ahead(0), ahead(1), ahead(2),
                  pl.BlockSpec(memory_space=pl.ANY),
                  pl.BlockSpec((1, 1, d, 2 * EXPERT_FF), lambda b, be, nu: (layer, be[b], 0, 0)),
                  pl.BlockSpec((1, 1, EXPERT_FF, d), lambda b, be, nu: (layer, be[b], 0, 0))],
        out_specs=pl.BlockSpec((te, d), lambda b, be, nu: (b, 0)),
        scratch_shapes=[pltpu.VMEM((3, te, d), F32), pltpu.SemaphoreType.DMA((3,)),
                        pltpu.VMEM((d, 2 * EXPERT_FF), BF16), pltpu.VMEM((EXPERT_FF, d), BF16)],
    )
    slots = slot_tok.reshape(n_blocks, 1, te)
    return pl.pallas_call(
        functools.partial(_expert_kernel, te=te),
        grid_spec=grid_spec,
        out_shape=jax.ShapeDtypeStruct((n_blocks * te, d), F32),
        compiler_params=_cparams("arbitrary"),
        name="moe_experts",
    )(blk_expert, n_used, slots, slots, slots, h2, w_up, w_down)


def _combine_kernel(pos_ref, pos_next_ref, x_ref, route_ref, y_hbm, o_ref, ybuf, sem, *, tc):
    i = pl.program_id(0)
    slot = i % 2

    def gather(rows_ref, dst):
        def issue(r, c):
            pltpu.make_async_copy(y_hbm.at[pl.ds(rows_ref[0, 0, r], 1), :],
                                  ybuf.at[dst, pl.ds(r, 1), :], sem.at[dst]).start()
            return c
        lax.fori_loop(0, 2 * tc, issue, 0, unroll=8)

    @pl.when(i == 0)
    def _():
        gather(pos_ref, 0)

    @pl.when(i + 1 < pl.num_programs(0))
    def _():
        gather(pos_next_ref, 1 - slot)

    pltpu.make_async_copy(y_hbm.at[pl.ds(0, 2 * tc), :], ybuf.at[slot], sem.at[slot]).wait()
    route = route_ref[...]
    o_ref[...] = (x_ref[...] + route[:, 2:3] * ybuf[slot, 0:tc, :]
                  + route[:, 3:4] * ybuf[slot, tc:2 * tc, :])


def _combine(pos, x2, route, yb, tc):
    n, d = x2.shape
    nt = n // tc
    pos_t = jnp.transpose(pos.reshape(nt, tc, 2), (0, 2, 1)).reshape(nt, 1, 2 * tc)
    return pl.pallas_call(
        functools.partial(_combine_kernel, tc=tc),
        grid=(nt,),
        in_specs=[pl.BlockSpec((1, 1, 2 * tc), lambda i: (i, 0, 0), memory_space=pltpu.SMEM),
                  pl.BlockSpec((1, 1, 2 * tc), lambda i: (jnp.minimum(i + 1, nt - 1), 0, 0),
                               memory_space=pltpu.SMEM),
                  pl.BlockSpec((tc, d), lambda i: (i, 0)),
                  pl.BlockSpec((tc, LANES), lambda i: (i, 0)),
                  pl.BlockSpec(memory_space=pl.ANY)],
        out_specs=pl.BlockSpec((tc, d), lambda i: (i, 0)),
        out_shape=jax.ShapeDtypeStruct((n, d), F32),
        scratch_shapes=[pltpu.VMEM((2, 2 * tc, d), F32), pltpu.SemaphoreType.DMA((2,))],
        compiler_params=_cparams("arbitrary"),
        name="moe_combine",
    )(pos_t, pos_t, x2, route, yb)


def _swap_mid_heads(w, axis):
    h = jnp.split(w, 4, axis=axis)
    return jnp.concatenate([h[0], h[2], h[1], h[3]], axis=axis)


def _t5_lookup(t5_table, dist):
    onehot = (_t5_bucket(dist)[..., None] == jnp.arange(T5_BUCKETS)).astype(F32)
    return jnp.einsum("...b,bh->...h", onehot, t5_table.astype(F32), precision=lax.Precision.HIGHEST)


def _t5_bucket(dist):
    n = jnp.maximum(dist, 0)
    max_exact = T5_BUCKETS // 2
    nf = jnp.maximum(n, 1).astype(F32)
    large = max_exact + (jnp.log(nf / max_exact) / math.log(T5_MAX_DISTANCE / max_exact)
                         * (T5_BUCKETS - max_exact)).astype(I32)
    large = jnp.minimum(large, T5_BUCKETS - 1)
    return jnp.where(n < max_exact, n, large)


def _swa_bias(t5_table):
    t = SWA_BLOCK
    dist = t + jnp.arange(t)[None, :] - jnp.arange(2 * t)[:, None]
    tile = jnp.where(((dist >= 0) & (dist < t))[..., None], _t5_lookup(t5_table, dist)[..., :4], NEG_INF)
    return jnp.concatenate([tile[..., h] for h in (0, 2, 1, 3)], axis=1)


def _layer_weights(w_in, qk_gain, forget_bias, w_branch):
    offs = np.concatenate([[0], np.cumsum(IN_SPLITS)]).tolist()
    part = lambda k: w_in[:, offs[k]:offs[k + 1]]
    dup = lambda w: jnp.concatenate([w, w], axis=1)
    aq = _swap_mid_heads(part(0), 1)
    cols = [aq, part(1), part(2), part(3), part(4), part(5), part(7), part(8), part(9),
            part(10), dup(part(11)), dup(part(12)), part(13), dup(part(14))]
    w1 = jnp.concatenate(cols, axis=1).astype(BF16)
    d = w_in.shape[0]
    wm = _hi_lo(jnp.concatenate([part(6), part(15), jnp.zeros((d, LANES - 8), F32)], axis=1))
    wg = part(16).astype(BF16)
    tile = lambda g, reps, scale: jnp.pad(jnp.tile(g, reps) * scale, (0, 256 - reps * HEAD_DIM))
    gains = jnp.stack([tile(qk_gain[0, 0], 4, ATTN_SCALE), tile(qk_gain[0, 1], 2, 1.0),
                       tile(qk_gain[1, 0], 4, ATTN_SCALE * LOG2E), tile(qk_gain[1, 1], 4, 1.0),
                       tile(qk_gain[2, 0], 4, ATTN_SCALE * LOG2E), tile(qk_gain[2, 1], 2, 1.0),
                       jnp.zeros((256,), F32), jnp.zeros((256,), F32)]).astype(F32)
    fb = jnp.pad(forget_bias.astype(F32), (0, LANES - 4)).reshape(1, LANES)
    wb = jnp.stack([_swap_mid_heads(w_branch[0], 0), w_branch[1], w_branch[2], w_branch[3]]).astype(BF16)
    return w1, wm, wg, gains, fb, wb


def kernel(x, norm_mix_g, w_in, forget_bias, attn_sinks, qk_gain, w_branch, w_out, t5_table, norm_ffn_g,
           w_router_group, b_router_group, w_router_expert, b_router_expert, w_expert_up, w_expert_down):
    b, s, d = x.shape
    n = b * s
    depth = w_in.shape[0]
    top_k = min(DSA_TOPK_MAX, s // 4)
    tm_proj = min(512, s)
    fox_t, fox_tk = min(256, s), min(512, s)
    sb_t = min(256, s)
    dsa_t = min(256, s)
    te = 256
    tc = 128

    gseg = (jnp.arange(256)[:, None] // HEAD_DIM == jnp.arange(256)[None, :] // HEAD_DIM).astype(BF16)
    bias_swa = _swa_bias(t5_table)
    bias_dsa = _dsa_bias(t5_table, dsa_t) * LOG2E
    n_blocks = -(-2 * n // te) + N_EXPERTS
    tok_ids = jnp.repeat(jnp.arange(n, dtype=I32), 2)

    for layer in range(depth):
        w1, wm, wg, gains, fb, wb = _layer_weights(w_in[layer], qk_gain[layer], forget_bias[layer],
                                                   w_branch[layer])
        sink_row = jnp.repeat(attn_sinks[layer].astype(F32)[jnp.array([0, 2, 1, 3])], SWA_BLOCK)
        sinks = jnp.broadcast_to(sink_row[None, :], (8, 4 * SWA_BLOCK))
        (aq, ak, av, fq, fk, fv, sq, sk, sv, dq, dkk, dvv, iq, ikk, cm) = _proj(
            x, norm_mix_g[layer].reshape(1, d), w1, wm, gseg, gains, fb, tm_proj)

        o_swa = _swa(aq, ak, av, bias_swa, sinks, min(4, s // SWA_BLOCK))
        cmt = jnp.transpose(cm[:, :, :8], (0, 2, 1))
        o_fox = _fox(fq, fk, fv, cmt, cm, fox_t, fox_tk)
        o_sb = _sb(sq, sk, sv, sb_t)
        o_dsa = _dsa(dq, dkk, dvv, iq, ikk, cmt, bias_dsa, top_k, dsa_t)

        wr = _hi_lo(jnp.concatenate([w_router_group[layer], w_router_expert[layer],
                                     jnp.zeros((d, LANES - N_GROUPS - N_EXPERTS), F32)], axis=1))
        br = jnp.concatenate([b_router_group[layer], b_router_expert[layer],
                              jnp.zeros((LANES - N_GROUPS - N_EXPERTS,), F32)]).reshape(1, LANES)
        x2, h2, route = _merge(
            x.reshape(n, d), norm_mix_g[layer].reshape(1, d), wg,
            o_swa.reshape(n, 256), o_fox.reshape(n, 256), o_sb.reshape(n, 256), o_dsa.reshape(n, 256),
            wb, w_out[layer].astype(BF16), norm_ffn_g[layer].reshape(1, d), wr, br, min(512, n))

        rank, cnt = _rank(route, min(512, n))
        counts = cnt[0, :N_EXPERTS].astype(I32)
        padded = (counts + te - 1) // te * te
        pend = jnp.cumsum(padded)
        pstart = pend - padded
        expert = route[:, :2].astype(I32)
        own = expert[:, :, None] == jnp.arange(N_EXPERTS, dtype=I32)
        pos = jnp.sum(jnp.where(own, pstart, 0), axis=-1) + rank[:, :2].astype(I32)
        slot_tok = jnp.zeros((n_blocks * te,), I32).at[pos.reshape(-1)].set(tok_ids)
        blk_start = jnp.arange(n_blocks, dtype=I32)[:, None] * te
        blk_expert = jnp.minimum(jnp.sum((pend[None, :] <= blk_start).astype(I32), axis=1), N_EXPERTS - 1)
        n_used = (pend[-1:] // te).astype(I32)

        yb = _experts(blk_expert, n_used, slot_tok, h2, w_expert_up, w_expert_down, layer, te)
        x = _combine(pos, x2, route, yb, tc).reshape(b, s, d)
    return x
```

```python
import functools
import math

import jax
import jax.numpy as jnp
import numpy as np
from jax import lax
from jax.experimental import pallas as pl
from jax.experimental.pallas import tpu as pltpu

F32 = jnp.float32
BF16 = jnp.bfloat16
I32 = jnp.int32

HEAD_DIM = 64
LANES = 128
NORM_EPS = 1e-6
NEG_INF = -1e30
M_INIT = -1e29
ATTN_SCALE = HEAD_DIM ** -0.5
LOG2E = math.log2(math.e)
SWA_BLOCK = 128
IDX_SCALE = 64 ** -0.5
IDX_HEADS = 4
DSA_TOPK_MAX = 256
T5_BUCKETS = 32
T5_MAX_DISTANCE = 128
N_GROUPS = 4
EXPERTS_PER_GROUP = 8
N_EXPERTS = N_GROUPS * EXPERTS_PER_GROUP
EXPERT_FF = 512
SB_DEAD = -110.0
INT_MIN = -2 ** 31
VMEM_LIMIT = 56 * 1024 * 1024

IN_SPLITS = (256, 128, 128, 256, 256, 256, 4, 256, 256, 256, 256, 64, 64, 256, 64, 4, 4096)

_SEG = dict(aq=(0, 256), ak=(256, 128), av=(384, 128), fq=(512, 256), fk=(768, 256), fv=(1024, 256),
            sq=(1280, 256), sk=(1536, 256), sv=(1792, 256), dq=(2048, 256), dkk=(2304, 128),
            dvv=(2432, 128), iq=(2560, 256), ikk=(2816, 128))
_W1_COLS = 2944
_SEG_ORDER = ("aq", "ak", "av", "fq", "fk", "fv", "sq", "sk", "sv", "dq", "dkk", "dvv", "iq", "ikk")


def _cparams(*sem):
    return pltpu.CompilerParams(dimension_semantics=sem, vmem_limit_bytes=VMEM_LIMIT)


def _rms(x, g):
    return x * lax.rsqrt(jnp.mean(x * x, axis=-1, keepdims=True) + NORM_EPS) * g


def _log_sigmoid(z):
    return jnp.minimum(z, 0.0) - jnp.log(1.0 + jnp.exp(-jnp.abs(z)))


def _dot_nt(a, b):
    return lax.dot_general(a, b, (((1,), (1,)), ((), ())), preferred_element_type=F32)


def _split3(x):
    p1 = x.astype(BF16)
    r = x - p1.astype(F32)
    p2 = r.astype(BF16)
    return p1, p2, (r - p2.astype(F32)).astype(BF16)


def _hi_lo(w):
    hi = w.astype(BF16)
    return jnp.stack([hi, (w - hi.astype(F32)).astype(BF16)])


def _dot_x3(a, b_hi, b_lo):
    a_hi = a.astype(BF16)
    a_lo = (a - a_hi.astype(F32)).astype(BF16)
    return (jnp.dot(a_hi, b_hi, preferred_element_type=F32) + jnp.dot(a_lo, b_hi, preferred_element_type=F32)
            + jnp.dot(a_hi, b_lo, preferred_element_type=F32))


def _split_heads(qp):
    lo = lax.broadcasted_iota(I32, (1, LANES), 1) < HEAD_DIM
    zero = jnp.zeros_like(qp)
    return jnp.concatenate([jnp.where(lo, qp, zero), jnp.where(lo, zero, qp)], axis=0)


def _merge_heads(o, t):
    lo = lax.broadcasted_iota(I32, (1, LANES), 1) < HEAD_DIM
    return jnp.where(lo, o[:t], o[t:])


def _proj_kernel(x_ref, g_ref, w1_ref, wm_ref, gseg_ref, gains_ref, fb_ref, ltri_ref, *rest):
    outs = dict(zip(_SEG_ORDER, rest[:len(_SEG_ORDER)]))
    cm_ref = rest[len(_SEG_ORDER)]
    carry_ref = rest[len(_SEG_ORDER) + 1]

    @pl.when(pl.program_id(1) == 0)
    def _():
        carry_ref[...] = jnp.zeros_like(carry_ref)

    h = _rms(x_ref[0], g_ref[...])
    hb = h.astype(BF16)

    def seg(name):
        off, width = _SEG[name]
        return jnp.dot(hb, w1_ref[:, off:off + width], preferred_element_type=F32)

    def head_norm(t, row):
        width = t.shape[1]
        ssq = jnp.dot((t * t).astype(BF16), gseg_ref[:width, :width], preferred_element_type=F32)
        return t * lax.rsqrt(ssq * (1.0 / HEAD_DIM) + NORM_EPS) * gains_ref[row:row + 1, :width]

    normed = dict(aq=0, ak=1, fq=2, fk=3, dq=4, dkk=5)
    scaled = dict(sq=ATTN_SCALE, iq=IDX_SCALE)
    for name in _SEG_ORDER:
        t = seg(name)
        if name in normed:
            t = head_norm(t, normed[name])
        elif name in scaled:
            t = t * scaled[name]
        outs[name][0] = t.astype(BF16)

    misc = _dot_x3(h, wm_ref[0], wm_ref[1])
    lane = lax.broadcasted_iota(I32, misc.shape, 1)
    logf = jnp.where(lane < 4, _log_sigmoid(misc + fb_ref[...]), 0.0)
    ltri = ltri_ref[...]
    c = carry_ref[0:1, :]
    for piece in _split3(logf):
        c = c + jnp.dot(ltri, piece, preferred_element_type=F32)
    tm = misc.shape[0]
    carry_ref[0:1, :] = c[tm - 1:tm, :]
    cm_ref[0] = jnp.where(lane < 4, c, misc)


def _proj(x, g, w1, wm, gseg, gains, fb, tm):
    b, s, d = x.shape
    ltri = jnp.tril(jnp.ones((tm, tm), BF16))
    full = lambda shape: pl.BlockSpec(shape, lambda bi, si: (0,) * len(shape))
    out_shapes = [jax.ShapeDtypeStruct((b, s, _SEG[n][1]), BF16) for n in _SEG_ORDER]
    out_shapes.append(jax.ShapeDtypeStruct((b, s, LANES), F32))
    out_specs = [pl.BlockSpec((1, tm, _SEG[n][1]), lambda bi, si: (bi, si, 0)) for n in _SEG_ORDER]
    out_specs.append(pl.BlockSpec((1, tm, LANES), lambda bi, si: (bi, si, 0)))
    return pl.pallas_call(
        _proj_kernel,
        grid=(b, s // tm),
        in_specs=[pl.BlockSpec((1, tm, d), lambda bi, si: (bi, si, 0)),
                  full((1, d)), full(w1.shape), full(wm.shape), full(gseg.shape),
                  full(gains.shape), full(fb.shape), full((tm, tm))],
        out_specs=out_specs,
        out_shape=out_shapes,
        scratch_shapes=[pltpu.VMEM((8, LANES), F32)],
        compiler_params=_cparams("arbitrary", "arbitrary"),
        name="proj",
    )(x, g, w1, wm, gseg, gains, fb, ltri)


def _swa_kernel(q_ref, kp_ref, kc_ref, vtp_ref, vtc_ref, bias_ref, sink_ref, o_ref, *, nsub):
    i = pl.program_id(1)
    t = SWA_BLOCK
    no_prev = (lax.broadcasted_iota(I32, (2 * t, 4 * t), 0) < t) & (i == 0)
    sink = sink_ref[0:1, :]
    for u in range(nsub):
        rows = slice(u * t, (u + 1) * t)
        q = q_ref[0, rows, :]
        qs = jnp.concatenate([_split_heads(q[:, :LANES]), _split_heads(q[:, LANES:])], axis=0)
        if u == 0:
            kcat = jnp.concatenate([kp_ref[0], kc_ref[0, rows, :]], axis=0)
            vt = jnp.concatenate([vtp_ref[0], vtc_ref[0, :, rows]], axis=1)
        else:
            kcat = kc_ref[0, (u - 1) * t:(u + 1) * t, :]
            vt = vtc_ref[0, :, (u - 1) * t:(u + 1) * t]
        st = _dot_nt(kcat, qs) + bias_ref[...]
        if u == 0:
            st = jnp.where(no_prev, NEG_INF, st)
        m = jnp.maximum(jnp.max(st, axis=0, keepdims=True), sink)
        p = jnp.exp(st - m)
        denom = jnp.sum(p, axis=0, keepdims=True) + jnp.exp(sink - m)
        ot = jnp.dot(vt, p.astype(BF16), preferred_element_type=F32) / denom
        ot = jnp.concatenate([ot[:HEAD_DIM, 0:t], ot[HEAD_DIM:, t:2 * t],
                              ot[:HEAD_DIM, 2 * t:3 * t], ot[HEAD_DIM:, 3 * t:]], axis=0)
        o_ref[0, rows, :] = ot.T.astype(BF16)


def _swa(aq, ak, av, bias, sinks, nsub):
    b, s, _ = aq.shape
    t = SWA_BLOCK
    avt = jnp.transpose(av, (0, 2, 1))
    cur = lambda bi, i: (bi, i, 0)
    prev = lambda bi, i: (bi, jnp.maximum(i * nsub - 1, 0), 0)
    return pl.pallas_call(
        functools.partial(_swa_kernel, nsub=nsub),
        grid=(b, s // (t * nsub)),
        in_specs=[pl.BlockSpec((1, t * nsub, 256), cur),
                  pl.BlockSpec((1, t, LANES), prev), pl.BlockSpec((1, t * nsub, LANES), cur),
                  pl.BlockSpec((1, LANES, t), lambda bi, i: (bi, 0, jnp.maximum(i * nsub - 1, 0))),
                  pl.BlockSpec((1, LANES, t * nsub), lambda bi, i: (bi, 0, i)),
                  pl.BlockSpec(bias.shape, lambda bi, i: (0, 0)),
                  pl.BlockSpec(sinks.shape, lambda bi, i: (0, 0))],
        out_specs=pl.BlockSpec((1, t * nsub, 256), cur),
        out_shape=jax.ShapeDtypeStruct((b, s, 256), BF16),
        compiler_params=_cparams("arbitrary", "arbitrary"),
        name="swa",
    )(aq, ak, ak, avt, avt, bias, sinks)


def _fox_kernel(q_ref, k_ref, vt_ref, ct_ref, ccol_ref, o_ref, ckb_ref, *, t):
    i = pl.program_id(1)
    n_tiles = ckb_ref.shape[1] // t

    @pl.when(i == 0)
    def _():
        def fill(j, c):
            rows = pl.ds(pl.multiple_of(j * t, t), t)
            cc = ccol_ref[0, rows, :] * LOG2E
            for h in range(4):
                ckb_ref[h, rows, :] = jnp.broadcast_to(cc[:, h:h + 1], (t, LANES))
            return c
        lax.fori_loop(0, n_tiles, fill, 0)

    q = q_ref[0]
    ct = ct_ref[0] * LOG2E
    qs = [_split_heads(q[:, :LANES]), _split_heads(q[:, LANES:])]
    valid = lax.broadcasted_iota(I32, (t, t), 0) <= lax.broadcasted_iota(I32, (t, t), 1)

    def tiles(j0, n, carry, masked):
        m, l, accs = carry
        rows = pl.ds(pl.multiple_of(j0 * t, t), n * t)
        cols = []
        for pair in range(2):
            st = _dot_nt(k_ref[0, rows, pair * LANES:(pair + 1) * LANES], qs[pair])
            for hh in range(2):
                head = 2 * pair + hh
                ck = ckb_ref[head, rows, :]
                for c in range(t // LANES):
                    cs = slice(c * LANES, (c + 1) * LANES)
                    sh = st[:, hh * t + c * LANES:hh * t + (c + 1) * LANES] + (ct[head:head + 1, cs] - ck)
                    if masked:
                        sh = jnp.where(valid[:, cs], sh, NEG_INF)
                    cols.append(sh)
        st = jnp.concatenate(cols, axis=1)
        m_new = jnp.maximum(m, jnp.max(st, axis=0, keepdims=True))
        alpha = jnp.exp2(m - m_new)
        p = jnp.exp2(st - m_new)
        l = alpha * l + jnp.sum(p, axis=0, keepdims=True)
        pb = p.astype(BF16)
        new = []
        for pair in range(2):
            lanes = slice(pair * 2 * t, (pair + 1) * 2 * t)
            vt = jnp.concatenate([vt_ref[0, j0 + u, pair * LANES:(pair + 1) * LANES, :] for u in range(n)],
                                 axis=1)
            pv = jnp.dot(vt, pb[:, lanes], preferred_element_type=F32)
            new.append(alpha[:, lanes] * accs[pair] + pv)
        return m_new, l, tuple(new)

    init = (jnp.full((1, 4 * t), M_INIT, F32), jnp.zeros((1, 4 * t), F32),
            (jnp.zeros((LANES, 2 * t), F32), jnp.zeros((LANES, 2 * t), F32)))
    carry = lax.fori_loop(0, i // 4, lambda g, c: tiles(4 * g, 4, c, False), init)
    rest = i - i % 4
    carry = lax.cond(i % 4 >= 2, lambda c: tiles(rest, 2, c, False), lambda c: c, carry)
    carry = lax.cond(i % 2 == 1, lambda c: tiles(i - 1, 1, c, False), lambda c: c, carry)
    _, l, accs = tiles(i, 1, carry, True)
    outs = []
    for pair in range(2):
        o = accs[pair] / l[:, pair * 2 * t:(pair + 1) * 2 * t]
        outs.append(jnp.concatenate([o[:HEAD_DIM, :t], o[HEAD_DIM:, t:]], axis=0))
    o_ref[0] = jnp.concatenate(outs, axis=0).T.astype(BF16)


def _fox(fq, fk, fv, ct, ccol, t):
    b, s, _ = fq.shape
    nt = s // t
    vt = jnp.transpose(fv.reshape(b, nt, t, 256), (0, 1, 3, 2))
    return pl.pallas_call(
        functools.partial(_fox_kernel, t=t),
        grid=(b, nt),
        in_specs=[pl.BlockSpec((1, t, 256), lambda bi, i: (bi, i, 0)),
                  pl.BlockSpec((1, s, 256), lambda bi, i: (bi, 0, 0)),
                  pl.BlockSpec((1, nt, 256, t), lambda bi, i: (bi, 0, 0, 0)),
                  pl.BlockSpec((1, 8, t), lambda bi, i: (bi, 0, i)),
                  pl.BlockSpec((1, s, LANES), lambda bi, i: (bi, 0, 0))],
        out_specs=pl.BlockSpec((1, t, 256), lambda bi, i: (bi, i, 0)),
        out_shape=jax.ShapeDtypeStruct((b, s, 256), BF16),
        scratch_shapes=[pltpu.VMEM((4, s, LANES), F32)],
        compiler_params=_cparams("arbitrary", "arbitrary"),
        name="fox",
    )(fq, fk, vt, ct, ccol)


def _sb_kernel(q_ref, k_ref, vt_ref, lgt_ref, o_ref, *, t):
    i = pl.program_id(1)
    q = q_ref[0]
    lgt = lgt_ref[...]
    qs = [_split_heads(q[:, :LANES]), _split_heads(q[:, LANES:])]
    key_i = lax.broadcasted_iota(I32, (t, 4 * t), 0)
    query_i = lax.broadcasted_iota(I32, (t, 4 * t), 1) & (t - 1)
    strict = key_i < query_i

    def tile(j, r, accs, masked):
        rows = pl.ds(pl.multiple_of(j * t, t), t)
        z = jnp.concatenate([_dot_nt(k_ref[0, rows, p * LANES:(p + 1) * LANES], qs[p]) for p in range(2)],
                            axis=1)
        sp = jnp.log(1.0 + jnp.exp(-jnp.abs(z)))
        log_beta = jnp.minimum(z, 0.0) - sp
        log_keep = jnp.minimum(-z, 0.0) - sp
        if masked:
            log_keep = jnp.where(strict, log_keep, 0.0)
        hi = log_keep.astype(BF16)
        lo = (log_keep - hi.astype(F32)).astype(BF16)
        later = (jnp.dot(lgt, hi, preferred_element_type=F32)
                 + jnp.dot(lgt, lo, preferred_element_type=F32))
        a = jnp.exp(log_beta + later + r)
        if masked:
            a = jnp.where(strict, a, 0.0)
        ab = a.astype(BF16)
        new = tuple(accs[p] + jnp.dot(vt_ref[0, j, p * LANES:(p + 1) * LANES, :],
                                      ab[:, p * 2 * t:(p + 1) * 2 * t], preferred_element_type=F32)
                    for p in range(2))
        return r + jnp.sum(log_keep, axis=0, keepdims=True), new

    zero_acc = jnp.zeros((LANES, 2 * t), F32)
    r, accs = tile(i, jnp.zeros((1, 4 * t), F32), (zero_acc, zero_acc), True)

    def cond(c):
        return (c[0] >= 0) & (c[1] > 0)

    def body(c):
        j, _, r, accs = c
        r, accs = tile(j, r, accs, False)
        return j - 1, (jnp.max(r) > SB_DEAD).astype(I32), r, accs

    _, _, _, accs = lax.while_loop(cond, body, (i - 1, (jnp.max(r) > SB_DEAD).astype(I32), r, accs))
    outs = [jnp.concatenate([accs[p][:HEAD_DIM, :t], accs[p][HEAD_DIM:, t:]], axis=0) for p in range(2)]
    o_ref[0] = jnp.concatenate(outs, axis=0).T.astype(BF16)


def _sb(sq, sk, sv, t):
    b, s, _ = sq.shape
    nt = s // t
    lgt = (jnp.arange(t)[:, None] < jnp.arange(t)[None, :]).astype(BF16)
    vt = jnp.transpose(sv.reshape(b, nt, t, 256), (0, 1, 3, 2))
    return pl.pallas_call(
        functools.partial(_sb_kernel, t=t),
        grid=(b, nt),
        in_specs=[pl.BlockSpec((1, t, 256), lambda bi, i: (bi, i, 0)),
                  pl.BlockSpec((1, s, 256), lambda bi, i: (bi, 0, 0)),
                  pl.BlockSpec((1, nt, 256, t), lambda bi, i: (bi, 0, 0, 0)),
                  pl.BlockSpec((t, t), lambda bi, i: (0, 0))],
        out_specs=pl.BlockSpec((1, t, 256), lambda bi, i: (bi, i, 0)),
        out_shape=jax.ShapeDtypeStruct((b, s, 256), BF16),
        compiler_params=_cparams("arbitrary", "arbitrary"),
        name="stickbreak",
    )(sq, sk, vt, lgt)


def _bit_planes(words):
    words = list(words)
    j, m = 16, 0x0000FFFF
    while j:
        k = 0
        while k < 32:
            tt = (words[k] ^ lax.shift_right_logical(words[k + j], jnp.full_like(words[k + j], j))) & m
            words[k] = words[k] ^ tt
            words[k + j] = words[k + j] ^ (tt << j)
            k = (k + j + 1) & ~j
        j >>= 1
        m = (m ^ (m << j)) & 0xFFFFFFFF
    return words


def _dsa_kernel(q_ref, kk_ref, vt_ref, iq_ref, ikk_ref, wt_ref, bias_ref, lstrict_ref, o_ref,
                key_ref, plane_ref, *, t, top_k):
    i = pl.program_id(1)
    assert t == 8 * 32

    @pl.when(i == 0)
    def _():
        plane_ref[...] = jnp.zeros_like(plane_ref)
    causal = lax.broadcasted_iota(I32, (t, t), 0) <= lax.broadcasted_iota(I32, (t, t), 1)

    def head_stack(x):
        return jnp.concatenate([_split_heads(x[:, :LANES]), _split_heads(x[:, LANES:])], axis=0)

    def key_rows(ref, j0, n):
        return ref[0, pl.ds(pl.multiple_of(j0 * t, t), n * t), :]

    iqs = head_stack(iq_ref[0])
    wt = wt_ref[0]
    w = [wt[4 + h:5 + h, :] * (IDX_HEADS ** -0.5) for h in range(IDX_HEADS)]

    def score_tiles(j0, n, masked):
        lg = _dot_nt(key_rows(ikk_ref, j0, n), iqs)
        sc = w[0] * jnp.maximum(lg[:, 0:t], 0.0)
        for h in range(1, IDX_HEADS):
            sc = sc + w[h] * jnp.maximum(lg[:, h * t:(h + 1) * t], 0.0)
        bits = pltpu.bitcast(sc, I32)
        key = bits ^ ((bits >> 31) & 0x7FFFFFFF)
        key = jnp.where(key == -1, 0, key)
        if masked:
            key = jnp.where(causal, key, INT_MIN)
        for u in range(n):
            key_u = key[u * t:(u + 1) * t]
            key_ref[j0 + u] = key_u
            for p, plane in enumerate(_bit_planes([key_u[8 * g:8 * g + 8, :] for g in range(32)])):
                plane_ref[p, j0 + u] = plane

    def p1(p, c):
        score_tiles(2 * p, 2, False)
        return c

    lax.fori_loop(0, i // 2, p1, 0)

    @pl.when(i % 2 == 1)
    def _():
        score_tiles(i - 1, 1, False)

    score_tiles(i, 1, True)

    def popcount_rows(words):
        per_tile = jnp.sum(lax.population_count(words), axis=0)
        return jnp.sum(per_tile.astype(F32), axis=0, keepdims=True)

    def bis_body(p, c):
        alive, n_gt, thr_u = c
        plane = plane_ref[p] ^ jnp.where(p == 0, -1, 0)
        ones = alive & plane
        cnt = popcount_rows(ones)
        take = n_gt + cnt >= top_k
        alive = jnp.where(take, ones, alive ^ ones)
        n_gt = jnp.where(take, n_gt, n_gt + cnt)
        thr_u = jnp.where(take, thr_u | (jnp.int32(1) << (31 - p)), thr_u)
        return alive, n_gt, thr_u

    n_tiles = key_ref.shape[0]
    alive0 = jnp.where(lax.broadcasted_iota(I32, (n_tiles, 8, t), 0) <= i, -1, 0)
    alive, n_gt, thr_u = lax.fori_loop(
        0, 32, bis_body, (alive0, jnp.zeros((1, t), F32), jnp.zeros((1, t), I32)))
    thr = jnp.maximum(thr_u ^ INT_MIN, INT_MIN + 1)
    n_avail = (i * t + lax.broadcasted_iota(I32, (1, t), 1) + 1).astype(F32)
    n_ge = jnp.where(n_avail > top_k, n_gt + popcount_rows(alive), 0.0)
    surplus = jnp.max(n_ge) > top_k

    def tie_pass():
        need = top_k - n_gt

        def tb(j, seen):
            k = key_ref[j]
            eq = k == thr
            eqf = jnp.where(eq, 1.0, 0.0)
            before = jnp.dot(lstrict_ref[...], eqf.astype(BF16), preferred_element_type=F32) + seen
            sel = (k > thr) | (eq & (before < need))
            key_ref[j] = jnp.where(sel, 1, INT_MIN)
            return seen + jnp.sum(eqf, axis=0, keepdims=True)

        lax.fori_loop(0, i + 1, tb, jnp.zeros((1, t), F32))
        return jnp.zeros((1, t), I32)

    thr = lax.cond(surplus, tie_pass, lambda: thr)

    qs = head_stack(q_ref[0])

    def attn_tiles(j0, n, carry, bias_kind):
        m, l, acc = carry
        st = _dot_nt(key_rows(kk_ref, j0, n), qs)
        selb = jnp.concatenate([jnp.where(key_ref[j0 + u] >= thr, 0.0, NEG_INF) for u in range(n)], axis=0)
        parts = []
        for h in range(4):
            sh = st[:, h * t:(h + 1) * t] + selb
            if bias_kind is not None:
                sh = sh + bias_ref[bias_kind, h]
            parts.append(sh)
        st = jnp.concatenate(parts, axis=1)
        m_new = jnp.maximum(m, jnp.max(st, axis=0, keepdims=True))
        alpha = jnp.exp2(m - m_new)
        p = jnp.exp2(st - m_new)
        l = alpha * l + jnp.sum(p, axis=0, keepdims=True)
        vt = jnp.concatenate([vt_ref[0, j0 + u] for u in range(n)], axis=1)
        acc = alpha * acc + jnp.dot(vt, p.astype(BF16), preferred_element_type=F32)
        return m_new, l, acc

    init = (jnp.full((1, 4 * t), M_INIT, F32), jnp.zeros((1, 4 * t), F32), jnp.zeros((HEAD_DIM, 4 * t), F32))
    far = jnp.maximum(i - 1, 0)
    carry = lax.fori_loop(0, far // 4, lambda g, c: attn_tiles(4 * g, 4, c, None), init)
    carry = lax.cond(far % 4 >= 2, lambda c: attn_tiles(far - far % 4, 2, c, None), lambda c: c, carry)
    carry = lax.cond(far % 2 == 1, lambda c: attn_tiles(far - 1, 1, c, None), lambda c: c, carry)
    carry = lax.cond(i > 0, lambda c: attn_tiles(i - 1, 1, c, 0), lambda c: c, carry)
    _, l, acc = attn_tiles(i, 1, carry, 1)
    ot = acc / l
    ot = jnp.concatenate([ot[:, h * t:(h + 1) * t] for h in range(4)], axis=0)
    o_ref[0] = ot.T.astype(BF16)


def _dsa_bias(t5_table, t):
    assert t + 1 >= T5_MAX_DISTANCE
    k = jnp.arange(t)[:, None]
    q = jnp.arange(t)[None, :]
    far = t5_table[T5_BUCKETS - 1, 4:].astype(F32)
    tiles = []
    for off in (t, 0):
        dist = off + q - k
        b = jnp.transpose(_t5_lookup(t5_table, dist)[..., 4:], (2, 0, 1)) - far[:, None, None]
        tiles.append(jnp.where((dist >= 0)[None], b, 0.0))
    return jnp.stack(tiles)


def _dsa(dq, dkk, dvv, iq, ikk, wt, bias, top_k, t):
    b, s, _ = dq.shape
    nt = s // t
    lstrict = (jnp.arange(t)[:, None] > jnp.arange(t)[None, :]).astype(BF16)
    vt = jnp.transpose(dvv[:, :, :HEAD_DIM].reshape(b, nt, t, HEAD_DIM), (0, 1, 3, 2))
    blk = lambda w: pl.BlockSpec((1, t, w), lambda bi, i: (bi, i, 0))
    seq = lambda w: pl.BlockSpec((1, s, w), lambda bi, i: (bi, 0, 0))
    return pl.pallas_call(
        functools.partial(_dsa_kernel, t=t, top_k=top_k),
        grid=(b, nt),
        in_specs=[blk(256), seq(LANES),
                  pl.BlockSpec((1, nt, HEAD_DIM, t), lambda bi, i: (bi, 0, 0, 0)),
                  blk(256), seq(LANES),
                  pl.BlockSpec((1, 8, t), lambda bi, i: (bi, 0, i)),
                  pl.BlockSpec(bias.shape, lambda bi, i: (0, 0, 0, 0)),
                  pl.BlockSpec((t, t), lambda bi, i: (0, 0))],
        out_specs=blk(256),
        out_shape=jax.ShapeDtypeStruct((b, s, 256), BF16),
        scratch_shapes=[pltpu.VMEM((nt, t, t), I32), pltpu.VMEM((32, nt, 8, t), I32)],
        compiler_params=_cparams("arbitrary", "arbitrary"),
        name="dsa",
    )(dq, dkk, vt, iq, ikk, wt, bias, lstrict)


def _merge_kernel(x_ref, gm_ref, wg_ref, oa_ref, of_ref, os_ref, od_ref, wb_ref, wo_ref, gf_ref,
                  wr_ref, br_ref, xo_ref, h2_ref, route_ref):
    x = x_ref[...]
    hb = _rms(x, gm_ref[...]).astype(BF16)
    d = x.shape[1]
    merged = None
    for bi, o_ref in enumerate((oa_ref, of_ref, os_ref, od_ref)):
        gate = jax.nn.sigmoid(jnp.dot(hb, wg_ref[:, bi * d:(bi + 1) * d], preferred_element_type=F32))
        term = gate * jnp.dot(o_ref[...], wb_ref[bi], preferred_element_type=F32)
        merged = term if merged is None else merged + term
    xn = x + jnp.dot(merged.astype(BF16), wo_ref[...], preferred_element_type=F32)
    xo_ref[...] = xn
    h2 = _rms(xn, gf_ref[...])
    h2_ref[...] = h2

    logits = _dot_x3(h2, wr_ref[0], wr_ref[1]) + br_ref[...]
    lane = lax.broadcasted_iota(I32, logits.shape, 1).astype(F32)
    big = 1e9
    gl = jnp.where(lane < N_GROUPS, logits, -jnp.inf)
    gmax = jnp.max(gl, axis=1, keepdims=True)
    grp = jnp.min(jnp.where(gl == gmax, lane, big), axis=1, keepdims=True)
    p_grp = 1.0 / jnp.sum(jnp.exp(gl - gmax), axis=1, keepdims=True)
    first = N_GROUPS + grp * EXPERTS_PER_GROUP
    el = jnp.where((lane >= first) & (lane < first + EXPERTS_PER_GROUP), logits, -jnp.inf)
    l1 = jnp.max(el, axis=1, keepdims=True)
    i1 = jnp.min(jnp.where(el == l1, lane, big), axis=1, keepdims=True)
    el2 = jnp.where(lane == i1, -jnp.inf, el)
    l2 = jnp.max(el2, axis=1, keepdims=True)
    i2 = jnp.min(jnp.where(el2 == l2, lane, big), axis=1, keepdims=True)
    e2 = jnp.exp(l2 - l1)
    g1 = p_grp / (1.0 + e2)
    g2 = p_grp * e2 / (1.0 + e2)
    route = jnp.where(lane == 0, i1 - N_GROUPS,
                      jnp.where(lane == 1, i2 - N_GROUPS,
                                jnp.where(lane == 2, g1, jnp.where(lane == 3, g2, 0.0))))
    route_ref[...] = route


def _merge(x2, gm, wg, o_a, o_f, o_s, o_d, wb, wo, gf, wr, br, tm):
    n, d = x2.shape
    row = lambda w: pl.BlockSpec((tm, w), lambda i: (i, 0))
    full = lambda a: pl.BlockSpec(a.shape, lambda i: (0,) * a.ndim, pipeline_mode=pl.Buffered(1))
    return pl.pallas_call(
        _merge_kernel,
        grid=(n // tm,),
        in_specs=[row(d), full(gm), full(wg), row(256), row(256), row(256), row(256),
                  full(wb), full(wo), full(gf), full(wr), full(br)],
        out_specs=[row(d), row(d), row(LANES)],
        out_shape=[jax.ShapeDtypeStruct((n, d), F32), jax.ShapeDtypeStruct((n, d), F32),
                   jax.ShapeDtypeStruct((n, LANES), F32)],
        compiler_params=_cparams("arbitrary"),
        name="merge",
    )(x2, gm, wg, o_a, o_f, o_s, o_d, wb, wo, gf, wr, br)


def _rank_kernel(route_ref, ltri_ref, rank_ref, cnt_ref, carry_ref):
    @pl.when(pl.program_id(0) == 0)
    def _():
        carry_ref[...] = jnp.zeros_like(carry_ref)

    route = route_ref[...]
    lane = lax.broadcasted_iota(I32, route.shape, 1)
    e0 = route[:, 0:1].astype(I32)
    e1 = route[:, 1:2].astype(I32)
    oh0 = (lane == e0).astype(F32)
    oh1 = (lane == e1).astype(F32)
    both = oh0 + oh1
    before = jnp.dot(ltri_ref[...], both.astype(BF16), preferred_element_type=F32) + carry_ref[0:1, :]
    r0 = jnp.sum(oh0 * before, axis=1, keepdims=True)
    r1 = jnp.sum(oh1 * (before + oh0), axis=1, keepdims=True)
    rank_ref[...] = jnp.where(lane == 0, r0, jnp.where(lane == 1, r1, 0.0))
    total = carry_ref[0:1, :] + jnp.sum(both, axis=0, keepdims=True)
    carry_ref[0:1, :] = total
    cnt_ref[...] = jnp.broadcast_to(total, cnt_ref.shape)


def _rank(route, tm):
    n = route.shape[0]
    ltri = (jnp.arange(tm)[:, None] > jnp.arange(tm)[None, :]).astype(BF16)
    return pl.pallas_call(
        _rank_kernel,
        grid=(n // tm,),
        in_specs=[pl.BlockSpec((tm, LANES), lambda i: (i, 0)), pl.BlockSpec((tm, tm), lambda i: (0, 0))],
        out_specs=[pl.BlockSpec((tm, LANES), lambda i: (i, 0)), pl.BlockSpec((8, LANES), lambda i: (0, 0))],
        out_shape=[jax.ShapeDtypeStruct((n, LANES), F32), jax.ShapeDtypeStruct((8, LANES), F32)],
        scratch_shapes=[pltpu.VMEM((8, LANES), F32)],
        compiler_params=_cparams("arbitrary"),
        name="moe_rank",
    )(route, ltri)


def _expert_kernel(be_ref, nu_ref, tok_ref, tok1_ref, tok2_ref, h_hbm, wup_ref, wdn_ref, y_ref,
                   xbuf, sem, wup_b, wdn_b, *, te):
    b = pl.program_id(0)
    n_used = nu_ref[0]
    slot = b % 3

    def start_row(tokens_ref, dst, r):
        pltpu.make_async_copy(h_hbm.at[pl.ds(tokens_ref[0, 0, r], 1), :],
                              xbuf.at[dst, pl.ds(r, 1), :], sem.at[dst]).start()

    def start_block(tokens_ref, dst):
        def issue(r, c):
            start_row(tokens_ref, dst, r)
            return c
        lax.fori_loop(0, te, issue, 0, unroll=8)

    def block(prefetch):
        pltpu.make_async_copy(h_hbm.at[pl.ds(0, te), :], xbuf.at[slot], sem.at[slot]).wait()
        xb = xbuf[slot].astype(BF16)
        if prefetch:
            dst = (b + 2) % 3
            for r in range(te):
                start_row(tok2_ref, dst, r)
        gu = jnp.dot(xb, wup_b[...], preferred_element_type=F32)
        g = gu[:, :EXPERT_FF]
        act = g * jax.nn.sigmoid(g) * gu[:, EXPERT_FF:]
        y_ref[...] = jnp.dot(act.astype(BF16), wdn_b[...], preferred_element_type=F32)

    @pl.when((b < n_used) & ((b == 0) | (be_ref[b] != be_ref[jnp.maximum(b - 1, 0)])))
    def _():
        wup_b[...] = wup_ref[0, 0].astype(BF16)
        wdn_b[...] = wdn_ref[0, 0].astype(BF16)

    @pl.when((b == 0) & (n_used > 0))
    def _():
        start_block(tok_ref, 0)

    @pl.when((b == 0) & (n_used > 1))
    def _():
        start_block(tok1_ref, 1)

    @pl.when(b + 2 < n_used)
    def _():
        block(True)

    @pl.when((b < n_used) & (b + 2 >= n_used))
    def _():
        block(False)

    @pl.when(b >= n_used)
    def _():
        y_ref[...] = jnp.zeros_like(y_ref)


def _experts(blk_expert, n_used, slot_tok, h2, w_up, w_down, layer, te):
    n_blocks = blk_expert.shape[0]
    d = h2.shape[1]
    ahead = lambda k: pl.BlockSpec((1, 1, te), lambda b, be, nu: (jnp.minimum(b + k, n_blocks - 1), 0, 0),
                                   memory_space=pltpu.SMEM)
    grid_spec = pltpu.PrefetchScalarGridSpec(
        num_scalar_prefetch=2,
        grid=(n_blocks,),
        in_specs=[ahead(0), ahead(1), ahead(2),
                  pl.BlockSpec(memory_space=pl.ANY),
                  pl.BlockSpec((1, 1, d, 2 * EXPERT_FF), lambda b, be, nu: (layer, be[b], 0, 0)),
                  pl.BlockSpec((1, 1, EXPERT_FF, d), lambda b, be, nu: (layer, be[b], 0, 0))],
        out_specs=pl.BlockSpec((te, d), lambda b, be, nu: (b, 0)),
        scratch_shapes=[pltpu.VMEM((3, te, d), F32), pltpu.SemaphoreType.DMA((3,)),
                        pltpu.VMEM((d, 2 * EXPERT_FF), BF16), pltpu.VMEM((EXPERT_FF, d), BF16)],
    )
    slots = slot_tok.reshape(n_blocks, 1, te)
    return pl.pallas_call(
        functools.partial(_expert_kernel, te=te),
        grid_spec=grid_spec,
        out_shape=jax.ShapeDtypeStruct((n_blocks * te, d), F32),
        compiler_params=_cparams("arbitrary"),
        name="moe_experts",
    )(blk_expert, n_used, slots, slots, slots, h2, w_up, w_down)


def _combine_kernel(pos_ref, pos_next_ref, x_ref, route_ref, y_hbm, o_ref, ybuf, sem, *, tc):
    i = pl.program_id(0)
    slot = i % 2

    def gather(rows_ref, dst):
        def issue(r, c):
            pltpu.make_async_copy(y_hbm.at[pl.ds(rows_ref[0, 0, r], 1), :],
                                  ybuf.at[dst, pl.ds(r, 1), :], sem.at[dst]).start()
            return c
        lax.fori_loop(0, 2 * tc, issue, 0, unroll=8)

    @pl.when(i == 0)
    def _():
        gather(pos_ref, 0)

    @pl.when(i + 1 < pl.num_programs(0))
    def _():
        gather(pos_next_ref, 1 - slot)

    pltpu.make_async_copy(y_hbm.at[pl.ds(0, 2 * tc), :], ybuf.at[slot], sem.at[slot]).wait()
    route = route_ref[...]
    o_ref[...] = (x_ref[...] + route[:, 2:3] * ybuf[slot, 0:tc, :]
                  + route[:, 3:4] * ybuf[slot, tc:2 * tc, :])


def _combine(pos, x2, route, yb, tc):
    n, d = x2.shape
    nt = n // tc
    pos_t = jnp.transpose(pos.reshape(nt, tc, 2), (0, 2, 1)).reshape(nt, 1, 2 * tc)
    return pl.pallas_call(
        functools.partial(_combine_kernel, tc=tc),
        grid=(nt,),
        in_specs=[pl.BlockSpec((1, 1, 2 * tc), lambda i: (i, 0, 0), memory_space=pltpu.SMEM),
                  pl.BlockSpec((1, 1, 2 * tc), lambda i: (jnp.minimum(i + 1, nt - 1), 0, 0),
                               memory_space=pltpu.SMEM),
                  pl.BlockSpec((tc, d), lambda i: (i, 0)),
                  pl.BlockSpec((tc, LANES), lambda i: (i, 0)),
                  pl.BlockSpec(memory_space=pl.ANY)],
        out_specs=pl.BlockSpec((tc, d), lambda i: (i, 0)),
        out_shape=jax.ShapeDtypeStruct((n, d), F32),
        scratch_shapes=[pltpu.VMEM((2, 2 * tc, d), F32), pltpu.SemaphoreType.DMA((2,))],
        compiler_params=_cparams("arbitrary"),
        name="moe_combine",
    )(pos_t, pos_t, x2, route, yb)


def _swap_mid_heads(w, axis):
    h = jnp.split(w, 4, axis=axis)
    return jnp.concatenate([h[0], h[2], h[1], h[3]], axis=axis)


def _t5_lookup(t5_table, dist):
    onehot = (_t5_bucket(dist)[..., None] == jnp.arange(T5_BUCKETS)).astype(F32)
    return jnp.einsum("...b,bh->...h", onehot, t5_table.astype(F32), precision=lax.Precision.HIGHEST)


def _t5_bucket(dist):
    n = jnp.maximum(dist, 0)
    max_exact = T5_BUCKETS // 2
    nf = jnp.maximum(n, 1).astype(F32)
    large = max_exact + (jnp.log(nf / max_exact) / math.log(T5_MAX_DISTANCE / max_exact)
                         * (T5_BUCKETS - max_exact)).astype(I32)
    large = jnp.minimum(large, T5_BUCKETS - 1)
    return jnp.where(n < max_exact, n, large)


def _swa_bias(t5_table):
    t = SWA_BLOCK
    dist = t + jnp.arange(t)[None, :] - jnp.arange(2 * t)[:, None]
    tile = jnp.where(((dist >= 0) & (dist < t))[..., None], _t5_lookup(t5_table, dist)[..., :4], NEG_INF)
    return jnp.concatenate([tile[..., h] for h in (0, 2, 1, 3)], axis=1)


def _layer_weights(w_in, qk_gain, forget_bias, w_branch):
    offs = np.concatenate([[0], np.cumsum(IN_SPLITS)]).tolist()
    part = lambda k: w_in[:, offs[k]:offs[k + 1]]
    dup = lambda w: jnp.concatenate([w, w], axis=1)
    aq = _swap_mid_heads(part(0), 1)
    cols = [aq, part(1), part(2), part(3), part(4), part(5), part(7), part(8), part(9),
            part(10), dup(part(11)), dup(part(12)), part(13), dup(part(14))]
    w1 = jnp.concatenate(cols, axis=1).astype(BF16)
    d = w_in.shape[0]
    wm = _hi_lo(jnp.concatenate([part(6), part(15), jnp.zeros((d, LANES - 8), F32)], axis=1))
    wg = part(16).astype(BF16)
    tile = lambda g, reps, scale: jnp.pad(jnp.tile(g, reps) * scale, (0, 256 - reps * HEAD_DIM))
    gains = jnp.stack([tile(qk_gain[0, 0], 4, ATTN_SCALE), tile(qk_gain[0, 1], 2, 1.0),
                       tile(qk_gain[1, 0], 4, ATTN_SCALE * LOG2E), tile(qk_gain[1, 1], 4, 1.0),
                       tile(qk_gain[2, 0], 4, ATTN_SCALE * LOG2E), tile(qk_gain[2, 1], 2, 1.0),
                       jnp.zeros((256,), F32), jnp.zeros((256,), F32)]).astype(F32)
    fb = jnp.pad(forget_bias.astype(F32), (0, LANES - 4)).reshape(1, LANES)
    wb = jnp.stack([_swap_mid_heads(w_branch[0], 0), w_branch[1], w_branch[2], w_branch[3]]).astype(BF16)
    return w1, wm, wg, gains, fb, wb


def kernel(x, norm_mix_g, w_in, forget_bias, attn_sinks, qk_gain, w_branch, w_out, t5_table, norm_ffn_g,
           w_router_group, b_router_group, w_router_expert, b_router_expert, w_expert_up, w_expert_down):
    b, s, d = x.shape
    n = b * s
    depth = w_in.shape[0]
    top_k = min(DSA_TOPK_MAX, s // 4)
    tm_proj = min(512, s)
    fox_t = min(256, s)
    sb_t = min(256, s)
    dsa_t = min(256, s)
    te = 256
    tc = 128

    gseg = (jnp.arange(256)[:, None] // HEAD_DIM == jnp.arange(256)[None, :] // HEAD_DIM).astype(BF16)
    bias_swa = _swa_bias(t5_table)
    bias_dsa = _dsa_bias(t5_table, dsa_t) * LOG2E
    n_blocks = -(-2 * n // te) + N_EXPERTS
    tok_ids = jnp.repeat(jnp.arange(n, dtype=I32), 2)

    for layer in range(depth):
        w1, wm, wg, gains, fb, wb = _layer_weights(w_in[layer], qk_gain[layer], forget_bias[layer],
                                                   w_branch[layer])
        sink_row = jnp.repeat(attn_sinks[layer].astype(F32)[jnp.array([0, 2, 1, 3])], SWA_BLOCK)
        sinks = jnp.broadcast_to(sink_row[None, :], (8, 4 * SWA_BLOCK))
        (aq, ak, av, fq, fk, fv, sq, sk, sv, dq, dkk, dvv, iq, ikk, cm) = _proj(
            x, norm_mix_g[layer].reshape(1, d), w1, wm, gseg, gains, fb, tm_proj)

        o_swa = _swa(aq, ak, av, bias_swa, sinks, min(4, s // SWA_BLOCK))
        cmt = jnp.transpose(cm[:, :, :8], (0, 2, 1))
        o_fox = _fox(fq, fk, fv, cmt, cm, fox_t)
        o_sb = _sb(sq, sk, sv, sb_t)
        o_dsa = _dsa(dq, dkk, dvv, iq, ikk, cmt, bias_dsa, top_k, dsa_t)

        wr = _hi_lo(jnp.concatenate([w_router_group[layer], w_router_expert[layer],
                                     jnp.zeros((d, LANES - N_GROUPS - N_EXPERTS), F32)], axis=1))
        br = jnp.concatenate([b_router_group[layer], b_router_expert[layer],
                              jnp.zeros((LANES - N_GROUPS - N_EXPERTS,), F32)]).reshape(1, LANES)
        x2, h2, route = _merge(
            x.reshape(n, d), norm_mix_g[layer].reshape(1, d), wg,
            o_swa.reshape(n, 256), o_fox.reshape(n, 256), o_sb.reshape(n, 256), o_dsa.reshape(n, 256),
            wb, w_out[layer].astype(BF16), norm_ffn_g[layer].reshape(1, d), wr, br, min(512, n))

        rank, cnt = _rank(route, min(512, n))
        counts = cnt[0, :N_EXPERTS].astype(I32)
        padded = (counts + te - 1) // te * te
        pend = jnp.cumsum(padded)
        pstart = pend - padded
        expert = route[:, :2].astype(I32)
        own = expert[:, :, None] == jnp.arange(N_EXPERTS, dtype=I32)
        pos = jnp.sum(jnp.where(own, pstart, 0), axis=-1) + rank[:, :2].astype(I32)
        slot_tok = jnp.zeros((n_blocks * te,), I32).at[pos.reshape(-1)].set(tok_ids)
        blk_start = jnp.arange(n_blocks, dtype=I32)[:, None] * te
        blk_expert = jnp.minimum(jnp.sum((pend[None, :] <= blk_start).astype(I32), axis=1), N_EXPERTS - 1)
        n_used = (pend[-1:] // te).astype(I32)

        yb = _experts(blk_expert, n_used, slot_tok, h2, w_expert_up, w_expert_down, layer, te)
        x = _combine(pos, x2, route, yb, tc).reshape(b, s, d)
    return x
```

```python
import functools
import math

import jax
import jax.numpy as jnp
import numpy as np
from jax import lax
from jax.experimental import pallas as pl
from jax.experimental.pallas import tpu as pltpu

F32 = jnp.float32
BF16 = jnp.bfloat16
I32 = jnp.int32

HEAD_DIM = 64
LANES = 128
NORM_EPS = 1e-6
NEG_INF = -1e30
M_INIT = -1e29
ATTN_SCALE = HEAD_DIM ** -0.5
LOG2E = math.log2(math.e)
SWA_BLOCK = 128
IDX_SCALE = 64 ** -0.5
IDX_HEADS = 4
DSA_TOPK_MAX = 256
T5_BUCKETS = 32
T5_MAX_DISTANCE = 128
N_GROUPS = 4
EXPERTS_PER_GROUP = 8
N_EXPERTS = N_GROUPS * EXPERTS_PER_GROUP
EXPERT_FF = 512
SB_DEAD = -110.0
INT_MIN = -2 ** 31
VMEM_LIMIT = 56 * 1024 * 1024

IN_SPLITS = (256, 128, 128, 256, 256, 256, 4, 256, 256, 256, 256, 64, 64, 256, 64, 4, 4096)

_SEG = dict(aq=(0, 256), ak=(256, 128), av=(384, 128), fq=(512, 256), fk=(768, 256), fv=(1024, 256),
            sq=(1280, 256), sk=(1536, 256), sv=(1792, 256), dq=(2048, 256), dkk=(2304, 128),
            dvv=(2432, 128), iq=(2560, 256), ikk=(2816, 128))
_W1_COLS = 2944
_SEG_ORDER = ("aq", "ak", "av", "fq", "fk", "fv", "sq", "sk", "sv", "dq", "dkk", "dvv", "iq", "ikk")


def _cparams(*sem):
    return pltpu.CompilerParams(dimension_semantics=sem, vmem_limit_bytes=VMEM_LIMIT)


def _rms(x, g):
    return x * lax.rsqrt(jnp.mean(x * x, axis=-1, keepdims=True) + NORM_EPS) * g


def _log_sigmoid(z):
    return jnp.minimum(z, 0.0) - jnp.log(1.0 + jnp.exp(-jnp.abs(z)))


def _dot_nt(a, b):
    return lax.dot_general(a, b, (((1,), (1,)), ((), ())), preferred_element_type=F32)


def _split3(x):
    p1 = x.astype(BF16)
    r = x - p1.astype(F32)
    p2 = r.astype(BF16)
    return p1, p2, (r - p2.astype(F32)).astype(BF16)


def _hi_lo(w):
    hi = w.astype(BF16)
    return jnp.stack([hi, (w - hi.astype(F32)).astype(BF16)])


def _dot_x3(a, b_hi, b_lo):
    a_hi = a.astype(BF16)
    a_lo = (a - a_hi.astype(F32)).astype(BF16)
    return (jnp.dot(a_hi, b_hi, preferred_element_type=F32) + jnp.dot(a_lo, b_hi, preferred_element_type=F32)
            + jnp.dot(a_hi, b_lo, preferred_element_type=F32))


def _split_heads(qp):
    lo = lax.broadcasted_iota(I32, (1, LANES), 1) < HEAD_DIM
    zero = jnp.zeros_like(qp)
    return jnp.concatenate([jnp.where(lo, qp, zero), jnp.where(lo, zero, qp)], axis=0)


def _merge_heads(o, t):
    lo = lax.broadcasted_iota(I32, (1, LANES), 1) < HEAD_DIM
    return jnp.where(lo, o[:t], o[t:])


def _proj_kernel(x_ref, g_ref, w1_ref, wm_ref, gseg_ref, gains_ref, fb_ref, ltri_ref, *rest):
    outs = dict(zip(_SEG_ORDER, rest[:len(_SEG_ORDER)]))
    cm_ref = rest[len(_SEG_ORDER)]
    carry_ref = rest[len(_SEG_ORDER) + 1]

    @pl.when(pl.program_id(1) == 0)
    def _():
        carry_ref[...] = jnp.zeros_like(carry_ref)

    h = _rms(x_ref[0], g_ref[...])
    hb = h.astype(BF16)

    def seg(name):
        off, width = _SEG[name]
        return jnp.dot(hb, w1_ref[:, off:off + width], preferred_element_type=F32)

    def head_norm(t, row):
        width = t.shape[1]
        ssq = jnp.dot((t * t).astype(BF16), gseg_ref[:width, :width], preferred_element_type=F32)
        return t * lax.rsqrt(ssq * (1.0 / HEAD_DIM) + NORM_EPS) * gains_ref[row:row + 1, :width]

    normed = dict(aq=0, ak=1, fq=2, fk=3, dq=4, dkk=5)
    scaled = dict(sq=ATTN_SCALE, iq=IDX_SCALE)
    for name in _SEG_ORDER:
        t = seg(name)
        if name in normed:
            t = head_norm(t, normed[name])
        elif name in scaled:
            t = t * scaled[name]
        outs[name][0] = t.astype(BF16)

    misc = _dot_x3(h, wm_ref[0], wm_ref[1])
    lane = lax.broadcasted_iota(I32, misc.shape, 1)
    logf = jnp.where(lane < 4, _log_sigmoid(misc + fb_ref[...]), 0.0)
    ltri = ltri_ref[...]
    c = carry_ref[0:1, :]
    for piece in _split3(logf):
        c = c + jnp.dot(ltri, piece, preferred_element_type=F32)
    tm = misc.shape[0]
    carry_ref[0:1, :] = c[tm - 1:tm, :]
    cm_ref[0] = jnp.where(lane < 4, c, misc)


def _proj(x, g, w1, wm, gseg, gains, fb, tm):
    b, s, d = x.shape
    ltri = jnp.tril(jnp.ones((tm, tm), BF16))
    full = lambda shape: pl.BlockSpec(shape, lambda bi, si: (0,) * len(shape))
    out_shapes = [jax.ShapeDtypeStruct((b, s, _SEG[n][1]), BF16) for n in _SEG_ORDER]
    out_shapes.append(jax.ShapeDtypeStruct((b, s, LANES), F32))
    out_specs = [pl.BlockSpec((1, tm, _SEG[n][1]), lambda bi, si: (bi, si, 0)) for n in _SEG_ORDER]
    out_specs.append(pl.BlockSpec((1, tm, LANES), lambda bi, si: (bi, si, 0)))
    return pl.pallas_call(
        _proj_kernel,
        grid=(b, s // tm),
        in_specs=[pl.BlockSpec((1, tm, d), lambda bi, si: (bi, si, 0)),
                  full((1, d)), full(w1.shape), full(wm.shape), full(gseg.shape),
                  full(gains.shape), full(fb.shape), full((tm, tm))],
        out_specs=out_specs,
        out_shape=out_shapes,
        scratch_shapes=[pltpu.VMEM((8, LANES), F32)],
        compiler_params=_cparams("arbitrary", "arbitrary"),
        name="proj",
    )(x, g, w1, wm, gseg, gains, fb, ltri)


def _swa_kernel(q_ref, kp_ref, kc_ref, vtp_ref, vtc_ref, bias_ref, sink_ref, o_ref, *, nsub):
    i = pl.program_id(1)
    t = SWA_BLOCK
    no_prev = (lax.broadcasted_iota(I32, (2 * t, 4 * t), 0) < t) & (i == 0)
    sink = sink_ref[0:1, :]
    for u in range(nsub):
        rows = slice(u * t, (u + 1) * t)
        q = q_ref[0, rows, :]
        qs = jnp.concatenate([_split_heads(q[:, :LANES]), _split_heads(q[:, LANES:])], axis=0)
        if u == 0:
            kcat = jnp.concatenate([kp_ref[0], kc_ref[0, rows, :]], axis=0)
            vt = jnp.concatenate([vtp_ref[0], vtc_ref[0, :, rows]], axis=1)
        else:
            kcat = kc_ref[0, (u - 1) * t:(u + 1) * t, :]
            vt = vtc_ref[0, :, (u - 1) * t:(u + 1) * t]
        st = _dot_nt(kcat, qs) + bias_ref[...]
        if u == 0:
            st = jnp.where(no_prev, NEG_INF, st)
        m = jnp.maximum(jnp.max(st, axis=0, keepdims=True), sink)
        p = jnp.exp(st - m)
        denom = jnp.sum(p, axis=0, keepdims=True) + jnp.exp(sink - m)
        ot = jnp.dot(vt, p.astype(BF16), preferred_element_type=F32) / denom
        ot = jnp.concatenate([ot[:HEAD_DIM, 0:t], ot[HEAD_DIM:, t:2 * t],
                              ot[:HEAD_DIM, 2 * t:3 * t], ot[HEAD_DIM:, 3 * t:]], axis=0)
        o_ref[0, rows, :] = ot.T.astype(BF16)


def _swa(aq, ak, av, bias, sinks, nsub):
    b, s, _ = aq.shape
    t = SWA_BLOCK
    avt = jnp.transpose(av, (0, 2, 1))
    cur = lambda bi, i: (bi, i, 0)
    prev = lambda bi, i: (bi, jnp.maximum(i * nsub - 1, 0), 0)
    return pl.pallas_call(
        functools.partial(_swa_kernel, nsub=nsub),
        grid=(b, s // (t * nsub)),
        in_specs=[pl.BlockSpec((1, t * nsub, 256), cur),
                  pl.BlockSpec((1, t, LANES), prev), pl.BlockSpec((1, t * nsub, LANES), cur),
                  pl.BlockSpec((1, LANES, t), lambda bi, i: (bi, 0, jnp.maximum(i * nsub - 1, 0))),
                  pl.BlockSpec((1, LANES, t * nsub), lambda bi, i: (bi, 0, i)),
                  pl.BlockSpec(bias.shape, lambda bi, i: (0, 0)),
                  pl.BlockSpec(sinks.shape, lambda bi, i: (0, 0))],
        out_specs=pl.BlockSpec((1, t * nsub, 256), cur),
        out_shape=jax.ShapeDtypeStruct((b, s, 256), BF16),
        compiler_params=_cparams("arbitrary", "arbitrary"),
        name="swa",
    )(aq, ak, ak, avt, avt, bias, sinks)


def _fox_kernel(q_ref, k_ref, vt_ref, ct_ref, ccol_ref, o_ref, ckb_ref, *, t):
    i = pl.program_id(1)
    n_tiles = ckb_ref.shape[1] // t

    @pl.when(i == 0)
    def _():
        def fill(j, c):
            rows = pl.ds(pl.multiple_of(j * t, t), t)
            cc = ccol_ref[0, rows, :] * LOG2E
            for h in range(4):
                ckb_ref[h, rows, :] = jnp.broadcast_to(cc[:, h:h + 1], (t, LANES))
            return c
        lax.fori_loop(0, n_tiles, fill, 0)

    q = q_ref[0]
    ct = ct_ref[0] * LOG2E
    qs = [_split_heads(q[:, :LANES]), _split_heads(q[:, LANES:])]
    valid = lax.broadcasted_iota(I32, (t, t), 0) <= lax.broadcasted_iota(I32, (t, t), 1)

    def tiles(j0, n, carry, masked):
        m, l, accs = carry
        rows = pl.ds(pl.multiple_of(j0 * t, t), n * t)
        cols = []
        for pair in range(2):
            st = _dot_nt(k_ref[0, rows, pair * LANES:(pair + 1) * LANES], qs[pair])
            for hh in range(2):
                head = 2 * pair + hh
                ck = ckb_ref[head, rows, :]
                for c in range(t // LANES):
                    cs = slice(c * LANES, (c + 1) * LANES)
                    sh = st[:, hh * t + c * LANES:hh * t + (c + 1) * LANES] + (ct[head:head + 1, cs] - ck)
                    if masked:
                        sh = jnp.where(valid[:, cs], sh, NEG_INF)
                    cols.append(sh)
        st = jnp.concatenate(cols, axis=1)
        m_new = jnp.maximum(m, jnp.max(st, axis=0, keepdims=True))
        alpha = jnp.exp2(m - m_new)
        p = jnp.exp2(st - m_new)
        l = alpha * l + jnp.sum(p, axis=0, keepdims=True)
        pb = p.astype(BF16)
        new = []
        for pair in range(2):
            lanes = slice(pair * 2 * t, (pair + 1) * 2 * t)
            vt = jnp.concatenate([vt_ref[0, j0 + u, pair * LANES:(pair + 1) * LANES, :] for u in range(n)],
                                 axis=1)
            pv = jnp.dot(vt, pb[:, lanes], preferred_element_type=F32)
            new.append(alpha[:, lanes] * accs[pair] + pv)
        return m_new, l, tuple(new)

    init = (jnp.full((1, 4 * t), M_INIT, F32), jnp.zeros((1, 4 * t), F32),
            (jnp.zeros((LANES, 2 * t), F32), jnp.zeros((LANES, 2 * t), F32)))
    carry = lax.fori_loop(0, i // 4, lambda g, c: tiles(4 * g, 4, c, False), init)
    rest = i - i % 4
    carry = lax.cond(i % 4 >= 2, lambda c: tiles(rest, 2, c, False), lambda c: c, carry)
    carry = lax.cond(i % 2 == 1, lambda c: tiles(i - 1, 1, c, False), lambda c: c, carry)
    _, l, accs = tiles(i, 1, carry, True)
    outs = []
    for pair in range(2):
        o = accs[pair] / l[:, pair * 2 * t:(pair + 1) * 2 * t]
        outs.append(jnp.concatenate([o[:HEAD_DIM, :t], o[HEAD_DIM:, t:]], axis=0))
    o_ref[0] = jnp.concatenate(outs, axis=0).T.astype(BF16)


def _fox(fq, fk, fv, ct, ccol, t):
    b, s, _ = fq.shape
    nt = s // t
    vt = jnp.transpose(fv.reshape(b, nt, t, 256), (0, 1, 3, 2))
    return pl.pallas_call(
        functools.partial(_fox_kernel, t=t),
        grid=(b, nt),
        in_specs=[pl.BlockSpec((1, t, 256), lambda bi, i: (bi, i, 0)),
                  pl.BlockSpec((1, s, 256), lambda bi, i: (bi, 0, 0)),
                  pl.BlockSpec((1, nt, 256, t), lambda bi, i: (bi, 0, 0, 0)),
                  pl.BlockSpec((1, 8, t), lambda bi, i: (bi, 0, i)),
                  pl.BlockSpec((1, s, LANES), lambda bi, i: (bi, 0, 0))],
        out_specs=pl.BlockSpec((1, t, 256), lambda bi, i: (bi, i, 0)),
        out_shape=jax.ShapeDtypeStruct((b, s, 256), BF16),
        scratch_shapes=[pltpu.VMEM((4, s, LANES), F32)],
        compiler_params=_cparams("arbitrary", "arbitrary"),
        name="fox",
    )(fq, fk, vt, ct, ccol)


def _sb_kernel(q_ref, k_ref, vt_ref, lgt_ref, o_ref, *, t):
    i = pl.program_id(1)
    q = q_ref[0]
    lgt = lgt_ref[...]
    qs = [_split_heads(q[:, :LANES]), _split_heads(q[:, LANES:])]
    key_i = lax.broadcasted_iota(I32, (t, 4 * t), 0)
    query_i = lax.broadcasted_iota(I32, (t, 4 * t), 1) & (t - 1)
    strict = key_i < query_i

    def tile(j, r, accs, masked):
        rows = pl.ds(pl.multiple_of(j * t, t), t)
        z = jnp.concatenate([_dot_nt(k_ref[0, rows, p * LANES:(p + 1) * LANES], qs[p]) for p in range(2)],
                            axis=1)
        sp = jnp.log(1.0 + jnp.exp(-jnp.abs(z)))
        log_beta = jnp.minimum(z, 0.0) - sp
        log_keep = jnp.minimum(-z, 0.0) - sp
        if masked:
            log_keep = jnp.where(strict, log_keep, 0.0)
        hi = log_keep.astype(BF16)
        lo = (log_keep - hi.astype(F32)).astype(BF16)
        later = (jnp.dot(lgt, hi, preferred_element_type=F32)
                 + jnp.dot(lgt, lo, preferred_element_type=F32))
        a = jnp.exp(log_beta + later + r)
        if masked:
            a = jnp.where(strict, a, 0.0)
        ab = a.astype(BF16)
        new = tuple(accs[p] + jnp.dot(vt_ref[0, j, p * LANES:(p + 1) * LANES, :],
                                      ab[:, p * 2 * t:(p + 1) * 2 * t], preferred_element_type=F32)
                    for p in range(2))
        return r + jnp.sum(log_keep, axis=0, keepdims=True), new

    zero_acc = jnp.zeros((LANES, 2 * t), F32)
    r, accs = tile(i, jnp.zeros((1, 4 * t), F32), (zero_acc, zero_acc), True)

    def cond(c):
        return (c[0] >= 0) & (c[1] > 0)

    def body(c):
        j, _, r, accs = c
        r, accs = tile(j, r, accs, False)
        return j - 1, (jnp.max(r) > SB_DEAD).astype(I32), r, accs

    _, _, _, accs = lax.while_loop(cond, body, (i - 1, (jnp.max(r) > SB_DEAD).astype(I32), r, accs))
    outs = [jnp.concatenate([accs[p][:HEAD_DIM, :t], accs[p][HEAD_DIM:, t:]], axis=0) for p in range(2)]
    o_ref[0] = jnp.concatenate(outs, axis=0).T.astype(BF16)


def _sb(sq, sk, sv, t):
    b, s, _ = sq.shape
    nt = s // t
    lgt = (jnp.arange(t)[:, None] < jnp.arange(t)[None, :]).astype(BF16)
    vt = jnp.transpose(sv.reshape(b, nt, t, 256), (0, 1, 3, 2))
    return pl.pallas_call(
        functools.partial(_sb_kernel, t=t),
        grid=(b, nt),
        in_specs=[pl.BlockSpec((1, t, 256), lambda bi, i: (bi, i, 0)),
                  pl.BlockSpec((1, s, 256), lambda bi, i: (bi, 0, 0)),
                  pl.BlockSpec((1, nt, 256, t), lambda bi, i: (bi, 0, 0, 0)),
                  pl.BlockSpec((t, t), lambda bi, i: (0, 0))],
        out_specs=pl.BlockSpec((1, t, 256), lambda bi, i: (bi, i, 0)),
        out_shape=jax.ShapeDtypeStruct((b, s, 256), BF16),
        compiler_params=_cparams("arbitrary", "arbitrary"),
        name="stickbreak",
    )(sq, sk, vt, lgt)


def _bit_planes(words):
    words = list(words)
    j, m = 16, 0x0000FFFF
    while j:
        k = 0
        while k < 32:
            tt = (words[k] ^ lax.shift_right_logical(words[k + j], jnp.full_like(words[k + j], j))) & m
            words[k] = words[k] ^ tt
            words[k + j] = words[k + j] ^ (tt << j)
            k = (k + j + 1) & ~j
        j >>= 1
        m = (m ^ (m << j)) & 0xFFFFFFFF
    return words


def _dsa_kernel(q_ref, kk_ref, vt_ref, iq_ref, ikk_ref, wt_ref, bias_ref, lstrict_ref, o_ref,
                key_ref, plane_ref, *, t, top_k):
    i = pl.program_id(1)
    assert t == 8 * 32

    @pl.when(i == 0)
    def _():
        plane_ref[...] = jnp.zeros_like(plane_ref)
    causal = lax.broadcasted_iota(I32, (t, t), 0) <= lax.broadcasted_iota(I32, (t, t), 1)

    def head_stack(x):
        return jnp.concatenate([_split_heads(x[:, :LANES]), _split_heads(x[:, LANES:])], axis=0)

    def key_rows(ref, j0, n):
        return ref[0, pl.ds(pl.multiple_of(j0 * t, t), n * t), :]

    iqs = head_stack(iq_ref[0])
    wt = wt_ref[0]
    w = [wt[4 + h:5 + h, :] * (IDX_HEADS ** -0.5) for h in range(IDX_HEADS)]

    def score_tiles(j0, n, masked):
        lg = _dot_nt(key_rows(ikk_ref, j0, n), iqs)
        sc = w[0] * jnp.maximum(lg[:, 0:t], 0.0)
        for h in range(1, IDX_HEADS):
            sc = sc + w[h] * jnp.maximum(lg[:, h * t:(h + 1) * t], 0.0)
        bits = pltpu.bitcast(sc, I32)
        key = bits ^ ((bits >> 31) & 0x7FFFFFFF)
        key = jnp.where(key == -1, 0, key)
        if masked:
            key = jnp.where(causal, key, INT_MIN)
        for u in range(n):
            key_u = key[u * t:(u + 1) * t]
            key_ref[j0 + u] = key_u
            for p, plane in enumerate(_bit_planes([key_u[8 * g:8 * g + 8, :] for g in range(32)])):
                plane_ref[p, j0 + u] = plane

    def p1(g, c):
        score_tiles(4 * g, 4, False)
        return c

    lax.fori_loop(0, i // 4, p1, 0)

    @pl.when(i % 4 >= 2)
    def _():
        score_tiles(i - i % 4, 2, False)

    @pl.when(i % 2 == 1)
    def _():
        score_tiles(i - 1, 1, False)

    score_tiles(i, 1, True)

    def popcount_rows(words):
        per_tile = jnp.sum(lax.population_count(words), axis=0)
        return jnp.sum(per_tile.astype(F32), axis=0, keepdims=True)

    def bis_body(p, c):
        alive, n_gt, thr_u = c
        plane = plane_ref[p] ^ jnp.where(p == 0, -1, 0)
        ones = alive & plane
        cnt = popcount_rows(ones)
        take = n_gt + cnt >= top_k
        alive = jnp.where(take, ones, alive ^ ones)
        n_gt = jnp.where(take, n_gt, n_gt + cnt)
        thr_u = jnp.where(take, thr_u | (jnp.int32(1) << (31 - p)), thr_u)
        return alive, n_gt, thr_u

    n_tiles = key_ref.shape[0]
    alive0 = jnp.where(lax.broadcasted_iota(I32, (n_tiles, 8, t), 0) <= i, -1, 0)
    alive, n_gt, thr_u = lax.fori_loop(
        0, 32, bis_body, (alive0, jnp.zeros((1, t), F32), jnp.zeros((1, t), I32)))
    thr = jnp.maximum(thr_u ^ INT_MIN, INT_MIN + 1)
    n_avail = (i * t + lax.broadcasted_iota(I32, (1, t), 1) + 1).astype(F32)
    n_ge = jnp.where(n_avail > top_k, n_gt + popcount_rows(alive), 0.0)
    surplus = jnp.max(n_ge) > top_k

    def tie_pass():
        need = top_k - n_gt

        def tb(j, seen):
            k = key_ref[j]
            eq = k == thr
            eqf = jnp.where(eq, 1.0, 0.0)
            before = jnp.dot(lstrict_ref[...], eqf.astype(BF16), preferred_element_type=F32) + seen
            sel = (k > thr) | (eq & (before < need))
            key_ref[j] = jnp.where(sel, 1, INT_MIN)
            return seen + jnp.sum(eqf, axis=0, keepdims=True)

        lax.fori_loop(0, i + 1, tb, jnp.zeros((1, t), F32))
        return jnp.zeros((1, t), I32)

    thr = lax.cond(surplus, tie_pass, lambda: thr)

    qs = head_stack(q_ref[0])

    def attn_tiles(j0, n, carry, bias_kind):
        m, l, acc = carry
        st = _dot_nt(key_rows(kk_ref, j0, n), qs)
        selb = jnp.concatenate([jnp.where(key_ref[j0 + u] >= thr, 0.0, NEG_INF) for u in range(n)], axis=0)
        parts = []
        for h in range(4):
            sh = st[:, h * t:(h + 1) * t] + selb
            if bias_kind is not None:
                sh = sh + bias_ref[bias_kind, h]
            parts.append(sh)
        st = jnp.concatenate(parts, axis=1)
        m_new = jnp.maximum(m, jnp.max(st, axis=0, keepdims=True))
        alpha = jnp.exp2(m - m_new)
        p = jnp.exp2(st - m_new)
        l = alpha * l + jnp.sum(p, axis=0, keepdims=True)
        vt = jnp.concatenate([vt_ref[0, j0 + u] for u in range(n)], axis=1)
        acc = alpha * acc + jnp.dot(vt, p.astype(BF16), preferred_element_type=F32)
        return m_new, l, acc

    init = (jnp.full((1, 4 * t), M_INIT, F32), jnp.zeros((1, 4 * t), F32), jnp.zeros((HEAD_DIM, 4 * t), F32))
    far = jnp.maximum(i - 1, 0)
    carry = lax.fori_loop(0, far // 4, lambda g, c: attn_tiles(4 * g, 4, c, None), init)
    carry = lax.cond(far % 4 >= 2, lambda c: attn_tiles(far - far % 4, 2, c, None), lambda c: c, carry)
    carry = lax.cond(far % 2 == 1, lambda c: attn_tiles(far - 1, 1, c, None), lambda c: c, carry)
    carry = lax.cond(i > 0, lambda c: attn_tiles(i - 1, 1, c, 0), lambda c: c, carry)
    _, l, acc = attn_tiles(i, 1, carry, 1)
    ot = acc / l
    ot = jnp.concatenate([ot[:, h * t:(h + 1) * t] for h in range(4)], axis=0)
    o_ref[0] = ot.T.astype(BF16)


def _dsa_bias(t5_table, t):
    assert t + 1 >= T5_MAX_DISTANCE
    k = jnp.arange(t)[:, None]
    q = jnp.arange(t)[None, :]
    far = t5_table[T5_BUCKETS - 1, 4:].astype(F32)
    tiles = []
    for off in (t, 0):
        dist = off + q - k
        b = jnp.transpose(_t5_lookup(t5_table, dist)[..., 4:], (2, 0, 1)) - far[:, None, None]
        tiles.append(jnp.where((dist >= 0)[None], b, 0.0))
    return jnp.stack(tiles)


def _dsa(dq, dkk, dvv, iq, ikk, wt, bias, top_k, t):
    b, s, _ = dq.shape
    nt = s // t
    lstrict = (jnp.arange(t)[:, None] > jnp.arange(t)[None, :]).astype(BF16)
    vt = jnp.transpose(dvv[:, :, :HEAD_DIM].reshape(b, nt, t, HEAD_DIM), (0, 1, 3, 2))
    blk = lambda w: pl.BlockSpec((1, t, w), lambda bi, i: (bi, i, 0))
    seq = lambda w: pl.BlockSpec((1, s, w), lambda bi, i: (bi, 0, 0))
    return pl.pallas_call(
        functools.partial(_dsa_kernel, t=t, top_k=top_k),
        grid=(b, nt),
        in_specs=[blk(256), seq(LANES),
                  pl.BlockSpec((1, nt, HEAD_DIM, t), lambda bi, i: (bi, 0, 0, 0)),
                  blk(256), seq(LANES),
                  pl.BlockSpec((1, 8, t), lambda bi, i: (bi, 0, i)),
                  pl.BlockSpec(bias.shape, lambda bi, i: (0, 0, 0, 0)),
                  pl.BlockSpec((t, t), lambda bi, i: (0, 0))],
        out_specs=blk(256),
        out_shape=jax.ShapeDtypeStruct((b, s, 256), BF16),
        scratch_shapes=[pltpu.VMEM((nt, t, t), I32), pltpu.VMEM((32, nt, 8, t), I32)],
        compiler_params=_cparams("arbitrary", "arbitrary"),
        name="dsa",
    )(dq, dkk, vt, iq, ikk, wt, bias, lstrict)


def _merge_kernel(x_ref, gm_ref, wg_ref, oa_ref, of_ref, os_ref, od_ref, wb_ref, wo_ref, gf_ref,
                  wr_ref, br_ref, xo_ref, h2_ref, route_ref):
    x = x_ref[...]
    hb = _rms(x, gm_ref[...]).astype(BF16)
    d = x.shape[1]
    merged = None
    for bi, o_ref in enumerate((oa_ref, of_ref, os_ref, od_ref)):
        gate = jax.nn.sigmoid(jnp.dot(hb, wg_ref[:, bi * d:(bi + 1) * d], preferred_element_type=F32))
        term = gate * jnp.dot(o_ref[...], wb_ref[bi], preferred_element_type=F32)
        merged = term if merged is None else merged + term
    xn = x + jnp.dot(merged.astype(BF16), wo_ref[...], preferred_element_type=F32)
    xo_ref[...] = xn
    h2 = _rms(xn, gf_ref[...])
    h2_ref[...] = h2

    logits = _dot_x3(h2, wr_ref[0], wr_ref[1]) + br_ref[...]
    lane = lax.broadcasted_iota(I32, logits.shape, 1).astype(F32)
    big = 1e9
    gl = jnp.where(lane < N_GROUPS, logits, -jnp.inf)
    gmax = jnp.max(gl, axis=1, keepdims=True)
    grp = jnp.min(jnp.where(gl == gmax, lane, big), axis=1, keepdims=True)
    p_grp = 1.0 / jnp.sum(jnp.exp(gl - gmax), axis=1, keepdims=True)
    first = N_GROUPS + grp * EXPERTS_PER_GROUP
    el = jnp.where((lane >= first) & (lane < first + EXPERTS_PER_GROUP), logits, -jnp.inf)
    l1 = jnp.max(el, axis=1, keepdims=True)
    i1 = jnp.min(jnp.where(el == l1, lane, big), axis=1, keepdims=True)
    el2 = jnp.where(lane == i1, -jnp.inf, el)
    l2 = jnp.max(el2, axis=1, keepdims=True)
    i2 = jnp.min(jnp.where(el2 == l2, lane, big), axis=1, keepdims=True)
    e2 = jnp.exp(l2 - l1)
    g1 = p_grp / (1.0 + e2)
    g2 = p_grp * e2 / (1.0 + e2)
    route = jnp.where(lane == 0, i1 - N_GROUPS,
                      jnp.where(lane == 1, i2 - N_GROUPS,
                                jnp.where(lane == 2, g1, jnp.where(lane == 3, g2, 0.0))))
    route_ref[...] = route


def _merge(x2, gm, wg, o_a, o_f, o_s, o_d, wb, wo, gf, wr, br, tm):
    n, d = x2.shape
    row = lambda w: pl.BlockSpec((tm, w), lambda i: (i, 0))
    full = lambda a: pl.BlockSpec(a.shape, lambda i: (0,) * a.ndim, pipeline_mode=pl.Buffered(1))
    return pl.pallas_call(
        _merge_kernel,
        grid=(n // tm,),
        in_specs=[row(d), full(gm), full(wg), row(256), row(256), row(256), row(256),
                  full(wb), full(wo), full(gf), full(wr), full(br)],
        out_specs=[row(d), row(d), row(LANES)],
        out_shape=[jax.ShapeDtypeStruct((n, d), F32), jax.ShapeDtypeStruct((n, d), F32),
                   jax.ShapeDtypeStruct((n, LANES), F32)],
        compiler_params=_cparams("arbitrary"),
        name="merge",
    )(x2, gm, wg, o_a, o_f, o_s, o_d, wb, wo, gf, wr, br)


def _rank_kernel(route_ref, ltri_ref, rank_ref, cnt_ref, carry_ref):
    @pl.when(pl.program_id(0) == 0)
    def _():
        carry_ref[...] = jnp.zeros_like(carry_ref)

    route = route_ref[...]
    lane = lax.broadcasted_iota(I32, route.shape, 1)
    e0 = route[:, 0:1].astype(I32)
    e1 = route[:, 1:2].astype(I32)
    oh0 = (lane == e0).astype(F32)
    oh1 = (lane == e1).astype(F32)
    both = oh0 + oh1
    before = jnp.dot(ltri_ref[...], both.astype(BF16), preferred_element_type=F32) + carry_ref[0:1, :]
    r0 = jnp.sum(oh0 * before, axis=1, keepdims=True)
    r1 = jnp.sum(oh1 * (before + oh0), axis=1, keepdims=True)
    rank_ref[...] = jnp.where(lane == 0, r0, jnp.where(lane == 1, r1, 0.0))
    total = carry_ref[0:1, :] + jnp.sum(both, axis=0, keepdims=True)
    carry_ref[0:1, :] = total
    cnt_ref[...] = jnp.broadcast_to(total, cnt_ref.shape)


def _rank(route, tm):
    n = route.shape[0]
    ltri = (jnp.arange(tm)[:, None] > jnp.arange(tm)[None, :]).astype(BF16)
    return pl.pallas_call(
        _rank_kernel,
        grid=(n // tm,),
        in_specs=[pl.BlockSpec((tm, LANES), lambda i: (i, 0)), pl.BlockSpec((tm, tm), lambda i: (0, 0))],
        out_specs=[pl.BlockSpec((tm, LANES), lambda i: (i, 0)), pl.BlockSpec((8, LANES), lambda i: (0, 0))],
        out_shape=[jax.ShapeDtypeStruct((n, LANES), F32), jax.ShapeDtypeStruct((8, LANES), F32)],
        scratch_shapes=[pltpu.VMEM((8, LANES), F32)],
        compiler_params=_cparams("arbitrary"),
        name="moe_rank",
    )(route, ltri)


def _expert_kernel(be_ref, nu_ref, tok_ref, tok1_ref, tok2_ref, h_hbm, wup_ref, wdn_ref, y_ref,
                   xbuf, sem, wup_b, wdn_b, *, te):
    b = pl.program_id(0)
    n_used = nu_ref[0]
    slot = b % 3

    def start_row(tokens_ref, dst, r):
        pltpu.make_async_copy(h_hbm.at[pl.ds(tokens_ref[0, 0, r], 1), :],
                              xbuf.at[dst, pl.ds(r, 1), :], sem.at[dst]).start()

    def start_block(tokens_ref, dst):
        def issue(r, c):
            start_row(tokens_ref, dst, r)
            return c
        lax.fori_loop(0, te, issue, 0, unroll=8)

    def block(prefetch):
        pltpu.make_async_copy(h_hbm.at[pl.ds(0, te), :], xbuf.at[slot], sem.at[slot]).wait()
        xb = xbuf[slot].astype(BF16)
        if prefetch:
            dst = (b + 2) % 3
            for r in range(te):
                start_row(tok2_ref, dst, r)
        gu = jnp.dot(xb, wup_b[...], preferred_element_type=F32)
        g = gu[:, :EXPERT_FF]
        act = g * jax.nn.sigmoid(g) * gu[:, EXPERT_FF:]
        y_ref[...] = jnp.dot(act.astype(BF16), wdn_b[...], preferred_element_type=F32)

    @pl.when((b < n_used) & ((b == 0) | (be_ref[b] != be_ref[jnp.maximum(b - 1, 0)])))
    def _():
        wup_b[...] = wup_ref[0, 0].astype(BF16)
        wdn_b[...] = wdn_ref[0, 0].astype(BF16)

    @pl.when((b == 0) & (n_used > 0))
    def _():
        start_block(tok_ref, 0)

    @pl.when((b == 0) & (n_used > 1))
    def _():
        start_block(tok1_ref, 1)

    @pl.when(b + 2 < n_used)
    def _():
        block(True)

    @pl.when((b < n_used) & (b + 2 >= n_used))
    def _():
        block(False)

    @pl.when(b >= n_used)
    def _():
        y_ref[...] = jnp.zeros_like(y_ref)


def _experts(blk_expert, n_used, slot_tok, h2, w_up, w_down, layer, te):
    n_blocks = blk_expert.shape[0]
    d = h2.shape[1]
    ahead = lambda k: pl.BlockSpec((1, 1, te), lambda b, be, nu: (jnp.minimum(b + k, n_blocks - 1), 0, 0),
                                   memory_space=pltpu.SMEM)
    grid_spec = pltpu.PrefetchScalarGridSpec(
        num_scalar_prefetch=2,
        grid=(n_blocks,),
        in_specs=[ahead(0), ahead(1), ahead(2),
                  pl.BlockSpec(memory_space=pl.ANY),
                  pl.BlockSpec((1, 1, d, 2 * EXPERT_FF), lambda b, be, nu: (layer, be[b], 0, 0)),
                  pl.BlockSpec((1, 1, EXPERT_FF, d), lambda b, be, nu: (layer, be[b], 0, 0))],
        out_specs=pl.BlockSpec((te, d), lambda b, be, nu: (b, 0)),
        scratch_shapes=[pltpu.VMEM((3, te, d), F32), pltpu.SemaphoreType.DMA((3,)),
                        pltpu.VMEM((d, 2 * EXPERT_FF), BF16), pltpu.VMEM((EXPERT_FF, d), BF16)],
    )
    slots = slot_tok.reshape(n_blocks, 1, te)
    return pl.pallas_call(
        functools.partial(_expert_kernel, te=te),
        grid_spec=grid_spec,
        out_shape=jax.ShapeDtypeStruct((n_blocks * te, d), F32),
        compiler_params=_cparams("arbitrary"),
        name="moe_experts",
    )(blk_expert, n_used, slots, slots, slots, h2, w_up, w_down)


def _combine_kernel(pos_ref, pos_next_ref, x_ref, route_ref, y_hbm, o_ref, ybuf, sem, *, tc):
    i = pl.program_id(0)
    slot = i % 2

    def gather(rows_ref, dst):
        for r in range(2 * tc):
            pltpu.make_async_copy(y_hbm.at[pl.ds(rows_ref[0, 0, r], 1), :],
                                  ybuf.at[dst, pl.ds(r, 1), :], sem.at[dst]).start()

    @pl.when(i == 0)
    def _():
        gather(pos_ref, 0)

    @pl.when(i + 1 < pl.num_programs(0))
    def _():
        gather(pos_next_ref, 1 - slot)

    pltpu.make_async_copy(y_hbm.at[pl.ds(0, 2 * tc), :], ybuf.at[slot], sem.at[slot]).wait()
    route = route_ref[...]
    o_ref[...] = (x_ref[...] + route[:, 2:3] * ybuf[slot, 0:tc, :]
                  + route[:, 3:4] * ybuf[slot, tc:2 * tc, :])


def _combine(pos, x2, route, yb, tc):
    n, d = x2.shape
    nt = n // tc
    pos_t = jnp.transpose(pos.reshape(nt, tc, 2), (0, 2, 1)).reshape(nt, 1, 2 * tc)
    return pl.pallas_call(
        functools.partial(_combine_kernel, tc=tc),
        grid=(nt,),
        in_specs=[pl.BlockSpec((1, 1, 2 * tc), lambda i: (i, 0, 0), memory_space=pltpu.SMEM),
                  pl.BlockSpec((1, 1, 2 * tc), lambda i: (jnp.minimum(i + 1, nt - 1), 0, 0),
                               memory_space=pltpu.SMEM),
                  pl.BlockSpec((tc, d), lambda i: (i, 0)),
                  pl.BlockSpec((tc, LANES), lambda i: (i, 0)),
                  pl.BlockSpec(memory_space=pl.ANY)],
        out_specs=pl.BlockSpec((tc, d), lambda i: (i, 0)),
        out_shape=jax.ShapeDtypeStruct((n, d), F32),
        scratch_shapes=[pltpu.VMEM((2, 2 * tc, d), F32), pltpu.SemaphoreType.DMA((2,))],
        compiler_params=_cparams("arbitrary"),
        name="moe_combine",
    )(pos_t, pos_t, x2, route, yb)


def _swap_mid_heads(w, axis):
    h = jnp.split(w, 4, axis=axis)
    return jnp.concatenate([h[0], h[2], h[1], h[3]], axis=axis)


def _t5_lookup(t5_table, dist):
    onehot = (_t5_bucket(dist)[..., None] == jnp.arange(T5_BUCKETS)).astype(F32)
    return jnp.einsum("...b,bh->...h", onehot, t5_table.astype(F32), precision=lax.Precision.HIGHEST)


def _t5_bucket(dist):
    n = jnp.maximum(dist, 0)
    max_exact = T5_BUCKETS // 2
    nf = jnp.maximum(n, 1).astype(F32)
    large = max_exact + (jnp.log(nf / max_exact) / math.log(T5_MAX_DISTANCE / max_exact)
                         * (T5_BUCKETS - max_exact)).astype(I32)
    large = jnp.minimum(large, T5_BUCKETS - 1)
    return jnp.where(n < max_exact, n, large)


def _swa_bias(t5_table):
    t = SWA_BLOCK
    dist = t + jnp.arange(t)[None, :] - jnp.arange(2 * t)[:, None]
    tile = jnp.where(((dist >= 0) & (dist < t))[..., None], _t5_lookup(t5_table, dist)[..., :4], NEG_INF)
    return jnp.concatenate([tile[..., h] for h in (0, 2, 1, 3)], axis=1)


def _layer_weights(w_in, qk_gain, forget_bias, w_branch):
    offs = np.concatenate([[0], np.cumsum(IN_SPLITS)]).tolist()
    part = lambda k: w_in[:, offs[k]:offs[k + 1]]
    dup = lambda w: jnp.concatenate([w, w], axis=1)
    aq = _swap_mid_heads(part(0), 1)
    cols = [aq, part(1), part(2), part(3), part(4), part(5), part(7), part(8), part(9),
            part(10), dup(part(11)), dup(part(12)), part(13), dup(part(14))]
    w1 = jnp.concatenate(cols, axis=1).astype(BF16)
    d = w_in.shape[0]
    wm = _hi_lo(jnp.concatenate([part(6), part(15), jnp.zeros((d, LANES - 8), F32)], axis=1))
    wg = part(16).astype(BF16)
    tile = lambda g, reps, scale: jnp.pad(jnp.tile(g, reps) * scale, (0, 256 - reps * HEAD_DIM))
    gains = jnp.stack([tile(qk_gain[0, 0], 4, ATTN_SCALE), tile(qk_gain[0, 1], 2, 1.0),
                       tile(qk_gain[1, 0], 4, ATTN_SCALE * LOG2E), tile(qk_gain[1, 1], 4, 1.0),
                       tile(qk_gain[2, 0], 4, ATTN_SCALE * LOG2E), tile(qk_gain[2, 1], 2, 1.0),
                       jnp.zeros((256,), F32), jnp.zeros((256,), F32)]).astype(F32)
    fb = jnp.pad(forget_bias.astype(F32), (0, LANES - 4)).reshape(1, LANES)
    wb = jnp.stack([_swap_mid_heads(w_branch[0], 0), w_branch[1], w_branch[2], w_branch[3]]).astype(BF16)
    return w1, wm, wg, gains, fb, wb


def kernel(x, norm_mix_g, w_in, forget_bias, attn_sinks, qk_gain, w_branch, w_out, t5_table, norm_ffn_g,
           w_router_group, b_router_group, w_router_expert, b_router_expert, w_expert_up, w_expert_down):
    b, s, d = x.shape
    n = b * s
    depth = w_in.shape[0]
    top_k = min(DSA_TOPK_MAX, s // 4)
    tm_proj = min(512, s)
    fox_t = min(256, s)
    sb_t = min(256, s)
    dsa_t = min(256, s)
    te = 256
    tc = 128

    gseg = (jnp.arange(256)[:, None] // HEAD_DIM == jnp.arange(256)[None, :] // HEAD_DIM).astype(BF16)
    bias_swa = _swa_bias(t5_table)
    bias_dsa = _dsa_bias(t5_table, dsa_t) * LOG2E
    n_blocks = -(-2 * n // te) + N_EXPERTS
    tok_ids = jnp.repeat(jnp.arange(n, dtype=I32), 2)

    for layer in range(depth):
        w1, wm, wg, gains, fb, wb = _layer_weights(w_in[layer], qk_gain[layer], forget_bias[layer],
                                                   w_branch[layer])
        sink_row = jnp.repeat(attn_sinks[layer].astype(F32)[jnp.array([0, 2, 1, 3])], SWA_BLOCK)
        sinks = jnp.broadcast_to(sink_row[None, :], (8, 4 * SWA_BLOCK))
        (aq, ak, av, fq, fk, fv, sq, sk, sv, dq, dkk, dvv, iq, ikk, cm) = _proj(
            x, norm_mix_g[layer].reshape(1, d), w1, wm, gseg, gains, fb, tm_proj)

        o_swa = _swa(aq, ak, av, bias_swa, sinks, min(4, s // SWA_BLOCK))
        cmt = jnp.transpose(cm[:, :, :8], (0, 2, 1))
        o_fox = _fox(fq, fk, fv, cmt, cm, fox_t)
        o_sb = _sb(sq, sk, sv, sb_t)
        o_dsa = _dsa(dq, dkk, dvv, iq, ikk, cmt, bias_dsa, top_k, dsa_t)

        wr = _hi_lo(jnp.concatenate([w_router_group[layer], w_router_expert[layer],
                                     jnp.zeros((d, LANES - N_GROUPS - N_EXPERTS), F32)], axis=1))
        br = jnp.concatenate([b_router_group[layer], b_router_expert[layer],
                              jnp.zeros((LANES - N_GROUPS - N_EXPERTS,), F32)]).reshape(1, LANES)
        x2, h2, route = _merge(
            x.reshape(n, d), norm_mix_g[layer].reshape(1, d), wg,
            o_swa.reshape(n, 256), o_fox.reshape(n, 256), o_sb.reshape(n, 256), o_dsa.reshape(n, 256),
            wb, w_out[layer].astype(BF16), norm_ffn_g[layer].reshape(1, d), wr, br, min(512, n))

        rank, cnt = _rank(route, min(512, n))
        counts = cnt[0, :N_EXPERTS].astype(I32)
        padded = (counts + te - 1) // te * te
        pend = jnp.cumsum(padded)
        pstart = pend - padded
        expert = route[:, :2].astype(I32)
        own = expert[:, :, None] == jnp.arange(N_EXPERTS, dtype=I32)
        pos = jnp.sum(jnp.where(own, pstart, 0), axis=-1) + rank[:, :2].astype(I32)
        slot_tok = jnp.zeros((n_blocks * te,), I32).at[pos.reshape(-1)].set(tok_ids)
        blk_start = jnp.arange(n_blocks, dtype=I32)[:, None] * te
        blk_expert = jnp.minimum(jnp.sum((pend[None, :] <= blk_start).astype(I32), axis=1), N_EXPERTS - 1)
        n_used = (pend[-1:] // te).astype(I32)

        yb = _experts(blk_expert, n_used, slot_tok, h2, w_expert_up, w_expert_down, layer, te)
        x = _combine(pos, x2, route, yb, tc).reshape(b, s, d)
    return x
```

```python
import functools
import math

import jax
import jax.numpy as jnp
import numpy as np
from jax import lax
from jax.experimental import pallas as pl
from jax.experimental.pallas import tpu as pltpu

F32 = jnp.float32
BF16 = jnp.bfloat16
I32 = jnp.int32

HEAD_DIM = 64
LANES = 128
NORM_EPS = 1e-6
NEG_INF = -1e30
M_INIT = -1e29
ATTN_SCALE = HEAD_DIM ** -0.5
LOG2E = math.log2(math.e)
SWA_BLOCK = 128
IDX_SCALE = 64 ** -0.5
IDX_HEADS = 4
DSA_TOPK_MAX = 256
T5_BUCKETS = 32
T5_MAX_DISTANCE = 128
N_GROUPS = 4
EXPERTS_PER_GROUP = 8
N_EXPERTS = N_GROUPS * EXPERTS_PER_GROUP
EXPERT_FF = 512
SB_DEAD = -110.0
INT_MIN = -2 ** 31
VMEM_LIMIT = 56 * 1024 * 1024

IN_SPLITS = (256, 128, 128, 256, 256, 256, 4, 256, 256, 256, 256, 64, 64, 256, 64, 4, 4096)

_SEG = dict(aq=(0, 256), ak=(256, 128), av=(384, 128), fq=(512, 256), fk=(768, 256), fv=(1024, 256),
            sq=(1280, 256), sk=(1536, 256), sv=(1792, 256), dq=(2048, 256), dkk=(2304, 128),
            dvv=(2432, 128), iq=(2560, 256), ikk=(2816, 128))
_W1_COLS = 2944
_SEG_ORDER = ("aq", "ak", "av", "fq", "fk", "fv", "sq", "sk", "sv", "dq", "dkk", "dvv", "iq", "ikk")


def _cparams(*sem):
    return pltpu.CompilerParams(dimension_semantics=sem, vmem_limit_bytes=VMEM_LIMIT)


def _rms(x, g):
    return x * lax.rsqrt(jnp.mean(x * x, axis=-1, keepdims=True) + NORM_EPS) * g


def _log_sigmoid(z):
    return jnp.minimum(z, 0.0) - jnp.log(1.0 + jnp.exp(-jnp.abs(z)))


def _dot_nt(a, b):
    return lax.dot_general(a, b, (((1,), (1,)), ((), ())), preferred_element_type=F32)


def _split3(x):
    p1 = x.astype(BF16)
    r = x - p1.astype(F32)
    p2 = r.astype(BF16)
    return p1, p2, (r - p2.astype(F32)).astype(BF16)


def _hi_lo(w):
    hi = w.astype(BF16)
    return jnp.stack([hi, (w - hi.astype(F32)).astype(BF16)])


def _dot_x3(a, b_hi, b_lo):
    a_hi = a.astype(BF16)
    a_lo = (a - a_hi.astype(F32)).astype(BF16)
    return (jnp.dot(a_hi, b_hi, preferred_element_type=F32) + jnp.dot(a_lo, b_hi, preferred_element_type=F32)
            + jnp.dot(a_hi, b_lo, preferred_element_type=F32))


def _split_heads(qp):
    lo = lax.broadcasted_iota(I32, (1, LANES), 1) < HEAD_DIM
    zero = jnp.zeros_like(qp)
    return jnp.concatenate([jnp.where(lo, qp, zero), jnp.where(lo, zero, qp)], axis=0)


def _merge_heads(o, t):
    lo = lax.broadcasted_iota(I32, (1, LANES), 1) < HEAD_DIM
    return jnp.where(lo, o[:t], o[t:])


def _proj_kernel(x_ref, g_ref, w1_ref, wm_ref, gseg_ref, gains_ref, fb_ref, ltri_ref, *rest):
    outs = dict(zip(_SEG_ORDER, rest[:len(_SEG_ORDER)]))
    cm_ref = rest[len(_SEG_ORDER)]
    carry_ref = rest[len(_SEG_ORDER) + 1]

    @pl.when(pl.program_id(1) == 0)
    def _():
        carry_ref[...] = jnp.zeros_like(carry_ref)

    h = _rms(x_ref[0], g_ref[...])
    hb = h.astype(BF16)

    def seg(name):
        off, width = _SEG[name]
        return jnp.dot(hb, w1_ref[:, off:off + width], preferred_element_type=F32)

    def head_norm(t, row):
        width = t.shape[1]
        ssq = jnp.dot((t * t).astype(BF16), gseg_ref[:width, :width], preferred_element_type=F32)
        return t * lax.rsqrt(ssq * (1.0 / HEAD_DIM) + NORM_EPS) * gains_ref[row:row + 1, :width]

    normed = dict(aq=0, ak=1, fq=2, fk=3, dq=4, dkk=5)
    scaled = dict(sq=ATTN_SCALE * LOG2E, iq=IDX_SCALE)
    for name in _SEG_ORDER:
        t = seg(name)
        if name in normed:
            t = head_norm(t, normed[name])
        elif name in scaled:
            t = t * scaled[name]
        outs[name][0] = t.astype(BF16)

    misc = _dot_x3(h, wm_ref[0], wm_ref[1])
    lane = lax.broadcasted_iota(I32, misc.shape, 1)
    logf = jnp.where(lane < 4, _log_sigmoid(misc + fb_ref[...]), 0.0)
    ltri = ltri_ref[...]
    c = carry_ref[0:1, :]
    for piece in _split3(logf):
        c = c + jnp.dot(ltri, piece, preferred_element_type=F32)
    tm = misc.shape[0]
    carry_ref[0:1, :] = c[tm - 1:tm, :]
    cm_ref[0] = jnp.where(lane < 4, c, misc)


def _proj(x, g, w1, wm, gseg, gains, fb, tm):
    b, s, d = x.shape
    ltri = jnp.tril(jnp.ones((tm, tm), BF16))
    full = lambda shape: pl.BlockSpec(shape, lambda bi, si: (0,) * len(shape))
    out_shapes = [jax.ShapeDtypeStruct((b, s, _SEG[n][1]), BF16) for n in _SEG_ORDER]
    out_shapes.append(jax.ShapeDtypeStruct((b, s, LANES), F32))
    out_specs = [pl.BlockSpec((1, tm, _SEG[n][1]), lambda bi, si: (bi, si, 0)) for n in _SEG_ORDER]
    out_specs.append(pl.BlockSpec((1, tm, LANES), lambda bi, si: (bi, si, 0)))
    return pl.pallas_call(
        _proj_kernel,
        grid=(b, s // tm),
        in_specs=[pl.BlockSpec((1, tm, d), lambda bi, si: (bi, si, 0)),
                  full((1, d)), full(w1.shape), full(wm.shape), full(gseg.shape),
                  full(gains.shape), full(fb.shape), full((tm, tm))],
        out_specs=out_specs,
        out_shape=out_shapes,
        scratch_shapes=[pltpu.VMEM((8, LANES), F32)],
        compiler_params=_cparams("arbitrary", "arbitrary"),
        name="proj",
    )(x, g, w1, wm, gseg, gains, fb, ltri)


def _swa_kernel(q_ref, kp_ref, kc_ref, vtp_ref, vtc_ref, bias_ref, sink_ref, o_ref, *, nsub):
    i = pl.program_id(1)
    t = SWA_BLOCK
    no_prev = (lax.broadcasted_iota(I32, (2 * t, 4 * t), 0) < t) & (i == 0)
    sink = sink_ref[0:1, :]
    for u in range(nsub):
        rows = slice(u * t, (u + 1) * t)
        q = q_ref[0, rows, :]
        qs = jnp.concatenate([_split_heads(q[:, :LANES]), _split_heads(q[:, LANES:])], axis=0)
        if u == 0:
            kcat = jnp.concatenate([kp_ref[0], kc_ref[0, rows, :]], axis=0)
            vt = jnp.concatenate([vtp_ref[0], vtc_ref[0, :, rows]], axis=1)
        else:
            kcat = kc_ref[0, (u - 1) * t:(u + 1) * t, :]
            vt = vtc_ref[0, :, (u - 1) * t:(u + 1) * t]
        st = _dot_nt(kcat, qs) + bias_ref[...]
        if u == 0:
            st = jnp.where(no_prev, NEG_INF, st)
        m = jnp.maximum(jnp.max(st, axis=0, keepdims=True), sink)
        p = jnp.exp(st - m)
        denom = jnp.sum(p, axis=0, keepdims=True) + jnp.exp(sink - m)
        ot = jnp.dot(vt, p.astype(BF16), preferred_element_type=F32) / denom
        ot = jnp.concatenate([ot[:HEAD_DIM, 0:t], ot[HEAD_DIM:, t:2 * t],
                              ot[:HEAD_DIM, 2 * t:3 * t], ot[HEAD_DIM:, 3 * t:]], axis=0)
        o_ref[0, rows, :] = ot.T.astype(BF16)


def _swa(aq, ak, av, bias, sinks, nsub):
    b, s, _ = aq.shape
    t = SWA_BLOCK
    avt = jnp.transpose(av, (0, 2, 1))
    cur = lambda bi, i: (bi, i, 0)
    prev = lambda bi, i: (bi, jnp.maximum(i * nsub - 1, 0), 0)
    return pl.pallas_call(
        functools.partial(_swa_kernel, nsub=nsub),
        grid=(b, s // (t * nsub)),
        in_specs=[pl.BlockSpec((1, t * nsub, 256), cur),
                  pl.BlockSpec((1, t, LANES), prev), pl.BlockSpec((1, t * nsub, LANES), cur),
                  pl.BlockSpec((1, LANES, t), lambda bi, i: (bi, 0, jnp.maximum(i * nsub - 1, 0))),
                  pl.BlockSpec((1, LANES, t * nsub), lambda bi, i: (bi, 0, i)),
                  pl.BlockSpec(bias.shape, lambda bi, i: (0, 0)),
                  pl.BlockSpec(sinks.shape, lambda bi, i: (0, 0))],
        out_specs=pl.BlockSpec((1, t * nsub, 256), cur),
        out_shape=jax.ShapeDtypeStruct((b, s, 256), BF16),
        compiler_params=_cparams("arbitrary", "arbitrary"),
        name="swa",
    )(aq, ak, ak, avt, avt, bias, sinks)


def _fox_kernel(q_ref, k_ref, vt_ref, ct_ref, ccol_ref, o_ref, ckb_ref, *, t):
    i = pl.program_id(1)
    n_tiles = ckb_ref.shape[1] // t

    @pl.when(i == 0)
    def _():
        def fill(j, c):
            rows = pl.ds(pl.multiple_of(j * t, t), t)
            cc = ccol_ref[0, rows, :] * LOG2E
            for h in range(4):
                ckb_ref[h, rows, :] = jnp.broadcast_to(cc[:, h:h + 1], (t, LANES))
            return c
        lax.fori_loop(0, n_tiles, fill, 0)

    q = q_ref[0]
    ct = ct_ref[0] * LOG2E
    qs = [_split_heads(q[:, :LANES]), _split_heads(q[:, LANES:])]
    valid = lax.broadcasted_iota(I32, (t, t), 0) <= lax.broadcasted_iota(I32, (t, t), 1)

    def tiles(j0, n, carry, masked):
        m, l, accs = carry
        rows = pl.ds(pl.multiple_of(j0 * t, t), n * t)
        cols = []
        for pair in range(2):
            st = _dot_nt(k_ref[0, rows, pair * LANES:(pair + 1) * LANES], qs[pair])
            for hh in range(2):
                head = 2 * pair + hh
                ck = ckb_ref[head, rows, :]
                for c in range(t // LANES):
                    cs = slice(c * LANES, (c + 1) * LANES)
                    sh = st[:, hh * t + c * LANES:hh * t + (c + 1) * LANES] + (ct[head:head + 1, cs] - ck)
                    if masked:
                        sh = jnp.where(valid[:, cs], sh, NEG_INF)
                    cols.append(sh)
        st = jnp.concatenate(cols, axis=1)
        m_new = jnp.maximum(m, jnp.max(st, axis=0, keepdims=True))
        alpha = jnp.exp2(m - m_new)
        p = jnp.exp2(st - m_new)
        l = alpha * l + jnp.sum(p, axis=0, keepdims=True)
        pb = p.astype(BF16)
        new = []
        for pair in range(2):
            lanes = slice(pair * 2 * t, (pair + 1) * 2 * t)
            vt = jnp.concatenate([vt_ref[0, j0 + u, pair * LANES:(pair + 1) * LANES, :] for u in range(n)],
                                 axis=1)
            pv = jnp.dot(vt, pb[:, lanes], preferred_element_type=F32)
            new.append(alpha[:, lanes] * accs[pair] + pv)
        return m_new, l, tuple(new)

    init = (jnp.full((1, 4 * t), M_INIT, F32), jnp.zeros((1, 4 * t), F32),
            (jnp.zeros((LANES, 2 * t), F32), jnp.zeros((LANES, 2 * t), F32)))
    carry = lax.fori_loop(0, i // 4, lambda g, c: tiles(4 * g, 4, c, False), init)
    rest = i - i % 4
    carry = lax.cond(i % 4 >= 2, lambda c: tiles(rest, 2, c, False), lambda c: c, carry)
    carry = lax.cond(i % 2 == 1, lambda c: tiles(i - 1, 1, c, False), lambda c: c, carry)
    _, l, accs = tiles(i, 1, carry, True)
    outs = []
    for pair in range(2):
        o = accs[pair] / l[:, pair * 2 * t:(pair + 1) * 2 * t]
        outs.append(jnp.concatenate([o[:HEAD_DIM, :t], o[HEAD_DIM:, t:]], axis=0))
    o_ref[0] = jnp.concatenate(outs, axis=0).T.astype(BF16)


def _fox(fq, fk, fv, ct, ccol, t):
    b, s, _ = fq.shape
    nt = s // t
    vt = jnp.transpose(fv.reshape(b, nt, t, 256), (0, 1, 3, 2))
    return pl.pallas_call(
        functools.partial(_fox_kernel, t=t),
        grid=(b, nt),
        in_specs=[pl.BlockSpec((1, t, 256), lambda bi, i: (bi, i, 0)),
                  pl.BlockSpec((1, s, 256), lambda bi, i: (bi, 0, 0)),
                  pl.BlockSpec((1, nt, 256, t), lambda bi, i: (bi, 0, 0, 0)),
                  pl.BlockSpec((1, 8, t), lambda bi, i: (bi, 0, i)),
                  pl.BlockSpec((1, s, LANES), lambda bi, i: (bi, 0, 0))],
        out_specs=pl.BlockSpec((1, t, 256), lambda bi, i: (bi, i, 0)),
        out_shape=jax.ShapeDtypeStruct((b, s, 256), BF16),
        scratch_shapes=[pltpu.VMEM((4, s, LANES), F32)],
        compiler_params=_cparams("arbitrary", "arbitrary"),
        name="fox",
    )(fq, fk, vt, ct, ccol)


def _sb_kernel(q_ref, k_ref, vt_ref, lgt_ref, o_ref, *, t):
    i = pl.program_id(1)
    q = q_ref[0]
    lgt = lgt_ref[...]
    qs = [_split_heads(q[:, :LANES]), _split_heads(q[:, LANES:])]
    key_i = lax.broadcasted_iota(I32, (t, 4 * t), 0)
    query_i = lax.broadcasted_iota(I32, (t, 4 * t), 1) & (t - 1)
    strict = key_i < query_i

    def tile(j, r, accs, masked):
        rows = pl.ds(pl.multiple_of(j * t, t), t)
        z = jnp.concatenate([_dot_nt(k_ref[0, rows, p * LANES:(p + 1) * LANES], qs[p]) for p in range(2)],
                            axis=1)
        sp = jnp.log2(1.0 + jnp.exp2(-jnp.abs(z)))
        log_beta = jnp.minimum(z, 0.0) - sp
        log_keep = jnp.minimum(-z, 0.0) - sp
        if masked:
            log_keep = jnp.where(strict, log_keep, 0.0)
        hi = log_keep.astype(BF16)
        lo = (log_keep - hi.astype(F32)).astype(BF16)
        later = (jnp.dot(lgt, hi, preferred_element_type=F32)
                 + jnp.dot(lgt, lo, preferred_element_type=F32))
        a = jnp.exp2(log_beta + later + r)
        if masked:
            a = jnp.where(strict, a, 0.0)
        ab = a.astype(BF16)
        new = tuple(accs[p] + jnp.dot(vt_ref[0, j, p * LANES:(p + 1) * LANES, :],
                                      ab[:, p * 2 * t:(p + 1) * 2 * t], preferred_element_type=F32)
                    for p in range(2))
        return r + jnp.sum(log_keep, axis=0, keepdims=True), new

    zero_acc = jnp.zeros((LANES, 2 * t), F32)
    r, accs = tile(i, jnp.zeros((1, 4 * t), F32), (zero_acc, zero_acc), True)

    def cond(c):
        return (c[0] >= 0) & (c[1] > 0)

    def body(c):
        j, _, r, accs = c
        r, accs = tile(j, r, accs, False)
        return j - 1, (jnp.max(r) > SB_DEAD * LOG2E).astype(I32), r, accs

    _, _, _, accs = lax.while_loop(cond, body, (i - 1, (jnp.max(r) > SB_DEAD * LOG2E).astype(I32), r, accs))
    outs = [jnp.concatenate([accs[p][:HEAD_DIM, :t], accs[p][HEAD_DIM:, t:]], axis=0) for p in range(2)]
    o_ref[0] = jnp.concatenate(outs, axis=0).T.astype(BF16)


def _sb(sq, sk, sv, t):
    b, s, _ = sq.shape
    nt = s // t
    lgt = (jnp.arange(t)[:, None] < jnp.arange(t)[None, :]).astype(BF16)
    vt = jnp.transpose(sv.reshape(b, nt, t, 256), (0, 1, 3, 2))
    return pl.pallas_call(
        functools.partial(_sb_kernel, t=t),
        grid=(b, nt),
        in_specs=[pl.BlockSpec((1, t, 256), lambda bi, i: (bi, i, 0)),
                  pl.BlockSpec((1, s, 256), lambda bi, i: (bi, 0, 0)),
                  pl.BlockSpec((1, nt, 256, t), lambda bi, i: (bi, 0, 0, 0)),
                  pl.BlockSpec((t, t), lambda bi, i: (0, 0))],
        out_specs=pl.BlockSpec((1, t, 256), lambda bi, i: (bi, i, 0)),
        out_shape=jax.ShapeDtypeStruct((b, s, 256), BF16),
        compiler_params=_cparams("arbitrary", "arbitrary"),
        name="stickbreak",
    )(sq, sk, vt, lgt)


def _bit_planes(words):
    words = list(words)
    j, m = 16, 0x0000FFFF
    while j:
        k = 0
        while k < 32:
            tt = (words[k] ^ lax.shift_right_logical(words[k + j], jnp.full_like(words[k + j], j))) & m
            words[k] = words[k] ^ tt
            words[k + j] = words[k + j] ^ (tt << j)
            k = (k + j + 1) & ~j
        j >>= 1
        m = (m ^ (m << j)) & 0xFFFFFFFF
    return words


def _dsa_kernel(q_ref, kk_ref, vt_ref, iq_ref, ikk_ref, wt_ref, bias_ref, lstrict_ref, o_ref,
                key_ref, plane_ref, *, t, top_k):
    i = pl.program_id(1)
    assert t == 8 * 32

    @pl.when(i == 0)
    def _():
        plane_ref[...] = jnp.zeros_like(plane_ref)
    causal = lax.broadcasted_iota(I32, (t, t), 0) <= lax.broadcasted_iota(I32, (t, t), 1)

    def head_stack(x):
        return jnp.concatenate([_split_heads(x[:, :LANES]), _split_heads(x[:, LANES:])], axis=0)

    def key_rows(ref, j0, n):
        return ref[0, pl.ds(pl.multiple_of(j0 * t, t), n * t), :]

    iqs = head_stack(iq_ref[0])
    wt = wt_ref[0]
    w = [wt[4 + h:5 + h, :] * (IDX_HEADS ** -0.5) for h in range(IDX_HEADS)]

    def score_tiles(j0, n, masked):
        lg = _dot_nt(key_rows(ikk_ref, j0, n), iqs)
        sc = w[0] * jnp.maximum(lg[:, 0:t], 0.0)
        for h in range(1, IDX_HEADS):
            sc = sc + w[h] * jnp.maximum(lg[:, h * t:(h + 1) * t], 0.0)
        bits = pltpu.bitcast(sc, I32)
        key = bits ^ ((bits >> 31) & 0x7FFFFFFF)
        key = jnp.where(key == -1, 0, key)
        if masked:
            key = jnp.where(causal, key, INT_MIN)
        for u in range(n):
            key_u = key[u * t:(u + 1) * t]
            key_ref[j0 + u] = key_u
            for p, plane in enumerate(_bit_planes([key_u[8 * g:8 * g + 8, :] for g in range(32)])):
                plane_ref[p, j0 + u] = plane

    def p1(g, c):
        score_tiles(4 * g, 4, False)
        return c

    lax.fori_loop(0, i // 4, p1, 0)

    @pl.when(i % 4 >= 2)
    def _():
        score_tiles(i - i % 4, 2, False)

    @pl.when(i % 2 == 1)
    def _():
        score_tiles(i - 1, 1, False)

    score_tiles(i, 1, True)

    def popcount_rows(words):
        per_tile = jnp.sum(lax.population_count(words), axis=0)
        return jnp.sum(per_tile.astype(F32), axis=0, keepdims=True)

    def bis_body(p, c):
        alive, n_gt, thr_u = c
        plane = plane_ref[p] ^ jnp.where(p == 0, -1, 0)
        ones = alive & plane
        cnt = popcount_rows(ones)
        take = n_gt + cnt >= top_k
        alive = jnp.where(take, ones, alive ^ ones)
        n_gt = jnp.where(take, n_gt, n_gt + cnt)
        thr_u = jnp.where(take, thr_u | (jnp.int32(1) << (31 - p)), thr_u)
        return alive, n_gt, thr_u

    n_tiles = key_ref.shape[0]
    alive0 = jnp.where(lax.broadcasted_iota(I32, (n_tiles, 8, t), 0) <= i, -1, 0)
    alive, n_gt, thr_u = lax.fori_loop(
        0, 32, bis_body, (alive0, jnp.zeros((1, t), F32), jnp.zeros((1, t), I32)))
    thr = jnp.maximum(thr_u ^ INT_MIN, INT_MIN + 1)
    n_avail = (i * t + lax.broadcasted_iota(I32, (1, t), 1) + 1).astype(F32)
    n_ge = jnp.where(n_avail > top_k, n_gt + popcount_rows(alive), 0.0)
    surplus = jnp.max(n_ge) > top_k

    def tie_pass():
        need = top_k - n_gt

        def tb(j, seen):
            k = key_ref[j]
            eq = k == thr
            eqf = jnp.where(eq, 1.0, 0.0)
            before = jnp.dot(lstrict_ref[...], eqf.astype(BF16), preferred_element_type=F32) + seen
            sel = (k > thr) | (eq & (before < need))
            key_ref[j] = jnp.where(sel, 1, INT_MIN)
            return seen + jnp.sum(eqf, axis=0, keepdims=True)

        lax.fori_loop(0, i + 1, tb, jnp.zeros((1, t), F32))
        return jnp.zeros((1, t), I32)

    thr = lax.cond(surplus, tie_pass, lambda: thr)

    qs = head_stack(q_ref[0])

    def attn_tiles(j0, n, carry, bias_kind):
        m, l, acc = carry
        st = _dot_nt(key_rows(kk_ref, j0, n), qs)
        selb = jnp.concatenate([jnp.where(key_ref[j0 + u] >= thr, 0.0, NEG_INF) for u in range(n)], axis=0)
        parts = []
        for h in range(4):
            sh = st[:, h * t:(h + 1) * t] + selb
            if bias_kind is not None:
                sh = sh + bias_ref[bias_kind, h]
            parts.append(sh)
        st = jnp.concatenate(parts, axis=1)
        m_new = jnp.maximum(m, jnp.max(st, axis=0, keepdims=True))
        alpha = jnp.exp2(m - m_new)
        p = jnp.exp2(st - m_new)
        l = alpha * l + jnp.sum(p, axis=0, keepdims=True)
        vt = jnp.concatenate([vt_ref[0, j0 + u] for u in range(n)], axis=1)
        acc = alpha * acc + jnp.dot(vt, p.astype(BF16), preferred_element_type=F32)
        return m_new, l, acc

    init = (jnp.full((1, 4 * t), M_INIT, F32), jnp.zeros((1, 4 * t), F32), jnp.zeros((HEAD_DIM, 4 * t), F32))
    far = jnp.maximum(i - 1, 0)
    carry = lax.fori_loop(0, far // 4, lambda g, c: attn_tiles(4 * g, 4, c, None), init)
    carry = lax.cond(far % 4 >= 2, lambda c: attn_tiles(far - far % 4, 2, c, None), lambda c: c, carry)
    carry = lax.cond(far % 2 == 1, lambda c: attn_tiles(far - 1, 1, c, None), lambda c: c, carry)
    carry = lax.cond(i > 0, lambda c: attn_tiles(i - 1, 1, c, 0), lambda c: c, carry)
    _, l, acc = attn_tiles(i, 1, carry, 1)
    ot = acc / l
    ot = jnp.concatenate([ot[:, h * t:(h + 1) * t] for h in range(4)], axis=0)
    o_ref[0] = ot.T.astype(BF16)


def _dsa_bias(t5_table, t):
    assert t + 1 >= T5_MAX_DISTANCE
    k = jnp.arange(t)[:, None]
    q = jnp.arange(t)[None, :]
    far = t5_table[T5_BUCKETS - 1, 4:].astype(F32)
    tiles = []
    for off in (t, 0):
        dist = off + q - k
        b = jnp.transpose(_t5_lookup(t5_table, dist)[..., 4:], (2, 0, 1)) - far[:, None, None]
        tiles.append(jnp.where((dist >= 0)[None], b, 0.0))
    return jnp.stack(tiles)


def _dsa(dq, dkk, dvv, iq, ikk, wt, bias, top_k, t):
    b, s, _ = dq.shape
    nt = s // t
    lstrict = (jnp.arange(t)[:, None] > jnp.arange(t)[None, :]).astype(BF16)
    vt = jnp.transpose(dvv[:, :, :HEAD_DIM].reshape(b, nt, t, HEAD_DIM), (0, 1, 3, 2))
    blk = lambda w: pl.BlockSpec((1, t, w), lambda bi, i: (bi, i, 0))
    seq = lambda w: pl.BlockSpec((1, s, w), lambda bi, i: (bi, 0, 0))
    return pl.pallas_call(
        functools.partial(_dsa_kernel, t=t, top_k=top_k),
        grid=(b, nt),
        in_specs=[blk(256), seq(LANES),
                  pl.BlockSpec((1, nt, HEAD_DIM, t), lambda bi, i: (bi, 0, 0, 0)),
                  blk(256), seq(LANES),
                  pl.BlockSpec((1, 8, t), lambda bi, i: (bi, 0, i)),
                  pl.BlockSpec(bias.shape, lambda bi, i: (0, 0, 0, 0)),
                  pl.BlockSpec((t, t), lambda bi, i: (0, 0))],
        out_specs=blk(256),
        out_shape=jax.ShapeDtypeStruct((b, s, 256), BF16),
        scratch_shapes=[pltpu.VMEM((nt, t, t), I32), pltpu.VMEM((32, nt, 8, t), I32)],
        compiler_params=_cparams("arbitrary", "arbitrary"),
        name="dsa",
    )(dq, dkk, vt, iq, ikk, wt, bias, lstrict)


def _merge_kernel(x_ref, gm_ref, wg_ref, oa_ref, of_ref, os_ref, od_ref, wb_ref, wo_ref, gf_ref,
                  wr_ref, br_ref, xo_ref, h2_ref, route_ref):
    x = x_ref[...]
    hb = _rms(x, gm_ref[...]).astype(BF16)
    d = x.shape[1]
    merged = None
    for bi, o_ref in enumerate((oa_ref, of_ref, os_ref, od_ref)):
        gate = jax.nn.sigmoid(jnp.dot(hb, wg_ref[:, bi * d:(bi + 1) * d], preferred_element_type=F32))
        term = gate * jnp.dot(o_ref[...], wb_ref[bi], preferred_element_type=F32)
        merged = term if merged is None else merged + term
    xn = x + jnp.dot(merged.astype(BF16), wo_ref[...], preferred_element_type=F32)
    xo_ref[...] = xn
    h2 = _rms(xn, gf_ref[...])
    h2_ref[...] = h2

    logits = _dot_x3(h2, wr_ref[0], wr_ref[1]) + br_ref[...]
    lane = lax.broadcasted_iota(I32, logits.shape, 1).astype(F32)
    big = 1e9
    gl = jnp.where(lane < N_GROUPS, logits, -jnp.inf)
    gmax = jnp.max(gl, axis=1, keepdims=True)
    grp = jnp.min(jnp.where(gl == gmax, lane, big), axis=1, keepdims=True)
    p_grp = 1.0 / jnp.sum(jnp.exp(gl - gmax), axis=1, keepdims=True)
    first = N_GROUPS + grp * EXPERTS_PER_GROUP
    el = jnp.where((lane >= first) & (lane < first + EXPERTS_PER_GROUP), logits, -jnp.inf)
    l1 = jnp.max(el, axis=1, keepdims=True)
    i1 = jnp.min(jnp.where(el == l1, lane, big), axis=1, keepdims=True)
    el2 = jnp.where(lane == i1, -jnp.inf, el)
    l2 = jnp.max(el2, axis=1, keepdims=True)
    i2 = jnp.min(jnp.where(el2 == l2, lane, big), axis=1, keepdims=True)
    e2 = jnp.exp(l2 - l1)
    g1 = p_grp / (1.0 + e2)
    g2 = p_grp * e2 / (1.0 + e2)
    route = jnp.where(lane == 0, i1 - N_GROUPS,
                      jnp.where(lane == 1, i2 - N_GROUPS,
                                jnp.where(lane == 2, g1, jnp.where(lane == 3, g2, 0.0))))
    route_ref[...] = route


def _merge(x2, gm, wg, o_a, o_f, o_s, o_d, wb, wo, gf, wr, br, tm):
    n, d = x2.shape
    row = lambda w: pl.BlockSpec((tm, w), lambda i: (i, 0))
    full = lambda a: pl.BlockSpec(a.shape, lambda i: (0,) * a.ndim, pipeline_mode=pl.Buffered(1))
    return pl.pallas_call(
        _merge_kernel,
        grid=(n // tm,),
        in_specs=[row(d), full(gm), full(wg), row(256), row(256), row(256), row(256),
                  full(wb), full(wo), full(gf), full(wr), full(br)],
        out_specs=[row(d), row(d), row(LANES)],
        out_shape=[jax.ShapeDtypeStruct((n, d), F32), jax.ShapeDtypeStruct((n, d), F32),
                   jax.ShapeDtypeStruct((n, LANES), F32)],
        compiler_params=_cparams("arbitrary"),
        name="merge",
    )(x2, gm, wg, o_a, o_f, o_s, o_d, wb, wo, gf, wr, br)


def _rank_kernel(route_ref, ltri_ref, rank_ref, cnt_ref, carry_ref):
    @pl.when(pl.program_id(0) == 0)
    def _():
        carry_ref[...] = jnp.zeros_like(carry_ref)

    route = route_ref[...]
    lane = lax.broadcasted_iota(I32, route.shape, 1)
    e0 = route[:, 0:1].astype(I32)
    e1 = route[:, 1:2].astype(I32)
    oh0 = (lane == e0).astype(F32)
    oh1 = (lane == e1).astype(F32)
    both = oh0 + oh1
    before = jnp.dot(ltri_ref[...], both.astype(BF16), preferred_element_type=F32) + carry_ref[0:1, :]
    r0 = jnp.sum(oh0 * before, axis=1, keepdims=True)
    r1 = jnp.sum(oh1 * (before + oh0), axis=1, keepdims=True)
    rank_ref[...] = jnp.where(lane == 0, r0, jnp.where(lane == 1, r1, 0.0))
    total = carry_ref[0:1, :] + jnp.sum(both, axis=0, keepdims=True)
    carry_ref[0:1, :] = total
    cnt_ref[...] = jnp.broadcast_to(total, cnt_ref.shape)


def _rank(route, tm):
    n = route.shape[0]
    ltri = (jnp.arange(tm)[:, None] > jnp.arange(tm)[None, :]).astype(BF16)
    return pl.pallas_call(
        _rank_kernel,
        grid=(n // tm,),
        in_specs=[pl.BlockSpec((tm, LANES), lambda i: (i, 0)), pl.BlockSpec((tm, tm), lambda i: (0, 0))],
        out_specs=[pl.BlockSpec((tm, LANES), lambda i: (i, 0)), pl.BlockSpec((8, LANES), lambda i: (0, 0))],
        out_shape=[jax.ShapeDtypeStruct((n, LANES), F32), jax.ShapeDtypeStruct((8, LANES), F32)],
        scratch_shapes=[pltpu.VMEM((8, LANES), F32)],
        compiler_params=_cparams("arbitrary"),
        name="moe_rank",
    )(route, ltri)


def _expert_kernel(be_ref, nu_ref, tok_ref, tok1_ref, tok2_ref, h_hbm, wup_ref, wdn_ref, y_ref,
                   xbuf, sem, wup_b, wdn_b, *, te):
    b = pl.program_id(0)
    n_used = nu_ref[0]
    slot = b % 3

    def start_row(tokens_ref, dst, r):
        pltpu.make_async_copy(h_hbm.at[pl.ds(tokens_ref[0, 0, r], 1), :],
                              xbuf.at[dst, pl.ds(r, 1), :], sem.at[dst]).start()

    def start_block(tokens_ref, dst):
        def issue(r, c):
            start_row(tokens_ref, dst, r)
            return c
        lax.fori_loop(0, te, issue, 0, unroll=8)

    def block(prefetch):
        pltpu.make_async_copy(h_hbm.at[pl.ds(0, te), :], xbuf.at[slot], sem.at[slot]).wait()
        xb = xbuf[slot].astype(BF16)
        if prefetch:
            dst = (b + 2) % 3
            for r in range(te):
                start_row(tok2_ref, dst, r)
        gu = jnp.dot(xb, wup_b[...], preferred_element_type=F32)
        g = gu[:, :EXPERT_FF]
        act = g * jax.nn.sigmoid(g) * gu[:, EXPERT_FF:]
        y_ref[...] = jnp.dot(act.astype(BF16), wdn_b[...], preferred_element_type=F32)

    @pl.when((b < n_used) & ((b == 0) | (be_ref[b] != be_ref[jnp.maximum(b - 1, 0)])))
    def _():
        wup_b[...] = wup_ref[0, 0].astype(BF16)
        wdn_b[...] = wdn_ref[0, 0].astype(BF16)

    @pl.when((b == 0) & (n_used > 0))
    def _():
        start_block(tok_ref, 0)

    @pl.when((b == 0) & (n_used > 1))
    def _():
        start_block(tok1_ref, 1)

    @pl.when(b + 2 < n_used)
    def _():
        block(True)

    @pl.when((b < n_used) & (b + 2 >= n_used))
    def _():
        block(False)

    @pl.when(b >= n_used)
    def _():
        y_ref[...] = jnp.zeros_like(y_ref)


def _experts(blk_expert, n_used, slot_tok, h2, w_up, w_down, layer, te):
    n_blocks = blk_expert.shape[0]
    d = h2.shape[1]
    ahead = lambda k: pl.BlockSpec((1, 1, te), lambda b, be, nu: (jnp.minimum(b + k, n_blocks - 1), 0, 0),
                                   memory_space=pltpu.SMEM)
    grid_spec = pltpu.PrefetchScalarGridSpec(
        num_scalar_prefetch=2,
        grid=(n_blocks,),
        in_specs=[ahead(0), ahead(1), ahead(2),
                  pl.BlockSpec(memory_space=pl.ANY),
                  pl.BlockSpec((1, 1, d, 2 * EXPERT_FF), lambda b, be, nu: (layer, be[b], 0, 0)),
                  pl.BlockSpec((1, 1, EXPERT_FF, d), lambda b, be, nu: (layer, be[b], 0, 0))],
        out_specs=pl.BlockSpec((te, d), lambda b, be, nu: (b, 0)),
        scratch_shapes=[pltpu.VMEM((3, te, d), F32), pltpu.SemaphoreType.DMA((3,)),
                        pltpu.VMEM((d, 2 * EXPERT_FF), BF16), pltpu.VMEM((EXPERT_FF, d), BF16)],
    )
    slots = slot_tok.reshape(n_blocks, 1, te)
    return pl.pallas_call(
        functools.partial(_expert_kernel, te=te),
        grid_spec=grid_spec,
        out_shape=jax.ShapeDtypeStruct((n_blocks * te, d), F32),
        compiler_params=_cparams("arbitrary"),
        name="moe_experts",
    )(blk_expert, n_used, slots, slots, slots, h2, w_up, w_down)


def _combine_kernel(pos_ref, pos_next_ref, x_ref, route_ref, y_hbm, o_ref, ybuf, sem, *, tc):
    i = pl.program_id(0)
    slot = i % 2

    def gather(rows_ref, dst):
        for r in range(2 * tc):
            pltpu.make_async_copy(y_hbm.at[pl.ds(rows_ref[0, 0, r], 1), :],
                                  ybuf.at[dst, pl.ds(r, 1), :], sem.at[dst]).start()

    @pl.when(i == 0)
    def _():
        gather(pos_ref, 0)

    @pl.when(i + 1 < pl.num_programs(0))
    def _():
        gather(pos_next_ref, 1 - slot)

    pltpu.make_async_copy(y_hbm.at[pl.ds(0, 2 * tc), :], ybuf.at[slot], sem.at[slot]).wait()
    route = route_ref[...]
    o_ref[...] = (x_ref[...] + route[:, 2:3] * ybuf[slot, 0:tc, :]
                  + route[:, 3:4] * ybuf[slot, tc:2 * tc, :])


def _combine(pos, x2, route, yb, tc):
    n, d = x2.shape
    nt = n // tc
    pos_t = jnp.transpose(pos.reshape(nt, tc, 2), (0, 2, 1)).reshape(nt, 1, 2 * tc)
    return pl.pallas_call(
        functools.partial(_combine_kernel, tc=tc),
        grid=(nt,),
        in_specs=[pl.BlockSpec((1, 1, 2 * tc), lambda i: (i, 0, 0), memory_space=pltpu.SMEM),
                  pl.BlockSpec((1, 1, 2 * tc), lambda i: (jnp.minimum(i + 1, nt - 1), 0, 0),
                               memory_space=pltpu.SMEM),
                  pl.BlockSpec((tc, d), lambda i: (i, 0)),
                  pl.BlockSpec((tc, LANES), lambda i: (i, 0)),
                  pl.BlockSpec(memory_space=pl.ANY)],
        out_specs=pl.BlockSpec((tc, d), lambda i: (i, 0)),
        out_shape=jax.ShapeDtypeStruct((n, d), F32),
        scratch_shapes=[pltpu.VMEM((2, 2 * tc, d), F32), pltpu.SemaphoreType.DMA((2,))],
        compiler_params=_cparams("arbitrary"),
        name="moe_combine",
    )(pos_t, pos_t, x2, route, yb)


def _swap_mid_heads(w, axis):
    h = jnp.split(w, 4, axis=axis)
    return jnp.concatenate([h[0], h[2], h[1], h[3]], axis=axis)


def _t5_lookup(t5_table, dist):
    onehot = (_t5_bucket(dist)[..., None] == jnp.arange(T5_BUCKETS)).astype(F32)
    return jnp.einsum("...b,bh->...h", onehot, t5_table.astype(F32), precision=lax.Precision.HIGHEST)


def _t5_bucket(dist):
    n = jnp.maximum(dist, 0)
    max_exact = T5_BUCKETS // 2
    nf = jnp.maximum(n, 1).astype(F32)
    large = max_exact + (jnp.log(nf / max_exact) / math.log(T5_MAX_DISTANCE / max_exact)
                         * (T5_BUCKETS - max_exact)).astype(I32)
    large = jnp.minimum(large, T5_BUCKETS - 1)
    return jnp.where(n < max_exact, n, large)


def _swa_bias(t5_table):
    t = SWA_BLOCK
    dist = t + jnp.arange(t)[None, :] - jnp.arange(2 * t)[:, None]
    tile = jnp.where(((dist >= 0) & (dist < t))[..., None], _t5_lookup(t5_table, dist)[..., :4], NEG_INF)
    return jnp.concatenate([tile[..., h] for h in (0, 2, 1, 3)], axis=1)


def _layer_weights(w_in, qk_gain, forget_bias, w_branch):
    offs = np.concatenate([[0], np.cumsum(IN_SPLITS)]).tolist()
    part = lambda k: w_in[:, offs[k]:offs[k + 1]]
    dup = lambda w: jnp.concatenate([w, w], axis=1)
    aq = _swap_mid_heads(part(0), 1)
    cols = [aq, part(1), part(2), part(3), part(4), part(5), part(7), part(8), part(9),
            part(10), dup(part(11)), dup(part(12)), part(13), dup(part(14))]
    w1 = jnp.concatenate(cols, axis=1).astype(BF16)
    d = w_in.shape[0]
    wm = _hi_lo(jnp.concatenate([part(6), part(15), jnp.zeros((d, LANES - 8), F32)], axis=1))
    wg = part(16).astype(BF16)
    tile = lambda g, reps, scale: jnp.pad(jnp.tile(g, reps) * scale, (0, 256 - reps * HEAD_DIM))
    gains = jnp.stack([tile(qk_gain[0, 0], 4, ATTN_SCALE), tile(qk_gain[0, 1], 2, 1.0),
                       tile(qk_gain[1, 0], 4, ATTN_SCALE * LOG2E), tile(qk_gain[1, 1], 4, 1.0),
                       tile(qk_gain[2, 0], 4, ATTN_SCALE * LOG2E), tile(qk_gain[2, 1], 2, 1.0),
                       jnp.zeros((256,), F32), jnp.zeros((256,), F32)]).astype(F32)
    fb = jnp.pad(forget_bias.astype(F32), (0, LANES - 4)).reshape(1, LANES)
    wb = jnp.stack([_swap_mid_heads(w_branch[0], 0), w_branch[1], w_branch[2], w_branch[3]]).astype(BF16)
    return w1, wm, wg, gains, fb, wb


def kernel(x, norm_mix_g, w_in, forget_bias, attn_sinks, qk_gain, w_branch, w_out, t5_table, norm_ffn_g,
           w_router_group, b_router_group, w_router_expert, b_router_expert, w_expert_up, w_expert_down):
    b, s, d = x.shape
    n = b * s
    depth = w_in.shape[0]
    top_k = min(DSA_TOPK_MAX, s // 4)
    tm_proj = min(512, s)
    fox_t = min(256, s)
    sb_t = min(256, s)
    dsa_t = min(256, s)
    te = 256
    tc = 256

    gseg = (jnp.arange(256)[:, None] // HEAD_DIM == jnp.arange(256)[None, :] // HEAD_DIM).astype(BF16)
    bias_swa = _swa_bias(t5_table)
    bias_dsa = _dsa_bias(t5_table, dsa_t) * LOG2E
    n_blocks = -(-2 * n // te) + N_EXPERTS
    tok_ids = jnp.repeat(jnp.arange(n, dtype=I32), 2)

    for layer in range(depth):
        w1, wm, wg, gains, fb, wb = _layer_weights(w_in[layer], qk_gain[layer], forget_bias[layer],
                                                   w_branch[layer])
        sink_row = jnp.repeat(attn_sinks[layer].astype(F32)[jnp.array([0, 2, 1, 3])], SWA_BLOCK)
        sinks = jnp.broadcast_to(sink_row[None, :], (8, 4 * SWA_BLOCK))
        (aq, ak, av, fq, fk, fv, sq, sk, sv, dq, dkk, dvv, iq, ikk, cm) = _proj(
            x, norm_mix_g[layer].reshape(1, d), w1, wm, gseg, gains, fb, tm_proj)

        o_swa = _swa(aq, ak, av, bias_swa, sinks, min(4, s // SWA_BLOCK))
        cmt = jnp.transpose(cm[:, :, :8], (0, 2, 1))
        o_fox = _fox(fq, fk, fv, cmt, cm, fox_t)
        o_sb = _sb(sq, sk, sv, sb_t)
        o_dsa = _dsa(dq, dkk, dvv, iq, ikk, cmt, bias_dsa, top_k, dsa_t)

        wr = _hi_lo(jnp.concatenate([w_router_group[layer], w_router_expert[layer],
                                     jnp.zeros((d, LANES - N_GROUPS - N_EXPERTS), F32)], axis=1))
        br = jnp.concatenate([b_router_group[layer], b_router_expert[layer],
                              jnp.zeros((LANES - N_GROUPS - N_EXPERTS,), F32)]).reshape(1, LANES)
        x2, h2, route = _merge(
            x.reshape(n, d), norm_mix_g[layer].reshape(1, d), wg,
            o_swa.reshape(n, 256), o_fox.reshape(n, 256), o_sb.reshape(n, 256), o_dsa.reshape(n, 256),
            wb, w_out[layer].astype(BF16), norm_ffn_g[layer].reshape(1, d), wr, br, min(512, n))

        rank, cnt = _rank(route, min(512, n))
        counts = cnt[0, :N_EXPERTS].astype(I32)
        padded = (counts + te - 1) // te * te
        pend = jnp.cumsum(padded)
        pstart = pend - padded
        expert = route[:, :2].astype(I32)
        own = expert[:, :, None] == jnp.arange(N_EXPERTS, dtype=I32)
        pos = jnp.sum(jnp.where(own, pstart, 0), axis=-1) + rank[:, :2].astype(I32)
        slot_tok = jnp.zeros((n_blocks * te,), I32).at[pos.reshape(-1)].set(
            tok_ids, unique_indices=True, mode="promise_in_bounds")
        blk_start = jnp.arange(n_blocks, dtype=I32)[:, None] * te
        blk_expert = jnp.minimum(jnp.sum((pend[None, :] <= blk_start).astype(I32), axis=1), N_EXPERTS - 1)
        n_used = (pend[-1:] // te).astype(I32)

        yb = _experts(blk_expert, n_used, slot_tok, h2, w_expert_up, w_expert_down, layer, te)
        x = _combine(pos, x2, route, yb, tc).reshape(b, s, d)
    return x
```

```python
import functools
import math

import jax
import jax.numpy as jnp
import numpy as np
from jax import lax
from jax.experimental import pallas as pl
from jax.experimental.pallas import tpu as pltpu

F32 = jnp.float32
BF16 = jnp.bfloat16
I32 = jnp.int32

HEAD_DIM = 64
LANES = 128
NORM_EPS = 1e-6
NEG_INF = -1e30
M_INIT = -1e29
ATTN_SCALE = HEAD_DIM ** -0.5
LOG2E = math.log2(math.e)
SWA_BLOCK = 128
IDX_SCALE = 64 ** -0.5
IDX_HEADS = 4
DSA_TOPK_MAX = 256
T5_BUCKETS = 32
T5_MAX_DISTANCE = 128
N_GROUPS = 4
EXPERTS_PER_GROUP = 8
N_EXPERTS = N_GROUPS * EXPERTS_PER_GROUP
EXPERT_FF = 512
SB_DEAD = -110.0
INT_MIN = -2 ** 31
VMEM_LIMIT = 56 * 1024 * 1024

IN_SPLITS = (256, 128, 128, 256, 256, 256, 4, 256, 256, 256, 256, 64, 64, 256, 64, 4, 4096)

_SEG = dict(aq=(0, 256), ak=(256, 128), av=(384, 128), fq=(512, 256), fk=(768, 256), fv=(1024, 256),
            sq=(1280, 256), sk=(1536, 256), sv=(1792, 256), dq=(2048, 256), dkk=(2304, 128),
            dvv=(2432, 128), iq=(2560, 256), ikk=(2816, 128))
_W1_COLS = 2944
_SEG_ORDER = ("aq", "ak", "av", "fq", "fk", "fv", "sq", "sk", "sv", "dq", "dkk", "dvv", "iq", "ikk")


def _cparams(*sem):
    return pltpu.CompilerParams(dimension_semantics=sem, vmem_limit_bytes=VMEM_LIMIT)


def _rms(x, g):
    return x * lax.rsqrt(jnp.mean(x * x, axis=-1, keepdims=True) + NORM_EPS) * g


def _log_sigmoid(z):
    return jnp.minimum(z, 0.0) - jnp.log(1.0 + jnp.exp(-jnp.abs(z)))


def _dot_nt(a, b):
    return lax.dot_general(a, b, (((1,), (1,)), ((), ())), preferred_element_type=F32)


def _split3(x):
    p1 = x.astype(BF16)
    r = x - p1.astype(F32)
    p2 = r.astype(BF16)
    return p1, p2, (r - p2.astype(F32)).astype(BF16)


def _hi_lo(w):
    hi = w.astype(BF16)
    return jnp.stack([hi, (w - hi.astype(F32)).astype(BF16)])


def _dot_x3(a, b_hi, b_lo):
    a_hi = a.astype(BF16)
    a_lo = (a - a_hi.astype(F32)).astype(BF16)
    return (jnp.dot(a_hi, b_hi, preferred_element_type=F32) + jnp.dot(a_lo, b_hi, preferred_element_type=F32)
            + jnp.dot(a_hi, b_lo, preferred_element_type=F32))


def _split_heads(qp):
    lo = lax.broadcasted_iota(I32, (1, LANES), 1) < HEAD_DIM
    zero = jnp.zeros_like(qp)
    return jnp.concatenate([jnp.where(lo, qp, zero), jnp.where(lo, zero, qp)], axis=0)


def _merge_heads(o, t):
    lo = lax.broadcasted_iota(I32, (1, LANES), 1) < HEAD_DIM
    return jnp.where(lo, o[:t], o[t:])


def _proj_kernel(x_ref, g_ref, w1_ref, wm_ref, gseg_ref, gains_ref, fb_ref, ltri_ref, *rest):
    outs = dict(zip(_SEG_ORDER, rest[:len(_SEG_ORDER)]))
    cm_ref = rest[len(_SEG_ORDER)]
    carry_ref = rest[len(_SEG_ORDER) + 1]

    @pl.when(pl.program_id(1) == 0)
    def _():
        carry_ref[...] = jnp.zeros_like(carry_ref)

    h = _rms(x_ref[0], g_ref[...])
    hb = h.astype(BF16)

    def seg(name):
        off, width = _SEG[name]
        return jnp.dot(hb, w1_ref[:, off:off + width], preferred_element_type=F32)

    def head_norm(t, row):
        width = t.shape[1]
        ssq = jnp.dot((t * t).astype(BF16), gseg_ref[:width, :width], preferred_element_type=F32)
        return t * lax.rsqrt(ssq * (1.0 / HEAD_DIM) + NORM_EPS) * gains_ref[row:row + 1, :width]

    normed = dict(aq=0, ak=1, fq=2, fk=3, dq=4, dkk=5)
    scaled = dict(sq=ATTN_SCALE * LOG2E, iq=IDX_SCALE)
    for name in _SEG_ORDER:
        t = seg(name)
        if name in normed:
            t = head_norm(t, normed[name])
        elif name in scaled:
            t = t * scaled[name]
        outs[name][0] = t.astype(BF16)

    misc = _dot_x3(h, wm_ref[0], wm_ref[1])
    lane = lax.broadcasted_iota(I32, misc.shape, 1)
    logf = jnp.where(lane < 4, _log_sigmoid(misc + fb_ref[...]), 0.0)
    ltri = ltri_ref[...]
    c = carry_ref[0:1, :]
    for piece in _split3(logf):
        c = c + jnp.dot(ltri, piece, preferred_element_type=F32)
    tm = misc.shape[0]
    carry_ref[0:1, :] = c[tm - 1:tm, :]
    cm_ref[0] = jnp.where(lane < 4, c, misc)


def _proj(x, g, w1, wm, gseg, gains, fb, tm):
    b, s, d = x.shape
    ltri = jnp.tril(jnp.ones((tm, tm), BF16))
    full = lambda shape: pl.BlockSpec(shape, lambda bi, si: (0,) * len(shape))
    out_shapes = [jax.ShapeDtypeStruct((b, s, _SEG[n][1]), BF16) for n in _SEG_ORDER]
    out_shapes.append(jax.ShapeDtypeStruct((b, s, LANES), F32))
    out_specs = [pl.BlockSpec((1, tm, _SEG[n][1]), lambda bi, si: (bi, si, 0)) for n in _SEG_ORDER]
    out_specs.append(pl.BlockSpec((1, tm, LANES), lambda bi, si: (bi, si, 0)))
    return pl.pallas_call(
        _proj_kernel,
        grid=(b, s // tm),
        in_specs=[pl.BlockSpec((1, tm, d), lambda bi, si: (bi, si, 0)),
                  full((1, d)), full(w1.shape), full(wm.shape), full(gseg.shape),
                  full(gains.shape), full(fb.shape), full((tm, tm))],
        out_specs=out_specs,
        out_shape=out_shapes,
        scratch_shapes=[pltpu.VMEM((8, LANES), F32)],
        compiler_params=_cparams("arbitrary", "arbitrary"),
        name="proj",
    )(x, g, w1, wm, gseg, gains, fb, ltri)


def _swa_kernel(q_ref, kp_ref, kc_ref, vtp_ref, vtc_ref, bias_ref, sink_ref, o_ref, *, nsub):
    i = pl.program_id(1)
    t = SWA_BLOCK
    no_prev = (lax.broadcasted_iota(I32, (2 * t, 4 * t), 0) < t) & (i == 0)
    sink = sink_ref[0:1, :]
    for u in range(nsub):
        rows = slice(u * t, (u + 1) * t)
        q = q_ref[0, rows, :]
        qs = jnp.concatenate([_split_heads(q[:, :LANES]), _split_heads(q[:, LANES:])], axis=0)
        if u == 0:
            kcat = jnp.concatenate([kp_ref[0], kc_ref[0, rows, :]], axis=0)
            vt = jnp.concatenate([vtp_ref[0], vtc_ref[0, :, rows]], axis=1)
        else:
            kcat = kc_ref[0, (u - 1) * t:(u + 1) * t, :]
            vt = vtc_ref[0, :, (u - 1) * t:(u + 1) * t]
        st = _dot_nt(kcat, qs) + bias_ref[...]
        if u == 0:
            st = jnp.where(no_prev, NEG_INF, st)
        m = jnp.maximum(jnp.max(st, axis=0, keepdims=True), sink)
        p = jnp.exp(st - m)
        denom = jnp.sum(p, axis=0, keepdims=True) + jnp.exp(sink - m)
        ot = jnp.dot(vt, p.astype(BF16), preferred_element_type=F32) / denom
        ot = jnp.concatenate([ot[:HEAD_DIM, 0:t], ot[HEAD_DIM:, t:2 * t],
                              ot[:HEAD_DIM, 2 * t:3 * t], ot[HEAD_DIM:, 3 * t:]], axis=0)
        o_ref[0, rows, :] = ot.T.astype(BF16)


def _swa(aq, ak, av, bias, sinks, nsub):
    b, s, _ = aq.shape
    t = SWA_BLOCK
    avt = jnp.transpose(av, (0, 2, 1))
    cur = lambda bi, i: (bi, i, 0)
    prev = lambda bi, i: (bi, jnp.maximum(i * nsub - 1, 0), 0)
    return pl.pallas_call(
        functools.partial(_swa_kernel, nsub=nsub),
        grid=(b, s // (t * nsub)),
        in_specs=[pl.BlockSpec((1, t * nsub, 256), cur),
                  pl.BlockSpec((1, t, LANES), prev), pl.BlockSpec((1, t * nsub, LANES), cur),
                  pl.BlockSpec((1, LANES, t), lambda bi, i: (bi, 0, jnp.maximum(i * nsub - 1, 0))),
                  pl.BlockSpec((1, LANES, t * nsub), lambda bi, i: (bi, 0, i)),
                  pl.BlockSpec(bias.shape, lambda bi, i: (0, 0)),
                  pl.BlockSpec(sinks.shape, lambda bi, i: (0, 0))],
        out_specs=pl.BlockSpec((1, t * nsub, 256), cur),
        out_shape=jax.ShapeDtypeStruct((b, s, 256), BF16),
        compiler_params=_cparams("arbitrary", "arbitrary"),
        name="swa",
    )(aq, ak, ak, avt, avt, bias, sinks)


def _fox_kernel(q_ref, k_ref, vt_ref, ct_ref, ccol_ref, o_ref, ckb_ref, *, t):
    i = pl.program_id(1)
    n_tiles = ckb_ref.shape[1] // t

    @pl.when(i == 0)
    def _():
        def fill(j, c):
            rows = pl.ds(pl.multiple_of(j * t, t), t)
            cc = ccol_ref[0, rows, :] * LOG2E
            for h in range(4):
                ckb_ref[h, rows, :] = jnp.broadcast_to(cc[:, h:h + 1], (t, LANES))
            return c
        lax.fori_loop(0, n_tiles, fill, 0)

    q = q_ref[0]
    ct = ct_ref[0] * LOG2E
    qs = [_split_heads(q[:, :LANES]), _split_heads(q[:, LANES:])]
    valid = lax.broadcasted_iota(I32, (t, t), 0) <= lax.broadcasted_iota(I32, (t, t), 1)

    def tiles(j0, n, carry, masked):
        m, l, accs = carry
        rows = pl.ds(pl.multiple_of(j0 * t, t), n * t)
        cols = []
        for pair in range(2):
            st = _dot_nt(k_ref[0, rows, pair * LANES:(pair + 1) * LANES], qs[pair])
            for hh in range(2):
                head = 2 * pair + hh
                ck = ckb_ref[head, rows, :]
                for c in range(t // LANES):
                    cs = slice(c * LANES, (c + 1) * LANES)
                    sh = st[:, hh * t + c * LANES:hh * t + (c + 1) * LANES] + (ct[head:head + 1, cs] - ck)
                    if masked:
                        last = jnp.where(valid[:, cs], sh[(n - 1) * t:], NEG_INF)
                        sh = jnp.concatenate([sh[:(n - 1) * t], last], axis=0) if n > 1 else last
                    cols.append(sh)
        st = jnp.concatenate(cols, axis=1)
        m_new = jnp.maximum(m, jnp.max(st, axis=0, keepdims=True))
        alpha = jnp.exp2(m - m_new)
        p = jnp.exp2(st - m_new)
        l = alpha * l + jnp.sum(p, axis=0, keepdims=True)
        pb = p.astype(BF16)
        new = []
        for pair in range(2):
            lanes = slice(pair * 2 * t, (pair + 1) * 2 * t)
            vt = jnp.concatenate([vt_ref[0, j0 + u, pair * LANES:(pair + 1) * LANES, :] for u in range(n)],
                                 axis=1)
            pv = jnp.dot(vt, pb[:, lanes], preferred_element_type=F32)
            new.append(alpha[:, lanes] * accs[pair] + pv)
        return m_new, l, tuple(new)

    init = (jnp.full((1, 4 * t), M_INIT, F32), jnp.zeros((1, 4 * t), F32),
            (jnp.zeros((LANES, 2 * t), F32), jnp.zeros((LANES, 2 * t), F32)))
    carry = lax.fori_loop(0, i // 4, lambda g, c: tiles(4 * g, 4, c, False), init)
    rest = i - i % 4
    _, l, accs = lax.switch(i % 4, [functools.partial(lambda c, n: tiles(rest, n, c, True), n=r + 1)
                                    for r in range(4)], carry)
    outs = []
    for pair in range(2):
        o = accs[pair] / l[:, pair * 2 * t:(pair + 1) * 2 * t]
        outs.append(jnp.concatenate([o[:HEAD_DIM, :t], o[HEAD_DIM:, t:]], axis=0))
    o_ref[0] = jnp.concatenate(outs, axis=0).T.astype(BF16)


def _fox(fq, fk, fv, ct, ccol, t):
    b, s, _ = fq.shape
    nt = s // t
    vt = jnp.transpose(fv.reshape(b, nt, t, 256), (0, 1, 3, 2))
    return pl.pallas_call(
        functools.partial(_fox_kernel, t=t),
        grid=(b, nt),
        in_specs=[pl.BlockSpec((1, t, 256), lambda bi, i: (bi, i, 0)),
                  pl.BlockSpec((1, s, 256), lambda bi, i: (bi, 0, 0)),
                  pl.BlockSpec((1, nt, 256, t), lambda bi, i: (bi, 0, 0, 0)),
                  pl.BlockSpec((1, 8, t), lambda bi, i: (bi, 0, i)),
                  pl.BlockSpec((1, s, LANES), lambda bi, i: (bi, 0, 0))],
        out_specs=pl.BlockSpec((1, t, 256), lambda bi, i: (bi, i, 0)),
        out_shape=jax.ShapeDtypeStruct((b, s, 256), BF16),
        scratch_shapes=[pltpu.VMEM((4, s, LANES), F32)],
        compiler_params=_cparams("arbitrary", "arbitrary"),
        name="fox",
    )(fq, fk, vt, ct, ccol)


def _sb_kernel(q_ref, k_ref, vt_ref, lgt_ref, o_ref, *, t):
    i = pl.program_id(1)
    q = q_ref[0]
    lgt = lgt_ref[...]
    qs = [_split_heads(q[:, :LANES]), _split_heads(q[:, LANES:])]
    key_i = lax.broadcasted_iota(I32, (t, 4 * t), 0)
    query_i = lax.broadcasted_iota(I32, (t, 4 * t), 1) & (t - 1)
    strict = key_i < query_i

    def tile(j, r, accs, masked):
        rows = pl.ds(pl.multiple_of(j * t, t), t)
        z = jnp.concatenate([_dot_nt(k_ref[0, rows, p * LANES:(p + 1) * LANES], qs[p]) for p in range(2)],
                            axis=1)
        sp = jnp.log2(1.0 + jnp.exp2(-jnp.abs(z)))
        log_beta = jnp.minimum(z, 0.0) - sp
        log_keep = jnp.minimum(-z, 0.0) - sp
        if masked:
            log_keep = jnp.where(strict, log_keep, 0.0)
        hi = log_keep.astype(BF16)
        lo = (log_keep - hi.astype(F32)).astype(BF16)
        later = (jnp.dot(lgt, hi, preferred_element_type=F32)
                 + jnp.dot(lgt, lo, preferred_element_type=F32))
        a = jnp.exp2(log_beta + later + r)
        if masked:
            a = jnp.where(strict, a, 0.0)
        ab = a.astype(BF16)
        new = tuple(accs[p] + jnp.dot(vt_ref[0, j, p * LANES:(p + 1) * LANES, :],
                                      ab[:, p * 2 * t:(p + 1) * 2 * t], preferred_element_type=F32)
                    for p in range(2))
        return r + jnp.sum(log_keep, axis=0, keepdims=True), new

    zero_acc = jnp.zeros((LANES, 2 * t), F32)
    r, accs = tile(i, jnp.zeros((1, 4 * t), F32), (zero_acc, zero_acc), True)

    def cond(c):
        return (c[0] >= 0) & (c[1] > 0)

    def body(c):
        j, _, r, accs = c
        r, accs = tile(j, r, accs, False)
        return j - 1, (jnp.max(r) > SB_DEAD * LOG2E).astype(I32), r, accs

    _, _, _, accs = lax.while_loop(cond, body, (i - 1, (jnp.max(r) > SB_DEAD * LOG2E).astype(I32), r, accs))
    outs = [jnp.concatenate([accs[p][:HEAD_DIM, :t], accs[p][HEAD_DIM:, t:]], axis=0) for p in range(2)]
    o_ref[0] = jnp.concatenate(outs, axis=0).T.astype(BF16)


def _sb(sq, sk, sv, t):
    b, s, _ = sq.shape
    nt = s // t
    lgt = (jnp.arange(t)[:, None] < jnp.arange(t)[None, :]).astype(BF16)
    vt = jnp.transpose(sv.reshape(b, nt, t, 256), (0, 1, 3, 2))
    return pl.pallas_call(
        functools.partial(_sb_kernel, t=t),
        grid=(b, nt),
        in_specs=[pl.BlockSpec((1, t, 256), lambda bi, i: (bi, i, 0)),
                  pl.BlockSpec((1, s, 256), lambda bi, i: (bi, 0, 0)),
                  pl.BlockSpec((1, nt, 256, t), lambda bi, i: (bi, 0, 0, 0)),
                  pl.BlockSpec((t, t), lambda bi, i: (0, 0))],
        out_specs=pl.BlockSpec((1, t, 256), lambda bi, i: (bi, i, 0)),
        out_shape=jax.ShapeDtypeStruct((b, s, 256), BF16),
        compiler_params=_cparams("arbitrary", "arbitrary"),
        name="stickbreak",
    )(sq, sk, vt, lgt)


def _bit_planes(words):
    words = list(words)
    j, m = 16, 0x0000FFFF
    while j:
        k = 0
        while k < 32:
            tt = (words[k] ^ lax.shift_right_logical(words[k + j], jnp.full_like(words[k + j], j))) & m
            words[k] = words[k] ^ tt
            words[k + j] = words[k + j] ^ (tt << j)
            k = (k + j + 1) & ~j
        j >>= 1
        m = (m ^ (m << j)) & 0xFFFFFFFF
    return words


def _dsa_kernel(q_ref, kk_ref, vt_ref, iq_ref, ikk_ref, wt_ref, bias_ref, lstrict_ref, o_ref,
                key_ref, plane_ref, *, t, top_k):
    i = pl.program_id(1)
    assert t == 8 * 32

    @pl.when(i == 0)
    def _():
        plane_ref[...] = jnp.zeros_like(plane_ref)
    causal = lax.broadcasted_iota(I32, (t, t), 0) <= lax.broadcasted_iota(I32, (t, t), 1)

    def head_stack(x):
        return jnp.concatenate([_split_heads(x[:, :LANES]), _split_heads(x[:, LANES:])], axis=0)

    def key_rows(ref, j0, n):
        return ref[0, pl.ds(pl.multiple_of(j0 * t, t), n * t), :]

    iqs = head_stack(iq_ref[0])
    wt = wt_ref[0]
    w = [wt[4 + h:5 + h, :] * (IDX_HEADS ** -0.5) for h in range(IDX_HEADS)]

    def score_tiles(j0, n, masked):
        lg = _dot_nt(key_rows(ikk_ref, j0, n), iqs)
        sc = w[0] * jnp.maximum(lg[:, 0:t], 0.0)
        for h in range(1, IDX_HEADS):
            sc = sc + w[h] * jnp.maximum(lg[:, h * t:(h + 1) * t], 0.0)
        bits = pltpu.bitcast(sc, I32)
        key = bits ^ ((bits >> 31) & 0x7FFFFFFF)
        key = jnp.where(key == -1, 0, key)
        for u in range(n):
            key_u = key[u * t:(u + 1) * t]
            if masked and u == n - 1:
                key_u = jnp.where(causal, key_u, INT_MIN)
            key_ref[j0 + u] = key_u
            for p, plane in enumerate(_bit_planes([key_u[8 * g:8 * g + 8, :] for g in range(32)])):
                plane_ref[p, j0 + u] = plane

    def p1(g, c):
        score_tiles(4 * g, 4, False)
        return c

    lax.fori_loop(0, i // 4, p1, 0)
    lax.switch(i % 4, [functools.partial(score_tiles, i - i % 4, r + 1, True) for r in range(4)])

    def popcount_rows(words):
        per_tile = jnp.sum(lax.population_count(words), axis=0)
        return jnp.sum(per_tile.astype(F32), axis=0, keepdims=True)

    def bis_body(p, c):
        alive, n_gt, thr_u = c
        plane = plane_ref[p] ^ jnp.where(p == 0, -1, 0)
        ones = alive & plane
        cnt = popcount_rows(ones)
        take = n_gt + cnt >= top_k
        alive = jnp.where(take, ones, alive ^ ones)
        n_gt = jnp.where(take, n_gt, n_gt + cnt)
        thr_u = jnp.where(take, thr_u | (jnp.int32(1) << (31 - p)), thr_u)
        return alive, n_gt, thr_u

    n_tiles = key_ref.shape[0]
    alive0 = jnp.where(lax.broadcasted_iota(I32, (n_tiles, 8, t), 0) <= i, -1, 0)
    alive, n_gt, thr_u = lax.fori_loop(
        0, 32, bis_body, (alive0, jnp.zeros((1, t), F32), jnp.zeros((1, t), I32)))
    thr = jnp.maximum(thr_u ^ INT_MIN, INT_MIN + 1)
    n_avail = (i * t + lax.broadcasted_iota(I32, (1, t), 1) + 1).astype(F32)
    n_ge = jnp.where(n_avail > top_k, n_gt + popcount_rows(alive), 0.0)
    surplus = jnp.max(n_ge) > top_k

    def tie_pass():
        need = top_k - n_gt

        def tb(j, seen):
            k = key_ref[j]
            eq = k == thr
            eqf = jnp.where(eq, 1.0, 0.0)
            before = jnp.dot(lstrict_ref[...], eqf.astype(BF16), preferred_element_type=F32) + seen
            sel = (k > thr) | (eq & (before < need))
            key_ref[j] = jnp.where(sel, 1, INT_MIN)
            return seen + jnp.sum(eqf, axis=0, keepdims=True)

        lax.fori_loop(0, i + 1, tb, jnp.zeros((1, t), F32))
        return jnp.zeros((1, t), I32)

    thr = lax.cond(surplus, tie_pass, lambda: thr)

    qs = head_stack(q_ref[0])

    def attn_tiles(j0, n, carry, near):
        m, l, acc = carry
        st = _dot_nt(key_rows(kk_ref, j0, n), qs)
        selb = jnp.concatenate([jnp.where(key_ref[j0 + u] >= thr, 0.0, NEG_INF) for u in range(n)], axis=0)
        parts = []
        for h in range(4):
            sh = st[:, h * t:(h + 1) * t] + selb
            if near:
                biased = [sh[(n - near + v) * t:(n - near + v + 1) * t] + bias_ref[2 - near + v, h]
                          for v in range(near)]
                sh = jnp.concatenate(([sh[:(n - near) * t]] if n > near else []) + biased, axis=0)
            parts.append(sh)
        st = jnp.concatenate(parts, axis=1)
        m_new = jnp.maximum(m, jnp.max(st, axis=0, keepdims=True))
        alpha = jnp.exp2(m - m_new)
        p = jnp.exp2(st - m_new)
        l = alpha * l + jnp.sum(p, axis=0, keepdims=True)
        vt = jnp.concatenate([vt_ref[0, j0 + u] for u in range(n)], axis=1)
        acc = alpha * acc + jnp.dot(vt, p.astype(BF16), preferred_element_type=F32)
        return m_new, l, acc

    init = (jnp.full((1, 4 * t), M_INIT, F32), jnp.zeros((1, 4 * t), F32), jnp.zeros((HEAD_DIM, 4 * t), F32))
    far = jnp.maximum(i - 1, 0)
    carry = lax.fori_loop(0, far // 4, lambda g, c: attn_tiles(4 * g, 4, c, 0), init)
    rest = far - far % 4
    last = [lambda c: attn_tiles(0, 1, c, 1)]
    last += [functools.partial(lambda c, n: attn_tiles(rest, n, c, 2), n=r + 2) for r in range(4)]
    _, l, acc = lax.switch(jnp.where(i == 0, 0, 1 + far % 4), last, carry)
    ot = acc / l
    ot = jnp.concatenate([ot[:, h * t:(h + 1) * t] for h in range(4)], axis=0)
    o_ref[0] = ot.T.astype(BF16)


def _dsa_bias(t5_table, t):
    assert t + 1 >= T5_MAX_DISTANCE
    k = jnp.arange(t)[:, None]
    q = jnp.arange(t)[None, :]
    far = t5_table[T5_BUCKETS - 1, 4:].astype(F32)
    tiles = []
    for off in (t, 0):
        dist = off + q - k
        b = jnp.transpose(_t5_lookup(t5_table, dist)[..., 4:], (2, 0, 1)) - far[:, None, None]
        tiles.append(jnp.where((dist >= 0)[None], b, 0.0))
    return jnp.stack(tiles)


def _dsa(dq, dkk, dvv, iq, ikk, wt, bias, top_k, t):
    b, s, _ = dq.shape
    nt = s // t
    lstrict = (jnp.arange(t)[:, None] > jnp.arange(t)[None, :]).astype(BF16)
    vt = jnp.transpose(dvv[:, :, :HEAD_DIM].reshape(b, nt, t, HEAD_DIM), (0, 1, 3, 2))
    blk = lambda w: pl.BlockSpec((1, t, w), lambda bi, i: (bi, i, 0))
    seq = lambda w: pl.BlockSpec((1, s, w), lambda bi, i: (bi, 0, 0))
    return pl.pallas_call(
        functools.partial(_dsa_kernel, t=t, top_k=top_k),
        grid=(b, nt),
        in_specs=[blk(256), seq(LANES),
                  pl.BlockSpec((1, nt, HEAD_DIM, t), lambda bi, i: (bi, 0, 0, 0)),
                  blk(256), seq(LANES),
                  pl.BlockSpec((1, 8, t), lambda bi, i: (bi, 0, i)),
                  pl.BlockSpec(bias.shape, lambda bi, i: (0, 0, 0, 0)),
                  pl.BlockSpec((t, t), lambda bi, i: (0, 0))],
        out_specs=blk(256),
        out_shape=jax.ShapeDtypeStruct((b, s, 256), BF16),
        scratch_shapes=[pltpu.VMEM((nt, t, t), I32), pltpu.VMEM((32, nt, 8, t), I32)],
        compiler_params=_cparams("arbitrary", "arbitrary"),
        name="dsa",
    )(dq, dkk, vt, iq, ikk, wt, bias, lstrict)


def _merge_kernel(x_ref, gm_ref, wg_ref, oa_ref, of_ref, os_ref, od_ref, wb_ref, wo_ref, gf_ref,
                  wr_ref, br_ref, xo_ref, h2_ref, route_ref):
    x = x_ref[...]
    hb = _rms(x, gm_ref[...]).astype(BF16)
    d = x.shape[1]
    merged = None
    for bi, o_ref in enumerate((oa_ref, of_ref, os_ref, od_ref)):
        gate = jax.nn.sigmoid(jnp.dot(hb, wg_ref[:, bi * d:(bi + 1) * d], preferred_element_type=F32))
        term = gate * jnp.dot(o_ref[...], wb_ref[bi], preferred_element_type=F32)
        merged = term if merged is None else merged + term
    xn = x + jnp.dot(merged.astype(BF16), wo_ref[...], preferred_element_type=F32)
    xo_ref[...] = xn
    h2 = _rms(xn, gf_ref[...])
    h2_ref[...] = h2

    logits = _dot_x3(h2, wr_ref[0], wr_ref[1]) + br_ref[...]
    lane = lax.broadcasted_iota(I32, logits.shape, 1).astype(F32)
    big = 1e9
    gl = jnp.where(lane < N_GROUPS, logits, -jnp.inf)
    gmax = jnp.max(gl, axis=1, keepdims=True)
    grp = jnp.min(jnp.where(gl == gmax, lane, big), axis=1, keepdims=True)
    p_grp = 1.0 / jnp.sum(jnp.exp(gl - gmax), axis=1, keepdims=True)
    first = N_GROUPS + grp * EXPERTS_PER_GROUP
    el = jnp.where((lane >= first) & (lane < first + EXPERTS_PER_GROUP), logits, -jnp.inf)
    l1 = jnp.max(el, axis=1, keepdims=True)
    i1 = jnp.min(jnp.where(el == l1, lane, big), axis=1, keepdims=True)
    el2 = jnp.where(lane == i1, -jnp.inf, el)
    l2 = jnp.max(el2, axis=1, keepdims=True)
    i2 = jnp.min(jnp.where(el2 == l2, lane, big), axis=1, keepdims=True)
    e2 = jnp.exp(l2 - l1)
    g1 = p_grp / (1.0 + e2)
    g2 = p_grp * e2 / (1.0 + e2)
    route = jnp.where(lane == 0, i1 - N_GROUPS,
                      jnp.where(lane == 1, i2 - N_GROUPS,
                                jnp.where(lane == 2, g1, jnp.where(lane == 3, g2, 0.0))))
    route_ref[...] = route


def _merge(x2, gm, wg, o_a, o_f, o_s, o_d, wb, wo, gf, wr, br, tm):
    n, d = x2.shape
    row = lambda w: pl.BlockSpec((tm, w), lambda i: (i, 0))
    full = lambda a: pl.BlockSpec(a.shape, lambda i: (0,) * a.ndim, pipeline_mode=pl.Buffered(1))
    return pl.pallas_call(
        _merge_kernel,
        grid=(n // tm,),
        in_specs=[row(d), full(gm), full(wg), row(256), row(256), row(256), row(256),
                  full(wb), full(wo), full(gf), full(wr), full(br)],
        out_specs=[row(d), row(d), row(LANES)],
        out_shape=[jax.ShapeDtypeStruct((n, d), F32), jax.ShapeDtypeStruct((n, d), F32),
                   jax.ShapeDtypeStruct((n, LANES), F32)],
        compiler_params=_cparams("arbitrary"),
        name="merge",
    )(x2, gm, wg, o_a, o_f, o_s, o_d, wb, wo, gf, wr, br)


def _rank_kernel(route_ref, ltri_ref, rank_ref, cnt_ref, carry_ref):
    @pl.when(pl.program_id(0) == 0)
    def _():
        carry_ref[...] = jnp.zeros_like(carry_ref)

    route = route_ref[...]
    lane = lax.broadcasted_iota(I32, route.shape, 1)
    e0 = route[:, 0:1].astype(I32)
    e1 = route[:, 1:2].astype(I32)
    oh0 = (lane == e0).astype(F32)
    oh1 = (lane == e1).astype(F32)
    both = oh0 + oh1
    before = jnp.dot(ltri_ref[...], both.astype(BF16), preferred_element_type=F32) + carry_ref[0:1, :]
    r0 = jnp.sum(oh0 * before, axis=1, keepdims=True)
    r1 = jnp.sum(oh1 * (before + oh0), axis=1, keepdims=True)
    rank_ref[...] = jnp.where(lane == 0, r0, jnp.where(lane == 1, r1, 0.0))
    total = carry_ref[0:1, :] + jnp.sum(both, axis=0, keepdims=True)
    carry_ref[0:1, :] = total
    cnt_ref[...] = jnp.broadcast_to(total, cnt_ref.shape)


def _rank(route, tm):
    n = route.shape[0]
    ltri = (jnp.arange(tm)[:, None] > jnp.arange(tm)[None, :]).astype(BF16)
    return pl.pallas_call(
        _rank_kernel,
        grid=(n // tm,),
        in_specs=[pl.BlockSpec((tm, LANES), lambda i: (i, 0)), pl.BlockSpec((tm, tm), lambda i: (0, 0))],
        out_specs=[pl.BlockSpec((tm, LANES), lambda i: (i, 0)), pl.BlockSpec((8, LANES), lambda i: (0, 0))],
        out_shape=[jax.ShapeDtypeStruct((n, LANES), F32), jax.ShapeDtypeStruct((8, LANES), F32)],
        scratch_shapes=[pltpu.VMEM((8, LANES), F32)],
        compiler_params=_cparams("arbitrary"),
        name="moe_rank",
    )(route, ltri)


def _expert_kernel(be_ref, nu_ref, tok_ref, tok1_ref, tok2_ref, h_hbm, wup_ref, wdn_ref, y_ref,
                   xbuf, sem, wup_b, wdn_b, *, te):
    b = pl.program_id(0)
    n_used = nu_ref[0]
    slot = b % 3

    def start_row(tokens_ref, dst, r):
        pltpu.make_async_copy(h_hbm.at[pl.ds(tokens_ref[0, 0, r], 1), :],
                              xbuf.at[dst, pl.ds(r, 1), :], sem.at[dst]).start()

    def start_block(tokens_ref, dst):
        def issue(r, c):
            start_row(tokens_ref, dst, r)
            return c
        lax.fori_loop(0, te, issue, 0, unroll=8)

    def block(prefetch):
        pltpu.make_async_copy(h_hbm.at[pl.ds(0, te), :], xbuf.at[slot], sem.at[slot]).wait()
        xb = xbuf[slot].astype(BF16)
        if prefetch:
            dst = (b + 2) % 3
            for r in range(te):
                start_row(tok2_ref, dst, r)
        gu = jnp.dot(xb, wup_b[...], preferred_element_type=F32)
        g = gu[:, :EXPERT_FF]
        act = g * jax.nn.sigmoid(g) * gu[:, EXPERT_FF:]
        y_ref[...] = jnp.dot(act.astype(BF16), wdn_b[...], preferred_element_type=F32)

    @pl.when((b < n_used) & ((b == 0) | (be_ref[b] != be_ref[jnp.maximum(b - 1, 0)])))
    def _():
        wup_b[...] = wup_ref[0, 0].astype(BF16)
        wdn_b[...] = wdn_ref[0, 0].astype(BF16)

    @pl.when((b == 0) & (n_used > 0))
    def _():
        start_block(tok_ref, 0)

    @pl.when((b == 0) & (n_used > 1))
    def _():
        start_block(tok1_ref, 1)

    @pl.when(b + 2 < n_used)
    def _():
        block(True)

    @pl.when((b < n_used) & (b + 2 >= n_used))
    def _():
        block(False)

    @pl.when(b >= n_used)
    def _():
        y_ref[...] = jnp.zeros_like(y_ref)


def _experts(blk_expert, n_used, slot_tok, h2, w_up, w_down, layer, te):
    n_blocks = blk_expert.shape[0]
    d = h2.shape[1]
    ahead = lambda k: pl.BlockSpec((1, 1, te), lambda b, be, nu: (jnp.minimum(b + k, n_blocks - 1), 0, 0),
                                   memory_space=pltpu.SMEM)
    grid_spec = pltpu.PrefetchScalarGridSpec(
        num_scalar_prefetch=2,
        grid=(n_blocks,),
        in_specs=[ahead(0), ahead(1), ahead(2),
                  pl.BlockSpec(memory_space=pl.ANY),
                  pl.BlockSpec((1, 1, d, 2 * EXPERT_FF), lambda b, be, nu: (layer, be[b], 0, 0)),
                  pl.BlockSpec((1, 1, EXPERT_FF, d), lambda b, be, nu: (layer, be[b], 0, 0))],
        out_specs=pl.BlockSpec((te, d), lambda b, be, nu: (b, 0)),
        scratch_shapes=[pltpu.VMEM((3, te, d), F32), pltpu.SemaphoreType.DMA((3,)),
                        pltpu.VMEM((d, 2 * EXPERT_FF), BF16), pltpu.VMEM((EXPERT_FF, d), BF16)],
    )
    slots = slot_tok.reshape(n_blocks, 1, te)
    return pl.pallas_call(
        functools.partial(_expert_kernel, te=te),
        grid_spec=grid_spec,
        out_shape=jax.ShapeDtypeStruct((n_blocks * te, d), F32),
        compiler_params=_cparams("arbitrary"),
        name="moe_experts",
    )(blk_expert, n_used, slots, slots, slots, h2, w_up, w_down)


def _combine_kernel(pos_ref, pos_next_ref, x_ref, route_ref, y_hbm, o_ref, ybuf, sem, *, tc):
    i = pl.program_id(0)
    slot = i % 2

    def gather(rows_ref, dst):
        for r in range(2 * tc):
            pltpu.make_async_copy(y_hbm.at[pl.ds(rows_ref[0, 0, r], 1), :],
                                  ybuf.at[dst, pl.ds(r, 1), :], sem.at[dst]).start()

    @pl.when(i == 0)
    def _():
        gather(pos_ref, 0)

    @pl.when(i + 1 < pl.num_programs(0))
    def _():
        gather(pos_next_ref, 1 - slot)

    pltpu.make_async_copy(y_hbm.at[pl.ds(0, 2 * tc), :], ybuf.at[slot], sem.at[slot]).wait()
    route = route_ref[...]
    o_ref[...] = (x_ref[...] + route[:, 2:3] * ybuf[slot, 0:tc, :]
                  + route[:, 3:4] * ybuf[slot, tc:2 * tc, :])


def _combine(pos, x2, route, yb, tc):
    n, d = x2.shape
    nt = n // tc
    pos_t = jnp.transpose(pos.reshape(nt, tc, 2), (0, 2, 1)).reshape(nt, 1, 2 * tc)
    return pl.pallas_call(
        functools.partial(_combine_kernel, tc=tc),
        grid=(nt,),
        in_specs=[pl.BlockSpec((1, 1, 2 * tc), lambda i: (i, 0, 0), memory_space=pltpu.SMEM),
                  pl.BlockSpec((1, 1, 2 * tc), lambda i: (jnp.minimum(i + 1, nt - 1), 0, 0),
                               memory_space=pltpu.SMEM),
                  pl.BlockSpec((tc, d), lambda i: (i, 0)),
                  pl.BlockSpec((tc, LANES), lambda i: (i, 0)),
                  pl.BlockSpec(memory_space=pl.ANY)],
        out_specs=pl.BlockSpec((tc, d), lambda i: (i, 0)),
        out_shape=jax.ShapeDtypeStruct((n, d), F32),
        scratch_shapes=[pltpu.VMEM((2, 2 * tc, d), F32), pltpu.SemaphoreType.DMA((2,))],
        compiler_params=_cparams("arbitrary"),
        name="moe_combine",
    )(pos_t, pos_t, x2, route, yb)


def _swap_mid_heads(w, axis):
    h = jnp.split(w, 4, axis=axis)
    return jnp.concatenate([h[0], h[2], h[1], h[3]], axis=axis)


def _t5_lookup(t5_table, dist):
    onehot = (_t5_bucket(dist)[..., None] == jnp.arange(T5_BUCKETS)).astype(F32)
    return jnp.einsum("...b,bh->...h", onehot, t5_table.astype(F32), precision=lax.Precision.HIGHEST)


def _t5_bucket(dist):
    n = jnp.maximum(dist, 0)
    max_exact = T5_BUCKETS // 2
    nf = jnp.maximum(n, 1).astype(F32)
    large = max_exact + (jnp.log(nf / max_exact) / math.log(T5_MAX_DISTANCE / max_exact)
                         * (T5_BUCKETS - max_exact)).astype(I32)
    large = jnp.minimum(large, T5_BUCKETS - 1)
    return jnp.where(n < max_exact, n, large)


def _swa_bias(t5_table):
    t = SWA_BLOCK
    dist = t + jnp.arange(t)[None, :] - jnp.arange(2 * t)[:, None]
    tile = jnp.where(((dist >= 0) & (dist < t))[..., None], _t5_lookup(t5_table, dist)[..., :4], NEG_INF)
    return jnp.concatenate([tile[..., h] for h in (0, 2, 1, 3)], axis=1)


def _layer_weights(w_in, qk_gain, forget_bias, w_branch):
    offs = np.concatenate([[0], np.cumsum(IN_SPLITS)]).tolist()
    part = lambda k: w_in[:, offs[k]:offs[k + 1]]
    dup = lambda w: jnp.concatenate([w, w], axis=1)
    aq = _swap_mid_heads(part(0), 1)
    cols = [aq, part(1), part(2), part(3), part(4), part(5), part(7), part(8), part(9),
            part(10), dup(part(11)), dup(part(12)), part(13), dup(part(14))]
    w1 = jnp.concatenate(cols, axis=1).astype(BF16)
    d = w_in.shape[0]
    wm = _hi_lo(jnp.concatenate([part(6), part(15), jnp.zeros((d, LANES - 8), F32)], axis=1))
    wg = part(16).astype(BF16)
    tile = lambda g, reps, scale: jnp.pad(jnp.tile(g, reps) * scale, (0, 256 - reps * HEAD_DIM))
    gains = jnp.stack([tile(qk_gain[0, 0], 4, ATTN_SCALE), tile(qk_gain[0, 1], 2, 1.0),
                       tile(qk_gain[1, 0], 4, ATTN_SCALE * LOG2E), tile(qk_gain[1, 1], 4, 1.0),
                       tile(qk_gain[2, 0], 4, ATTN_SCALE * LOG2E), tile(qk_gain[2, 1], 2, 1.0),
                       jnp.zeros((256,), F32), jnp.zeros((256,), F32)]).astype(F32)
    fb = jnp.pad(forget_bias.astype(F32), (0, LANES - 4)).reshape(1, LANES)
    wb = jnp.stack([_swap_mid_heads(w_branch[0], 0), w_branch[1], w_branch[2], w_branch[3]]).astype(BF16)
    return w1, wm, wg, gains, fb, wb


def kernel(x, norm_mix_g, w_in, forget_bias, attn_sinks, qk_gain, w_branch, w_out, t5_table, norm_ffn_g,
           w_router_group, b_router_group, w_router_expert, b_router_expert, w_expert_up, w_expert_down):
    b, s, d = x.shape
    n = b * s
    depth = w_in.shape[0]
    top_k = min(DSA_TOPK_MAX, s // 4)
    tm_proj = min(512, s)
    fox_t = min(256, s)
    sb_t = min(256, s)
    dsa_t = min(256, s)
    te = 256
    tc = 256

    gseg = (jnp.arange(256)[:, None] // HEAD_DIM == jnp.arange(256)[None, :] // HEAD_DIM).astype(BF16)
    bias_swa = _swa_bias(t5_table)
    bias_dsa = _dsa_bias(t5_table, dsa_t) * LOG2E
    n_blocks = -(-2 * n // te) + N_EXPERTS
    tok_ids = jnp.repeat(jnp.arange(n, dtype=I32), 2)

    for layer in range(depth):
        w1, wm, wg, gains, fb, wb = _layer_weights(w_in[layer], qk_gain[layer], forget_bias[layer],
                                                   w_branch[layer])
        sink_row = jnp.repeat(attn_sinks[layer].astype(F32)[jnp.array([0, 2, 1, 3])], SWA_BLOCK)
        sinks = jnp.broadcast_to(sink_row[None, :], (8, 4 * SWA_BLOCK))
        (aq, ak, av, fq, fk, fv, sq, sk, sv, dq, dkk, dvv, iq, ikk, cm) = _proj(
            x, norm_mix_g[layer].reshape(1, d), w1, wm, gseg, gains, fb, tm_proj)

        o_swa = _swa(aq, ak, av, bias_swa, sinks, min(4, s // SWA_BLOCK))
        cmt = jnp.transpose(cm[:, :, :8], (0, 2, 1))
        o_fox = _fox(fq, fk, fv, cmt, cm, fox_t)
        o_sb = _sb(sq, sk, sv, sb_t)
        o_dsa = _dsa(dq, dkk, dvv, iq, ikk, cmt, bias_dsa, top_k, dsa_t)

        wr = _hi_lo(jnp.concatenate([w_router_group[layer], w_router_expert[layer],
                                     jnp.zeros((d, LANES - N_GROUPS - N_EXPERTS), F32)], axis=1))
        br = jnp.concatenate([b_router_group[layer], b_router_expert[layer],
                              jnp.zeros((LANES - N_GROUPS - N_EXPERTS,), F32)]).reshape(1, LANES)
        x2, h2, route = _merge(
            x.reshape(n, d), norm_mix_g[layer].reshape(1, d), wg,
            o_swa.reshape(n, 256), o_fox.reshape(n, 256), o_sb.reshape(n, 256), o_dsa.reshape(n, 256),
            wb, w_out[layer].astype(BF16), norm_ffn_g[layer].reshape(1, d), wr, br, min(512, n))

        rank, cnt = _rank(route, min(512, n))
        counts = cnt[0, :N_EXPERTS].astype(I32)
        padded = (counts + te - 1) // te * te
        pend = jnp.cumsum(padded)
        pstart = pend - padded
        expert = route[:, :2].astype(I32)
        own = expert[:, :, None] == jnp.arange(N_EXPERTS, dtype=I32)
        pos = jnp.sum(jnp.where(own, pstart, 0), axis=-1) + rank[:, :2].astype(I32)
        slot_tok = jnp.zeros((n_blocks * te,), I32).at[pos.reshape(-1)].set(
            tok_ids, unique_indices=True, mode="promise_in_bounds")
        blk_start = jnp.arange(n_blocks, dtype=I32)[:, None] * te
        blk_expert = jnp.minimum(jnp.sum((pend[None, :] <= blk_start).astype(I32), axis=1), N_EXPERTS - 1)
        n_used = (pend[-1:] // te).astype(I32)

        yb = _experts(blk_expert, n_used, slot_tok, h2, w_expert_up, w_expert_down, layer, te)
        x = _combine(pos, x2, route, yb, tc).reshape(b, s, d)
    return x
```

```python
import functools
import math

import jax
import jax.numpy as jnp
import numpy as np
from jax import lax
from jax.experimental import pallas as pl
from jax.experimental.pallas import tpu as pltpu

F32 = jnp.float32
BF16 = jnp.bfloat16
I32 = jnp.int32

HEAD_DIM = 64
LANES = 128
NORM_EPS = 1e-6
NEG_INF = -1e30
M_INIT = -1e29
ATTN_SCALE = HEAD_DIM ** -0.5
LOG2E = math.log2(math.e)
SWA_BLOCK = 128
IDX_SCALE = 64 ** -0.5
IDX_HEADS = 4
DSA_TOPK_MAX = 256
T5_BUCKETS = 32
T5_MAX_DISTANCE = 128
N_GROUPS = 4
EXPERTS_PER_GROUP = 8
N_EXPERTS = N_GROUPS * EXPERTS_PER_GROUP
EXPERT_FF = 512
SB_DEAD = -110.0
INT_MIN = -2 ** 31
ONES_ROWS = 16
VMEM_LIMIT = 56 * 1024 * 1024

IN_SPLITS = (256, 128, 128, 256, 256, 256, 4, 256, 256, 256, 256, 64, 64, 256, 64, 4, 4096)

_SEG = dict(aq=(0, 256), ak=(256, 128), av=(384, 128), fq=(512, 256), fk=(768, 256), fv=(1024, 256),
            sq=(1280, 256), sk=(1536, 256), sv=(1792, 256), dq=(2048, 256), dkk=(2304, 128),
            dvv=(2432, 128), iq=(2560, 256), ikk=(2816, 128))
_SEG_ORDER = ("aq", "ak", "av", "fq", "fk", "fv", "sq", "sk", "sv", "dq", "dkk", "dvv", "iq", "ikk")


def _cparams(*sem):
    return pltpu.CompilerParams(dimension_semantics=sem, vmem_limit_bytes=VMEM_LIMIT)


def _rms(x, g):
    return x * lax.rsqrt(jnp.mean(x * x, axis=-1, keepdims=True) + NORM_EPS) * g


def _log_sigmoid(z):
    return jnp.minimum(z, 0.0) - jnp.log(1.0 + jnp.exp(-jnp.abs(z)))


def _dot_nt(a, b):
    return lax.dot_general(a, b, (((1,), (1,)), ((), ())), preferred_element_type=F32)


def _with_ones_rows(vt):
    ones = jnp.ones(vt.shape[:-2] + (ONES_ROWS, vt.shape[-1]), vt.dtype)
    return jnp.concatenate([vt, ones], axis=-2)


def _split3(x):
    p1 = x.astype(BF16)
    r = x - p1.astype(F32)
    p2 = r.astype(BF16)
    return p1, p2, (r - p2.astype(F32)).astype(BF16)


def _hi_lo(w):
    hi = w.astype(BF16)
    return jnp.stack([hi, (w - hi.astype(F32)).astype(BF16)])


def _dot_x3(a, b_hi, b_lo):
    a_hi = a.astype(BF16)
    a_lo = (a - a_hi.astype(F32)).astype(BF16)
    return (jnp.dot(a_hi, b_hi, preferred_element_type=F32) + jnp.dot(a_lo, b_hi, preferred_element_type=F32)
            + jnp.dot(a_hi, b_lo, preferred_element_type=F32))


def _split_heads(qp):
    lo = lax.broadcasted_iota(I32, (1, LANES), 1) < HEAD_DIM
    zero = jnp.zeros_like(qp)
    return jnp.concatenate([jnp.where(lo, qp, zero), jnp.where(lo, zero, qp)], axis=0)


def _proj_kernel(x_ref, g_ref, w1_ref, wm_ref, gseg_ref, gains_ref, fb_ref, ltri_ref, *rest):
    outs = dict(zip(_SEG_ORDER, rest[:len(_SEG_ORDER)]))
    cm_ref = rest[len(_SEG_ORDER)]
    carry_ref = rest[len(_SEG_ORDER) + 1]

    @pl.when(pl.program_id(1) == 0)
    def _():
        carry_ref[...] = jnp.zeros_like(carry_ref)

    h = _rms(x_ref[0], g_ref[...])
    hb = h.astype(BF16)

    def seg(name):
        off, width = _SEG[name]
        return jnp.dot(hb, w1_ref[:, off:off + width], preferred_element_type=F32)

    def head_norm(t, row):
        width = t.shape[1]
        ssq = jnp.dot((t * t).astype(BF16), gseg_ref[:width, :width], preferred_element_type=F32)
        return t * lax.rsqrt(ssq * (1.0 / HEAD_DIM) + NORM_EPS) * gains_ref[row:row + 1, :width]

    normed = dict(aq=0, ak=1, fq=2, fk=3, dq=4, dkk=5)
    scaled = dict(sq=ATTN_SCALE * LOG2E, iq=IDX_SCALE)
    for name in _SEG_ORDER:
        t = seg(name)
        if name in normed:
            t = head_norm(t, normed[name])
        elif name in scaled:
            t = t * scaled[name]
        outs[name][0] = t.astype(BF16)

    misc = _dot_x3(h, wm_ref[0], wm_ref[1])
    lane = lax.broadcasted_iota(I32, misc.shape, 1)
    logf = jnp.where(lane < 4, _log_sigmoid(misc + fb_ref[...]), 0.0)
    ltri = ltri_ref[...]
    c = carry_ref[0:1, :]
    for piece in _split3(logf):
        c = c + jnp.dot(ltri, piece, preferred_element_type=F32)
    tm = misc.shape[0]
    carry_ref[0:1, :] = c[tm - 1:tm, :]
    cm_ref[0] = jnp.where(lane < 4, c, misc)


def _proj(x, g, w1, wm, gseg, gains, fb, tm):
    b, s, d = x.shape
    ltri = jnp.tril(jnp.ones((tm, tm), BF16))
    full = lambda shape: pl.BlockSpec(shape, lambda bi, si: (0,) * len(shape))
    out_shapes = [jax.ShapeDtypeStruct((b, s, _SEG[n][1]), BF16) for n in _SEG_ORDER]
    out_shapes.append(jax.ShapeDtypeStruct((b, s, LANES), F32))
    out_specs = [pl.BlockSpec((1, tm, _SEG[n][1]), lambda bi, si: (bi, si, 0)) for n in _SEG_ORDER]
    out_specs.append(pl.BlockSpec((1, tm, LANES), lambda bi, si: (bi, si, 0)))
    return pl.pallas_call(
        _proj_kernel,
        grid=(b, s // tm),
        in_specs=[pl.BlockSpec((1, tm, d), lambda bi, si: (bi, si, 0)),
                  full((1, d)), full(w1.shape), full(wm.shape), full(gseg.shape),
                  full(gains.shape), full(fb.shape), full((tm, tm))],
        out_specs=out_specs,
        out_shape=out_shapes,
        scratch_shapes=[pltpu.VMEM((8, LANES), F32)],
        compiler_params=_cparams("arbitrary", "arbitrary"),
        name="proj",
    )(x, g, w1, wm, gseg, gains, fb, ltri)


def _swa_kernel(q_ref, kp_ref, kc_ref, vtp_ref, vtc_ref, bias_ref, sink_ref, o_ref, *, nsub):
    i = pl.program_id(1)
    t = SWA_BLOCK
    no_prev = (lax.broadcasted_iota(I32, (2 * t, 4 * t), 0) < t) & (i == 0)
    sink = sink_ref[0:1, :]
    for u in range(nsub):
        rows = slice(u * t, (u + 1) * t)
        q = q_ref[0, rows, :]
        qs = jnp.concatenate([_split_heads(q[:, :LANES]), _split_heads(q[:, LANES:])], axis=0)
        if u == 0:
            kcat = jnp.concatenate([kp_ref[0], kc_ref[0, rows, :]], axis=0)
            vt = jnp.concatenate([vtp_ref[0], vtc_ref[0, :, rows]], axis=1)
        else:
            kcat = kc_ref[0, (u - 1) * t:(u + 1) * t, :]
            vt = vtc_ref[0, :, (u - 1) * t:(u + 1) * t]
        st = _dot_nt(kcat, qs) + bias_ref[...]
        if u == 0:
            st = jnp.where(no_prev, NEG_INF, st)
        m = jnp.maximum(jnp.max(st, axis=0, keepdims=True), sink)
        p = jnp.exp(st - m)
        denom = jnp.sum(p, axis=0, keepdims=True) + jnp.exp(sink - m)
        ot = jnp.dot(vt, p.astype(BF16), preferred_element_type=F32) / denom
        ot = jnp.concatenate([ot[:HEAD_DIM, 0:t], ot[HEAD_DIM:, t:2 * t],
                              ot[:HEAD_DIM, 2 * t:3 * t], ot[HEAD_DIM:, 3 * t:]], axis=0)
        o_ref[0, rows, :] = ot.T.astype(BF16)


def _swa(aq, ak, av, bias, sinks, nsub):
    b, s, _ = aq.shape
    t = SWA_BLOCK
    avt = jnp.transpose(av, (0, 2, 1))
    cur = lambda bi, i: (bi, i, 0)
    prev = lambda bi, i: (bi, jnp.maximum(i * nsub - 1, 0), 0)
    return pl.pallas_call(
        functools.partial(_swa_kernel, nsub=nsub),
        grid=(b, s // (t * nsub)),
        in_specs=[pl.BlockSpec((1, t * nsub, 256), cur),
                  pl.BlockSpec((1, t, LANES), prev), pl.BlockSpec((1, t * nsub, LANES), cur),
                  pl.BlockSpec((1, LANES, t), lambda bi, i: (bi, 0, jnp.maximum(i * nsub - 1, 0))),
                  pl.BlockSpec((1, LANES, t * nsub), lambda bi, i: (bi, 0, i)),
                  pl.BlockSpec(bias.shape, lambda bi, i: (0, 0)),
                  pl.BlockSpec(sinks.shape, lambda bi, i: (0, 0))],
        out_specs=pl.BlockSpec((1, t * nsub, 256), cur),
        out_shape=jax.ShapeDtypeStruct((b, s, 256), BF16),
        compiler_params=_cparams("arbitrary", "arbitrary"),
        name="swa",
    )(aq, ak, ak, avt, avt, bias, sinks)


def _fox_kernel(q_ref, k_ref, vt_ref, ct_ref, ccol_ref, o_ref, ckb_ref, *, t):
    i = pl.program_id(1)
    n_tiles = ckb_ref.shape[1] // t

    @pl.when(i == 0)
    def _():
        def fill(j, c):
            rows = pl.ds(pl.multiple_of(j * t, t), t)
            cc = ccol_ref[0, rows, :] * LOG2E
            for h in range(4):
                ckb_ref[h, rows, :] = jnp.broadcast_to(cc[:, h:h + 1], (t, LANES))
            return c
        lax.fori_loop(0, n_tiles, fill, 0)

    q = q_ref[0]
    ct = ct_ref[0] * LOG2E
    qs = [_split_heads(q[:, :LANES]), _split_heads(q[:, LANES:])]
    valid = lax.broadcasted_iota(I32, (t, t), 0) <= lax.broadcasted_iota(I32, (t, t), 1)

    def tiles(j0, n, carry, masked):
        m, accs = carry
        rows = pl.ds(pl.multiple_of(j0 * t, t), n * t)
        cols = []
        for pair in range(2):
            st = _dot_nt(k_ref[0, rows, pair * LANES:(pair + 1) * LANES], qs[pair])
            for hh in range(2):
                head = 2 * pair + hh
                ck = ckb_ref[head, rows, :]
                for c in range(t // LANES):
                    cs = slice(c * LANES, (c + 1) * LANES)
                    sh = st[:, hh * t + c * LANES:hh * t + (c + 1) * LANES] + (ct[head:head + 1, cs] - ck)
                    if masked:
                        last = jnp.where(valid[:, cs], sh[(n - 1) * t:], NEG_INF)
                        sh = jnp.concatenate([sh[:(n - 1) * t], last], axis=0) if n > 1 else last
                    cols.append(sh)
        st = jnp.concatenate(cols, axis=1)
        m_new = jnp.maximum(m, jnp.max(st, axis=0, keepdims=True))
        alpha = jnp.exp2(m - m_new)
        pb = jnp.exp2(st - m_new).astype(BF16)
        new = []
        for pair in range(2):
            lanes = slice(pair * 2 * t, (pair + 1) * 2 * t)
            vt = jnp.concatenate([vt_ref[0, j0 + u, pair] for u in range(n)], axis=1)
            pv = jnp.dot(vt, pb[:, lanes], preferred_element_type=F32)
            new.append(alpha[:, lanes] * accs[pair] + pv)
        return m_new, tuple(new)

    zero_acc = jnp.zeros((LANES + ONES_ROWS, 2 * t), F32)
    carry = lax.fori_loop(0, i // 4, lambda g, c: tiles(4 * g, 4, c, False),
                          (jnp.full((1, 4 * t), M_INIT, F32), (zero_acc, zero_acc)))
    rest = i - i % 4
    _, accs = lax.switch(i % 4, [functools.partial(lambda c, n: tiles(rest, n, c, True), n=r + 1)
                                 for r in range(4)], carry)
    outs = []
    for pair in range(2):
        o = accs[pair][:LANES] / accs[pair][LANES:LANES + 1]
        outs.append(jnp.concatenate([o[:HEAD_DIM, :t], o[HEAD_DIM:, t:]], axis=0))
    o_ref[0] = jnp.concatenate(outs, axis=0).T.astype(BF16)


def _fox(fq, fk, fv, ct, ccol, t):
    b, s, _ = fq.shape
    nt = s // t
    vt = _with_ones_rows(jnp.transpose(fv.reshape(b, nt, t, 2, LANES), (0, 1, 3, 4, 2)))
    return pl.pallas_call(
        functools.partial(_fox_kernel, t=t),
        grid=(b, nt),
        in_specs=[pl.BlockSpec((1, t, 256), lambda bi, i: (bi, i, 0)),
                  pl.BlockSpec((1, s, 256), lambda bi, i: (bi, 0, 0)),
                  pl.BlockSpec((1, nt, 2, LANES + ONES_ROWS, t), lambda bi, i: (bi, 0, 0, 0, 0)),
                  pl.BlockSpec((1, 8, t), lambda bi, i: (bi, 0, i)),
                  pl.BlockSpec((1, s, LANES), lambda bi, i: (bi, 0, 0))],
        out_specs=pl.BlockSpec((1, t, 256), lambda bi, i: (bi, i, 0)),
        out_shape=jax.ShapeDtypeStruct((b, s, 256), BF16),
        scratch_shapes=[pltpu.VMEM((4, s, LANES), F32)],
        compiler_params=_cparams("arbitrary", "arbitrary"),
        name="fox",
    )(fq, fk, vt, ct, ccol)


def _sb_kernel(q_ref, k_ref, vt_ref, lgt_ref, o_ref, *, t):
    i = pl.program_id(1)
    q = q_ref[0]
    lgt = lgt_ref[...]
    qs = [_split_heads(q[:, :LANES]), _split_heads(q[:, LANES:])]
    key_i = lax.broadcasted_iota(I32, (t, 4 * t), 0)
    query_i = lax.broadcasted_iota(I32, (t, 4 * t), 1) & (t - 1)
    strict = key_i < query_i

    def tile(j, r, accs, masked):
        rows = pl.ds(pl.multiple_of(j * t, t), t)
        z = jnp.concatenate([_dot_nt(k_ref[0, rows, p * LANES:(p + 1) * LANES], qs[p]) for p in range(2)],
                            axis=1)
        sp = jnp.log2(1.0 + jnp.exp2(-jnp.abs(z)))
        log_beta = jnp.minimum(z, 0.0) - sp
        log_keep = jnp.minimum(-z, 0.0) - sp
        if masked:
            log_keep = jnp.where(strict, log_keep, 0.0)
        hi = log_keep.astype(BF16)
        lo = (log_keep - hi.astype(F32)).astype(BF16)
        later = (jnp.dot(lgt, hi, preferred_element_type=F32)
                 + jnp.dot(lgt, lo, preferred_element_type=F32))
        a = jnp.exp2(log_beta + later + r)
        if masked:
            a = jnp.where(strict, a, 0.0)
        ab = a.astype(BF16)
        new = tuple(accs[p] + jnp.dot(vt_ref[0, j, p * LANES:(p + 1) * LANES, :],
                                      ab[:, p * 2 * t:(p + 1) * 2 * t], preferred_element_type=F32)
                    for p in range(2))
        return r + jnp.sum(log_keep, axis=0, keepdims=True), new

    zero_acc = jnp.zeros((LANES, 2 * t), F32)
    r, accs = tile(i, jnp.zeros((1, 4 * t), F32), (zero_acc, zero_acc), True)

    def cond(c):
        return (c[0] >= 0) & (c[1] > 0)

    def body(c):
        j, _, r, accs = c
        r, accs = tile(j, r, accs, False)
        return j - 1, (jnp.max(r) > SB_DEAD * LOG2E).astype(I32), r, accs

    _, _, _, accs = lax.while_loop(cond, body, (i - 1, (jnp.max(r) > SB_DEAD * LOG2E).astype(I32), r, accs))
    outs = [jnp.concatenate([accs[p][:HEAD_DIM, :t], accs[p][HEAD_DIM:, t:]], axis=0) for p in range(2)]
    o_ref[0] = jnp.concatenate(outs, axis=0).T.astype(BF16)


def _sb(sq, sk, sv, t):
    b, s, _ = sq.shape
    nt = s // t
    lgt = (jnp.arange(t)[:, None] < jnp.arange(t)[None, :]).astype(BF16)
    vt = jnp.transpose(sv.reshape(b, nt, t, 256), (0, 1, 3, 2))
    return pl.pallas_call(
        functools.partial(_sb_kernel, t=t),
        grid=(b, nt),
        in_specs=[pl.BlockSpec((1, t, 256), lambda bi, i: (bi, i, 0)),
                  pl.BlockSpec((1, s, 256), lambda bi, i: (bi, 0, 0)),
                  pl.BlockSpec((1, nt, 256, t), lambda bi, i: (bi, 0, 0, 0)),
                  pl.BlockSpec((t, t), lambda bi, i: (0, 0))],
        out_specs=pl.BlockSpec((1, t, 256), lambda bi, i: (bi, i, 0)),
        out_shape=jax.ShapeDtypeStruct((b, s, 256), BF16),
        compiler_params=_cparams("arbitrary", "arbitrary"),
        name="stickbreak",
    )(sq, sk, vt, lgt)


def _bit_planes(words):
    words = list(words)
    j, m = 16, 0x0000FFFF
    while j:
        k = 0
        while k < 32:
            tt = (words[k] ^ lax.shift_right_logical(words[k + j], jnp.full_like(words[k + j], j))) & m
            words[k] = words[k] ^ tt
            words[k + j] = words[k + j] ^ (tt << j)
            k = (k + j + 1) & ~j
        j >>= 1
        m = (m ^ (m << j)) & 0xFFFFFFFF
    return words


def _dsa_kernel(q_ref, kk_ref, vt_ref, iq_ref, ikk_ref, wt_ref, bias_ref, lstrict_ref, o_ref,
                key_ref, plane_ref, *, t, top_k):
    i = pl.program_id(1)
    assert t == 8 * 32

    @pl.when(i == 0)
    def _():
        plane_ref[...] = jnp.zeros_like(plane_ref)
    causal = lax.broadcasted_iota(I32, (t, t), 0) <= lax.broadcasted_iota(I32, (t, t), 1)

    def head_stack(x):
        return jnp.concatenate([_split_heads(x[:, :LANES]), _split_heads(x[:, LANES:])], axis=0)

    def key_rows(ref, j0, n):
        return ref[0, pl.ds(pl.multiple_of(j0 * t, t), n * t), :]

    iqs = head_stack(iq_ref[0])
    wt = wt_ref[0]
    w = [wt[4 + h:5 + h, :] * (IDX_HEADS ** -0.5) for h in range(IDX_HEADS)]

    def score_tiles(j0, n, masked):
        lg = _dot_nt(key_rows(ikk_ref, j0, n), iqs)
        sc = w[0] * jnp.maximum(lg[:, 0:t], 0.0)
        for h in range(1, IDX_HEADS):
            sc = sc + w[h] * jnp.maximum(lg[:, h * t:(h + 1) * t], 0.0)
        bits = pltpu.bitcast(sc, I32)
        key = bits ^ ((bits >> 31) & 0x7FFFFFFF)
        key = jnp.where(key == -1, 0, key)
        for u in range(n):
            key_u = key[u * t:(u + 1) * t]
            if masked and u == n - 1:
                key_u = jnp.where(causal, key_u, INT_MIN)
            key_ref[j0 + u] = key_u
            for p, plane in enumerate(_bit_planes([key_u[8 * g:8 * g + 8, :] for g in range(32)])):
                plane_ref[p, j0 + u] = plane

    def p1(g, c):
        score_tiles(4 * g, 4, False)
        return c

    lax.fori_loop(0, i // 4, p1, 0)
    lax.switch(i % 4, [functools.partial(score_tiles, i - i % 4, r + 1, True) for r in range(4)])

    def popcount_rows(words):
        per_tile = jnp.sum(lax.population_count(words), axis=0)
        return jnp.sum(per_tile.astype(F32), axis=0, keepdims=True)

    def bis_body(p, c):
        alive, n_gt, thr_u = c
        plane = plane_ref[p] ^ jnp.where(p == 0, -1, 0)
        ones = alive & plane
        cnt = popcount_rows(ones)
        take = n_gt + cnt >= top_k
        alive = jnp.where(take, ones, alive ^ ones)
        n_gt = jnp.where(take, n_gt, n_gt + cnt)
        thr_u = jnp.where(take, thr_u | (jnp.int32(1) << (31 - p)), thr_u)
        return alive, n_gt, thr_u

    n_tiles = key_ref.shape[0]
    alive0 = jnp.where(lax.broadcasted_iota(I32, (n_tiles, 8, t), 0) <= i, -1, 0)
    alive, n_gt, thr_u = lax.fori_loop(
        0, 32, bis_body, (alive0, jnp.zeros((1, t), F32), jnp.zeros((1, t), I32)))
    thr = jnp.maximum(thr_u ^ INT_MIN, INT_MIN + 1)
    n_avail = (i * t + lax.broadcasted_iota(I32, (1, t), 1) + 1).astype(F32)
    n_ge = jnp.where(n_avail > top_k, n_gt + popcount_rows(alive), 0.0)
    surplus = jnp.max(n_ge) > top_k

    def tie_pass():
        need = top_k - n_gt

        def tb(j, seen):
            k = key_ref[j]
            eq = k == thr
            eqf = jnp.where(eq, 1.0, 0.0)
            before = jnp.dot(lstrict_ref[...], eqf.astype(BF16), preferred_element_type=F32) + seen
            sel = (k > thr) | (eq & (before < need))
            key_ref[j] = jnp.where(sel, 1, INT_MIN)
            return seen + jnp.sum(eqf, axis=0, keepdims=True)

        lax.fori_loop(0, i + 1, tb, jnp.zeros((1, t), F32))
        return jnp.zeros((1, t), I32)

    thr = lax.cond(surplus, tie_pass, lambda: thr)

    qs = head_stack(q_ref[0])

    def attn_tiles(j0, n, carry, near):
        m, acc = carry
        st = _dot_nt(key_rows(kk_ref, j0, n), qs)
        selb = jnp.concatenate([jnp.where(key_ref[j0 + u] >= thr, 0.0, NEG_INF) for u in range(n)], axis=0)
        parts = []
        for h in range(4):
            sh = st[:, h * t:(h + 1) * t] + selb
            if near:
                biased = [sh[(n - near + v) * t:(n - near + v + 1) * t] + bias_ref[2 - near + v, h]
                          for v in range(near)]
                sh = jnp.concatenate(([sh[:(n - near) * t]] if n > near else []) + biased, axis=0)
            parts.append(sh)
        st = jnp.concatenate(parts, axis=1)
        m_new = jnp.maximum(m, jnp.max(st, axis=0, keepdims=True))
        alpha = jnp.exp2(m - m_new)
        pb = jnp.exp2(st - m_new).astype(BF16)
        vt = jnp.concatenate([vt_ref[0, j0 + u] for u in range(n)], axis=1)
        acc = alpha * acc + jnp.dot(vt, pb, preferred_element_type=F32)
        return m_new, acc

    init = (jnp.full((1, 4 * t), M_INIT, F32), jnp.zeros((HEAD_DIM + ONES_ROWS, 4 * t), F32))
    far = jnp.maximum(i - 1, 0)
    carry = lax.fori_loop(0, far // 4, lambda g, c: attn_tiles(4 * g, 4, c, 0), init)
    rest = far - far % 4
    last = [lambda c: attn_tiles(0, 1, c, 1)]
    last += [functools.partial(lambda c, n: attn_tiles(rest, n, c, 2), n=r + 2) for r in range(4)]
    _, acc = lax.switch(jnp.where(i == 0, 0, 1 + far % 4), last, carry)
    ot = acc[:HEAD_DIM] / acc[HEAD_DIM:HEAD_DIM + 1]
    ot = jnp.concatenate([ot[:, h * t:(h + 1) * t] for h in range(4)], axis=0)
    o_ref[0] = ot.T.astype(BF16)


def _dsa_bias(t5_table, t):
    assert t + 1 >= T5_MAX_DISTANCE
    k = jnp.arange(t)[:, None]
    q = jnp.arange(t)[None, :]
    far = t5_table[T5_BUCKETS - 1, 4:].astype(F32)
    tiles = []
    for off in (t, 0):
        dist = off + q - k
        b = jnp.transpose(_t5_lookup(t5_table, dist)[..., 4:], (2, 0, 1)) - far[:, None, None]
        tiles.append(jnp.where((dist >= 0)[None], b, 0.0))
    return jnp.stack(tiles)


def _dsa(dq, dkk, dvv, iq, ikk, wt, bias, top_k, t):
    b, s, _ = dq.shape
    nt = s // t
    lstrict = (jnp.arange(t)[:, None] > jnp.arange(t)[None, :]).astype(BF16)
    vt = _with_ones_rows(jnp.transpose(dvv[:, :, :HEAD_DIM].reshape(b, nt, t, HEAD_DIM), (0, 1, 3, 2)))
    blk = lambda w: pl.BlockSpec((1, t, w), lambda bi, i: (bi, i, 0))
    seq = lambda w: pl.BlockSpec((1, s, w), lambda bi, i: (bi, 0, 0))
    return pl.pallas_call(
        functools.partial(_dsa_kernel, t=t, top_k=top_k),
        grid=(b, nt),
        in_specs=[blk(256), seq(LANES),
                  pl.BlockSpec((1, nt, HEAD_DIM + ONES_ROWS, t), lambda bi, i: (bi, 0, 0, 0)),
                  blk(256), seq(LANES),
                  pl.BlockSpec((1, 8, t), lambda bi, i: (bi, 0, i)),
                  pl.BlockSpec(bias.shape, lambda bi, i: (0, 0, 0, 0)),
                  pl.BlockSpec((t, t), lambda bi, i: (0, 0))],
        out_specs=blk(256),
        out_shape=jax.ShapeDtypeStruct((b, s, 256), BF16),
        scratch_shapes=[pltpu.VMEM((nt, t, t), I32), pltpu.VMEM((32, nt, 8, t), I32)],
        compiler_params=_cparams("arbitrary", "arbitrary"),
        name="dsa",
    )(dq, dkk, vt, iq, ikk, wt, bias, lstrict)


def _merge_kernel(x_ref, gm_ref, wg_ref, oa_ref, of_ref, os_ref, od_ref, wb_ref, wo_ref, gf_ref,
                  wr_ref, br_ref, xo_ref, h2_ref, route_ref):
    x = x_ref[...]
    hb = _rms(x, gm_ref[...]).astype(BF16)
    d = x.shape[1]
    merged = None
    for bi, o_ref in enumerate((oa_ref, of_ref, os_ref, od_ref)):
        gate = jax.nn.sigmoid(jnp.dot(hb, wg_ref[:, bi * d:(bi + 1) * d], preferred_element_type=F32))
        term = gate * jnp.dot(o_ref[...], wb_ref[bi], preferred_element_type=F32)
        merged = term if merged is None else merged + term
    xn = x + jnp.dot(merged.astype(BF16), wo_ref[...], preferred_element_type=F32)
    xo_ref[...] = xn
    h2 = _rms(xn, gf_ref[...])
    h2_ref[...] = h2

    logits = _dot_x3(h2, wr_ref[0], wr_ref[1]) + br_ref[...]
    lane = lax.broadcasted_iota(I32, logits.shape, 1).astype(F32)
    big = 1e9
    gl = jnp.where(lane < N_GROUPS, logits, -jnp.inf)
    gmax = jnp.max(gl, axis=1, keepdims=True)
    grp = jnp.min(jnp.where(gl == gmax, lane, big), axis=1, keepdims=True)
    p_grp = 1.0 / jnp.sum(jnp.exp(gl - gmax), axis=1, keepdims=True)
    first = N_GROUPS + grp * EXPERTS_PER_GROUP
    el = jnp.where((lane >= first) & (lane < first + EXPERTS_PER_GROUP), logits, -jnp.inf)
    l1 = jnp.max(el, axis=1, keepdims=True)
    i1 = jnp.min(jnp.where(el == l1, lane, big), axis=1, keepdims=True)
    el2 = jnp.where(lane == i1, -jnp.inf, el)
    l2 = jnp.max(el2, axis=1, keepdims=True)
    i2 = jnp.min(jnp.where(el2 == l2, lane, big), axis=1, keepdims=True)
    e2 = jnp.exp(l2 - l1)
    g1 = p_grp / (1.0 + e2)
    g2 = p_grp * e2 / (1.0 + e2)
    route = jnp.where(lane == 0, i1 - N_GROUPS,
                      jnp.where(lane == 1, i2 - N_GROUPS,
                                jnp.where(lane == 2, g1, jnp.where(lane == 3, g2, 0.0))))
    route_ref[...] = route


def _merge(x2, gm, wg, o_a, o_f, o_s, o_d, wb, wo, gf, wr, br, tm):
    n, d = x2.shape
    row = lambda w: pl.BlockSpec((tm, w), lambda i: (i, 0))
    full = lambda a: pl.BlockSpec(a.shape, lambda i: (0,) * a.ndim, pipeline_mode=pl.Buffered(1))
    return pl.pallas_call(
        _merge_kernel,
        grid=(n // tm,),
        in_specs=[row(d), full(gm), full(wg), row(256), row(256), row(256), row(256),
                  full(wb), full(wo), full(gf), full(wr), full(br)],
        out_specs=[row(d), row(d), row(LANES)],
        out_shape=[jax.ShapeDtypeStruct((n, d), F32), jax.ShapeDtypeStruct((n, d), F32),
                   jax.ShapeDtypeStruct((n, LANES), F32)],
        compiler_params=_cparams("arbitrary"),
        name="merge",
    )(x2, gm, wg, o_a, o_f, o_s, o_d, wb, wo, gf, wr, br)


def _rank_kernel(route_ref, ltri_ref, rank_ref, cnt_ref, carry_ref):
    @pl.when(pl.program_id(0) == 0)
    def _():
        carry_ref[...] = jnp.zeros_like(carry_ref)

    route = route_ref[...]
    lane = lax.broadcasted_iota(I32, route.shape, 1)
    e0 = route[:, 0:1].astype(I32)
    e1 = route[:, 1:2].astype(I32)
    oh0 = (lane == e0).astype(F32)
    oh1 = (lane == e1).astype(F32)
    both = oh0 + oh1
    before = jnp.dot(ltri_ref[...], both.astype(BF16), preferred_element_type=F32) + carry_ref[0:1, :]
    r0 = jnp.sum(oh0 * before, axis=1, keepdims=True)
    r1 = jnp.sum(oh1 * (before + oh0), axis=1, keepdims=True)
    rank_ref[...] = jnp.where(lane == 0, r0, jnp.where(lane == 1, r1, 0.0))
    total = carry_ref[0:1, :] + jnp.sum(both, axis=0, keepdims=True)
    carry_ref[0:1, :] = total
    cnt_ref[...] = jnp.broadcast_to(total, cnt_ref.shape)


def _rank(route, tm):
    n = route.shape[0]
    ltri = (jnp.arange(tm)[:, None] > jnp.arange(tm)[None, :]).astype(BF16)
    return pl.pallas_call(
        _rank_kernel,
        grid=(n // tm,),
        in_specs=[pl.BlockSpec((tm, LANES), lambda i: (i, 0)), pl.BlockSpec((tm, tm), lambda i: (0, 0))],
        out_specs=[pl.BlockSpec((tm, LANES), lambda i: (i, 0)), pl.BlockSpec((8, LANES), lambda i: (0, 0))],
        out_shape=[jax.ShapeDtypeStruct((n, LANES), F32), jax.ShapeDtypeStruct((8, LANES), F32)],
        scratch_shapes=[pltpu.VMEM((8, LANES), F32)],
        compiler_params=_cparams("arbitrary"),
        name="moe_rank",
    )(route, ltri)


def _expert_kernel(be_ref, nu_ref, tok_ref, tok1_ref, tok2_ref, h_hbm, wup_ref, wdn_ref, y_ref,
                   xbuf, sem, wup_b, wdn_b, *, te):
    b = pl.program_id(0)
    n_used = nu_ref[0]
    slot = b % 3

    def start_row(tokens_ref, dst, r):
        pltpu.make_async_copy(h_hbm.at[pl.ds(tokens_ref[0, 0, r], 1), :],
                              xbuf.at[dst, pl.ds(r, 1), :], sem.at[dst]).start()

    def start_block(tokens_ref, dst):
        def issue(r, c):
            start_row(tokens_ref, dst, r)
            return c
        lax.fori_loop(0, te, issue, 0, unroll=8)

    def block(prefetch):
        pltpu.make_async_copy(h_hbm.at[pl.ds(0, te), :], xbuf.at[slot], sem.at[slot]).wait()
        xb = xbuf[slot].astype(BF16)
        if prefetch:
            dst = (b + 2) % 3
            for r in range(te):
                start_row(tok2_ref, dst, r)
        gu = jnp.dot(xb, wup_b[...], preferred_element_type=F32)
        g = gu[:, :EXPERT_FF]
        act = g * jax.nn.sigmoid(g) * gu[:, EXPERT_FF:]
        y_ref[...] = jnp.dot(act.astype(BF16), wdn_b[...], preferred_element_type=F32)

    @pl.when((b < n_used) & ((b == 0) | (be_ref[b] != be_ref[jnp.maximum(b - 1, 0)])))
    def _():
        wup_b[...] = wup_ref[0, 0].astype(BF16)
        wdn_b[...] = wdn_ref[0, 0].astype(BF16)

    @pl.when((b == 0) & (n_used > 0))
    def _():
        start_block(tok_ref, 0)

    @pl.when((b == 0) & (n_used > 1))
    def _():
        start_block(tok1_ref, 1)

    @pl.when(b + 2 < n_used)
    def _():
        block(True)

    @pl.when((b < n_used) & (b + 2 >= n_used))
    def _():
        block(False)

    @pl.when(b >= n_used)
    def _():
        y_ref[...] = jnp.zeros_like(y_ref)


def _experts(blk_expert, n_used, slot_tok, h2, w_up, w_down, layer, te):
    n_blocks = blk_expert.shape[0]
    d = h2.shape[1]
    ahead = lambda k: pl.BlockSpec((1, 1, te), lambda b, be, nu: (jnp.minimum(b + k, n_blocks - 1), 0, 0),
                                   memory_space=pltpu.SMEM)
    grid_spec = pltpu.PrefetchScalarGridSpec(
        num_scalar_prefetch=2,
        grid=(n_blocks,),
        in_specs=[ahead(0), ahead(1), ahead(2),
                  pl.BlockSpec(memory_space=pl.ANY),
                  pl.BlockSpec((1, 1, d, 2 * EXPERT_FF), lambda b, be, nu: (layer, be[b], 0, 0)),
                  pl.BlockSpec((1, 1, EXPERT_FF, d), lambda b, be, nu: (layer, be[b], 0, 0))],
        out_specs=pl.BlockSpec((te, d), lambda b, be, nu: (b, 0)),
        scratch_shapes=[pltpu.VMEM((3, te, d), F32), pltpu.SemaphoreType.DMA((3,)),
                        pltpu.VMEM((d, 2 * EXPERT_FF), BF16), pltpu.VMEM((EXPERT_FF, d), BF16)],
    )
    slots = slot_tok.reshape(n_blocks, 1, te)
    return pl.pallas_call(
        functools.partial(_expert_kernel, te=te),
        grid_spec=grid_spec,
        out_shape=jax.ShapeDtypeStruct((n_blocks * te, d), F32),
        compiler_params=_cparams("arbitrary"),
        name="moe_experts",
    )(blk_expert, n_used, slots, slots, slots, h2, w_up, w_down)


def _combine_kernel(pos_ref, pos_next_ref, x_ref, route_ref, y_hbm, o_ref, ybuf, sem, *, tc):
    i = pl.program_id(0)
    slot = i % 2

    def gather(rows_ref, dst):
        for r in range(2 * tc):
            pltpu.make_async_copy(y_hbm.at[pl.ds(rows_ref[0, 0, r], 1), :],
                                  ybuf.at[dst, pl.ds(r, 1), :], sem.at[dst]).start()

    @pl.when(i == 0)
    def _():
        gather(pos_ref, 0)

    @pl.when(i + 1 < pl.num_programs(0))
    def _():
        gather(pos_next_ref, 1 - slot)

    pltpu.make_async_copy(y_hbm.at[pl.ds(0, 2 * tc), :], ybuf.at[slot], sem.at[slot]).wait()
    route = route_ref[...]
    o_ref[...] = (x_ref[...] + route[:, 2:3] * ybuf[slot, 0:tc, :]
                  + route[:, 3:4] * ybuf[slot, tc:2 * tc, :])


def _combine(pos, x2, route, yb, tc):
    n, d = x2.shape
    nt = n // tc
    pos_t = jnp.transpose(pos.reshape(nt, tc, 2), (0, 2, 1)).reshape(nt, 1, 2 * tc)
    return pl.pallas_call(
        functools.partial(_combine_kernel, tc=tc),
        grid=(nt,),
        in_specs=[pl.BlockSpec((1, 1, 2 * tc), lambda i: (i, 0, 0), memory_space=pltpu.SMEM),
                  pl.BlockSpec((1, 1, 2 * tc), lambda i: (jnp.minimum(i + 1, nt - 1), 0, 0),
                               memory_space=pltpu.SMEM),
                  pl.BlockSpec((tc, d), lambda i: (i, 0)),
                  pl.BlockSpec((tc, LANES), lambda i: (i, 0)),
                  pl.BlockSpec(memory_space=pl.ANY)],
        out_specs=pl.BlockSpec((tc, d), lambda i: (i, 0)),
        out_shape=jax.ShapeDtypeStruct((n, d), F32),
        scratch_shapes=[pltpu.VMEM((2, 2 * tc, d), F32), pltpu.SemaphoreType.DMA((2,))],
        compiler_params=_cparams("arbitrary"),
        name="moe_combine",
    )(pos_t, pos_t, x2, route, yb)


def _swap_mid_heads(w, axis):
    h = jnp.split(w, 4, axis=axis)
    return jnp.concatenate([h[0], h[2], h[1], h[3]], axis=axis)


def _t5_lookup(t5_table, dist):
    onehot = (_t5_bucket(dist)[..., None] == jnp.arange(T5_BUCKETS)).astype(F32)
    return jnp.einsum("...b,bh->...h", onehot, t5_table.astype(F32), precision=lax.Precision.HIGHEST)


def _t5_bucket(dist):
    n = jnp.maximum(dist, 0)
    max_exact = T5_BUCKETS // 2
    nf = jnp.maximum(n, 1).astype(F32)
    large = max_exact + (jnp.log(nf / max_exact) / math.log(T5_MAX_DISTANCE / max_exact)
                         * (T5_BUCKETS - max_exact)).astype(I32)
    large = jnp.minimum(large, T5_BUCKETS - 1)
    return jnp.where(n < max_exact, n, large)


def _swa_bias(t5_table):
    t = SWA_BLOCK
    dist = t + jnp.arange(t)[None, :] - jnp.arange(2 * t)[:, None]
    tile = jnp.where(((dist >= 0) & (dist < t))[..., None], _t5_lookup(t5_table, dist)[..., :4], NEG_INF)
    return jnp.concatenate([tile[..., h] for h in (0, 2, 1, 3)], axis=1)


def _layer_weights(w_in, qk_gain, forget_bias, w_branch):
    offs = np.concatenate([[0], np.cumsum(IN_SPLITS)]).tolist()
    part = lambda k: w_in[:, offs[k]:offs[k + 1]]
    dup = lambda w: jnp.concatenate([w, w], axis=1)
    aq = _swap_mid_heads(part(0), 1)
    cols = [aq, part(1), part(2), part(3), part(4), part(5), part(7), part(8), part(9),
            part(10), dup(part(11)), dup(part(12)), part(13), dup(part(14))]
    w1 = jnp.concatenate(cols, axis=1).astype(BF16)
    d = w_in.shape[0]
    wm = _hi_lo(jnp.concatenate([part(6), part(15), jnp.zeros((d, LANES - 8), F32)], axis=1))
    wg = part(16).astype(BF16)
    tile = lambda g, reps, scale: jnp.pad(jnp.tile(g, reps) * scale, (0, 256 - reps * HEAD_DIM))
    gains = jnp.stack([tile(qk_gain[0, 0], 4, ATTN_SCALE), tile(qk_gain[0, 1], 2, 1.0),
                       tile(qk_gain[1, 0], 4, ATTN_SCALE * LOG2E), tile(qk_gain[1, 1], 4, 1.0),
                       tile(qk_gain[2, 0], 4, ATTN_SCALE * LOG2E), tile(qk_gain[2, 1], 2, 1.0),
                       jnp.zeros((256,), F32), jnp.zeros((256,), F32)]).astype(F32)
    fb = jnp.pad(forget_bias.astype(F32), (0, LANES - 4)).reshape(1, LANES)
    wb = jnp.stack([_swap_mid_heads(w_branch[0], 0), w_branch[1], w_branch[2], w_branch[3]]).astype(BF16)
    return w1, wm, wg, gains, fb, wb


def kernel(x, norm_mix_g, w_in, forget_bias, attn_sinks, qk_gain, w_branch, w_out, t5_table, norm_ffn_g,
           w_router_group, b_router_group, w_router_expert, b_router_expert, w_expert_up, w_expert_down):
    b, s, d = x.shape
    n = b * s
    depth = w_in.shape[0]
    top_k = min(DSA_TOPK_MAX, s // 4)
    tm_proj = min(512, s)
    fox_t = min(256, s)
    sb_t = min(256, s)
    dsa_t = min(256, s)
    te = 256
    tc = 256

    gseg = (jnp.arange(256)[:, None] // HEAD_DIM == jnp.arange(256)[None, :] // HEAD_DIM).astype(BF16)
    bias_swa = _swa_bias(t5_table)
    bias_dsa = _dsa_bias(t5_table, dsa_t) * LOG2E
    n_blocks = -(-2 * n // te) + N_EXPERTS
    tok_ids = jnp.repeat(jnp.arange(n, dtype=I32), 2)

    for layer in range(depth):
        w1, wm, wg, gains, fb, wb = _layer_weights(w_in[layer], qk_gain[layer], forget_bias[layer],
                                                   w_branch[layer])
        sink_row = jnp.repeat(attn_sinks[layer].astype(F32)[jnp.array([0, 2, 1, 3])], SWA_BLOCK)
        sinks = jnp.broadcast_to(sink_row[None, :], (8, 4 * SWA_BLOCK))
        (aq, ak, av, fq, fk, fv, sq, sk, sv, dq, dkk, dvv, iq, ikk, cm) = _proj(
            x, norm_mix_g[layer].reshape(1, d), w1, wm, gseg, gains, fb, tm_proj)

        o_swa = _swa(aq, ak, av, bias_swa, sinks, min(4, s // SWA_BLOCK))
        cmt = jnp.transpose(cm[:, :, :8], (0, 2, 1))
        o_fox = _fox(fq, fk, fv, cmt, cm, fox_t)
        o_sb = _sb(sq, sk, sv, sb_t)
        o_dsa = _dsa(dq, dkk, dvv, iq, ikk, cmt, bias_dsa, top_k, dsa_t)

        wr = _hi_lo(jnp.concatenate([w_router_group[layer], w_router_expert[layer],
                                     jnp.zeros((d, LANES - N_GROUPS - N_EXPERTS), F32)], axis=1))
        br = jnp.concatenate([b_router_group[layer], b_router_expert[layer],
                              jnp.zeros((LANES - N_GROUPS - N_EXPERTS,), F32)]).reshape(1, LANES)
        x2, h2, route = _merge(
            x.reshape(n, d), norm_mix_g[layer].reshape(1, d), wg,
            o_swa.reshape(n, 256), o_fox.reshape(n, 256), o_sb.reshape(n, 256), o_dsa.reshape(n, 256),
            wb, w_out[layer].astype(BF16), norm_ffn_g[layer].reshape(1, d), wr, br, min(512, n))

        rank, cnt = _rank(route, min(512, n))
        counts = cnt[0, :N_EXPERTS].astype(I32)
        padded = (counts + te - 1) // te * te
        pend = jnp.cumsum(padded)
        pstart = pend - padded
        expert = route[:, :2].astype(I32)
        own = expert[:, :, None] == jnp.arange(N_EXPERTS, dtype=I32)
        pos = jnp.sum(jnp.where(own, pstart, 0), axis=-1) + rank[:, :2].astype(I32)
        slot_tok = jnp.zeros((n_blocks * te,), I32).at[pos.reshape(-1)].set(
            tok_ids, unique_indices=True, mode="promise_in_bounds")
        blk_start = jnp.arange(n_blocks, dtype=I32)[:, None] * te
        blk_expert = jnp.minimum(jnp.sum((pend[None, :] <= blk_start).astype(I32), axis=1), N_EXPERTS - 1)
        n_used = (pend[-1:] // te).astype(I32)

        yb = _experts(blk_expert, n_used, slot_tok, h2, w_expert_up, w_expert_down, layer, te)
        x = _combine(pos, x2, route, yb, tc).reshape(b, s, d)
    return x
```

```python
import functools
import math

import jax
import jax.numpy as jnp
import numpy as np
from jax import lax
from jax.experimental import pallas as pl
from jax.experimental.pallas import tpu as pltpu

F32 = jnp.float32
BF16 = jnp.bfloat16
I32 = jnp.int32

HEAD_DIM = 64
LANES = 128
NORM_EPS = 1e-6
NEG_INF = -1e30
M_INIT = -1e29
ATTN_SCALE = HEAD_DIM ** -0.5
LOG2E = math.log2(math.e)
SWA_BLOCK = 128
IDX_SCALE = 64 ** -0.5
IDX_HEADS = 4
DSA_TOPK_MAX = 256
T5_BUCKETS = 32
T5_MAX_DISTANCE = 128
N_GROUPS = 4
EXPERTS_PER_GROUP = 8
N_EXPERTS = N_GROUPS * EXPERTS_PER_GROUP
EXPERT_FF = 512
SB_DEAD = -110.0
INT_MIN = -2 ** 31
ONES_ROWS = 16
VMEM_LIMIT = 56 * 1024 * 1024

IN_SPLITS = (256, 128, 128, 256, 256, 256, 4, 256, 256, 256, 256, 64, 64, 256, 64, 4, 4096)

_SEG = dict(aq=(0, 256), ak=(256, 128), av=(384, 128), fq=(512, 256), fk=(768, 256), fv=(1024, 256),
            sq=(1280, 256), sk=(1536, 256), sv=(1792, 256), dq=(2048, 256), dkk=(2304, 128),
            dvv=(2432, 128), iq=(2560, 256), ikk=(2816, 128))
_SEG_ORDER = ("aq", "ak", "av", "fq", "fk", "fv", "sq", "sk", "sv", "dq", "dkk", "dvv", "iq", "ikk")


def _cparams(*sem):
    return pltpu.CompilerParams(dimension_semantics=sem, vmem_limit_bytes=VMEM_LIMIT)


def _rms(x, g):
    return x * lax.rsqrt(jnp.mean(x * x, axis=-1, keepdims=True) + NORM_EPS) * g


def _log_sigmoid(z):
    return jnp.minimum(z, 0.0) - jnp.log(1.0 + jnp.exp(-jnp.abs(z)))


def _dot_nt(a, b):
    return lax.dot_general(a, b, (((1,), (1,)), ((), ())), preferred_element_type=F32)


def _with_ones_rows(vt):
    ones = jnp.ones(vt.shape[:-2] + (ONES_ROWS, vt.shape[-1]), vt.dtype)
    return jnp.concatenate([vt, ones], axis=-2)


def _split3(x):
    p1 = x.astype(BF16)
    r = x - p1.astype(F32)
    p2 = r.astype(BF16)
    return p1, p2, (r - p2.astype(F32)).astype(BF16)


def _hi_lo(w):
    hi = w.astype(BF16)
    return jnp.stack([hi, (w - hi.astype(F32)).astype(BF16)])


def _dot_x3(a, b_hi, b_lo):
    a_hi = a.astype(BF16)
    a_lo = (a - a_hi.astype(F32)).astype(BF16)
    return (jnp.dot(a_hi, b_hi, preferred_element_type=F32) + jnp.dot(a_lo, b_hi, preferred_element_type=F32)
            + jnp.dot(a_hi, b_lo, preferred_element_type=F32))


def _split_heads(qp):
    lo = lax.broadcasted_iota(I32, (1, LANES), 1) < HEAD_DIM
    zero = jnp.zeros_like(qp)
    return jnp.concatenate([jnp.where(lo, qp, zero), jnp.where(lo, zero, qp)], axis=0)


def _proj_kernel(x_ref, g_ref, w1_ref, wm_ref, gseg_ref, gains_ref, fb_ref, ltri_ref, *rest):
    outs = dict(zip(_SEG_ORDER, rest[:len(_SEG_ORDER)]))
    cm_ref = rest[len(_SEG_ORDER)]
    carry_ref = rest[len(_SEG_ORDER) + 1]

    @pl.when(pl.program_id(1) == 0)
    def _():
        carry_ref[...] = jnp.zeros_like(carry_ref)

    h = _rms(x_ref[0], g_ref[...])
    hb = h.astype(BF16)

    def seg(name):
        off, width = _SEG[name]
        return jnp.dot(hb, w1_ref[:, off:off + width], preferred_element_type=F32)

    def head_norm(t, row):
        width = t.shape[1]
        ssq = jnp.dot((t * t).astype(BF16), gseg_ref[:width, :width], preferred_element_type=F32)
        return t * lax.rsqrt(ssq * (1.0 / HEAD_DIM) + NORM_EPS) * gains_ref[row:row + 1, :width]

    normed = dict(aq=0, ak=1, fq=2, fk=3, dq=4, dkk=5)
    scaled = dict(sq=ATTN_SCALE * LOG2E, iq=IDX_SCALE)
    for name in _SEG_ORDER:
        t = seg(name)
        if name in normed:
            t = head_norm(t, normed[name])
        elif name in scaled:
            t = t * scaled[name]
        outs[name][0] = t.astype(BF16)

    misc = _dot_x3(h, wm_ref[0], wm_ref[1])
    lane = lax.broadcasted_iota(I32, misc.shape, 1)
    logf = jnp.where(lane < 4, _log_sigmoid(misc + fb_ref[...]), 0.0)
    ltri = ltri_ref[...]
    c = carry_ref[0:1, :]
    for piece in _split3(logf):
        c = c + jnp.dot(ltri, piece, preferred_element_type=F32)
    tm = misc.shape[0]
    carry_ref[0:1, :] = c[tm - 1:tm, :]
    cm_ref[0] = jnp.where(lane < 4, c, misc)


def _proj(x, g, w1, wm, gseg, gains, fb, tm):
    b, s, d = x.shape
    ltri = jnp.tril(jnp.ones((tm, tm), BF16))
    full = lambda shape: pl.BlockSpec(shape, lambda bi, si: (0,) * len(shape))
    out_shapes = [jax.ShapeDtypeStruct((b, s, _SEG[n][1]), BF16) for n in _SEG_ORDER]
    out_shapes.append(jax.ShapeDtypeStruct((b, s, LANES), F32))
    out_specs = [pl.BlockSpec((1, tm, _SEG[n][1]), lambda bi, si: (bi, si, 0)) for n in _SEG_ORDER]
    out_specs.append(pl.BlockSpec((1, tm, LANES), lambda bi, si: (bi, si, 0)))
    return pl.pallas_call(
        _proj_kernel,
        grid=(b, s // tm),
        in_specs=[pl.BlockSpec((1, tm, d), lambda bi, si: (bi, si, 0)),
                  full((1, d)), full(w1.shape), full(wm.shape), full(gseg.shape),
                  full(gains.shape), full(fb.shape), full((tm, tm))],
        out_specs=out_specs,
        out_shape=out_shapes,
        scratch_shapes=[pltpu.VMEM((8, LANES), F32)],
        compiler_params=_cparams("arbitrary", "arbitrary"),
        name="proj",
    )(x, g, w1, wm, gseg, gains, fb, ltri)


def _swa_kernel(q_ref, kp_ref, kc_ref, vtp_ref, vtc_ref, bias_ref, sink_ref, o_ref, *, nsub):
    i = pl.program_id(1)
    t = SWA_BLOCK
    no_prev = (lax.broadcasted_iota(I32, (2 * t, 4 * t), 0) < t) & (i == 0)
    sink = sink_ref[0:1, :]
    for u in range(nsub):
        rows = slice(u * t, (u + 1) * t)
        q = q_ref[0, rows, :]
        qs = jnp.concatenate([_split_heads(q[:, :LANES]), _split_heads(q[:, LANES:])], axis=0)
        if u == 0:
            kcat = jnp.concatenate([kp_ref[0], kc_ref[0, rows, :]], axis=0)
            vt = jnp.concatenate([vtp_ref[0], vtc_ref[0, :, rows]], axis=1)
        else:
            kcat = kc_ref[0, (u - 1) * t:(u + 1) * t, :]
            vt = vtc_ref[0, :, (u - 1) * t:(u + 1) * t]
        st = _dot_nt(kcat, qs) + bias_ref[...]
        if u == 0:
            st = jnp.where(no_prev, NEG_INF, st)
        m = jnp.maximum(jnp.max(st, axis=0, keepdims=True), sink)
        p = jnp.exp(st - m)
        denom = jnp.sum(p, axis=0, keepdims=True) + jnp.exp(sink - m)
        ot = jnp.dot(vt, p.astype(BF16), preferred_element_type=F32) / denom
        ot = jnp.concatenate([ot[:HEAD_DIM, 0:t], ot[HEAD_DIM:, t:2 * t],
                              ot[:HEAD_DIM, 2 * t:3 * t], ot[HEAD_DIM:, 3 * t:]], axis=0)
        o_ref[0, rows, :] = ot.T.astype(BF16)


def _swa(aq, ak, av, bias, sinks, nsub):
    b, s, _ = aq.shape
    t = SWA_BLOCK
    avt = jnp.transpose(av, (0, 2, 1))
    cur = lambda bi, i: (bi, i, 0)
    prev = lambda bi, i: (bi, jnp.maximum(i * nsub - 1, 0), 0)
    return pl.pallas_call(
        functools.partial(_swa_kernel, nsub=nsub),
        grid=(b, s // (t * nsub)),
        in_specs=[pl.BlockSpec((1, t * nsub, 256), cur),
                  pl.BlockSpec((1, t, LANES), prev), pl.BlockSpec((1, t * nsub, LANES), cur),
                  pl.BlockSpec((1, LANES, t), lambda bi, i: (bi, 0, jnp.maximum(i * nsub - 1, 0))),
                  pl.BlockSpec((1, LANES, t * nsub), lambda bi, i: (bi, 0, i)),
                  pl.BlockSpec(bias.shape, lambda bi, i: (0, 0)),
                  pl.BlockSpec(sinks.shape, lambda bi, i: (0, 0))],
        out_specs=pl.BlockSpec((1, t * nsub, 256), cur),
        out_shape=jax.ShapeDtypeStruct((b, s, 256), BF16),
        compiler_params=_cparams("arbitrary", "arbitrary"),
        name="swa",
    )(aq, ak, ak, avt, avt, bias, sinks)


def _fox_kernel(q_ref, k_ref, vt_ref, ct_ref, ccol_ref, o_ref, ckb_ref, *, t):
    i = pl.program_id(1)
    n_tiles = ckb_ref.shape[1] // t

    @pl.when(i == 0)
    def _():
        def fill(j, c):
            rows = pl.ds(pl.multiple_of(j * t, t), t)
            cc = ccol_ref[0, rows, :] * LOG2E
            for h in range(4):
                ckb_ref[h, rows, :] = jnp.broadcast_to(cc[:, h:h + 1], (t, LANES))
            return c
        lax.fori_loop(0, n_tiles, fill, 0)

    q = q_ref[0]
    ct = ct_ref[0] * LOG2E
    qs = [_split_heads(q[:, :LANES]), _split_heads(q[:, LANES:])]
    valid = lax.broadcasted_iota(I32, (t, t), 0) <= lax.broadcasted_iota(I32, (t, t), 1)

    def tiles(j0, n, carry, masked):
        m, accs = carry
        rows = pl.ds(pl.multiple_of(j0 * t, t), n * t)
        cols = []
        for pair in range(2):
            st = _dot_nt(k_ref[0, rows, pair * LANES:(pair + 1) * LANES], qs[pair])
            for hh in range(2):
                head = 2 * pair + hh
                ck = ckb_ref[head, rows, :]
                for c in range(t // LANES):
                    cs = slice(c * LANES, (c + 1) * LANES)
                    sh = st[:, hh * t + c * LANES:hh * t + (c + 1) * LANES] + (ct[head:head + 1, cs] - ck)
                    if masked:
                        last = jnp.where(valid[:, cs], sh[(n - 1) * t:], NEG_INF)
                        sh = jnp.concatenate([sh[:(n - 1) * t], last], axis=0) if n > 1 else last
                    cols.append(sh)
        st = jnp.concatenate(cols, axis=1)
        m_new = jnp.maximum(m, jnp.max(st, axis=0, keepdims=True))
        alpha = jnp.exp2(m - m_new)
        pb = jnp.exp2(st - m_new).astype(BF16)
        new = []
        for pair in range(2):
            lanes = slice(pair * 2 * t, (pair + 1) * 2 * t)
            vt = _with_ones_rows(jnp.concatenate(
                [vt_ref[0, j0 + u, pair * LANES:(pair + 1) * LANES, :] for u in range(n)], axis=1))
            pv = jnp.dot(vt, pb[:, lanes], preferred_element_type=F32)
            new.append(alpha[:, lanes] * accs[pair] + pv)
        return m_new, tuple(new)

    zero_acc = jnp.zeros((LANES + ONES_ROWS, 2 * t), F32)
    carry = lax.fori_loop(0, i // 4, lambda g, c: tiles(4 * g, 4, c, False),
                          (jnp.full((1, 4 * t), M_INIT, F32), (zero_acc, zero_acc)))
    rest = i - i % 4
    _, accs = lax.switch(i % 4, [functools.partial(lambda c, n: tiles(rest, n, c, True), n=r + 1)
                                 for r in range(4)], carry)
    outs = []
    for pair in range(2):
        o = accs[pair][:LANES] / accs[pair][LANES:LANES + 1]
        outs.append(jnp.concatenate([o[:HEAD_DIM, :t], o[HEAD_DIM:, t:]], axis=0))
    o_ref[0] = jnp.concatenate(outs, axis=0).T.astype(BF16)


def _fox(fq, fk, fv, ct, ccol, t):
    b, s, _ = fq.shape
    nt = s // t
    vt = jnp.transpose(fv.reshape(b, nt, t, 256), (0, 1, 3, 2))
    return pl.pallas_call(
        functools.partial(_fox_kernel, t=t),
        grid=(b, nt),
        in_specs=[pl.BlockSpec((1, t, 256), lambda bi, i: (bi, i, 0)),
                  pl.BlockSpec((1, s, 256), lambda bi, i: (bi, 0, 0)),
                  pl.BlockSpec((1, nt, 256, t), lambda bi, i: (bi, 0, 0, 0)),
                  pl.BlockSpec((1, 8, t), lambda bi, i: (bi, 0, i)),
                  pl.BlockSpec((1, s, LANES), lambda bi, i: (bi, 0, 0))],
        out_specs=pl.BlockSpec((1, t, 256), lambda bi, i: (bi, i, 0)),
        out_shape=jax.ShapeDtypeStruct((b, s, 256), BF16),
        scratch_shapes=[pltpu.VMEM((4, s, LANES), F32)],
        compiler_params=_cparams("arbitrary", "arbitrary"),
        name="fox",
    )(fq, fk, vt, ct, ccol)


def _sb_kernel(q_ref, k_ref, vt_ref, lgt_ref, o_ref, *, t):
    i = pl.program_id(1)
    q = q_ref[0]
    lgt = lgt_ref[...]
    qs = [_split_heads(q[:, :LANES]), _split_heads(q[:, LANES:])]
    key_i = lax.broadcasted_iota(I32, (t, 4 * t), 0)
    query_i = lax.broadcasted_iota(I32, (t, 4 * t), 1) & (t - 1)
    strict = key_i < query_i

    def tile(j, r, accs, masked):
        rows = pl.ds(pl.multiple_of(j * t, t), t)
        z = jnp.concatenate([_dot_nt(k_ref[0, rows, p * LANES:(p + 1) * LANES], qs[p]) for p in range(2)],
                            axis=1)
        sp = jnp.log2(1.0 + jnp.exp2(-jnp.abs(z)))
        log_beta = jnp.minimum(z, 0.0) - sp
        log_keep = jnp.minimum(-z, 0.0) - sp
        if masked:
            log_keep = jnp.where(strict, log_keep, 0.0)
        hi = log_keep.astype(BF16)
        lo = (log_keep - hi.astype(F32)).astype(BF16)
        later = (jnp.dot(lgt, hi, preferred_element_type=F32)
                 + jnp.dot(lgt, lo, preferred_element_type=F32))
        a = jnp.exp2(log_beta + later + r)
        if masked:
            a = jnp.where(strict, a, 0.0)
        ab = a.astype(BF16)
        new = tuple(accs[p] + jnp.dot(vt_ref[0, j, p * LANES:(p + 1) * LANES, :],
                                      ab[:, p * 2 * t:(p + 1) * 2 * t], preferred_element_type=F32)
                    for p in range(2))
        return r + jnp.sum(log_keep, axis=0, keepdims=True), new

    zero_acc = jnp.zeros((LANES, 2 * t), F32)
    r, accs = tile(i, jnp.zeros((1, 4 * t), F32), (zero_acc, zero_acc), True)

    def cond(c):
        return (c[0] >= 0) & (c[1] > 0)

    def body(c):
        j, _, r, accs = c
        r, accs = tile(j, r, accs, False)
        return j - 1, (jnp.max(r) > SB_DEAD * LOG2E).astype(I32), r, accs

    _, _, _, accs = lax.while_loop(cond, body, (i - 1, (jnp.max(r) > SB_DEAD * LOG2E).astype(I32), r, accs))
    outs = [jnp.concatenate([accs[p][:HEAD_DIM, :t], accs[p][HEAD_DIM:, t:]], axis=0) for p in range(2)]
    o_ref[0] = jnp.concatenate(outs, axis=0).T.astype(BF16)


def _sb(sq, sk, sv, t):
    b, s, _ = sq.shape
    nt = s // t
    lgt = (jnp.arange(t)[:, None] < jnp.arange(t)[None, :]).astype(BF16)
    vt = jnp.transpose(sv.reshape(b, nt, t, 256), (0, 1, 3, 2))
    return pl.pallas_call(
        functools.partial(_sb_kernel, t=t),
        grid=(b, nt),
        in_specs=[pl.BlockSpec((1, t, 256), lambda bi, i: (bi, i, 0)),
                  pl.BlockSpec((1, s, 256), lambda bi, i: (bi, 0, 0)),
                  pl.BlockSpec((1, nt, 256, t), lambda bi, i: (bi, 0, 0, 0)),
                  pl.BlockSpec((t, t), lambda bi, i: (0, 0))],
        out_specs=pl.BlockSpec((1, t, 256), lambda bi, i: (bi, i, 0)),
        out_shape=jax.ShapeDtypeStruct((b, s, 256), BF16),
        compiler_params=_cparams("arbitrary", "arbitrary"),
        name="stickbreak",
    )(sq, sk, vt, lgt)


def _bit_planes(words):
    words = list(words)
    j, m = 16, 0x0000FFFF
    while j:
        k = 0
        while k < 32:
            tt = (words[k] ^ lax.shift_right_logical(words[k + j], jnp.full_like(words[k + j], j))) & m
            words[k] = words[k] ^ tt
            words[k + j] = words[k + j] ^ (tt << j)
            k = (k + j + 1) & ~j
        j >>= 1
        m = (m ^ (m << j)) & 0xFFFFFFFF
    return words


def _dsa_kernel(q_ref, kk_ref, vt_ref, iq_ref, ikk_ref, wt_ref, bias_ref, lstrict_ref, o_ref,
                key_ref, plane_ref, *, t, top_k):
    i = pl.program_id(1)
    assert t == 8 * 32

    @pl.when(i == 0)
    def _():
        plane_ref[...] = jnp.zeros_like(plane_ref)
    causal = lax.broadcasted_iota(I32, (t, t), 0) <= lax.broadcasted_iota(I32, (t, t), 1)

    def head_stack(x):
        return jnp.concatenate([_split_heads(x[:, :LANES]), _split_heads(x[:, LANES:])], axis=0)

    def key_rows(ref, j0, n):
        return ref[0, pl.ds(pl.multiple_of(j0 * t, t), n * t), :]

    iqs = head_stack(iq_ref[0])
    wt = wt_ref[0]
    w = [wt[4 + h:5 + h, :] * (IDX_HEADS ** -0.5) for h in range(IDX_HEADS)]

    def score_tiles(j0, n, masked):
        lg = _dot_nt(key_rows(ikk_ref, j0, n), iqs)
        sc = w[0] * jnp.maximum(lg[:, 0:t], 0.0)
        for h in range(1, IDX_HEADS):
            sc = sc + w[h] * jnp.maximum(lg[:, h * t:(h + 1) * t], 0.0)
        bits = pltpu.bitcast(sc, I32)
        key = bits ^ ((bits >> 31) & 0x7FFFFFFF)
        key = jnp.where(key == -1, 0, key)
        for u in range(n):
            key_u = key[u * t:(u + 1) * t]
            if masked and u == n - 1:
                key_u = jnp.where(causal, key_u, INT_MIN)
            key_ref[j0 + u] = key_u
            for p, plane in enumerate(_bit_planes([key_u[8 * g:8 * g + 8, :] for g in range(32)])):
                plane_ref[p, j0 + u] = plane

    def p1(g, c):
        score_tiles(4 * g, 4, False)
        return c

    lax.fori_loop(0, i // 4, p1, 0)
    lax.switch(i % 4, [functools.partial(score_tiles, i - i % 4, r + 1, True) for r in range(4)])

    def popcount_rows(words):
        per_tile = jnp.sum(lax.population_count(words), axis=0)
        return jnp.sum(per_tile.astype(F32), axis=0, keepdims=True)

    def bis_body(p, c):
        alive, n_gt, thr_u = c
        plane = plane_ref[p] ^ jnp.where(p == 0, -1, 0)
        ones = alive & plane
        cnt = popcount_rows(ones)
        take = n_gt + cnt >= top_k
        alive = jnp.where(take, ones, alive ^ ones)
        n_gt = jnp.where(take, n_gt, n_gt + cnt)
        thr_u = jnp.where(take, thr_u | (jnp.int32(1) << (31 - p)), thr_u)
        return alive, n_gt, thr_u

    n_tiles = key_ref.shape[0]
    alive0 = jnp.where(lax.broadcasted_iota(I32, (n_tiles, 8, t), 0) <= i, -1, 0)
    alive, n_gt, thr_u = lax.fori_loop(
        0, 32, bis_body, (alive0, jnp.zeros((1, t), F32), jnp.zeros((1, t), I32)))
    thr = jnp.maximum(thr_u ^ INT_MIN, INT_MIN + 1)
    n_avail = (i * t + lax.broadcasted_iota(I32, (1, t), 1) + 1).astype(F32)
    n_ge = jnp.where(n_avail > top_k, n_gt + popcount_rows(alive), 0.0)
    surplus = jnp.max(n_ge) > top_k

    def tie_pass():
        need = top_k - n_gt

        def tb(j, seen):
            k = key_ref[j]
            eq = k == thr
            eqf = jnp.where(eq, 1.0, 0.0)
            before = jnp.dot(lstrict_ref[...], eqf.astype(BF16), preferred_element_type=F32) + seen
            sel = (k > thr) | (eq & (before < need))
            key_ref[j] = jnp.where(sel, 1, INT_MIN)
            return seen + jnp.sum(eqf, axis=0, keepdims=True)

        lax.fori_loop(0, i + 1, tb, jnp.zeros((1, t), F32))
        return jnp.zeros((1, t), I32)

    thr = lax.cond(surplus, tie_pass, lambda: thr)

    qs = head_stack(q_ref[0])

    def attn_tiles(j0, n, carry, near):
        m, acc = carry
        st = _dot_nt(key_rows(kk_ref, j0, n), qs)
        selb = jnp.concatenate([jnp.where(key_ref[j0 + u] >= thr, 0.0, NEG_INF) for u in range(n)], axis=0)
        parts = []
        for h in range(4):
            sh = st[:, h * t:(h + 1) * t] + selb
            if near:
                biased = [sh[(n - near + v) * t:(n - near + v + 1) * t] + bias_ref[2 - near + v, h]
                          for v in range(near)]
                sh = jnp.concatenate(([sh[:(n - near) * t]] if n > near else []) + biased, axis=0)
            parts.append(sh)
        st = jnp.concatenate(parts, axis=1)
        m_new = jnp.maximum(m, jnp.max(st, axis=0, keepdims=True))
        alpha = jnp.exp2(m - m_new)
        pb = jnp.exp2(st - m_new).astype(BF16)
        vt = _with_ones_rows(jnp.concatenate([vt_ref[0, j0 + u] for u in range(n)], axis=1))
        acc = alpha * acc + jnp.dot(vt, pb, preferred_element_type=F32)
        return m_new, acc

    init = (jnp.full((1, 4 * t), M_INIT, F32), jnp.zeros((HEAD_DIM + ONES_ROWS, 4 * t), F32))
    far = jnp.maximum(i - 1, 0)
    carry = lax.fori_loop(0, far // 4, lambda g, c: attn_tiles(4 * g, 4, c, 0), init)
    rest = far - far % 4
    last = [lambda c: attn_tiles(0, 1, c, 1)]
    last += [functools.partial(lambda c, n: attn_tiles(rest, n, c, 2), n=r + 2) for r in range(4)]
    _, acc = lax.switch(jnp.where(i == 0, 0, 1 + far % 4), last, carry)
    ot = acc[:HEAD_DIM] / acc[HEAD_DIM:HEAD_DIM + 1]
    ot = jnp.concatenate([ot[:, h * t:(h + 1) * t] for h in range(4)], axis=0)
    o_ref[0] = ot.T.astype(BF16)


def _dsa_bias(t5_table, t):
    assert t + 1 >= T5_MAX_DISTANCE
    k = jnp.arange(t)[:, None]
    q = jnp.arange(t)[None, :]
    far = t5_table[T5_BUCKETS - 1, 4:].astype(F32)
    tiles = []
    for off in (t, 0):
        dist = off + q - k
        b = jnp.transpose(_t5_lookup(t5_table, dist)[..., 4:], (2, 0, 1)) - far[:, None, None]
        tiles.append(jnp.where((dist >= 0)[None], b, 0.0))
    return jnp.stack(tiles)


def _dsa(dq, dkk, dvv, iq, ikk, wt, bias, top_k, t):
    b, s, _ = dq.shape
    nt = s // t
    lstrict = (jnp.arange(t)[:, None] > jnp.arange(t)[None, :]).astype(BF16)
    vt = jnp.transpose(dvv[:, :, :HEAD_DIM].reshape(b, nt, t, HEAD_DIM), (0, 1, 3, 2))
    blk = lambda w: pl.BlockSpec((1, t, w), lambda bi, i: (bi, i, 0))
    seq = lambda w: pl.BlockSpec((1, s, w), lambda bi, i: (bi, 0, 0))
    return pl.pallas_call(
        functools.partial(_dsa_kernel, t=t, top_k=top_k),
        grid=(b, nt),
        in_specs=[blk(256), seq(LANES),
                  pl.BlockSpec((1, nt, HEAD_DIM, t), lambda bi, i: (bi, 0, 0, 0)),
                  blk(256), seq(LANES),
                  pl.BlockSpec((1, 8, t), lambda bi, i: (bi, 0, i)),
                  pl.BlockSpec(bias.shape, lambda bi, i: (0, 0, 0, 0)),
                  pl.BlockSpec((t, t), lambda bi, i: (0, 0))],
        out_specs=blk(256),
        out_shape=jax.ShapeDtypeStruct((b, s, 256), BF16),
        scratch_shapes=[pltpu.VMEM((nt, t, t), I32), pltpu.VMEM((32, nt, 8, t), I32)],
        compiler_params=_cparams("arbitrary", "arbitrary"),
        name="dsa",
    )(dq, dkk, vt, iq, ikk, wt, bias, lstrict)


def _merge_kernel(x_ref, gm_ref, wg_ref, oa_ref, of_ref, os_ref, od_ref, wb_ref, wo_ref, gf_ref,
                  wr_ref, br_ref, ltri_ref, xo_ref, h2_ref, route_ref, cnt_ref, carry_ref):
    @pl.when(pl.program_id(0) == 0)
    def _():
        carry_ref[...] = jnp.zeros_like(carry_ref)

    x = x_ref[...]
    hb = _rms(x, gm_ref[...]).astype(BF16)
    d = x.shape[1]
    merged = None
    for bi, o_ref in enumerate((oa_ref, of_ref, os_ref, od_ref)):
        gate = jax.nn.sigmoid(jnp.dot(hb, wg_ref[:, bi * d:(bi + 1) * d], preferred_element_type=F32))
        term = gate * jnp.dot(o_ref[...], wb_ref[bi], preferred_element_type=F32)
        merged = term if merged is None else merged + term
    xn = x + jnp.dot(merged.astype(BF16), wo_ref[...], preferred_element_type=F32)
    xo_ref[...] = xn
    h2 = _rms(xn, gf_ref[...])
    h2_ref[...] = h2

    logits = _dot_x3(h2, wr_ref[0], wr_ref[1]) + br_ref[...]
    lane = lax.broadcasted_iota(I32, logits.shape, 1).astype(F32)
    big = 1e9
    gl = jnp.where(lane < N_GROUPS, logits, -jnp.inf)
    gmax = jnp.max(gl, axis=1, keepdims=True)
    grp = jnp.min(jnp.where(gl == gmax, lane, big), axis=1, keepdims=True)
    p_grp = 1.0 / jnp.sum(jnp.exp(gl - gmax), axis=1, keepdims=True)
    first = N_GROUPS + grp * EXPERTS_PER_GROUP
    el = jnp.where((lane >= first) & (lane < first + EXPERTS_PER_GROUP), logits, -jnp.inf)
    l1 = jnp.max(el, axis=1, keepdims=True)
    i1 = jnp.min(jnp.where(el == l1, lane, big), axis=1, keepdims=True)
    el2 = jnp.where(lane == i1, -jnp.inf, el)
    l2 = jnp.max(el2, axis=1, keepdims=True)
    i2 = jnp.min(jnp.where(el2 == l2, lane, big), axis=1, keepdims=True)
    e2 = jnp.exp(l2 - l1)
    g1 = p_grp / (1.0 + e2)
    g2 = p_grp * e2 / (1.0 + e2)
    e_first, e_second = i1 - N_GROUPS, i2 - N_GROUPS
    oh0 = jnp.where(lane == e_first, 1.0, 0.0)
    oh1 = jnp.where(lane == e_second, 1.0, 0.0)
    both = oh0 + oh1
    before = jnp.dot(ltri_ref[...], both.astype(BF16), preferred_element_type=F32) + carry_ref[0:1, :]
    r0 = jnp.sum(oh0 * before, axis=1, keepdims=True)
    r1 = jnp.sum(oh1 * (before + oh0), axis=1, keepdims=True)
    total = carry_ref[0:1, :] + jnp.sum(both, axis=0, keepdims=True)
    carry_ref[0:1, :] = total
    cnt_ref[...] = jnp.broadcast_to(total, cnt_ref.shape)
    cols = (e_first, e_second, g1, g2, r0, r1)
    route = jnp.zeros_like(logits)
    for k, col in enumerate(cols):
        route = jnp.where(lane == k, col, route)
    route_ref[...] = route


def _merge(x2, gm, wg, o_a, o_f, o_s, o_d, wb, wo, gf, wr, br, tm):
    n, d = x2.shape
    row = lambda w: pl.BlockSpec((tm, w), lambda i: (i, 0))
    full = lambda a: pl.BlockSpec(a.shape, lambda i: (0,) * a.ndim, pipeline_mode=pl.Buffered(1))
    ltri = (jnp.arange(tm)[:, None] > jnp.arange(tm)[None, :]).astype(BF16)
    return pl.pallas_call(
        _merge_kernel,
        grid=(n // tm,),
        in_specs=[row(d), full(gm), full(wg), row(256), row(256), row(256), row(256),
                  full(wb), full(wo), full(gf), full(wr), full(br), full(ltri)],
        out_specs=[row(d), row(d), row(LANES), pl.BlockSpec((8, LANES), lambda i: (0, 0))],
        out_shape=[jax.ShapeDtypeStruct((n, d), F32), jax.ShapeDtypeStruct((n, d), F32),
                   jax.ShapeDtypeStruct((n, LANES), F32), jax.ShapeDtypeStruct((8, LANES), F32)],
        scratch_shapes=[pltpu.VMEM((8, LANES), F32)],
        compiler_params=_cparams("arbitrary"),
        name="merge",
    )(x2, gm, wg, o_a, o_f, o_s, o_d, wb, wo, gf, wr, br, ltri)


def _expert_kernel(be_ref, nu_ref, tok_ref, tok1_ref, tok2_ref, h_hbm, wup_ref, wdn_ref, y_ref,
                   xbuf, sem, wup_b, wdn_b, *, te):
    b = pl.program_id(0)
    n_used = nu_ref[0]
    slot = b % 3

    def start_row(tokens_ref, dst, r):
        pltpu.make_async_copy(h_hbm.at[pl.ds(tokens_ref[0, 0, r], 1), :],
                              xbuf.at[dst, pl.ds(r, 1), :], sem.at[dst]).start()

    def start_block(tokens_ref, dst):
        def issue(r, c):
            start_row(tokens_ref, dst, r)
            return c
        lax.fori_loop(0, te, issue, 0, unroll=8)

    def block(prefetch):
        pltpu.make_async_copy(h_hbm.at[pl.ds(0, te), :], xbuf.at[slot], sem.at[slot]).wait()
        xb = xbuf[slot].astype(BF16)
        if prefetch:
            dst = (b + 2) % 3
            for r in range(te):
                start_row(tok2_ref, dst, r)
        gu = jnp.dot(xb, wup_b[...], preferred_element_type=F32)
        g = gu[:, :EXPERT_FF]
        act = g * jax.nn.sigmoid(g) * gu[:, EXPERT_FF:]
        y_ref[...] = jnp.dot(act.astype(BF16), wdn_b[...], preferred_element_type=F32)

    @pl.when((b < n_used) & ((b == 0) | (be_ref[b] != be_ref[jnp.maximum(b - 1, 0)])))
    def _():
        wup_b[...] = wup_ref[0, 0].astype(BF16)
        wdn_b[...] = wdn_ref[0, 0].astype(BF16)

    @pl.when((b == 0) & (n_used > 0))
    def _():
        start_block(tok_ref, 0)

    @pl.when((b == 0) & (n_used > 1))
    def _():
        start_block(tok1_ref, 1)

    @pl.when(b + 2 < n_used)
    def _():
        block(True)

    @pl.when((b < n_used) & (b + 2 >= n_used))
    def _():
        block(False)

    @pl.when(b >= n_used)
    def _():
        y_ref[...] = jnp.zeros_like(y_ref)


def _experts(blk_expert, n_used, slot_tok, h2, w_up, w_down, layer, te):
    n_blocks = blk_expert.shape[0]
    d = h2.shape[1]
    ahead = lambda k: pl.BlockSpec((1, 1, te), lambda b, be, nu: (jnp.minimum(b + k, n_blocks - 1), 0, 0),
                                   memory_space=pltpu.SMEM)
    grid_spec = pltpu.PrefetchScalarGridSpec(
        num_scalar_prefetch=2,
        grid=(n_blocks,),
        in_specs=[ahead(0), ahead(1), ahead(2),
                  pl.BlockSpec(memory_space=pl.ANY),
                  pl.BlockSpec((1, 1, d, 2 * EXPERT_FF), lambda b, be, nu: (layer, be[b], 0, 0)),
                  pl.BlockSpec((1, 1, EXPERT_FF, d), lambda b, be, nu: (layer, be[b], 0, 0))],
        out_specs=pl.BlockSpec((te, d), lambda b, be, nu: (b, 0)),
        scratch_shapes=[pltpu.VMEM((3, te, d), F32), pltpu.SemaphoreType.DMA((3,)),
                        pltpu.VMEM((d, 2 * EXPERT_FF), BF16), pltpu.VMEM((EXPERT_FF, d), BF16)],
    )
    slots = slot_tok.reshape(n_blocks, 1, te)
    return pl.pallas_call(
        functools.partial(_expert_kernel, te=te),
        grid_spec=grid_spec,
        out_shape=jax.ShapeDtypeStruct((n_blocks * te, d), F32),
        compiler_params=_cparams("arbitrary"),
        name="moe_experts",
    )(blk_expert, n_used, slots, slots, slots, h2, w_up, w_down)


def _combine_kernel(pos_ref, pos_next_ref, x_ref, route_ref, y_hbm, o_ref, ybuf, sem, *, tc):
    i = pl.program_id(0)
    slot = i % 2

    def gather(rows_ref, dst):
        for r in range(2 * tc):
            pltpu.make_async_copy(y_hbm.at[pl.ds(rows_ref[0, 0, r], 1), :],
                                  ybuf.at[dst, pl.ds(r, 1), :], sem.at[dst]).start()

    @pl.when(i == 0)
    def _():
        gather(pos_ref, 0)

    @pl.when(i + 1 < pl.num_programs(0))
    def _():
        gather(pos_next_ref, 1 - slot)

    pltpu.make_async_copy(y_hbm.at[pl.ds(0, 2 * tc), :], ybuf.at[slot], sem.at[slot]).wait()
    route = route_ref[...]
    o_ref[...] = (x_ref[...] + route[:, 2:3] * ybuf[slot, 0:tc, :]
                  + route[:, 3:4] * ybuf[slot, tc:2 * tc, :])


def _combine(pos, x2, route, yb, tc):
    n, d = x2.shape
    nt = n // tc
    pos_t = jnp.transpose(pos.reshape(nt, tc, 2), (0, 2, 1)).reshape(nt, 1, 2 * tc)
    return pl.pallas_call(
        functools.partial(_combine_kernel, tc=tc),
        grid=(nt,),
        in_specs=[pl.BlockSpec((1, 1, 2 * tc), lambda i: (i, 0, 0), memory_space=pltpu.SMEM),
                  pl.BlockSpec((1, 1, 2 * tc), lambda i: (jnp.minimum(i + 1, nt - 1), 0, 0),
                               memory_space=pltpu.SMEM),
                  pl.BlockSpec((tc, d), lambda i: (i, 0)),
                  pl.BlockSpec((tc, LANES), lambda i: (i, 0)),
                  pl.BlockSpec(memory_space=pl.ANY)],
        out_specs=pl.BlockSpec((tc, d), lambda i: (i, 0)),
        out_shape=jax.ShapeDtypeStruct((n, d), F32),
        scratch_shapes=[pltpu.VMEM((2, 2 * tc, d), F32), pltpu.SemaphoreType.DMA((2,))],
        compiler_params=_cparams("arbitrary"),
        name="moe_combine",
    )(pos_t, pos_t, x2, route, yb)


def _swap_mid_heads(w, axis):
    h = jnp.split(w, 4, axis=axis)
    return jnp.concatenate([h[0], h[2], h[1], h[3]], axis=axis)


def _t5_lookup(t5_table, dist):
    onehot = (_t5_bucket(dist)[..., None] == jnp.arange(T5_BUCKETS)).astype(F32)
    return jnp.einsum("...b,bh->...h", onehot, t5_table.astype(F32), precision=lax.Precision.HIGHEST)


def _t5_bucket(dist):
    n = jnp.maximum(dist, 0)
    max_exact = T5_BUCKETS // 2
    nf = jnp.maximum(n, 1).astype(F32)
    large = max_exact + (jnp.log(nf / max_exact) / math.log(T5_MAX_DISTANCE / max_exact)
                         * (T5_BUCKETS - max_exact)).astype(I32)
    large = jnp.minimum(large, T5_BUCKETS - 1)
    return jnp.where(n < max_exact, n, large)


def _swa_bias(t5_table):
    t = SWA_BLOCK
    dist = t + jnp.arange(t)[None, :] - jnp.arange(2 * t)[:, None]
    tile = jnp.where(((dist >= 0) & (dist < t))[..., None], _t5_lookup(t5_table, dist)[..., :4], NEG_INF)
    return jnp.concatenate([tile[..., h] for h in (0, 2, 1, 3)], axis=1)


def _layer_weights(w_in, qk_gain, forget_bias, w_branch):
    offs = np.concatenate([[0], np.cumsum(IN_SPLITS)]).tolist()
    part = lambda k: w_in[:, offs[k]:offs[k + 1]]
    dup = lambda w: jnp.concatenate([w, w], axis=1)
    aq = _swap_mid_heads(part(0), 1)
    cols = [aq, part(1), part(2), part(3), part(4), part(5), part(7), part(8), part(9),
            part(10), dup(part(11)), dup(part(12)), part(13), dup(part(14))]
    w1 = jnp.concatenate(cols, axis=1).astype(BF16)
    d = w_in.shape[0]
    wm = _hi_lo(jnp.concatenate([part(6), part(15), jnp.zeros((d, LANES - 8), F32)], axis=1))
    wg = part(16).astype(BF16)
    tile = lambda g, reps, scale: jnp.pad(jnp.tile(g, reps) * scale, (0, 256 - reps * HEAD_DIM))
    gains = jnp.stack([tile(qk_gain[0, 0], 4, ATTN_SCALE), tile(qk_gain[0, 1], 2, 1.0),
                       tile(qk_gain[1, 0], 4, ATTN_SCALE * LOG2E), tile(qk_gain[1, 1], 4, 1.0),
                       tile(qk_gain[2, 0], 4, ATTN_SCALE * LOG2E), tile(qk_gain[2, 1], 2, 1.0),
                       jnp.zeros((256,), F32), jnp.zeros((256,), F32)]).astype(F32)
    fb = jnp.pad(forget_bias.astype(F32), (0, LANES - 4)).reshape(1, LANES)
    wb = jnp.stack([_swap_mid_heads(w_branch[0], 0), w_branch[1], w_branch[2], w_branch[3]]).astype(BF16)
    return w1, wm, wg, gains, fb, wb


def kernel(x, norm_mix_g, w_in, forget_bias, attn_sinks, qk_gain, w_branch, w_out, t5_table, norm_ffn_g,
           w_router_group, b_router_group, w_router_expert, b_router_expert, w_expert_up, w_expert_down):
    b, s, d = x.shape
    n = b * s
    depth = w_in.shape[0]
    top_k = min(DSA_TOPK_MAX, s // 4)
    tm_proj = min(512, s)
    fox_t = min(256, s)
    sb_t = min(256, s)
    dsa_t = min(256, s)
    te = 256
    tc = 256

    gseg = (jnp.arange(256)[:, None] // HEAD_DIM == jnp.arange(256)[None, :] // HEAD_DIM).astype(BF16)
    bias_swa = _swa_bias(t5_table)
    bias_dsa = _dsa_bias(t5_table, dsa_t) * LOG2E
    n_blocks = -(-2 * n // te) + N_EXPERTS
    tok_ids = jnp.repeat(jnp.arange(n, dtype=I32), 2)

    for layer in range(depth):
        w1, wm, wg, gains, fb, wb = _layer_weights(w_in[layer], qk_gain[layer], forget_bias[layer],
                                                   w_branch[layer])
        sink_row = jnp.repeat(attn_sinks[layer].astype(F32)[jnp.array([0, 2, 1, 3])], SWA_BLOCK)
        sinks = jnp.broadcast_to(sink_row[None, :], (8, 4 * SWA_BLOCK))
        (aq, ak, av, fq, fk, fv, sq, sk, sv, dq, dkk, dvv, iq, ikk, cm) = _proj(
            x, norm_mix_g[layer].reshape(1, d), w1, wm, gseg, gains, fb, tm_proj)

        o_swa = _swa(aq, ak, av, bias_swa, sinks, min(4, s // SWA_BLOCK))
        cmt = jnp.transpose(cm[:, :, :8], (0, 2, 1))
        o_fox = _fox(fq, fk, fv, cmt, cm, fox_t)
        o_sb = _sb(sq, sk, sv, sb_t)
        o_dsa = _dsa(dq, dkk, dvv, iq, ikk, cmt, bias_dsa, top_k, dsa_t)

        wr = _hi_lo(jnp.concatenate([w_router_group[layer], w_router_expert[layer],
                                     jnp.zeros((d, LANES - N_GROUPS - N_EXPERTS), F32)], axis=1))
        br = jnp.concatenate([b_router_group[layer], b_router_expert[layer],
                              jnp.zeros((LANES - N_GROUPS - N_EXPERTS,), F32)]).reshape(1, LANES)
        x2, h2, route, cnt = _merge(
            x.reshape(n, d), norm_mix_g[layer].reshape(1, d), wg,
            o_swa.reshape(n, 256), o_fox.reshape(n, 256), o_sb.reshape(n, 256), o_dsa.reshape(n, 256),
            wb, w_out[layer].astype(BF16), norm_ffn_g[layer].reshape(1, d), wr, br, min(512, n))

        counts = cnt[0, :N_EXPERTS].astype(I32)
        padded = (counts + te - 1) // te * te
        pend = jnp.cumsum(padded)
        pstart = pend - padded
        expert = route[:, :2].astype(I32)
        own = expert[:, :, None] == jnp.arange(N_EXPERTS, dtype=I32)
        pos = jnp.sum(jnp.where(own, pstart, 0), axis=-1) + route[:, 4:6].astype(I32)
        slot_tok = jnp.zeros((n_blocks * te,), I32).at[pos.reshape(-1)].set(
            tok_ids, unique_indices=True, mode="promise_in_bounds")
        blk_start = jnp.arange(n_blocks, dtype=I32)[:, None] * te
        blk_expert = jnp.minimum(jnp.sum((pend[None, :] <= blk_start).astype(I32), axis=1), N_EXPERTS - 1)
        n_used = (pend[-1:] // te).astype(I32)

        yb = _experts(blk_expert, n_used, slot_tok, h2, w_expert_up, w_expert_down, layer, te)
        x = _combine(pos, x2, route, yb, tc).reshape(b, s, d)
    return x
```

```python
import functools
import math

import jax
import jax.numpy as jnp
import numpy as np
from jax import lax
from jax.experimental import pallas as pl
from jax.experimental.pallas import tpu as pltpu

F32 = jnp.float32
BF16 = jnp.bfloat16
I32 = jnp.int32

HEAD_DIM = 64
LANES = 128
NORM_EPS = 1e-6
NEG_INF = -1e30
M_INIT = -1e29
ATTN_SCALE = HEAD_DIM ** -0.5
LOG2E = math.log2(math.e)
SWA_BLOCK = 128
IDX_SCALE = 64 ** -0.5
IDX_HEADS = 4
DSA_TOPK_MAX = 256
T5_BUCKETS = 32
T5_MAX_DISTANCE = 128
N_GROUPS = 4
EXPERTS_PER_GROUP = 8
N_EXPERTS = N_GROUPS * EXPERTS_PER_GROUP
EXPERT_FF = 512
SB_DEAD = -110.0
INT_MIN = -2 ** 31
ONES_ROWS = 16
VMEM_LIMIT = 56 * 1024 * 1024

IN_SPLITS = (256, 128, 128, 256, 256, 256, 4, 256, 256, 256, 256, 64, 64, 256, 64, 4, 4096)

_SEG = dict(aq=(0, 256), ak=(256, 128), av=(384, 128), fq=(512, 256), fk=(768, 256), fv=(1024, 256),
            sq=(1280, 256), sk=(1536, 256), sv=(1792, 256), dq=(2048, 256), dkk=(2304, 128),
            dvv=(2432, 128), iq=(2560, 256), ikk=(2816, 128))
_SEG_ORDER = ("aq", "ak", "av", "fq", "fk", "fv", "sq", "sk", "sv", "dq", "dkk", "dvv", "iq", "ikk")


def _cparams(*sem):
    return pltpu.CompilerParams(dimension_semantics=sem, vmem_limit_bytes=VMEM_LIMIT)


def _rms(x, g):
    return x * lax.rsqrt(jnp.mean(x * x, axis=-1, keepdims=True) + NORM_EPS) * g


def _log_sigmoid(z):
    return jnp.minimum(z, 0.0) - jnp.log(1.0 + jnp.exp(-jnp.abs(z)))


def _dot_nt(a, b):
    return lax.dot_general(a, b, (((1,), (1,)), ((), ())), preferred_element_type=F32)


def _with_ones_rows(vt):
    ones = jnp.ones(vt.shape[:-2] + (ONES_ROWS, vt.shape[-1]), vt.dtype)
    return jnp.concatenate([vt, ones], axis=-2)


def _split3(x):
    p1 = x.astype(BF16)
    r = x - p1.astype(F32)
    p2 = r.astype(BF16)
    return p1, p2, (r - p2.astype(F32)).astype(BF16)


def _hi_lo(w):
    hi = w.astype(BF16)
    return jnp.stack([hi, (w - hi.astype(F32)).astype(BF16)])


def _dot_x3(a, b_hi, b_lo):
    a_hi = a.astype(BF16)
    a_lo = (a - a_hi.astype(F32)).astype(BF16)
    return (jnp.dot(a_hi, b_hi, preferred_element_type=F32) + jnp.dot(a_lo, b_hi, preferred_element_type=F32)
            + jnp.dot(a_hi, b_lo, preferred_element_type=F32))


def _split_heads(qp):
    lo = lax.broadcasted_iota(I32, (1, LANES), 1) < HEAD_DIM
    zero = jnp.zeros_like(qp)
    return jnp.concatenate([jnp.where(lo, qp, zero), jnp.where(lo, zero, qp)], axis=0)


def _proj_kernel(x_ref, g_ref, w1_ref, wm_ref, gseg_ref, gains_ref, fb_ref, ltri_ref, *rest):
    outs = dict(zip(_SEG_ORDER, rest[:len(_SEG_ORDER)]))
    cm_ref = rest[len(_SEG_ORDER)]
    carry_ref = rest[len(_SEG_ORDER) + 1]

    @pl.when(pl.program_id(1) == 0)
    def _():
        carry_ref[...] = jnp.zeros_like(carry_ref)

    h = _rms(x_ref[0], g_ref[...])
    hb = h.astype(BF16)

    def seg(name):
        off, width = _SEG[name]
        return jnp.dot(hb, w1_ref[:, off:off + width], preferred_element_type=F32)

    def head_norm(t, row):
        width = t.shape[1]
        ssq = jnp.dot((t * t).astype(BF16), gseg_ref[:width, :width], preferred_element_type=F32)
        return t * lax.rsqrt(ssq * (1.0 / HEAD_DIM) + NORM_EPS) * gains_ref[row:row + 1, :width]

    normed = dict(aq=0, ak=1, fq=2, fk=3, dq=4, dkk=5)
    scaled = dict(sq=ATTN_SCALE * LOG2E, iq=IDX_SCALE)
    for name in _SEG_ORDER:
        t = seg(name)
        if name in normed:
            t = head_norm(t, normed[name])
        elif name in scaled:
            t = t * scaled[name]
        outs[name][0] = t.astype(BF16)

    misc = _dot_x3(h, wm_ref[0], wm_ref[1])
    lane = lax.broadcasted_iota(I32, misc.shape, 1)
    logf = jnp.where(lane < 4, _log_sigmoid(misc + fb_ref[...]), 0.0)
    ltri = ltri_ref[...]
    c = carry_ref[0:1, :]
    for piece in _split3(logf):
        c = c + jnp.dot(ltri, piece, preferred_element_type=F32)
    tm = misc.shape[0]
    carry_ref[0:1, :] = c[tm - 1:tm, :]
    cm_ref[0] = jnp.where(lane < 4, c, misc)


def _proj(x, g, w1, wm, gseg, gains, fb, tm):
    b, s, d = x.shape
    ltri = jnp.tril(jnp.ones((tm, tm), BF16))
    full = lambda shape: pl.BlockSpec(shape, lambda bi, si: (0,) * len(shape))
    out_shapes = [jax.ShapeDtypeStruct((b, s, _SEG[n][1]), BF16) for n in _SEG_ORDER]
    out_shapes.append(jax.ShapeDtypeStruct((b, s, LANES), F32))
    out_specs = [pl.BlockSpec((1, tm, _SEG[n][1]), lambda bi, si: (bi, si, 0)) for n in _SEG_ORDER]
    out_specs.append(pl.BlockSpec((1, tm, LANES), lambda bi, si: (bi, si, 0)))
    return pl.pallas_call(
        _proj_kernel,
        grid=(b, s // tm),
        in_specs=[pl.BlockSpec((1, tm, d), lambda bi, si: (bi, si, 0)),
                  full((1, d)), full(w1.shape), full(wm.shape), full(gseg.shape),
                  full(gains.shape), full(fb.shape), full((tm, tm))],
        out_specs=out_specs,
        out_shape=out_shapes,
        scratch_shapes=[pltpu.VMEM((8, LANES), F32)],
        compiler_params=_cparams("arbitrary", "arbitrary"),
        name="proj",
    )(x, g, w1, wm, gseg, gains, fb, ltri)


def _swa_kernel(q_ref, kp_ref, kc_ref, vtp_ref, vtc_ref, bias_ref, sink_ref, o_ref, *, nsub):
    i = pl.program_id(1)
    t = SWA_BLOCK
    no_prev = (lax.broadcasted_iota(I32, (2 * t, 4 * t), 0) < t) & (i == 0)
    sink = sink_ref[0:1, :]
    for u in range(nsub):
        rows = slice(u * t, (u + 1) * t)
        q = q_ref[0, rows, :]
        qs = jnp.concatenate([_split_heads(q[:, :LANES]), _split_heads(q[:, LANES:])], axis=0)
        if u == 0:
            kcat = jnp.concatenate([kp_ref[0], kc_ref[0, rows, :]], axis=0)
            vt = jnp.concatenate([vtp_ref[0], vtc_ref[0, :, rows]], axis=1)
        else:
            kcat = kc_ref[0, (u - 1) * t:(u + 1) * t, :]
            vt = vtc_ref[0, :, (u - 1) * t:(u + 1) * t]
        st = _dot_nt(kcat, qs) + bias_ref[...]
        if u == 0:
            st = jnp.where(no_prev, NEG_INF, st)
        m = jnp.maximum(jnp.max(st, axis=0, keepdims=True), sink)
        p = jnp.exp(st - m)
        denom = jnp.sum(p, axis=0, keepdims=True) + jnp.exp(sink - m)
        ot = jnp.dot(vt, p.astype(BF16), preferred_element_type=F32) / denom
        ot = jnp.concatenate([ot[:HEAD_DIM, 0:t], ot[HEAD_DIM:, t:2 * t],
                              ot[:HEAD_DIM, 2 * t:3 * t], ot[HEAD_DIM:, 3 * t:]], axis=0)
        o_ref[0, rows, :] = ot.T.astype(BF16)


def _swa(aq, ak, av, bias, sinks, nsub):
    b, s, _ = aq.shape
    t = SWA_BLOCK
    avt = jnp.transpose(av, (0, 2, 1))
    cur = lambda bi, i: (bi, i, 0)
    prev = lambda bi, i: (bi, jnp.maximum(i * nsub - 1, 0), 0)
    return pl.pallas_call(
        functools.partial(_swa_kernel, nsub=nsub),
        grid=(b, s // (t * nsub)),
        in_specs=[pl.BlockSpec((1, t * nsub, 256), cur),
                  pl.BlockSpec((1, t, LANES), prev), pl.BlockSpec((1, t * nsub, LANES), cur),
                  pl.BlockSpec((1, LANES, t), lambda bi, i: (bi, 0, jnp.maximum(i * nsub - 1, 0))),
                  pl.BlockSpec((1, LANES, t * nsub), lambda bi, i: (bi, 0, i)),
                  pl.BlockSpec(bias.shape, lambda bi, i: (0, 0)),
                  pl.BlockSpec(sinks.shape, lambda bi, i: (0, 0))],
        out_specs=pl.BlockSpec((1, t * nsub, 256), cur),
        out_shape=jax.ShapeDtypeStruct((b, s, 256), BF16),
        compiler_params=_cparams("arbitrary", "arbitrary"),
        name="swa",
    )(aq, ak, ak, avt, avt, bias, sinks)


def _fox_kernel(q_ref, k_ref, vt_ref, ct_ref, ccol_ref, o_ref, ckb_ref, *, t):
    i = pl.program_id(1)
    n_tiles = ckb_ref.shape[1] // t

    @pl.when(i == 0)
    def _():
        def fill(j, c):
            rows = pl.ds(pl.multiple_of(j * t, t), t)
            cc = ccol_ref[0, rows, :] * LOG2E
            for h in range(4):
                ckb_ref[h, rows, :] = jnp.broadcast_to(cc[:, h:h + 1], (t, LANES))
            return c
        lax.fori_loop(0, n_tiles, fill, 0)

    q = q_ref[0]
    ct = ct_ref[0] * LOG2E
    qs = [_split_heads(q[:, :LANES]), _split_heads(q[:, LANES:])]
    valid = lax.broadcasted_iota(I32, (t, t), 0) <= lax.broadcasted_iota(I32, (t, t), 1)

    def tiles(j0, n, carry, masked):
        m, accs = carry
        rows = pl.ds(pl.multiple_of(j0 * t, t), n * t)
        cols = []
        for pair in range(2):
            st = _dot_nt(k_ref[0, rows, pair * LANES:(pair + 1) * LANES], qs[pair])
            for hh in range(2):
                head = 2 * pair + hh
                ck = ckb_ref[head, rows, :]
                for c in range(t // LANES):
                    cs = slice(c * LANES, (c + 1) * LANES)
                    sh = st[:, hh * t + c * LANES:hh * t + (c + 1) * LANES] + (ct[head:head + 1, cs] - ck)
                    if masked:
                        last = jnp.where(valid[:, cs], sh[(n - 1) * t:], NEG_INF)
                        sh = jnp.concatenate([sh[:(n - 1) * t], last], axis=0) if n > 1 else last
                    cols.append(sh)
        st = jnp.concatenate(cols, axis=1)
        m_new = jnp.maximum(m, jnp.max(st, axis=0, keepdims=True))
        alpha = jnp.exp2(m - m_new)
        pb = jnp.exp2(st - m_new).astype(BF16)
        new = []
        for pair in range(2):
            lanes = slice(pair * 2 * t, (pair + 1) * 2 * t)
            vt = _with_ones_rows(jnp.concatenate(
                [vt_ref[0, j0 + u, pair * LANES:(pair + 1) * LANES, :] for u in range(n)], axis=1))
            pv = jnp.dot(vt, pb[:, lanes], preferred_element_type=F32)
            new.append(alpha[:, lanes] * accs[pair] + pv)
        return m_new, tuple(new)

    zero_acc = jnp.zeros((LANES + ONES_ROWS, 2 * t), F32)
    carry = lax.fori_loop(0, i // 4, lambda g, c: tiles(4 * g, 4, c, False),
                          (jnp.full((1, 4 * t), M_INIT, F32), (zero_acc, zero_acc)))
    rest = i - i % 4
    _, accs = lax.switch(i % 4, [functools.partial(lambda c, n: tiles(rest, n, c, True), n=r + 1)
                                 for r in range(4)], carry)
    outs = []
    for pair in range(2):
        o = accs[pair][:LANES] / accs[pair][LANES:LANES + 1]
        outs.append(jnp.concatenate([o[:HEAD_DIM, :t], o[HEAD_DIM:, t:]], axis=0))
    o_ref[0] = jnp.concatenate(outs, axis=0).T.astype(BF16)


def _fox(fq, fk, fv, ct, ccol, t):
    b, s, _ = fq.shape
    nt = s // t
    vt = jnp.transpose(fv.reshape(b, nt, t, 256), (0, 1, 3, 2))
    return pl.pallas_call(
        functools.partial(_fox_kernel, t=t),
        grid=(b, nt),
        in_specs=[pl.BlockSpec((1, t, 256), lambda bi, i: (bi, i, 0)),
                  pl.BlockSpec((1, s, 256), lambda bi, i: (bi, 0, 0)),
                  pl.BlockSpec((1, nt, 256, t), lambda bi, i: (bi, 0, 0, 0)),
                  pl.BlockSpec((1, 8, t), lambda bi, i: (bi, 0, i)),
                  pl.BlockSpec((1, s, LANES), lambda bi, i: (bi, 0, 0))],
        out_specs=pl.BlockSpec((1, t, 256), lambda bi, i: (bi, i, 0)),
        out_shape=jax.ShapeDtypeStruct((b, s, 256), BF16),
        scratch_shapes=[pltpu.VMEM((4, s, LANES), F32)],
        compiler_params=_cparams("arbitrary", "arbitrary"),
        name="fox",
    )(fq, fk, vt, ct, ccol)


def _sb_kernel(q_ref, k_ref, vt_ref, lgt_ref, o_ref, *, t):
    i = pl.program_id(1)
    q = q_ref[0]
    lgt = lgt_ref[...]
    qs = [_split_heads(q[:, :LANES]), _split_heads(q[:, LANES:])]
    key_i = lax.broadcasted_iota(I32, (t, 4 * t), 0)
    query_i = lax.broadcasted_iota(I32, (t, 4 * t), 1) & (t - 1)
    strict = key_i < query_i

    def tile(j, r, accs, masked):
        rows = pl.ds(pl.multiple_of(j * t, t), t)
        z = jnp.concatenate([_dot_nt(k_ref[0, rows, p * LANES:(p + 1) * LANES], qs[p]) for p in range(2)],
                            axis=1)
        sp = jnp.log2(1.0 + jnp.exp2(-jnp.abs(z)))
        log_beta = jnp.minimum(z, 0.0) - sp
        log_keep = jnp.minimum(-z, 0.0) - sp
        if masked:
            log_keep = jnp.where(strict, log_keep, 0.0)
        hi = log_keep.astype(BF16)
        lo = (log_keep - hi.astype(F32)).astype(BF16)
        later = (jnp.dot(lgt, hi, preferred_element_type=F32)
                 + jnp.dot(lgt, lo, preferred_element_type=F32))
        a = jnp.exp2(log_beta + later + r)
        if masked:
            a = jnp.where(strict, a, 0.0)
        ab = a.astype(BF16)
        new = tuple(accs[p] + jnp.dot(vt_ref[0, j, p * LANES:(p + 1) * LANES, :],
                                      ab[:, p * 2 * t:(p + 1) * 2 * t], preferred_element_type=F32)
                    for p in range(2))
        return r + jnp.sum(log_keep, axis=0, keepdims=True), new

    zero_acc = jnp.zeros((LANES, 2 * t), F32)
    r, accs = tile(i, jnp.zeros((1, 4 * t), F32), (zero_acc, zero_acc), True)

    def cond(c):
        return (c[0] >= 0) & (c[1] > 0)

    def body(c):
        j, _, r, accs = c
        r, accs = tile(j, r, accs, False)
        return j - 1, (jnp.max(r) > SB_DEAD * LOG2E).astype(I32), r, accs

    _, _, _, accs = lax.while_loop(cond, body, (i - 1, (jnp.max(r) > SB_DEAD * LOG2E).astype(I32), r, accs))
    outs = [jnp.concatenate([accs[p][:HEAD_DIM, :t], accs[p][HEAD_DIM:, t:]], axis=0) for p in range(2)]
    o_ref[0] = jnp.concatenate(outs, axis=0).T.astype(BF16)


def _sb(sq, sk, sv, t):
    b, s, _ = sq.shape
    nt = s // t
    lgt = (jnp.arange(t)[:, None] < jnp.arange(t)[None, :]).astype(BF16)
    vt = jnp.transpose(sv.reshape(b, nt, t, 256), (0, 1, 3, 2))
    return pl.pallas_call(
        functools.partial(_sb_kernel, t=t),
        grid=(b, nt),
        in_specs=[pl.BlockSpec((1, t, 256), lambda bi, i: (bi, i, 0)),
                  pl.BlockSpec((1, s, 256), lambda bi, i: (bi, 0, 0)),
                  pl.BlockSpec((1, nt, 256, t), lambda bi, i: (bi, 0, 0, 0)),
                  pl.BlockSpec((t, t), lambda bi, i: (0, 0))],
        out_specs=pl.BlockSpec((1, t, 256), lambda bi, i: (bi, i, 0)),
        out_shape=jax.ShapeDtypeStruct((b, s, 256), BF16),
        compiler_params=_cparams("arbitrary", "arbitrary"),
        name="stickbreak",
    )(sq, sk, vt, lgt)


def _bit_planes(words):
    words = list(words)
    j, m = 16, 0x0000FFFF
    while j:
        k = 0
        while k < 32:
            tt = (words[k] ^ lax.shift_right_logical(words[k + j], jnp.full_like(words[k + j], j))) & m
            words[k] = words[k] ^ tt
            words[k + j] = words[k + j] ^ (tt << j)
            k = (k + j + 1) & ~j
        j >>= 1
        m = (m ^ (m << j)) & 0xFFFFFFFF
    return words


def _dsa_kernel(q_ref, kk_ref, vt_ref, iq_ref, ikk_ref, wt_ref, bias_ref, lstrict_ref, o_ref,
                key_ref, plane_ref, *, t, top_k):
    i = pl.program_id(1)
    assert t == 8 * 32

    @pl.when(i == 0)
    def _():
        plane_ref[...] = jnp.zeros_like(plane_ref)
    causal = lax.broadcasted_iota(I32, (t, t), 0) <= lax.broadcasted_iota(I32, (t, t), 1)

    def head_stack(x):
        return jnp.concatenate([_split_heads(x[:, :LANES]), _split_heads(x[:, LANES:])], axis=0)

    def key_rows(ref, j0, n):
        return ref[0, pl.ds(pl.multiple_of(j0 * t, t), n * t), :]

    iqs = head_stack(iq_ref[0])
    wt = wt_ref[0]
    w = [wt[4 + h:5 + h, :] * (IDX_HEADS ** -0.5) for h in range(IDX_HEADS)]

    def score_tiles(j0, n, masked):
        lg = _dot_nt(key_rows(ikk_ref, j0, n), iqs)
        sc = w[0] * jnp.maximum(lg[:, 0:t], 0.0)
        for h in range(1, IDX_HEADS):
            sc = sc + w[h] * jnp.maximum(lg[:, h * t:(h + 1) * t], 0.0)
        bits = pltpu.bitcast(sc, I32)
        key = bits ^ ((bits >> 31) & 0x7FFFFFFF)
        key = jnp.where(key == -1, 0, key)
        for u in range(n):
            key_u = key[u * t:(u + 1) * t]
            if masked and u == n - 1:
                key_u = jnp.where(causal, key_u, INT_MIN)
            key_ref[j0 + u] = key_u
            for p, plane in enumerate(_bit_planes([key_u[8 * g:8 * g + 8, :] for g in range(32)])):
                plane_ref[p, j0 + u] = plane

    def p1(g, c):
        score_tiles(4 * g, 4, False)
        return c

    lax.fori_loop(0, i // 4, p1, 0)
    lax.switch(i % 4, [functools.partial(score_tiles, i - i % 4, r + 1, True) for r in range(4)])

    def popcount_rows(words):
        per_tile = jnp.sum(lax.population_count(words), axis=0)
        return jnp.sum(per_tile.astype(F32), axis=0, keepdims=True)

    def bis_body(p, c):
        alive, n_gt, thr_u = c
        plane = plane_ref[p] ^ jnp.where(p == 0, -1, 0)
        ones = alive & plane
        cnt = popcount_rows(ones)
        take = n_gt + cnt >= top_k
        alive = jnp.where(take, ones, alive ^ ones)
        n_gt = jnp.where(take, n_gt, n_gt + cnt)
        thr_u = jnp.where(take, thr_u | (jnp.int32(1) << (31 - p)), thr_u)
        return alive, n_gt, thr_u

    n_tiles = key_ref.shape[0]
    alive0 = jnp.where(lax.broadcasted_iota(I32, (n_tiles, 8, t), 0) <= i, -1, 0)
    alive, n_gt, thr_u = lax.fori_loop(
        0, 32, bis_body, (alive0, jnp.zeros((1, t), F32), jnp.zeros((1, t), I32)))
    thr = jnp.maximum(thr_u ^ INT_MIN, INT_MIN + 1)
    n_avail = (i * t + lax.broadcasted_iota(I32, (1, t), 1) + 1).astype(F32)
    n_ge = jnp.where(n_avail > top_k, n_gt + popcount_rows(alive), 0.0)
    surplus = jnp.max(n_ge) > top_k

    def tie_pass():
        need = top_k - n_gt

        def tb(j, seen):
            k = key_ref[j]
            eq = k == thr
            eqf = jnp.where(eq, 1.0, 0.0)
            before = jnp.dot(lstrict_ref[...], eqf.astype(BF16), preferred_element_type=F32) + seen
            sel = (k > thr) | (eq & (before < need))
            key_ref[j] = jnp.where(sel, 1, INT_MIN)
            return seen + jnp.sum(eqf, axis=0, keepdims=True)

        lax.fori_loop(0, i + 1, tb, jnp.zeros((1, t), F32))
        return jnp.zeros((1, t), I32)

    thr = lax.cond(surplus, tie_pass, lambda: thr)

    qs = head_stack(q_ref[0])

    def attn_tiles(j0, n, carry, near):
        m, acc = carry
        st = _dot_nt(key_rows(kk_ref, j0, n), qs)
        selb = jnp.concatenate([jnp.where(key_ref[j0 + u] >= thr, 0.0, NEG_INF) for u in range(n)], axis=0)
        parts = []
        for h in range(4):
            sh = st[:, h * t:(h + 1) * t] + selb
            if near:
                biased = [sh[(n - near + v) * t:(n - near + v + 1) * t] + bias_ref[2 - near + v, h]
                          for v in range(near)]
                sh = jnp.concatenate(([sh[:(n - near) * t]] if n > near else []) + biased, axis=0)
            parts.append(sh)
        st = jnp.concatenate(parts, axis=1)
        m_new = jnp.maximum(m, jnp.max(st, axis=0, keepdims=True))
        alpha = jnp.exp2(m - m_new)
        pb = jnp.exp2(st - m_new).astype(BF16)
        vt = _with_ones_rows(jnp.concatenate([vt_ref[0, j0 + u] for u in range(n)], axis=1))
        acc = alpha * acc + jnp.dot(vt, pb, preferred_element_type=F32)
        return m_new, acc

    init = (jnp.full((1, 4 * t), M_INIT, F32), jnp.zeros((HEAD_DIM + ONES_ROWS, 4 * t), F32))
    far = jnp.maximum(i - 1, 0)
    carry = lax.fori_loop(0, far // 4, lambda g, c: attn_tiles(4 * g, 4, c, 0), init)
    rest = far - far % 4
    last = [lambda c: attn_tiles(0, 1, c, 1)]
    last += [functools.partial(lambda c, n: attn_tiles(rest, n, c, 2), n=r + 2) for r in range(4)]
    _, acc = lax.switch(jnp.where(i == 0, 0, 1 + far % 4), last, carry)
    ot = acc[:HEAD_DIM] / acc[HEAD_DIM:HEAD_DIM + 1]
    ot = jnp.concatenate([ot[:, h * t:(h + 1) * t] for h in range(4)], axis=0)
    o_ref[0] = ot.T.astype(BF16)


def _dsa_bias(t5_table, t):
    assert t + 1 >= T5_MAX_DISTANCE
    k = jnp.arange(t)[:, None]
    q = jnp.arange(t)[None, :]
    far = t5_table[T5_BUCKETS - 1, 4:].astype(F32)
    tiles = []
    for off in (t, 0):
        dist = off + q - k
        b = jnp.transpose(_t5_lookup(t5_table, dist)[..., 4:], (2, 0, 1)) - far[:, None, None]
        tiles.append(jnp.where((dist >= 0)[None], b, 0.0))
    return jnp.stack(tiles)


def _dsa(dq, dkk, dvv, iq, ikk, wt, bias, top_k, t):
    b, s, _ = dq.shape
    nt = s // t
    lstrict = (jnp.arange(t)[:, None] > jnp.arange(t)[None, :]).astype(BF16)
    vt = jnp.transpose(dvv[:, :, :HEAD_DIM].reshape(b, nt, t, HEAD_DIM), (0, 1, 3, 2))
    blk = lambda w: pl.BlockSpec((1, t, w), lambda bi, i: (bi, i, 0))
    seq = lambda w: pl.BlockSpec((1, s, w), lambda bi, i: (bi, 0, 0))
    return pl.pallas_call(
        functools.partial(_dsa_kernel, t=t, top_k=top_k),
        grid=(b, nt),
        in_specs=[blk(256), seq(LANES),
                  pl.BlockSpec((1, nt, HEAD_DIM, t), lambda bi, i: (bi, 0, 0, 0)),
                  blk(256), seq(LANES),
                  pl.BlockSpec((1, 8, t), lambda bi, i: (bi, 0, i)),
                  pl.BlockSpec(bias.shape, lambda bi, i: (0, 0, 0, 0)),
                  pl.BlockSpec((t, t), lambda bi, i: (0, 0))],
        out_specs=blk(256),
        out_shape=jax.ShapeDtypeStruct((b, s, 256), BF16),
        scratch_shapes=[pltpu.VMEM((nt, t, t), I32), pltpu.VMEM((32, nt, 8, t), I32)],
        compiler_params=_cparams("arbitrary", "arbitrary"),
        name="dsa",
    )(dq, dkk, vt, iq, ikk, wt, bias, lstrict)


def _merge_kernel(x_ref, gm_ref, wg_ref, oa_ref, of_ref, os_ref, od_ref, wb_ref, wo_ref, gf_ref,
                  wr_ref, br_ref, ltri_ref, xo_ref, h2_ref, route_ref, cnt_ref, carry_ref):
    @pl.when(pl.program_id(0) == 0)
    def _():
        carry_ref[...] = jnp.zeros_like(carry_ref)

    x = x_ref[...]
    hb = _rms(x, gm_ref[...]).astype(BF16)
    d = x.shape[1]
    merged = None
    for bi, o_ref in enumerate((oa_ref, of_ref, os_ref, od_ref)):
        gate = jax.nn.sigmoid(jnp.dot(hb, wg_ref[:, bi * d:(bi + 1) * d], preferred_element_type=F32))
        term = gate * jnp.dot(o_ref[...], wb_ref[bi], preferred_element_type=F32)
        merged = term if merged is None else merged + term
    xn = x + jnp.dot(merged.astype(BF16), wo_ref[...], preferred_element_type=F32)
    xo_ref[...] = xn
    h2 = _rms(xn, gf_ref[...])
    h2_ref[...] = h2

    logits = _dot_x3(h2, wr_ref[0], wr_ref[1]) + br_ref[...]
    lane = lax.broadcasted_iota(I32, logits.shape, 1).astype(F32)
    big = 1e9
    gl = jnp.where(lane < N_GROUPS, logits, -jnp.inf)
    gmax = jnp.max(gl, axis=1, keepdims=True)
    grp = jnp.min(jnp.where(gl == gmax, lane, big), axis=1, keepdims=True)
    p_grp = 1.0 / jnp.sum(jnp.exp(gl - gmax), axis=1, keepdims=True)
    first = N_GROUPS + grp * EXPERTS_PER_GROUP
    el = jnp.where((lane >= first) & (lane < first + EXPERTS_PER_GROUP), logits, -jnp.inf)
    l1 = jnp.max(el, axis=1, keepdims=True)
    i1 = jnp.min(jnp.where(el == l1, lane, big), axis=1, keepdims=True)
    el2 = jnp.where(lane == i1, -jnp.inf, el)
    l2 = jnp.max(el2, axis=1, keepdims=True)
    i2 = jnp.min(jnp.where(el2 == l2, lane, big), axis=1, keepdims=True)
    e2 = jnp.exp(l2 - l1)
    g1 = p_grp / (1.0 + e2)
    g2 = p_grp * e2 / (1.0 + e2)
    e_first, e_second = i1 - N_GROUPS, i2 - N_GROUPS
    oh0 = jnp.where(lane == e_first, 1.0, 0.0)
    oh1 = jnp.where(lane == e_second, 1.0, 0.0)
    both = oh0 + oh1
    before = jnp.dot(ltri_ref[...], both.astype(BF16), preferred_element_type=F32) + carry_ref[0:1, :]
    r0 = jnp.sum(oh0 * before, axis=1, keepdims=True)
    r1 = jnp.sum(oh1 * (before + oh0), axis=1, keepdims=True)
    total = carry_ref[0:1, :] + jnp.sum(both, axis=0, keepdims=True)
    carry_ref[0:1, :] = total
    cnt_ref[...] = jnp.broadcast_to(total, cnt_ref.shape)
    cols = (e_first, e_second, g1, g2, r0, r1)
    route = jnp.zeros_like(logits)
    for k, col in enumerate(cols):
        route = jnp.where(lane == k, col, route)
    route_ref[...] = route


def _merge(x2, gm, wg, o_a, o_f, o_s, o_d, wb, wo, gf, wr, br, tm):
    n, d = x2.shape
    row = lambda w: pl.BlockSpec((tm, w), lambda i: (i, 0))
    full = lambda a: pl.BlockSpec(a.shape, lambda i: (0,) * a.ndim, pipeline_mode=pl.Buffered(1))
    ltri = (jnp.arange(tm)[:, None] > jnp.arange(tm)[None, :]).astype(BF16)
    return pl.pallas_call(
        _merge_kernel,
        grid=(n // tm,),
        in_specs=[row(d), full(gm), full(wg), row(256), row(256), row(256), row(256),
                  full(wb), full(wo), full(gf), full(wr), full(br), full(ltri)],
        out_specs=[row(d), row(d), row(LANES), pl.BlockSpec((8, LANES), lambda i: (0, 0))],
        out_shape=[jax.ShapeDtypeStruct((n, d), F32), jax.ShapeDtypeStruct((n, d), F32),
                   jax.ShapeDtypeStruct((n, LANES), F32), jax.ShapeDtypeStruct((8, LANES), F32)],
        scratch_shapes=[pltpu.VMEM((8, LANES), F32)],
        compiler_params=_cparams("arbitrary"),
        name="merge",
    )(x2, gm, wg, o_a, o_f, o_s, o_d, wb, wo, gf, wr, br, ltri)


def _slot_kernel(pad_lo_ref, pad_hi_ref, pos_ref, o_ref, *, chunk):
    i = pl.program_id(0)

    @pl.when(i == 0)
    def _():
        def clear_range(e, c):
            def clear(k, c2):
                o_ref[k] = 0
                return c2
            lax.fori_loop(pad_lo_ref[e], pad_hi_ref[e], clear, 0)
            return c
        lax.fori_loop(0, pad_lo_ref.shape[0], clear_range, 0)

    def place(a, c):
        o_ref[pos_ref[a]] = (i * chunk + a) >> 1
        return c
    lax.fori_loop(0, chunk, place, 0, unroll=8)


def _slot_table(pos_flat, pad_lo, pad_hi, n_slots):
    chunk = min(8192, pos_flat.shape[0])
    grid_spec = pltpu.PrefetchScalarGridSpec(
        num_scalar_prefetch=2,
        grid=(pos_flat.shape[0] // chunk,),
        in_specs=[pl.BlockSpec((chunk,), lambda i, lo, hi: (i,), memory_space=pltpu.SMEM)],
        out_specs=pl.BlockSpec((n_slots,), lambda i, lo, hi: (0,), memory_space=pltpu.SMEM),
    )
    return pl.pallas_call(
        functools.partial(_slot_kernel, chunk=chunk),
        grid_spec=grid_spec,
        out_shape=jax.ShapeDtypeStruct((n_slots,), I32),
        compiler_params=_cparams("arbitrary"),
        name="moe_slots",
    )(pad_lo, pad_hi, pos_flat)


def _expert_kernel(be_ref, nu_ref, tok_ref, tok1_ref, tok2_ref, h_hbm, wup_ref, wdn_ref, y_ref,
                   xbuf, sem, wup_b, wdn_b, *, te):
    b = pl.program_id(0)
    n_used = nu_ref[0]
    slot = b % 3

    def start_row(tokens_ref, dst, r):
        pltpu.make_async_copy(h_hbm.at[pl.ds(tokens_ref[0, 0, r], 1), :],
                              xbuf.at[dst, pl.ds(r, 1), :], sem.at[dst]).start()

    def start_block(tokens_ref, dst):
        def issue(r, c):
            start_row(tokens_ref, dst, r)
            return c
        lax.fori_loop(0, te, issue, 0, unroll=8)

    def block(prefetch):
        pltpu.make_async_copy(h_hbm.at[pl.ds(0, te), :], xbuf.at[slot], sem.at[slot]).wait()
        xb = xbuf[slot].astype(BF16)
        if prefetch:
            dst = (b + 2) % 3
            for r in range(te):
                start_row(tok2_ref, dst, r)
        gu = jnp.dot(xb, wup_b[...], preferred_element_type=F32)
        g = gu[:, :EXPERT_FF]
        act = g * jax.nn.sigmoid(g) * gu[:, EXPERT_FF:]
        y_ref[...] = jnp.dot(act.astype(BF16), wdn_b[...], preferred_element_type=F32)

    @pl.when((b < n_used) & ((b == 0) | (be_ref[b] != be_ref[jnp.maximum(b - 1, 0)])))
    def _():
        wup_b[...] = wup_ref[0, 0].astype(BF16)
        wdn_b[...] = wdn_ref[0, 0].astype(BF16)

    @pl.when((b == 0) & (n_used > 0))
    def _():
        start_block(tok_ref, 0)

    @pl.when((b == 0) & (n_used > 1))
    def _():
        start_block(tok1_ref, 1)

    @pl.when(b + 2 < n_used)
    def _():
        block(True)

    @pl.when((b < n_used) & (b + 2 >= n_used))
    def _():
        block(False)

    @pl.when(b >= n_used)
    def _():
        y_ref[...] = jnp.zeros_like(y_ref)


def _experts(blk_expert, n_used, slot_tok, h2, w_up, w_down, layer, te):
    n_blocks = blk_expert.shape[0]
    d = h2.shape[1]
    ahead = lambda k: pl.BlockSpec((1, 1, te), lambda b, be, nu: (jnp.minimum(b + k, n_blocks - 1), 0, 0),
                                   memory_space=pltpu.SMEM)
    grid_spec = pltpu.PrefetchScalarGridSpec(
        num_scalar_prefetch=2,
        grid=(n_blocks,),
        in_specs=[ahead(0), ahead(1), ahead(2),
                  pl.BlockSpec(memory_space=pl.ANY),
                  pl.BlockSpec((1, 1, d, 2 * EXPERT_FF), lambda b, be, nu: (layer, be[b], 0, 0)),
                  pl.BlockSpec((1, 1, EXPERT_FF, d), lambda b, be, nu: (layer, be[b], 0, 0))],
        out_specs=pl.BlockSpec((te, d), lambda b, be, nu: (b, 0)),
        scratch_shapes=[pltpu.VMEM((3, te, d), F32), pltpu.SemaphoreType.DMA((3,)),
                        pltpu.VMEM((d, 2 * EXPERT_FF), BF16), pltpu.VMEM((EXPERT_FF, d), BF16)],
    )
    slots = slot_tok.reshape(n_blocks, 1, te)
    return pl.pallas_call(
        functools.partial(_expert_kernel, te=te),
        grid_spec=grid_spec,
        out_shape=jax.ShapeDtypeStruct((n_blocks * te, d), F32),
        compiler_params=_cparams("arbitrary"),
        name="moe_experts",
    )(blk_expert, n_used, slots, slots, slots, h2, w_up, w_down)


def _combine_kernel(pos_ref, pos_next_ref, x_ref, route_ref, y_hbm, o_ref, ybuf, sem, *, tc):
    i = pl.program_id(0)
    slot = i % 2

    def gather(rows_ref, dst):
        for r in range(2 * tc):
            pltpu.make_async_copy(y_hbm.at[pl.ds(rows_ref[0, 0, r], 1), :],
                                  ybuf.at[dst, pl.ds(r, 1), :], sem.at[dst]).start()

    @pl.when(i == 0)
    def _():
        gather(pos_ref, 0)

    @pl.when(i + 1 < pl.num_programs(0))
    def _():
        gather(pos_next_ref, 1 - slot)

    pltpu.make_async_copy(y_hbm.at[pl.ds(0, 2 * tc), :], ybuf.at[slot], sem.at[slot]).wait()
    route = route_ref[...]
    o_ref[...] = (x_ref[...] + route[:, 2:3] * ybuf[slot, 0:tc, :]
                  + route[:, 3:4] * ybuf[slot, tc:2 * tc, :])


def _combine(pos, x2, route, yb, tc):
    n, d = x2.shape
    nt = n // tc
    pos_t = jnp.transpose(pos.reshape(nt, tc, 2), (0, 2, 1)).reshape(nt, 1, 2 * tc)
    return pl.pallas_call(
        functools.partial(_combine_kernel, tc=tc),
        grid=(nt,),
        in_specs=[pl.BlockSpec((1, 1, 2 * tc), lambda i: (i, 0, 0), memory_space=pltpu.SMEM),
                  pl.BlockSpec((1, 1, 2 * tc), lambda i: (jnp.minimum(i + 1, nt - 1), 0, 0),
                               memory_space=pltpu.SMEM),
                  pl.BlockSpec((tc, d), lambda i: (i, 0)),
                  pl.BlockSpec((tc, LANES), lambda i: (i, 0)),
                  pl.BlockSpec(memory_space=pl.ANY)],
        out_specs=pl.BlockSpec((tc, d), lambda i: (i, 0)),
        out_shape=jax.ShapeDtypeStruct((n, d), F32),
        scratch_shapes=[pltpu.VMEM((2, 2 * tc, d), F32), pltpu.SemaphoreType.DMA((2,))],
        compiler_params=_cparams("arbitrary"),
        name="moe_combine",
    )(pos_t, pos_t, x2, route, yb)


def _swap_mid_heads(w, axis):
    h = jnp.split(w, 4, axis=axis)
    return jnp.concatenate([h[0], h[2], h[1], h[3]], axis=axis)


def _t5_lookup(t5_table, dist):
    onehot = (_t5_bucket(dist)[..., None] == jnp.arange(T5_BUCKETS)).astype(F32)
    return jnp.einsum("...b,bh->...h", onehot, t5_table.astype(F32), precision=lax.Precision.HIGHEST)


def _t5_bucket(dist):
    n = jnp.maximum(dist, 0)
    max_exact = T5_BUCKETS // 2
    nf = jnp.maximum(n, 1).astype(F32)
    large = max_exact + (jnp.log(nf / max_exact) / math.log(T5_MAX_DISTANCE / max_exact)
                         * (T5_BUCKETS - max_exact)).astype(I32)
    large = jnp.minimum(large, T5_BUCKETS - 1)
    return jnp.where(n < max_exact, n, large)


def _swa_bias(t5_table):
    t = SWA_BLOCK
    dist = t + jnp.arange(t)[None, :] - jnp.arange(2 * t)[:, None]
    tile = jnp.where(((dist >= 0) & (dist < t))[..., None], _t5_lookup(t5_table, dist)[..., :4], NEG_INF)
    return jnp.concatenate([tile[..., h] for h in (0, 2, 1, 3)], axis=1)


def _layer_weights(w_in, qk_gain, forget_bias, w_branch):
    offs = np.concatenate([[0], np.cumsum(IN_SPLITS)]).tolist()
    part = lambda k: w_in[:, offs[k]:offs[k + 1]]
    dup = lambda w: jnp.concatenate([w, w], axis=1)
    aq = _swap_mid_heads(part(0), 1)
    cols = [aq, part(1), part(2), part(3), part(4), part(5), part(7), part(8), part(9),
            part(10), dup(part(11)), dup(part(12)), part(13), dup(part(14))]
    w1 = jnp.concatenate(cols, axis=1).astype(BF16)
    d = w_in.shape[0]
    wm = _hi_lo(jnp.concatenate([part(6), part(15), jnp.zeros((d, LANES - 8), F32)], axis=1))
    wg = part(16).astype(BF16)
    tile = lambda g, reps, scale: jnp.pad(jnp.tile(g, reps) * scale, (0, 256 - reps * HEAD_DIM))
    gains = jnp.stack([tile(qk_gain[0, 0], 4, ATTN_SCALE), tile(qk_gain[0, 1], 2, 1.0),
                       tile(qk_gain[1, 0], 4, ATTN_SCALE * LOG2E), tile(qk_gain[1, 1], 4, 1.0),
                       tile(qk_gain[2, 0], 4, ATTN_SCALE * LOG2E), tile(qk_gain[2, 1], 2, 1.0),
                       jnp.zeros((256,), F32), jnp.zeros((256,), F32)]).astype(F32)
    fb = jnp.pad(forget_bias.astype(F32), (0, LANES - 4)).reshape(1, LANES)
    wb = jnp.stack([_swap_mid_heads(w_branch[0], 0), w_branch[1], w_branch[2], w_branch[3]]).astype(BF16)
    return w1, wm, wg, gains, fb, wb


def kernel(x, norm_mix_g, w_in, forget_bias, attn_sinks, qk_gain, w_branch, w_out, t5_table, norm_ffn_g,
           w_router_group, b_router_group, w_router_expert, b_router_expert, w_expert_up, w_expert_down):
    b, s, d = x.shape
    n = b * s
    depth = w_in.shape[0]
    top_k = min(DSA_TOPK_MAX, s // 4)
    tm_proj = min(512, s)
    fox_t = min(256, s)
    sb_t = min(256, s)
    dsa_t = min(256, s)
    te = 256
    tc = 256

    gseg = (jnp.arange(256)[:, None] // HEAD_DIM == jnp.arange(256)[None, :] // HEAD_DIM).astype(BF16)
    bias_swa = _swa_bias(t5_table)
    bias_dsa = _dsa_bias(t5_table, dsa_t) * LOG2E
    n_blocks = -(-2 * n // te) + N_EXPERTS

    for layer in range(depth):
        w1, wm, wg, gains, fb, wb = _layer_weights(w_in[layer], qk_gain[layer], forget_bias[layer],
                                                   w_branch[layer])
        sink_row = jnp.repeat(attn_sinks[layer].astype(F32)[jnp.array([0, 2, 1, 3])], SWA_BLOCK)
        sinks = jnp.broadcast_to(sink_row[None, :], (8, 4 * SWA_BLOCK))
        (aq, ak, av, fq, fk, fv, sq, sk, sv, dq, dkk, dvv, iq, ikk, cm) = _proj(
            x, norm_mix_g[layer].reshape(1, d), w1, wm, gseg, gains, fb, tm_proj)

        o_swa = _swa(aq, ak, av, bias_swa, sinks, min(4, s // SWA_BLOCK))
        cmt = jnp.transpose(cm[:, :, :8], (0, 2, 1))
        o_fox = _fox(fq, fk, fv, cmt, cm, fox_t)
        o_sb = _sb(sq, sk, sv, sb_t)
        o_dsa = _dsa(dq, dkk, dvv, iq, ikk, cmt, bias_dsa, top_k, dsa_t)

        wr = _hi_lo(jnp.concatenate([w_router_group[layer], w_router_expert[layer],
                                     jnp.zeros((d, LANES - N_GROUPS - N_EXPERTS), F32)], axis=1))
        br = jnp.concatenate([b_router_group[layer], b_router_expert[layer],
                              jnp.zeros((LANES - N_GROUPS - N_EXPERTS,), F32)]).reshape(1, LANES)
        x2, h2, route, cnt = _merge(
            x.reshape(n, d), norm_mix_g[layer].reshape(1, d), wg,
            o_swa.reshape(n, 256), o_fox.reshape(n, 256), o_sb.reshape(n, 256), o_dsa.reshape(n, 256),
            wb, w_out[layer].astype(BF16), norm_ffn_g[layer].reshape(1, d), wr, br, min(512, n))

        counts = cnt[0, :N_EXPERTS].astype(I32)
        padded = (counts + te - 1) // te * te
        pend = jnp.cumsum(padded)
        pstart = pend - padded
        expert = route[:, :2].astype(I32)
        own = expert[:, :, None] == jnp.arange(N_EXPERTS, dtype=I32)
        pos = jnp.sum(jnp.where(own, pstart, 0), axis=-1) + route[:, 4:6].astype(I32)
        n_slots = n_blocks * te
        pad_lo = jnp.concatenate([pstart + counts, pend[-1:]]).astype(I32)
        pad_hi = jnp.concatenate([pend, jnp.full((1,), n_slots, I32)]).astype(I32)
        slot_tok = _slot_table(pos.reshape(-1), pad_lo, pad_hi, n_slots)
        blk_start = jnp.arange(n_blocks, dtype=I32)[:, None] * te
        blk_expert = jnp.minimum(jnp.sum((pend[None, :] <= blk_start).astype(I32), axis=1), N_EXPERTS - 1)
        n_used = (pend[-1:] // te).astype(I32)

        yb = _experts(blk_expert, n_used, slot_tok, h2, w_expert_up, w_expert_down, layer, te)
        x = _combine(pos, x2, route, yb, tc).reshape(b, s, d)
    return x
```

```python
import functools
import math

import jax
import jax.numpy as jnp
import numpy as np
from jax import lax
from jax.experimental import pallas as pl
from jax.experimental.pallas import tpu as pltpu

F32 = jnp.float32
BF16 = jnp.bfloat16
I32 = jnp.int32

HEAD_DIM = 64
LANES = 128
NORM_EPS = 1e-6
NEG_INF = -1e30
M_INIT = -1e29
ATTN_SCALE = HEAD_DIM ** -0.5
LOG2E = math.log2(math.e)
SWA_BLOCK = 128
IDX_SCALE = 64 ** -0.5
IDX_HEADS = 4
DSA_TOPK_MAX = 256
T5_BUCKETS = 32
T5_MAX_DISTANCE = 128
N_GROUPS = 4
EXPERTS_PER_GROUP = 8
N_EXPERTS = N_GROUPS * EXPERTS_PER_GROUP
EXPERT_FF = 512
SB_DEAD = -110.0
INT_MIN = -2 ** 31
ONES_ROWS = 16
VMEM_LIMIT = 56 * 1024 * 1024

IN_SPLITS = (256, 128, 128, 256, 256, 256, 4, 256, 256, 256, 256, 64, 64, 256, 64, 4, 4096)

_SEG = dict(aq=(0, 256), ak=(256, 128), av=(384, 128), fq=(512, 256), fk=(768, 256), fv=(1024, 256),
            sq=(1280, 256), sk=(1536, 256), sv=(1792, 256), dq=(2048, 256), dkk=(2304, 128),
            dvv=(2432, 128), iq=(2560, 256), ikk=(2816, 128))
_SEG_ORDER = ("aq", "ak", "av", "fq", "fk", "fv", "sq", "sk", "sv", "dq", "dkk", "dvv", "iq", "ikk")


def _cparams(*sem):
    return pltpu.CompilerParams(dimension_semantics=sem, vmem_limit_bytes=VMEM_LIMIT)


def _rms(x, g):
    return x * lax.rsqrt(jnp.mean(x * x, axis=-1, keepdims=True) + NORM_EPS) * g


def _log_sigmoid(z):
    return jnp.minimum(z, 0.0) - jnp.log(1.0 + jnp.exp(-jnp.abs(z)))


def _dot_nt(a, b):
    return lax.dot_general(a, b, (((1,), (1,)), ((), ())), preferred_element_type=F32)


def _with_ones_rows(vt):
    ones = jnp.ones(vt.shape[:-2] + (ONES_ROWS, vt.shape[-1]), vt.dtype)
    return jnp.concatenate([vt, ones], axis=-2)


def _split3(x):
    p1 = x.astype(BF16)
    r = x - p1.astype(F32)
    p2 = r.astype(BF16)
    return p1, p2, (r - p2.astype(F32)).astype(BF16)


def _hi_lo(w):
    hi = w.astype(BF16)
    return jnp.stack([hi, (w - hi.astype(F32)).astype(BF16)])


def _dot_x3(a, b_hi, b_lo):
    a_hi = a.astype(BF16)
    a_lo = (a - a_hi.astype(F32)).astype(BF16)
    return (jnp.dot(a_hi, b_hi, preferred_element_type=F32) + jnp.dot(a_lo, b_hi, preferred_element_type=F32)
            + jnp.dot(a_hi, b_lo, preferred_element_type=F32))


def _split_heads(qp):
    lo = lax.broadcasted_iota(I32, (1, LANES), 1) < HEAD_DIM
    zero = jnp.zeros_like(qp)
    return jnp.concatenate([jnp.where(lo, qp, zero), jnp.where(lo, zero, qp)], axis=0)


def _proj_kernel(x_ref, g_ref, w1_ref, wm_ref, gseg_ref, gains_ref, fb_ref, ltri_ref, *rest):
    outs = dict(zip(_SEG_ORDER, rest[:len(_SEG_ORDER)]))
    cm_ref = rest[len(_SEG_ORDER)]
    carry_ref = rest[len(_SEG_ORDER) + 1]

    @pl.when(pl.program_id(1) == 0)
    def _():
        carry_ref[...] = jnp.zeros_like(carry_ref)

    h = _rms(x_ref[0], g_ref[...])
    hb = h.astype(BF16)

    def seg(name):
        off, width = _SEG[name]
        return jnp.dot(hb, w1_ref[:, off:off + width], preferred_element_type=F32)

    def head_norm(t, row):
        width = t.shape[1]
        ssq = jnp.dot((t * t).astype(BF16), gseg_ref[:width, :width], preferred_element_type=F32)
        return t * lax.rsqrt(ssq * (1.0 / HEAD_DIM) + NORM_EPS) * gains_ref[row:row + 1, :width]

    normed = dict(aq=0, ak=1, fq=2, fk=3, dq=4, dkk=5)
    scaled = dict(sq=ATTN_SCALE * LOG2E, iq=IDX_SCALE)
    for name in _SEG_ORDER:
        t = seg(name)
        if name in normed:
            t = head_norm(t, normed[name])
        elif name in scaled:
            t = t * scaled[name]
        outs[name][0] = t.astype(BF16)

    misc = _dot_x3(h, wm_ref[0], wm_ref[1])
    lane = lax.broadcasted_iota(I32, misc.shape, 1)
    logf = jnp.where(lane < 4, _log_sigmoid(misc + fb_ref[...]), 0.0)
    ltri = ltri_ref[...]
    c = carry_ref[0:1, :]
    for piece in _split3(logf):
        c = c + jnp.dot(ltri, piece, preferred_element_type=F32)
    tm = misc.shape[0]
    carry_ref[0:1, :] = c[tm - 1:tm, :]
    cm_ref[0] = jnp.where(lane < 4, c, misc)


def _proj(x, g, w1, wm, gseg, gains, fb, tm):
    b, s, d = x.shape
    ltri = jnp.tril(jnp.ones((tm, tm), BF16))
    full = lambda shape: pl.BlockSpec(shape, lambda bi, si: (0,) * len(shape))
    out_shapes = [jax.ShapeDtypeStruct((b, s, _SEG[n][1]), BF16) for n in _SEG_ORDER]
    out_shapes.append(jax.ShapeDtypeStruct((b, s, LANES), F32))
    out_specs = [pl.BlockSpec((1, tm, _SEG[n][1]), lambda bi, si: (bi, si, 0)) for n in _SEG_ORDER]
    out_specs.append(pl.BlockSpec((1, tm, LANES), lambda bi, si: (bi, si, 0)))
    return pl.pallas_call(
        _proj_kernel,
        grid=(b, s // tm),
        in_specs=[pl.BlockSpec((1, tm, d), lambda bi, si: (bi, si, 0)),
                  full((1, d)), full(w1.shape), full(wm.shape), full(gseg.shape),
                  full(gains.shape), full(fb.shape), full((tm, tm))],
        out_specs=out_specs,
        out_shape=out_shapes,
        scratch_shapes=[pltpu.VMEM((8, LANES), F32)],
        compiler_params=_cparams("arbitrary", "arbitrary"),
        name="proj",
    )(x, g, w1, wm, gseg, gains, fb, ltri)


def _swa_kernel(q_ref, kp_ref, kc_ref, vtp_ref, vtc_ref, bias_ref, sink_ref, o_ref, *, nsub):
    i = pl.program_id(1)
    t = SWA_BLOCK
    no_prev = (lax.broadcasted_iota(I32, (2 * t, 4 * t), 0) < t) & (i == 0)
    sink = sink_ref[0:1, :]
    for u in range(nsub):
        rows = slice(u * t, (u + 1) * t)
        q = q_ref[0, rows, :]
        qs = jnp.concatenate([_split_heads(q[:, :LANES]), _split_heads(q[:, LANES:])], axis=0)
        if u == 0:
            kcat = jnp.concatenate([kp_ref[0], kc_ref[0, rows, :]], axis=0)
            vt = jnp.concatenate([vtp_ref[0], vtc_ref[0, :, rows]], axis=1)
        else:
            kcat = kc_ref[0, (u - 1) * t:(u + 1) * t, :]
            vt = vtc_ref[0, :, (u - 1) * t:(u + 1) * t]
        st = _dot_nt(kcat, qs) + bias_ref[...]
        if u == 0:
            st = jnp.where(no_prev, NEG_INF, st)
        m = jnp.maximum(jnp.max(st, axis=0, keepdims=True), sink)
        p = jnp.exp(st - m)
        denom = jnp.sum(p, axis=0, keepdims=True) + jnp.exp(sink - m)
        ot = jnp.dot(vt, p.astype(BF16), preferred_element_type=F32) / denom
        ot = jnp.concatenate([ot[:HEAD_DIM, 0:t], ot[HEAD_DIM:, t:2 * t],
                              ot[:HEAD_DIM, 2 * t:3 * t], ot[HEAD_DIM:, 3 * t:]], axis=0)
        o_ref[0, rows, :] = ot.T.astype(BF16)


def _swa(aq, ak, av, bias, sinks, nsub):
    b, s, _ = aq.shape
    t = SWA_BLOCK
    avt = jnp.transpose(av, (0, 2, 1))
    cur = lambda bi, i: (bi, i, 0)
    prev = lambda bi, i: (bi, jnp.maximum(i * nsub - 1, 0), 0)
    return pl.pallas_call(
        functools.partial(_swa_kernel, nsub=nsub),
        grid=(b, s // (t * nsub)),
        in_specs=[pl.BlockSpec((1, t * nsub, 256), cur),
                  pl.BlockSpec((1, t, LANES), prev), pl.BlockSpec((1, t * nsub, LANES), cur),
                  pl.BlockSpec((1, LANES, t), lambda bi, i: (bi, 0, jnp.maximum(i * nsub - 1, 0))),
                  pl.BlockSpec((1, LANES, t * nsub), lambda bi, i: (bi, 0, i)),
                  pl.BlockSpec(bias.shape, lambda bi, i: (0, 0)),
                  pl.BlockSpec(sinks.shape, lambda bi, i: (0, 0))],
        out_specs=pl.BlockSpec((1, t * nsub, 256), cur),
        out_shape=jax.ShapeDtypeStruct((b, s, 256), BF16),
        compiler_params=_cparams("arbitrary", "arbitrary"),
        name="swa",
    )(aq, ak, ak, avt, avt, bias, sinks)


def _fox_kernel(q_ref, k_ref, vt_ref, ct_ref, ccol_ref, o_ref, ckb_ref, *, t):
    i = pl.program_id(1)
    n_tiles = ckb_ref.shape[1] // t

    @pl.when(i == 0)
    def _():
        def fill(j, c):
            rows = pl.ds(pl.multiple_of(j * t, t), t)
            cc = ccol_ref[0, rows, :] * LOG2E
            for h in range(4):
                ckb_ref[h, rows, :] = jnp.broadcast_to(cc[:, h:h + 1], (t, LANES))
            return c
        lax.fori_loop(0, n_tiles, fill, 0)

    q = q_ref[0]
    ct = ct_ref[0] * LOG2E
    qs = [_split_heads(q[:, :LANES]), _split_heads(q[:, LANES:])]
    valid = lax.broadcasted_iota(I32, (t, t), 0) <= lax.broadcasted_iota(I32, (t, t), 1)

    def tiles(j0, n, carry, masked):
        m, accs = carry
        rows = pl.ds(pl.multiple_of(j0 * t, t), n * t)
        cols = []
        for pair in range(2):
            st = _dot_nt(k_ref[0, rows, pair * LANES:(pair + 1) * LANES], qs[pair])
            for hh in range(2):
                head = 2 * pair + hh
                ck = ckb_ref[head, rows, :]
                for c in range(t // LANES):
                    cs = slice(c * LANES, (c + 1) * LANES)
                    sh = st[:, hh * t + c * LANES:hh * t + (c + 1) * LANES] + (ct[head:head + 1, cs] - ck)
                    if masked:
                        last = jnp.where(valid[:, cs], sh[(n - 1) * t:], NEG_INF)
                        sh = jnp.concatenate([sh[:(n - 1) * t], last], axis=0) if n > 1 else last
                    cols.append(sh)
        st = jnp.concatenate(cols, axis=1)
        m_new = jnp.maximum(m, jnp.max(st, axis=0, keepdims=True))
        alpha = jnp.exp2(m - m_new)
        pb = jnp.exp2(st - m_new).astype(BF16)
        new = []
        for pair in range(2):
            lanes = slice(pair * 2 * t, (pair + 1) * 2 * t)
            vt = _with_ones_rows(jnp.concatenate(
                [vt_ref[0, j0 + u, pair * LANES:(pair + 1) * LANES, :] for u in range(n)], axis=1))
            pv = jnp.dot(vt, pb[:, lanes], preferred_element_type=F32)
            new.append(alpha[:, lanes] * accs[pair] + pv)
        return m_new, tuple(new)

    zero_acc = jnp.zeros((LANES + ONES_ROWS, 2 * t), F32)
    carry = lax.fori_loop(0, i // 4, lambda g, c: tiles(4 * g, 4, c, False),
                          (jnp.full((1, 4 * t), M_INIT, F32), (zero_acc, zero_acc)))
    rest = i - i % 4
    _, accs = lax.switch(i % 4, [functools.partial(lambda c, n: tiles(rest, n, c, True), n=r + 1)
                                 for r in range(4)], carry)
    outs = []
    for pair in range(2):
        o = accs[pair][:LANES] / accs[pair][LANES:LANES + 1]
        outs.append(jnp.concatenate([o[:HEAD_DIM, :t], o[HEAD_DIM:, t:]], axis=0))
    o_ref[0] = jnp.concatenate(outs, axis=0).T.astype(BF16)


def _fox(fq, fk, fv, ct, ccol, t):
    b, s, _ = fq.shape
    nt = s // t
    vt = jnp.transpose(fv.reshape(b, nt, t, 256), (0, 1, 3, 2))
    return pl.pallas_call(
        functools.partial(_fox_kernel, t=t),
        grid=(b, nt),
        in_specs=[pl.BlockSpec((1, t, 256), lambda bi, i: (bi, i, 0)),
                  pl.BlockSpec((1, s, 256), lambda bi, i: (bi, 0, 0)),
                  pl.BlockSpec((1, nt, 256, t), lambda bi, i: (bi, 0, 0, 0)),
                  pl.BlockSpec((1, 8, t), lambda bi, i: (bi, 0, i)),
                  pl.BlockSpec((1, s, LANES), lambda bi, i: (bi, 0, 0))],
        out_specs=pl.BlockSpec((1, t, 256), lambda bi, i: (bi, i, 0)),
        out_shape=jax.ShapeDtypeStruct((b, s, 256), BF16),
        scratch_shapes=[pltpu.VMEM((4, s, LANES), F32)],
        compiler_params=_cparams("arbitrary", "arbitrary"),
        name="fox",
    )(fq, fk, vt, ct, ccol)


def _sb_kernel(q_ref, k_ref, vt_ref, lgt_ref, o_ref, *, t):
    i = pl.program_id(1)
    q = q_ref[0]
    lgt = lgt_ref[...]
    qs = [_split_heads(q[:, :LANES]), _split_heads(q[:, LANES:])]
    key_i = lax.broadcasted_iota(I32, (t, 4 * t), 0)
    query_i = lax.broadcasted_iota(I32, (t, 4 * t), 1) & (t - 1)
    strict = key_i < query_i

    def tile(j, r, accs, masked):
        rows = pl.ds(pl.multiple_of(j * t, t), t)
        z = jnp.concatenate([_dot_nt(k_ref[0, rows, p * LANES:(p + 1) * LANES], qs[p]) for p in range(2)],
                            axis=1)
        sp = jnp.log2(1.0 + jnp.exp2(-jnp.abs(z)))
        log_beta = jnp.minimum(z, 0.0) - sp
        log_keep = jnp.minimum(-z, 0.0) - sp
        if masked:
            log_keep = jnp.where(strict, log_keep, 0.0)
        hi = log_keep.astype(BF16)
        lo = (log_keep - hi.astype(F32)).astype(BF16)
        later = (jnp.dot(lgt, hi, preferred_element_type=F32)
                 + jnp.dot(lgt, lo, preferred_element_type=F32))
        a = jnp.exp2(log_beta + later + r)
        if masked:
            a = jnp.where(strict, a, 0.0)
        ab = a.astype(BF16)
        new = tuple(accs[p] + jnp.dot(vt_ref[0, j, p * LANES:(p + 1) * LANES, :],
                                      ab[:, p * 2 * t:(p + 1) * 2 * t], preferred_element_type=F32)
                    for p in range(2))
        return r + jnp.sum(log_keep, axis=0, keepdims=True), new

    zero_acc = jnp.zeros((LANES, 2 * t), F32)
    r, accs = tile(i, jnp.zeros((1, 4 * t), F32), (zero_acc, zero_acc), True)

    def cond(c):
        return (c[0] >= 0) & (c[1] > 0)

    def body(c):
        j, _, r, accs = c
        r, accs = tile(j, r, accs, False)
        return j - 1, (jnp.max(r) > SB_DEAD * LOG2E).astype(I32), r, accs

    _, _, _, accs = lax.while_loop(cond, body, (i - 1, (jnp.max(r) > SB_DEAD * LOG2E).astype(I32), r, accs))
    outs = [jnp.concatenate([accs[p][:HEAD_DIM, :t], accs[p][HEAD_DIM:, t:]], axis=0) for p in range(2)]
    o_ref[0] = jnp.concatenate(outs, axis=0).T.astype(BF16)


def _sb(sq, sk, sv, t):
    b, s, _ = sq.shape
    nt = s // t
    lgt = (jnp.arange(t)[:, None] < jnp.arange(t)[None, :]).astype(BF16)
    vt = jnp.transpose(sv.reshape(b, nt, t, 256), (0, 1, 3, 2))
    return pl.pallas_call(
        functools.partial(_sb_kernel, t=t),
        grid=(b, nt),
        in_specs=[pl.BlockSpec((1, t, 256), lambda bi, i: (bi, i, 0)),
                  pl.BlockSpec((1, s, 256), lambda bi, i: (bi, 0, 0)),
                  pl.BlockSpec((1, nt, 256, t), lambda bi, i: (bi, 0, 0, 0)),
                  pl.BlockSpec((t, t), lambda bi, i: (0, 0))],
        out_specs=pl.BlockSpec((1, t, 256), lambda bi, i: (bi, i, 0)),
        out_shape=jax.ShapeDtypeStruct((b, s, 256), BF16),
        compiler_params=_cparams("arbitrary", "arbitrary"),
        name="stickbreak",
    )(sq, sk, vt, lgt)


def _bit_planes(words):
    words = list(words)
    j, m = 16, 0x0000FFFF
    while j:
        k = 0
        while k < 32:
            tt = (words[k] ^ lax.shift_right_logical(words[k + j], jnp.full_like(words[k + j], j))) & m
            words[k] = words[k] ^ tt
            words[k + j] = words[k + j] ^ (tt << j)
            k = (k + j + 1) & ~j
        j >>= 1
        m = (m ^ (m << j)) & 0xFFFFFFFF
    return words


def _dsa_kernel(q_ref, kk_ref, vt_ref, iq_ref, ikk_ref, wt_ref, bias_ref, lstrict_ref, o_ref,
                key_ref, plane_ref, *, t, top_k):
    i = pl.program_id(1)
    assert t == 8 * 32

    @pl.when(i == 0)
    def _():
        plane_ref[...] = jnp.zeros_like(plane_ref)
    causal = lax.broadcasted_iota(I32, (t, t), 0) <= lax.broadcasted_iota(I32, (t, t), 1)

    def head_stack(x):
        return jnp.concatenate([_split_heads(x[:, :LANES]), _split_heads(x[:, LANES:])], axis=0)

    def key_rows(ref, j0, n):
        return ref[0, pl.ds(pl.multiple_of(j0 * t, t), n * t), :]

    iqs = head_stack(iq_ref[0])
    wt = wt_ref[0]
    w = [wt[4 + h:5 + h, :] * (IDX_HEADS ** -0.5) for h in range(IDX_HEADS)]

    def score_tiles(j0, n, masked):
        lg = _dot_nt(key_rows(ikk_ref, j0, n), iqs)
        sc = w[0] * jnp.maximum(lg[:, 0:t], 0.0)
        for h in range(1, IDX_HEADS):
            sc = sc + w[h] * jnp.maximum(lg[:, h * t:(h + 1) * t], 0.0)
        bits = pltpu.bitcast(sc, I32)
        key = bits ^ ((bits >> 31) & 0x7FFFFFFF)
        key = jnp.where(key == -1, 0, key)
        for u in range(n):
            key_u = key[u * t:(u + 1) * t]
            if masked and u == n - 1:
                key_u = jnp.where(causal, key_u, INT_MIN)
            key_ref[j0 + u] = key_u
            for p, plane in enumerate(_bit_planes([key_u[8 * g:8 * g + 8, :] for g in range(32)])):
                plane_ref[p, j0 + u] = plane

    def p1(g, c):
        score_tiles(4 * g, 4, False)
        return c

    lax.fori_loop(0, i // 4, p1, 0)
    lax.switch(i % 4, [functools.partial(score_tiles, i - i % 4, r + 1, True) for r in range(4)])

    def popcount_rows(words):
        per_tile = jnp.sum(lax.population_count(words), axis=0)
        return jnp.sum(per_tile.astype(F32), axis=0, keepdims=True)

    def bisect(nt_use):
        def bis_body(p, c):
            alive, n_gt, thr_u = c
            plane = plane_ref[p, :nt_use] ^ jnp.where(p == 0, -1, 0)
            ones = alive & plane
            cnt = popcount_rows(ones)
            take = n_gt + cnt >= top_k
            alive = jnp.where(take, ones, alive ^ ones)
            n_gt = jnp.where(take, n_gt, n_gt + cnt)
            thr_u = jnp.where(take, thr_u | (jnp.int32(1) << (31 - p)), thr_u)
            return alive, n_gt, thr_u

        alive0 = jnp.where(lax.broadcasted_iota(I32, (nt_use, 8, t), 0) <= i, -1, 0)
        alive, n_gt, thr_u = lax.fori_loop(
            0, 32, bis_body, (alive0, jnp.zeros((1, t), F32), jnp.zeros((1, t), I32)))
        return popcount_rows(alive), n_gt, thr_u

    n_tiles = key_ref.shape[0]
    if n_tiles >= 2:
        n_eq, n_gt, thr_u = lax.cond(i < n_tiles // 2, lambda: bisect(n_tiles // 2), lambda: bisect(n_tiles))
    else:
        n_eq, n_gt, thr_u = bisect(n_tiles)
    thr = jnp.maximum(thr_u ^ INT_MIN, INT_MIN + 1)
    n_avail = (i * t + lax.broadcasted_iota(I32, (1, t), 1) + 1).astype(F32)
    n_ge = jnp.where(n_avail > top_k, n_gt + n_eq, 0.0)
    surplus = jnp.max(n_ge) > top_k

    def tie_pass():
        need = top_k - n_gt

        def tb(j, seen):
            k = key_ref[j]
            eq = k == thr
            eqf = jnp.where(eq, 1.0, 0.0)
            before = jnp.dot(lstrict_ref[...], eqf.astype(BF16), preferred_element_type=F32) + seen
            sel = (k > thr) | (eq & (before < need))
            key_ref[j] = jnp.where(sel, 1, INT_MIN)
            return seen + jnp.sum(eqf, axis=0, keepdims=True)

        lax.fori_loop(0, i + 1, tb, jnp.zeros((1, t), F32))
        return jnp.zeros((1, t), I32)

    thr = lax.cond(surplus, tie_pass, lambda: thr)

    qs = head_stack(q_ref[0])

    def attn_tiles(j0, n, carry, near):
        m, acc = carry
        st = _dot_nt(key_rows(kk_ref, j0, n), qs)
        selb = jnp.concatenate([jnp.where(key_ref[j0 + u] >= thr, 0.0, NEG_INF) for u in range(n)], axis=0)
        parts = []
        for h in range(4):
            sh = st[:, h * t:(h + 1) * t] + selb
            if near:
                biased = [sh[(n - near + v) * t:(n - near + v + 1) * t] + bias_ref[2 - near + v, h]
                          for v in range(near)]
                sh = jnp.concatenate(([sh[:(n - near) * t]] if n > near else []) + biased, axis=0)
            parts.append(sh)
        st = jnp.concatenate(parts, axis=1)
        m_new = jnp.maximum(m, jnp.max(st, axis=0, keepdims=True))
        alpha = jnp.exp2(m - m_new)
        pb = jnp.exp2(st - m_new).astype(BF16)
        vt = _with_ones_rows(jnp.concatenate([vt_ref[0, j0 + u] for u in range(n)], axis=1))
        acc = alpha * acc + jnp.dot(vt, pb, preferred_element_type=F32)
        return m_new, acc

    init = (jnp.full((1, 4 * t), M_INIT, F32), jnp.zeros((HEAD_DIM + ONES_ROWS, 4 * t), F32))
    far = jnp.maximum(i - 1, 0)
    carry = lax.fori_loop(0, far // 4, lambda g, c: attn_tiles(4 * g, 4, c, 0), init)
    rest = far - far % 4
    last = [lambda c: attn_tiles(0, 1, c, 1)]
    last += [functools.partial(lambda c, n: attn_tiles(rest, n, c, 2), n=r + 2) for r in range(4)]
    _, acc = lax.switch(jnp.where(i == 0, 0, 1 + far % 4), last, carry)
    ot = acc[:HEAD_DIM] / acc[HEAD_DIM:HEAD_DIM + 1]
    ot = jnp.concatenate([ot[:, h * t:(h + 1) * t] for h in range(4)], axis=0)
    o_ref[0] = ot.T.astype(BF16)


def _dsa_bias(t5_table, t):
    assert t + 1 >= T5_MAX_DISTANCE
    k = jnp.arange(t)[:, None]
    q = jnp.arange(t)[None, :]
    far = t5_table[T5_BUCKETS - 1, 4:].astype(F32)
    tiles = []
    for off in (t, 0):
        dist = off + q - k
        b = jnp.transpose(_t5_lookup(t5_table, dist)[..., 4:], (2, 0, 1)) - far[:, None, None]
        tiles.append(jnp.where((dist >= 0)[None], b, 0.0))
    return jnp.stack(tiles)


def _dsa(dq, dkk, dvv, iq, ikk, wt, bias, top_k, t):
    b, s, _ = dq.shape
    nt = s // t
    lstrict = (jnp.arange(t)[:, None] > jnp.arange(t)[None, :]).astype(BF16)
    vt = jnp.transpose(dvv[:, :, :HEAD_DIM].reshape(b, nt, t, HEAD_DIM), (0, 1, 3, 2))
    blk = lambda w: pl.BlockSpec((1, t, w), lambda bi, i: (bi, i, 0))
    seq = lambda w: pl.BlockSpec((1, s, w), lambda bi, i: (bi, 0, 0))
    return pl.pallas_call(
        functools.partial(_dsa_kernel, t=t, top_k=top_k),
        grid=(b, nt),
        in_specs=[blk(256), seq(LANES),
                  pl.BlockSpec((1, nt, HEAD_DIM, t), lambda bi, i: (bi, 0, 0, 0)),
                  blk(256), seq(LANES),
                  pl.BlockSpec((1, 8, t), lambda bi, i: (bi, 0, i)),
                  pl.BlockSpec(bias.shape, lambda bi, i: (0, 0, 0, 0)),
                  pl.BlockSpec((t, t), lambda bi, i: (0, 0))],
        out_specs=blk(256),
        out_shape=jax.ShapeDtypeStruct((b, s, 256), BF16),
        scratch_shapes=[pltpu.VMEM((nt, t, t), I32), pltpu.VMEM((32, nt, 8, t), I32)],
        compiler_params=_cparams("arbitrary", "arbitrary"),
        name="dsa",
    )(dq, dkk, vt, iq, ikk, wt, bias, lstrict)


def _merge_kernel(x_ref, gm_ref, wg_ref, oa_ref, of_ref, os_ref, od_ref, wb_ref, wo_ref, gf_ref,
                  wr_ref, br_ref, ltri_ref, xo_ref, h2_ref, route_ref, cnt_ref, carry_ref):
    @pl.when(pl.program_id(0) == 0)
    def _():
        carry_ref[...] = jnp.zeros_like(carry_ref)

    x = x_ref[...]
    hb = _rms(x, gm_ref[...]).astype(BF16)
    d = x.shape[1]
    merged = None
    for bi, o_ref in enumerate((oa_ref, of_ref, os_ref, od_ref)):
        gate = jax.nn.sigmoid(jnp.dot(hb, wg_ref[:, bi * d:(bi + 1) * d], preferred_element_type=F32))
        term = gate * jnp.dot(o_ref[...], wb_ref[bi], preferred_element_type=F32)
        merged = term if merged is None else merged + term
    xn = x + jnp.dot(merged.astype(BF16), wo_ref[...], preferred_element_type=F32)
    xo_ref[...] = xn
    h2 = _rms(xn, gf_ref[...])
    h2_ref[...] = h2

    logits = _dot_x3(h2, wr_ref[0], wr_ref[1]) + br_ref[...]
    lane = lax.broadcasted_iota(I32, logits.shape, 1).astype(F32)
    big = 1e9
    gl = jnp.where(lane < N_GROUPS, logits, -jnp.inf)
    gmax = jnp.max(gl, axis=1, keepdims=True)
    grp = jnp.min(jnp.where(gl == gmax, lane, big), axis=1, keepdims=True)
    p_grp = 1.0 / jnp.sum(jnp.exp(gl - gmax), axis=1, keepdims=True)
    first = N_GROUPS + grp * EXPERTS_PER_GROUP
    el = jnp.where((lane >= first) & (lane < first + EXPERTS_PER_GROUP), logits, -jnp.inf)
    l1 = jnp.max(el, axis=1, keepdims=True)
    i1 = jnp.min(jnp.where(el == l1, lane, big), axis=1, keepdims=True)
    el2 = jnp.where(lane == i1, -jnp.inf, el)
    l2 = jnp.max(el2, axis=1, keepdims=True)
    i2 = jnp.min(jnp.where(el2 == l2, lane, big), axis=1, keepdims=True)
    e2 = jnp.exp(l2 - l1)
    g1 = p_grp / (1.0 + e2)
    g2 = p_grp * e2 / (1.0 + e2)
    e_first, e_second = i1 - N_GROUPS, i2 - N_GROUPS
    oh0 = jnp.where(lane == e_first, 1.0, 0.0)
    oh1 = jnp.where(lane == e_second, 1.0, 0.0)
    both = oh0 + oh1
    before = jnp.dot(ltri_ref[...], both.astype(BF16), preferred_element_type=F32) + carry_ref[0:1, :]
    r0 = jnp.sum(oh0 * before, axis=1, keepdims=True)
    r1 = jnp.sum(oh1 * (before + oh0), axis=1, keepdims=True)
    total = carry_ref[0:1, :] + jnp.sum(both, axis=0, keepdims=True)
    carry_ref[0:1, :] = total
    cnt_ref[...] = jnp.broadcast_to(total, cnt_ref.shape)
    cols = (e_first, e_second, g1, g2, r0, r1)
    route = jnp.zeros_like(logits)
    for k, col in enumerate(cols):
        route = jnp.where(lane == k, col, route)
    route_ref[...] = route


def _merge(x2, gm, wg, o_a, o_f, o_s, o_d, wb, wo, gf, wr, br, tm):
    n, d = x2.shape
    row = lambda w: pl.BlockSpec((tm, w), lambda i: (i, 0))
    full = lambda a: pl.BlockSpec(a.shape, lambda i: (0,) * a.ndim, pipeline_mode=pl.Buffered(1))
    ltri = (jnp.arange(tm)[:, None] > jnp.arange(tm)[None, :]).astype(BF16)
    return pl.pallas_call(
        _merge_kernel,
        grid=(n // tm,),
        in_specs=[row(d), full(gm), full(wg), row(256), row(256), row(256), row(256),
                  full(wb), full(wo), full(gf), full(wr), full(br), full(ltri)],
        out_specs=[row(d), row(d), row(LANES), pl.BlockSpec((8, LANES), lambda i: (0, 0))],
        out_shape=[jax.ShapeDtypeStruct((n, d), F32), jax.ShapeDtypeStruct((n, d), F32),
                   jax.ShapeDtypeStruct((n, LANES), F32), jax.ShapeDtypeStruct((8, LANES), F32)],
        scratch_shapes=[pltpu.VMEM((8, LANES), F32)],
        compiler_params=_cparams("arbitrary"),
        name="merge",
    )(x2, gm, wg, o_a, o_f, o_s, o_d, wb, wo, gf, wr, br, ltri)


def _slot_kernel(pad_lo_ref, pad_hi_ref, pos_ref, o_ref, *, chunk):
    i = pl.program_id(0)

    @pl.when(i == 0)
    def _():
        def clear_range(e, c):
            def clear(k, c2):
                o_ref[k] = 0
                return c2
            lax.fori_loop(pad_lo_ref[e], pad_hi_ref[e], clear, 0)
            return c
        lax.fori_loop(0, pad_lo_ref.shape[0], clear_range, 0)

    def place(a, c):
        o_ref[pos_ref[a]] = (i * chunk + a) >> 1
        return c
    lax.fori_loop(0, chunk, place, 0, unroll=8)


def _slot_table(pos_flat, pad_lo, pad_hi, n_slots):
    chunk = min(8192, pos_flat.shape[0])
    grid_spec = pltpu.PrefetchScalarGridSpec(
        num_scalar_prefetch=2,
        grid=(pos_flat.shape[0] // chunk,),
        in_specs=[pl.BlockSpec((chunk,), lambda i, lo, hi: (i,), memory_space=pltpu.SMEM)],
        out_specs=pl.BlockSpec((n_slots,), lambda i, lo, hi: (0,), memory_space=pltpu.SMEM),
    )
    return pl.pallas_call(
        functools.partial(_slot_kernel, chunk=chunk),
        grid_spec=grid_spec,
        out_shape=jax.ShapeDtypeStruct((n_slots,), I32),
        compiler_params=_cparams("arbitrary"),
        name="moe_slots",
    )(pad_lo, pad_hi, pos_flat)


def _expert_kernel(be_ref, nu_ref, tok_ref, tok1_ref, tok2_ref, h_hbm, wup_ref, wdn_ref, y_ref,
                   xbuf, sem, wup_b, wdn_b, *, te):
    b = pl.program_id(0)
    n_used = nu_ref[0]
    slot = b % 3

    def start_row(tokens_ref, dst, r):
        pltpu.make_async_copy(h_hbm.at[pl.ds(tokens_ref[0, 0, r], 1), :],
                              xbuf.at[dst, pl.ds(r, 1), :], sem.at[dst]).start()

    def start_block(tokens_ref, dst):
        def issue(r, c):
            start_row(tokens_ref, dst, r)
            return c
        lax.fori_loop(0, te, issue, 0, unroll=8)

    def block(prefetch):
        pltpu.make_async_copy(h_hbm.at[pl.ds(0, te), :], xbuf.at[slot], sem.at[slot]).wait()
        xb = xbuf[slot].astype(BF16)
        if prefetch:
            dst = (b + 2) % 3
            for r in range(te):
                start_row(tok2_ref, dst, r)
        gu = jnp.dot(xb, wup_b[...], preferred_element_type=F32)
        g = gu[:, :EXPERT_FF]
        act = g * jax.nn.sigmoid(g) * gu[:, EXPERT_FF:]
        y_ref[...] = jnp.dot(act.astype(BF16), wdn_b[...], preferred_element_type=F32)

    @pl.when((b < n_used) & ((b == 0) | (be_ref[b] != be_ref[jnp.maximum(b - 1, 0)])))
    def _():
        wup_b[...] = wup_ref[0, 0].astype(BF16)
        wdn_b[...] = wdn_ref[0, 0].astype(BF16)

    @pl.when((b == 0) & (n_used > 0))
    def _():
        start_block(tok_ref, 0)

    @pl.when((b == 0) & (n_used > 1))
    def _():
        start_block(tok1_ref, 1)

    @pl.when(b + 2 < n_used)
    def _():
        block(True)

    @pl.when((b < n_used) & (b + 2 >= n_used))
    def _():
        block(False)

    @pl.when(b >= n_used)
    def _():
        y_ref[...] = jnp.zeros_like(y_ref)


def _experts(blk_expert, n_used, slot_tok, h2, w_up, w_down, layer, te):
    n_blocks = blk_expert.shape[0]
    d = h2.shape[1]
    ahead = lambda k: pl.BlockSpec((1, 1, te), lambda b, be, nu: (jnp.minimum(b + k, n_blocks - 1), 0, 0),
                                   memory_space=pltpu.SMEM)
    grid_spec = pltpu.PrefetchScalarGridSpec(
        num_scalar_prefetch=2,
        grid=(n_blocks,),
        in_specs=[ahead(0), ahead(1), ahead(2),
                  pl.BlockSpec(memory_space=pl.ANY),
                  pl.BlockSpec((1, 1, d, 2 * EXPERT_FF), lambda b, be, nu: (layer, be[b], 0, 0)),
                  pl.BlockSpec((1, 1, EXPERT_FF, d), lambda b, be, nu: (layer, be[b], 0, 0))],
        out_specs=pl.BlockSpec((te, d), lambda b, be, nu: (b, 0)),
        scratch_shapes=[pltpu.VMEM((3, te, d), F32), pltpu.SemaphoreType.DMA((3,)),
                        pltpu.VMEM((d, 2 * EXPERT_FF), BF16), pltpu.VMEM((EXPERT_FF, d), BF16)],
    )
    slots = slot_tok.reshape(n_blocks, 1, te)
    return pl.pallas_call(
        functools.partial(_expert_kernel, te=te),
        grid_spec=grid_spec,
        out_shape=jax.ShapeDtypeStruct((n_blocks * te, d), F32),
        compiler_params=_cparams("arbitrary"),
        name="moe_experts",
    )(blk_expert, n_used, slots, slots, slots, h2, w_up, w_down)


def _combine_kernel(pos_ref, pos_next_ref, x_ref, route_ref, y_hbm, o_ref, ybuf, sem, *, tc):
    i = pl.program_id(0)
    slot = i % 2

    def gather(rows_ref, dst):
        for r in range(2 * tc):
            pltpu.make_async_copy(y_hbm.at[pl.ds(rows_ref[0, 0, r], 1), :],
                                  ybuf.at[dst, pl.ds(r, 1), :], sem.at[dst]).start()

    @pl.when(i == 0)
    def _():
        gather(pos_ref, 0)

    @pl.when(i + 1 < pl.num_programs(0))
    def _():
        gather(pos_next_ref, 1 - slot)

    pltpu.make_async_copy(y_hbm.at[pl.ds(0, 2 * tc), :], ybuf.at[slot], sem.at[slot]).wait()
    route = route_ref[...]
    o_ref[...] = (x_ref[...] + route[:, 2:3] * ybuf[slot, 0:tc, :]
                  + route[:, 3:4] * ybuf[slot, tc:2 * tc, :])


def _combine(pos, x2, route, yb, tc):
    n, d = x2.shape
    nt = n // tc
    pos_t = jnp.transpose(pos.reshape(nt, tc, 2), (0, 2, 1)).reshape(nt, 1, 2 * tc)
    return pl.pallas_call(
        functools.partial(_combine_kernel, tc=tc),
        grid=(nt,),
        in_specs=[pl.BlockSpec((1, 1, 2 * tc), lambda i: (i, 0, 0), memory_space=pltpu.SMEM),
                  pl.BlockSpec((1, 1, 2 * tc), lambda i: (jnp.minimum(i + 1, nt - 1), 0, 0),
                               memory_space=pltpu.SMEM),
                  pl.BlockSpec((tc, d), lambda i: (i, 0)),
                  pl.BlockSpec((tc, LANES), lambda i: (i, 0)),
                  pl.BlockSpec(memory_space=pl.ANY)],
        out_specs=pl.BlockSpec((tc, d), lambda i: (i, 0)),
        out_shape=jax.ShapeDtypeStruct((n, d), F32),
        scratch_shapes=[pltpu.VMEM((2, 2 * tc, d), F32), pltpu.SemaphoreType.DMA((2,))],
        compiler_params=_cparams("arbitrary"),
        name="moe_combine",
    )(pos_t, pos_t, x2, route, yb)


def _swap_mid_heads(w, axis):
    h = jnp.split(w, 4, axis=axis)
    return jnp.concatenate([h[0], h[2], h[1], h[3]], axis=axis)


def _t5_lookup(t5_table, dist):
    onehot = (_t5_bucket(dist)[..., None] == jnp.arange(T5_BUCKETS)).astype(F32)
    return jnp.einsum("...b,bh->...h", onehot, t5_table.astype(F32), precision=lax.Precision.HIGHEST)


def _t5_bucket(dist):
    n = jnp.maximum(dist, 0)
    max_exact = T5_BUCKETS // 2
    nf = jnp.maximum(n, 1).astype(F32)
    large = max_exact + (jnp.log(nf / max_exact) / math.log(T5_MAX_DISTANCE / max_exact)
                         * (T5_BUCKETS - max_exact)).astype(I32)
    large = jnp.minimum(large, T5_BUCKETS - 1)
    return jnp.where(n < max_exact, n, large)


def _swa_bias(t5_table):
    t = SWA_BLOCK
    dist = t + jnp.arange(t)[None, :] - jnp.arange(2 * t)[:, None]
    tile = jnp.where(((dist >= 0) & (dist < t))[..., None], _t5_lookup(t5_table, dist)[..., :4], NEG_INF)
    return jnp.concatenate([tile[..., h] for h in (0, 2, 1, 3)], axis=1)


def _layer_weights(w_in, qk_gain, forget_bias, w_branch):
    offs = np.concatenate([[0], np.cumsum(IN_SPLITS)]).tolist()
    part = lambda k: w_in[:, offs[k]:offs[k + 1]]
    dup = lambda w: jnp.concatenate([w, w], axis=1)
    aq = _swap_mid_heads(part(0), 1)
    cols = [aq, part(1), part(2), part(3), part(4), part(5), part(7), part(8), part(9),
            part(10), dup(part(11)), dup(part(12)), part(13), dup(part(14))]
    w1 = jnp.concatenate(cols, axis=1).astype(BF16)
    d = w_in.shape[0]
    wm = _hi_lo(jnp.concatenate([part(6), part(15), jnp.zeros((d, LANES - 8), F32)], axis=1))
    wg = part(16).astype(BF16)
    tile = lambda g, reps, scale: jnp.pad(jnp.tile(g, reps) * scale, (0, 256 - reps * HEAD_DIM))
    gains = jnp.stack([tile(qk_gain[0, 0], 4, ATTN_SCALE), tile(qk_gain[0, 1], 2, 1.0),
                       tile(qk_gain[1, 0], 4, ATTN_SCALE * LOG2E), tile(qk_gain[1, 1], 4, 1.0),
                       tile(qk_gain[2, 0], 4, ATTN_SCALE * LOG2E), tile(qk_gain[2, 1], 2, 1.0),
                       jnp.zeros((256,), F32), jnp.zeros((256,), F32)]).astype(F32)
    fb = jnp.pad(forget_bias.astype(F32), (0, LANES - 4)).reshape(1, LANES)
    wb = jnp.stack([_swap_mid_heads(w_branch[0], 0), w_branch[1], w_branch[2], w_branch[3]]).astype(BF16)
    return w1, wm, wg, gains, fb, wb


def kernel(x, norm_mix_g, w_in, forget_bias, attn_sinks, qk_gain, w_branch, w_out, t5_table, norm_ffn_g,
           w_router_group, b_router_group, w_router_expert, b_router_expert, w_expert_up, w_expert_down):
    b, s, d = x.shape
    n = b * s
    depth = w_in.shape[0]
    top_k = min(DSA_TOPK_MAX, s // 4)
    tm_proj = min(512, s)
    fox_t = min(256, s)
    sb_t = min(256, s)
    dsa_t = min(256, s)
    te = 256
    tc = 256

    gseg = (jnp.arange(256)[:, None] // HEAD_DIM == jnp.arange(256)[None, :] // HEAD_DIM).astype(BF16)
    bias_swa = _swa_bias(t5_table)
    bias_dsa = _dsa_bias(t5_table, dsa_t) * LOG2E
    n_blocks = -(-2 * n // te) + N_EXPERTS

    for layer in range(depth):
        w1, wm, wg, gains, fb, wb = _layer_weights(w_in[layer], qk_gain[layer], forget_bias[layer],
                                                   w_branch[layer])
        sink_row = jnp.repeat(attn_sinks[layer].astype(F32)[jnp.array([0, 2, 1, 3])], SWA_BLOCK)
        sinks = jnp.broadcast_to(sink_row[None, :], (8, 4 * SWA_BLOCK))
        (aq, ak, av, fq, fk, fv, sq, sk, sv, dq, dkk, dvv, iq, ikk, cm) = _proj(
            x, norm_mix_g[layer].reshape(1, d), w1, wm, gseg, gains, fb, tm_proj)

        o_swa = _swa(aq, ak, av, bias_swa, sinks, min(8, s // SWA_BLOCK))
        cmt = jnp.transpose(cm[:, :, :8], (0, 2, 1))
        o_fox = _fox(fq, fk, fv, cmt, cm, fox_t)
        o_sb = _sb(sq, sk, sv, sb_t)
        o_dsa = _dsa(dq, dkk, dvv, iq, ikk, cmt, bias_dsa, top_k, dsa_t)

        wr = _hi_lo(jnp.concatenate([w_router_group[layer], w_router_expert[layer],
                                     jnp.zeros((d, LANES - N_GROUPS - N_EXPERTS), F32)], axis=1))
        br = jnp.concatenate([b_router_group[layer], b_router_expert[layer],
                              jnp.zeros((LANES - N_GROUPS - N_EXPERTS,), F32)]).reshape(1, LANES)
        x2, h2, route, cnt = _merge(
            x.reshape(n, d), norm_mix_g[layer].reshape(1, d), wg,
            o_swa.reshape(n, 256), o_fox.reshape(n, 256), o_sb.reshape(n, 256), o_dsa.reshape(n, 256),
            wb, w_out[layer].astype(BF16), norm_ffn_g[layer].reshape(1, d), wr, br, min(512, n))

        counts = cnt[0, :N_EXPERTS].astype(I32)
        padded = (counts + te - 1) // te * te
        pend = jnp.cumsum(padded)
        pstart = pend - padded
        expert = route[:, :2].astype(I32)
        own = expert[:, :, None] == jnp.arange(N_EXPERTS, dtype=I32)
        pos = jnp.sum(jnp.where(own, pstart, 0), axis=-1) + route[:, 4:6].astype(I32)
        n_slots = n_blocks * te
        pad_lo = jnp.concatenate([pstart + counts, pend[-1:]]).astype(I32)
        pad_hi = jnp.concatenate([pend, jnp.full((1,), n_slots, I32)]).astype(I32)
        slot_tok = _slot_table(pos.reshape(-1), pad_lo, pad_hi, n_slots)
        blk_start = jnp.arange(n_blocks, dtype=I32)[:, None] * te
        blk_expert = jnp.minimum(jnp.sum((pend[None, :] <= blk_start).astype(I32), axis=1), N_EXPERTS - 1)
        n_used = (pend[-1:] // te).astype(I32)

        yb = _experts(blk_expert, n_used, slot_tok, h2, w_expert_up, w_expert_down, layer, te)
        x = _combine(pos, x2, route, yb, tc).reshape(b, s, d)
    return x
```

```python
import functools
import math

import jax
import jax.numpy as jnp
import numpy as np
from jax import lax
from jax.experimental import pallas as pl
from jax.experimental.pallas import tpu as pltpu

F32 = jnp.float32
BF16 = jnp.bfloat16
I32 = jnp.int32

HEAD_DIM = 64
LANES = 128
NORM_EPS = 1e-6
NEG_INF = -1e30
M_INIT = -1e29
ATTN_SCALE = HEAD_DIM ** -0.5
LOG2E = math.log2(math.e)
SWA_BLOCK = 128
IDX_SCALE = 64 ** -0.5
IDX_HEADS = 4
DSA_TOPK_MAX = 256
T5_BUCKETS = 32
T5_MAX_DISTANCE = 128
N_GROUPS = 4
EXPERTS_PER_GROUP = 8
N_EXPERTS = N_GROUPS * EXPERTS_PER_GROUP
EXPERT_FF = 512
SB_DEAD = -110.0
INT_MIN = -2 ** 31
ONES_ROWS = 16
VMEM_LIMIT = 56 * 1024 * 1024

IN_SPLITS = (256, 128, 128, 256, 256, 256, 4, 256, 256, 256, 256, 64, 64, 256, 64, 4, 4096)

_SEG = dict(aq=(0, 256), ak=(256, 128), av=(384, 128), fq=(512, 256), fk=(768, 256), fv=(1024, 256),
            sq=(1280, 256), sk=(1536, 256), sv=(1792, 256), dq=(2048, 256), dkk=(2304, 128),
            dvv=(2432, 128), iq=(2560, 256), ikk=(2816, 128))
_SEG_ORDER = ("aq", "ak", "av", "fq", "fk", "fv", "sq", "sk", "sv", "dq", "dkk", "dvv", "iq", "ikk")


def _cparams(*sem):
    return pltpu.CompilerParams(dimension_semantics=sem, vmem_limit_bytes=VMEM_LIMIT)


def _rms(x, g):
    return x * lax.rsqrt(jnp.mean(x * x, axis=-1, keepdims=True) + NORM_EPS) * g


def _log_sigmoid(z):
    return jnp.minimum(z, 0.0) - jnp.log(1.0 + jnp.exp(-jnp.abs(z)))


def _dot_nt(a, b):
    return lax.dot_general(a, b, (((1,), (1,)), ((), ())), preferred_element_type=F32)


def _with_ones_rows(vt):
    ones = jnp.ones(vt.shape[:-2] + (ONES_ROWS, vt.shape[-1]), vt.dtype)
    return jnp.concatenate([vt, ones], axis=-2)


def _split3(x):
    p1 = x.astype(BF16)
    r = x - p1.astype(F32)
    p2 = r.astype(BF16)
    return p1, p2, (r - p2.astype(F32)).astype(BF16)


def _hi_lo(w):
    hi = w.astype(BF16)
    return jnp.stack([hi, (w - hi.astype(F32)).astype(BF16)])


def _dot_x3(a, b_hi, b_lo):
    a_hi = a.astype(BF16)
    a_lo = (a - a_hi.astype(F32)).astype(BF16)
    return (jnp.dot(a_hi, b_hi, preferred_element_type=F32) + jnp.dot(a_lo, b_hi, preferred_element_type=F32)
            + jnp.dot(a_hi, b_lo, preferred_element_type=F32))


def _split_heads(qp):
    lo = lax.broadcasted_iota(I32, (1, LANES), 1) < HEAD_DIM
    zero = jnp.zeros_like(qp)
    return jnp.concatenate([jnp.where(lo, qp, zero), jnp.where(lo, zero, qp)], axis=0)


def _proj_kernel(x_ref, g_ref, w1_ref, wm_ref, gseg_ref, gains_ref, fb_ref, ltri_ref, *rest):
    outs = dict(zip(_SEG_ORDER, rest[:len(_SEG_ORDER)]))
    cm_ref = rest[len(_SEG_ORDER)]
    carry_ref = rest[len(_SEG_ORDER) + 1]

    @pl.when(pl.program_id(1) == 0)
    def _():
        carry_ref[...] = jnp.zeros_like(carry_ref)

    h = _rms(x_ref[0], g_ref[...])
    hb = h.astype(BF16)

    def seg(name):
        off, width = _SEG[name]
        return jnp.dot(hb, w1_ref[:, off:off + width], preferred_element_type=F32)

    def head_norm(t, row):
        width = t.shape[1]
        ssq = jnp.dot((t * t).astype(BF16), gseg_ref[:width, :width], preferred_element_type=F32)
        return t * lax.rsqrt(ssq * (1.0 / HEAD_DIM) + NORM_EPS) * gains_ref[row:row + 1, :width]

    normed = dict(aq=0, ak=1, fq=2, fk=3, dq=4, dkk=5)
    scaled = dict(sq=ATTN_SCALE * LOG2E, iq=IDX_SCALE)
    for name in _SEG_ORDER:
        t = seg(name)
        if name in normed:
            t = head_norm(t, normed[name])
        elif name in scaled:
            t = t * scaled[name]
        outs[name][0] = t.astype(BF16)

    misc = _dot_x3(h, wm_ref[0], wm_ref[1])
    lane = lax.broadcasted_iota(I32, misc.shape, 1)
    logf = jnp.where(lane < 4, _log_sigmoid(misc + fb_ref[...]), 0.0)
    ltri = ltri_ref[...]
    c = carry_ref[0:1, :]
    for piece in _split3(logf):
        c = c + jnp.dot(ltri, piece, preferred_element_type=F32)
    tm = misc.shape[0]
    carry_ref[0:1, :] = c[tm - 1:tm, :]
    cm_ref[0] = jnp.where(lane < 4, c, misc)


def _proj(x, g, w1, wm, gseg, gains, fb, tm):
    b, s, d = x.shape
    ltri = jnp.tril(jnp.ones((tm, tm), BF16))
    full = lambda shape: pl.BlockSpec(shape, lambda bi, si: (0,) * len(shape))
    out_shapes = [jax.ShapeDtypeStruct((b, s, _SEG[n][1]), BF16) for n in _SEG_ORDER]
    out_shapes.append(jax.ShapeDtypeStruct((b, s, LANES), F32))
    out_specs = [pl.BlockSpec((1, tm, _SEG[n][1]), lambda bi, si: (bi, si, 0)) for n in _SEG_ORDER]
    out_specs.append(pl.BlockSpec((1, tm, LANES), lambda bi, si: (bi, si, 0)))
    return pl.pallas_call(
        _proj_kernel,
        grid=(b, s // tm),
        in_specs=[pl.BlockSpec((1, tm, d), lambda bi, si: (bi, si, 0)),
                  full((1, d)), full(w1.shape), full(wm.shape), full(gseg.shape),
                  full(gains.shape), full(fb.shape), full((tm, tm))],
        out_specs=out_specs,
        out_shape=out_shapes,
        scratch_shapes=[pltpu.VMEM((8, LANES), F32)],
        compiler_params=_cparams("arbitrary", "arbitrary"),
        name="proj",
    )(x, g, w1, wm, gseg, gains, fb, ltri)


def _swa_kernel(q_ref, kp_ref, kc_ref, vtp_ref, vtc_ref, bias_ref, sink_ref, o_ref, *, nsub):
    i = pl.program_id(1)
    t = SWA_BLOCK
    no_prev = (lax.broadcasted_iota(I32, (2 * t, 4 * t), 0) < t) & (i == 0)
    sink = sink_ref[0:1, :]
    for u in range(nsub):
        rows = slice(u * t, (u + 1) * t)
        q = q_ref[0, rows, :]
        qs = jnp.concatenate([_split_heads(q[:, :LANES]), _split_heads(q[:, LANES:])], axis=0)
        if u == 0:
            kcat = jnp.concatenate([kp_ref[0], kc_ref[0, rows, :]], axis=0)
            vt = jnp.concatenate([vtp_ref[0], vtc_ref[0, :, rows]], axis=1)
        else:
            kcat = kc_ref[0, (u - 1) * t:(u + 1) * t, :]
            vt = vtc_ref[0, :, (u - 1) * t:(u + 1) * t]
        st = _dot_nt(kcat, qs) + bias_ref[...]
        if u == 0:
            st = jnp.where(no_prev, NEG_INF, st)
        m = jnp.maximum(jnp.max(st, axis=0, keepdims=True), sink)
        p = jnp.exp(st - m)
        denom = jnp.sum(p, axis=0, keepdims=True) + jnp.exp(sink - m)
        ot = jnp.dot(vt, p.astype(BF16), preferred_element_type=F32) / denom
        ot = jnp.concatenate([ot[:HEAD_DIM, 0:t], ot[HEAD_DIM:, t:2 * t],
                              ot[:HEAD_DIM, 2 * t:3 * t], ot[HEAD_DIM:, 3 * t:]], axis=0)
        o_ref[0, rows, :] = ot.T.astype(BF16)


def _swa(aq, ak, av, bias, sinks, nsub):
    b, s, _ = aq.shape
    t = SWA_BLOCK
    avt = jnp.transpose(av, (0, 2, 1))
    cur = lambda bi, i: (bi, i, 0)
    prev = lambda bi, i: (bi, jnp.maximum(i * nsub - 1, 0), 0)
    return pl.pallas_call(
        functools.partial(_swa_kernel, nsub=nsub),
        grid=(b, s // (t * nsub)),
        in_specs=[pl.BlockSpec((1, t * nsub, 256), cur),
                  pl.BlockSpec((1, t, LANES), prev), pl.BlockSpec((1, t * nsub, LANES), cur),
                  pl.BlockSpec((1, LANES, t), lambda bi, i: (bi, 0, jnp.maximum(i * nsub - 1, 0))),
                  pl.BlockSpec((1, LANES, t * nsub), lambda bi, i: (bi, 0, i)),
                  pl.BlockSpec(bias.shape, lambda bi, i: (0, 0)),
                  pl.BlockSpec(sinks.shape, lambda bi, i: (0, 0))],
        out_specs=pl.BlockSpec((1, t * nsub, 256), cur),
        out_shape=jax.ShapeDtypeStruct((b, s, 256), BF16),
        compiler_params=_cparams("arbitrary", "arbitrary"),
        name="swa",
    )(aq, ak, ak, avt, avt, bias, sinks)


def _fox_kernel(q_ref, k_ref, vt_ref, ct_ref, ccol_ref, o_ref, ckb_ref, *, t):
    i = pl.program_id(1)
    n_tiles = ckb_ref.shape[1] // t

    @pl.when(i == 0)
    def _():
        def fill(j, c):
            rows = pl.ds(pl.multiple_of(j * t, t), t)
            cc = ccol_ref[0, rows, :] * LOG2E
            for h in range(4):
                ckb_ref[h, rows, :] = jnp.broadcast_to(cc[:, h:h + 1], (t, LANES))
            return c
        lax.fori_loop(0, n_tiles, fill, 0)

    q = q_ref[0]
    ct = ct_ref[0] * LOG2E
    qs = [_split_heads(q[:, :LANES]), _split_heads(q[:, LANES:])]
    valid = lax.broadcasted_iota(I32, (t, t), 0) <= lax.broadcasted_iota(I32, (t, t), 1)

    def tiles(j0, n, carry, masked):
        m, accs = carry
        rows = pl.ds(pl.multiple_of(j0 * t, t), n * t)
        cols = []
        for pair in range(2):
            st = _dot_nt(k_ref[0, rows, pair * LANES:(pair + 1) * LANES], qs[pair])
            for hh in range(2):
                head = 2 * pair + hh
                ck = ckb_ref[head, rows, :]
                for c in range(t // LANES):
                    cs = slice(c * LANES, (c + 1) * LANES)
                    sh = st[:, hh * t + c * LANES:hh * t + (c + 1) * LANES] + (ct[head:head + 1, cs] - ck)
                    if masked:
                        last = jnp.where(valid[:, cs], sh[(n - 1) * t:], NEG_INF)
                        sh = jnp.concatenate([sh[:(n - 1) * t], last], axis=0) if n > 1 else last
                    cols.append(sh)
        st = jnp.concatenate(cols, axis=1)
        m_new = jnp.maximum(m, jnp.max(st, axis=0, keepdims=True))
        alpha = jnp.exp2(m - m_new)
        pb = jnp.exp2(st - m_new).astype(BF16)
        new = []
        for pair in range(2):
            lanes = slice(pair * 2 * t, (pair + 1) * 2 * t)
            vt = _with_ones_rows(jnp.concatenate(
                [vt_ref[0, j0 + u, pair * LANES:(pair + 1) * LANES, :] for u in range(n)], axis=1))
            pv = jnp.dot(vt, pb[:, lanes], preferred_element_type=F32)
            new.append(alpha[:, lanes] * accs[pair] + pv)
        return m_new, tuple(new)

    zero_acc = jnp.zeros((LANES + ONES_ROWS, 2 * t), F32)
    carry = lax.fori_loop(0, i // 4, lambda g, c: tiles(4 * g, 4, c, False),
                          (jnp.full((1, 4 * t), M_INIT, F32), (zero_acc, zero_acc)))
    rest = i - i % 4
    _, accs = lax.switch(i % 4, [functools.partial(lambda c, n: tiles(rest, n, c, True), n=r + 1)
                                 for r in range(4)], carry)
    outs = []
    for pair in range(2):
        o = accs[pair][:LANES] / accs[pair][LANES:LANES + 1]
        outs.append(jnp.concatenate([o[:HEAD_DIM, :t], o[HEAD_DIM:, t:]], axis=0))
    o_ref[0] = jnp.concatenate(outs, axis=0).T.astype(BF16)


def _fox(fq, fk, fv, ct, ccol, t):
    b, s, _ = fq.shape
    nt = s // t
    vt = jnp.transpose(fv.reshape(b, nt, t, 256), (0, 1, 3, 2))
    return pl.pallas_call(
        functools.partial(_fox_kernel, t=t),
        grid=(b, nt),
        in_specs=[pl.BlockSpec((1, t, 256), lambda bi, i: (bi, i, 0)),
                  pl.BlockSpec((1, s, 256), lambda bi, i: (bi, 0, 0)),
                  pl.BlockSpec((1, nt, 256, t), lambda bi, i: (bi, 0, 0, 0)),
                  pl.BlockSpec((1, 8, t), lambda bi, i: (bi, 0, i)),
                  pl.BlockSpec((1, s, LANES), lambda bi, i: (bi, 0, 0))],
        out_specs=pl.BlockSpec((1, t, 256), lambda bi, i: (bi, i, 0)),
        out_shape=jax.ShapeDtypeStruct((b, s, 256), BF16),
        scratch_shapes=[pltpu.VMEM((4, s, LANES), F32)],
        compiler_params=_cparams("arbitrary", "arbitrary"),
        name="fox",
    )(fq, fk, vt, ct, ccol)


def _sb_kernel(q_ref, k_ref, vt_ref, lgt_ref, o_ref, *, t):
    i = pl.program_id(1)
    q = q_ref[0]
    lgt = lgt_ref[...]
    qs = [_split_heads(q[:, :LANES]), _split_heads(q[:, LANES:])]
    key_i = lax.broadcasted_iota(I32, (t, 4 * t), 0)
    query_i = lax.broadcasted_iota(I32, (t, 4 * t), 1) & (t - 1)
    strict = key_i < query_i

    def tile(j, r, accs, masked):
        rows = pl.ds(pl.multiple_of(j * t, t), t)
        z = jnp.concatenate([_dot_nt(k_ref[0, rows, p * LANES:(p + 1) * LANES], qs[p]) for p in range(2)],
                            axis=1)
        sp = jnp.log2(1.0 + jnp.exp2(-jnp.abs(z)))
        log_beta = jnp.minimum(z, 0.0) - sp
        log_keep = jnp.minimum(-z, 0.0) - sp
        if masked:
            log_keep = jnp.where(strict, log_keep, 0.0)
        hi = log_keep.astype(BF16)
        lo = (log_keep - hi.astype(F32)).astype(BF16)
        later = (jnp.dot(lgt, hi, preferred_element_type=F32)
                 + jnp.dot(lgt, lo, preferred_element_type=F32))
        a = jnp.exp2(log_beta + later + r)
        if masked:
            a = jnp.where(strict, a, 0.0)
        ab = a.astype(BF16)
        new = tuple(accs[p] + jnp.dot(vt_ref[0, j, p * LANES:(p + 1) * LANES, :],
                                      ab[:, p * 2 * t:(p + 1) * 2 * t], preferred_element_type=F32)
                    for p in range(2))
        return r + jnp.sum(log_keep, axis=0, keepdims=True), new

    zero_acc = jnp.zeros((LANES, 2 * t), F32)
    r, accs = tile(i, jnp.zeros((1, 4 * t), F32), (zero_acc, zero_acc), True)

    def cond(c):
        return (c[0] >= 0) & (c[1] > 0)

    def body(c):
        j, _, r, accs = c
        r, accs = tile(j, r, accs, False)
        return j - 1, (jnp.max(r) > SB_DEAD * LOG2E).astype(I32), r, accs

    _, _, _, accs = lax.while_loop(cond, body, (i - 1, (jnp.max(r) > SB_DEAD * LOG2E).astype(I32), r, accs))
    outs = [jnp.concatenate([accs[p][:HEAD_DIM, :t], accs[p][HEAD_DIM:, t:]], axis=0) for p in range(2)]
    o_ref[0] = jnp.concatenate(outs, axis=0).T.astype(BF16)


def _sb(sq, sk, sv, t):
    b, s, _ = sq.shape
    nt = s // t
    lgt = (jnp.arange(t)[:, None] < jnp.arange(t)[None, :]).astype(BF16)
    vt = jnp.transpose(sv.reshape(b, nt, t, 256), (0, 1, 3, 2))
    return pl.pallas_call(
        functools.partial(_sb_kernel, t=t),
        grid=(b, nt),
        in_specs=[pl.BlockSpec((1, t, 256), lambda bi, i: (bi, i, 0)),
                  pl.BlockSpec((1, s, 256), lambda bi, i: (bi, 0, 0)),
                  pl.BlockSpec((1, nt, 256, t), lambda bi, i: (bi, 0, 0, 0)),
                  pl.BlockSpec((t, t), lambda bi, i: (0, 0))],
        out_specs=pl.BlockSpec((1, t, 256), lambda bi, i: (bi, i, 0)),
        out_shape=jax.ShapeDtypeStruct((b, s, 256), BF16),
        compiler_params=_cparams("arbitrary", "arbitrary"),
        name="stickbreak",
    )(sq, sk, vt, lgt)


def _bit_planes(words):
    words = list(words)
    j, m = 16, 0x0000FFFF
    while j:
        k = 0
        while k < 32:
            tt = (words[k] ^ lax.shift_right_logical(words[k + j], jnp.full_like(words[k + j], j))) & m
            words[k] = words[k] ^ tt
            words[k + j] = words[k + j] ^ (tt << j)
            k = (k + j + 1) & ~j
        j >>= 1
        m = (m ^ (m << j)) & 0xFFFFFFFF
    return words


def _dsa_kernel(q_ref, kk_ref, vt_ref, iq_ref, ikk_ref, wt_ref, bias_ref, lstrict_ref, o_ref,
                key_ref, plane_ref, *, t, top_k):
    i = pl.program_id(1)
    assert t == 8 * 32

    @pl.when(i == 0)
    def _():
        plane_ref[...] = jnp.zeros_like(plane_ref)
    causal = lax.broadcasted_iota(I32, (t, t), 0) <= lax.broadcasted_iota(I32, (t, t), 1)

    def head_stack(x):
        return jnp.concatenate([_split_heads(x[:, :LANES]), _split_heads(x[:, LANES:])], axis=0)

    def key_rows(ref, j0, n):
        return ref[0, pl.ds(pl.multiple_of(j0 * t, t), n * t), :]

    iqs = head_stack(iq_ref[0])
    wt = wt_ref[0]
    w = [wt[4 + h:5 + h, :] * (IDX_HEADS ** -0.5) for h in range(IDX_HEADS)]

    def score_tiles(j0, n, masked):
        lg = _dot_nt(key_rows(ikk_ref, j0, n), iqs)
        sc = w[0] * jnp.maximum(lg[:, 0:t], 0.0)
        for h in range(1, IDX_HEADS):
            sc = sc + w[h] * jnp.maximum(lg[:, h * t:(h + 1) * t], 0.0)
        bits = pltpu.bitcast(sc, I32)
        key = bits ^ ((bits >> 31) & 0x7FFFFFFF)
        key = jnp.where(key == -1, 0, key)
        for u in range(n):
            key_u = key[u * t:(u + 1) * t]
            if masked and u == n - 1:
                key_u = jnp.where(causal, key_u, INT_MIN)
            key_ref[j0 + u] = key_u
            for p, plane in enumerate(_bit_planes([key_u[8 * g:8 * g + 8, :] for g in range(32)])):
                plane_ref[p, j0 + u] = plane

    def p1(g, c):
        score_tiles(4 * g, 4, False)
        return c

    lax.fori_loop(0, i // 4, p1, 0)
    lax.switch(i % 4, [functools.partial(score_tiles, i - i % 4, r + 1, True) for r in range(4)])

    def popcount_rows(words):
        per_tile = jnp.sum(lax.population_count(words), axis=0)
        return jnp.sum(per_tile.astype(F32), axis=0, keepdims=True)

    def bisect(nt_use):
        def bis_body(p, c):
            alive, n_gt, thr_u = c
            plane = plane_ref[p, :nt_use] ^ jnp.where(p == 0, -1, 0)
            ones = alive & plane
            cnt = popcount_rows(ones)
            take = n_gt + cnt >= top_k
            alive = jnp.where(take, ones, alive ^ ones)
            n_gt = jnp.where(take, n_gt, n_gt + cnt)
            thr_u = jnp.where(take, thr_u | (jnp.int32(1) << (31 - p)), thr_u)
            return alive, n_gt, thr_u

        alive0 = jnp.where(lax.broadcasted_iota(I32, (nt_use, 8, t), 0) <= i, -1, 0)
        alive, n_gt, thr_u = lax.fori_loop(
            0, 32, bis_body, (alive0, jnp.zeros((1, t), F32), jnp.zeros((1, t), I32)))
        return popcount_rows(alive), n_gt, thr_u

    n_tiles = key_ref.shape[0]
    if n_tiles >= 4:
        n_eq, n_gt, thr_u = lax.switch(
            (i >= n_tiles // 4).astype(I32) + (i >= n_tiles // 2).astype(I32),
            [lambda: bisect(n_tiles // 4), lambda: bisect(n_tiles // 2), lambda: bisect(n_tiles)])
    else:
        n_eq, n_gt, thr_u = bisect(n_tiles)
    thr = jnp.maximum(thr_u ^ INT_MIN, INT_MIN + 1)
    n_avail = (i * t + lax.broadcasted_iota(I32, (1, t), 1) + 1).astype(F32)
    n_ge = jnp.where(n_avail > top_k, n_gt + n_eq, 0.0)
    surplus = jnp.max(n_ge) > top_k

    def tie_pass():
        need = top_k - n_gt

        def tb(j, seen):
            k = key_ref[j]
            eq = k == thr
            eqf = jnp.where(eq, 1.0, 0.0)
            before = jnp.dot(lstrict_ref[...], eqf.astype(BF16), preferred_element_type=F32) + seen
            sel = (k > thr) | (eq & (before < need))
            key_ref[j] = jnp.where(sel, 1, INT_MIN)
            return seen + jnp.sum(eqf, axis=0, keepdims=True)

        lax.fori_loop(0, i + 1, tb, jnp.zeros((1, t), F32))
        return jnp.zeros((1, t), I32)

    thr = lax.cond(surplus, tie_pass, lambda: thr)

    qs = head_stack(q_ref[0])

    def attn_tiles(j0, n, carry, near):
        m, acc = carry
        st = _dot_nt(key_rows(kk_ref, j0, n), qs)
        selb = jnp.concatenate([jnp.where(key_ref[j0 + u] >= thr, 0.0, NEG_INF) for u in range(n)], axis=0)
        parts = []
        for h in range(4):
            sh = st[:, h * t:(h + 1) * t] + selb
            if near:
                biased = [sh[(n - near + v) * t:(n - near + v + 1) * t] + bias_ref[2 - near + v, h]
                          for v in range(near)]
                sh = jnp.concatenate(([sh[:(n - near) * t]] if n > near else []) + biased, axis=0)
            parts.append(sh)
        st = jnp.concatenate(parts, axis=1)
        m_new = jnp.maximum(m, jnp.max(st, axis=0, keepdims=True))
        alpha = jnp.exp2(m - m_new)
        pb = jnp.exp2(st - m_new).astype(BF16)
        vt = _with_ones_rows(jnp.concatenate([vt_ref[0, j0 + u] for u in range(n)], axis=1))
        acc = alpha * acc + jnp.dot(vt, pb, preferred_element_type=F32)
        return m_new, acc

    init = (jnp.full((1, 4 * t), M_INIT, F32), jnp.zeros((HEAD_DIM + ONES_ROWS, 4 * t), F32))
    far = jnp.maximum(i - 1, 0)
    carry = lax.fori_loop(0, far // 4, lambda g, c: attn_tiles(4 * g, 4, c, 0), init)
    rest = far - far % 4
    last = [lambda c: attn_tiles(0, 1, c, 1)]
    last += [functools.partial(lambda c, n: attn_tiles(rest, n, c, 2), n=r + 2) for r in range(4)]
    _, acc = lax.switch(jnp.where(i == 0, 0, 1 + far % 4), last, carry)
    ot = acc[:HEAD_DIM] / acc[HEAD_DIM:HEAD_DIM + 1]
    ot = jnp.concatenate([ot[:, h * t:(h + 1) * t] for h in range(4)], axis=0)
    o_ref[0] = ot.T.astype(BF16)


def _dsa_bias(t5_table, t):
    assert t + 1 >= T5_MAX_DISTANCE
    k = jnp.arange(t)[:, None]
    q = jnp.arange(t)[None, :]
    far = t5_table[T5_BUCKETS - 1, 4:].astype(F32)
    tiles = []
    for off in (t, 0):
        dist = off + q - k
        b = jnp.transpose(_t5_lookup(t5_table, dist)[..., 4:], (2, 0, 1)) - far[:, None, None]
        tiles.append(jnp.where((dist >= 0)[None], b, 0.0))
    return jnp.stack(tiles)


def _dsa(dq, dkk, dvv, iq, ikk, wt, bias, top_k, t):
    b, s, _ = dq.shape
    nt = s // t
    lstrict = (jnp.arange(t)[:, None] > jnp.arange(t)[None, :]).astype(BF16)
    vt = jnp.transpose(dvv[:, :, :HEAD_DIM].reshape(b, nt, t, HEAD_DIM), (0, 1, 3, 2))
    blk = lambda w: pl.BlockSpec((1, t, w), lambda bi, i: (bi, i, 0))
    seq = lambda w: pl.BlockSpec((1, s, w), lambda bi, i: (bi, 0, 0))
    return pl.pallas_call(
        functools.partial(_dsa_kernel, t=t, top_k=top_k),
        grid=(b, nt),
        in_specs=[blk(256), seq(LANES),
                  pl.BlockSpec((1, nt, HEAD_DIM, t), lambda bi, i: (bi, 0, 0, 0)),
                  blk(256), seq(LANES),
                  pl.BlockSpec((1, 8, t), lambda bi, i: (bi, 0, i)),
                  pl.BlockSpec(bias.shape, lambda bi, i: (0, 0, 0, 0)),
                  pl.BlockSpec((t, t), lambda bi, i: (0, 0))],
        out_specs=blk(256),
        out_shape=jax.ShapeDtypeStruct((b, s, 256), BF16),
        scratch_shapes=[pltpu.VMEM((nt, t, t), I32), pltpu.VMEM((32, nt, 8, t), I32)],
        compiler_params=_cparams("arbitrary", "arbitrary"),
        name="dsa",
    )(dq, dkk, vt, iq, ikk, wt, bias, lstrict)


def _merge_kernel(x_ref, gm_ref, wg_ref, oa_ref, of_ref, os_ref, od_ref, wb_ref, wo_ref, gf_ref,
                  wr_ref, br_ref, ltri_ref, xo_ref, h2_ref, route_ref, cnt_ref, carry_ref):
    @pl.when(pl.program_id(0) == 0)
    def _():
        carry_ref[...] = jnp.zeros_like(carry_ref)

    x = x_ref[...]
    hb = _rms(x, gm_ref[...]).astype(BF16)
    d = x.shape[1]
    merged = None
    for bi, o_ref in enumerate((oa_ref, of_ref, os_ref, od_ref)):
        gate = jax.nn.sigmoid(jnp.dot(hb, wg_ref[:, bi * d:(bi + 1) * d], preferred_element_type=F32))
        term = gate * jnp.dot(o_ref[...], wb_ref[bi], preferred_element_type=F32)
        merged = term if merged is None else merged + term
    xn = x + jnp.dot(merged.astype(BF16), wo_ref[...], preferred_element_type=F32)
    xo_ref[...] = xn
    h2 = _rms(xn, gf_ref[...])
    h2_ref[...] = h2

    logits = _dot_x3(h2, wr_ref[0], wr_ref[1]) + br_ref[...]
    lane = lax.broadcasted_iota(I32, logits.shape, 1).astype(F32)
    big = 1e9
    gl = jnp.where(lane < N_GROUPS, logits, -jnp.inf)
    gmax = jnp.max(gl, axis=1, keepdims=True)
    grp = jnp.min(jnp.where(gl == gmax, lane, big), axis=1, keepdims=True)
    p_grp = 1.0 / jnp.sum(jnp.exp(gl - gmax), axis=1, keepdims=True)
    first = N_GROUPS + grp * EXPERTS_PER_GROUP
    el = jnp.where((lane >= first) & (lane < first + EXPERTS_PER_GROUP), logits, -jnp.inf)
    l1 = jnp.max(el, axis=1, keepdims=True)
    i1 = jnp.min(jnp.where(el == l1, lane, big), axis=1, keepdims=True)
    el2 = jnp.where(lane == i1, -jnp.inf, el)
    l2 = jnp.max(el2, axis=1, keepdims=True)
    i2 = jnp.min(jnp.where(el2 == l2, lane, big), axis=1, keepdims=True)
    e2 = jnp.exp(l2 - l1)
    g1 = p_grp / (1.0 + e2)
    g2 = p_grp * e2 / (1.0 + e2)
    e_first, e_second = i1 - N_GROUPS, i2 - N_GROUPS
    oh0 = jnp.where(lane == e_first, 1.0, 0.0)
    oh1 = jnp.where(lane == e_second, 1.0, 0.0)
    both = oh0 + oh1
    before = jnp.dot(ltri_ref[...], both.astype(BF16), preferred_element_type=F32) + carry_ref[0:1, :]
    r0 = jnp.sum(oh0 * before, axis=1, keepdims=True)
    r1 = jnp.sum(oh1 * (before + oh0), axis=1, keepdims=True)
    total = carry_ref[0:1, :] + jnp.sum(both, axis=0, keepdims=True)
    carry_ref[0:1, :] = total
    cnt_ref[...] = jnp.broadcast_to(total, cnt_ref.shape)
    cols = (e_first, e_second, g1, g2, r0, r1)
    route = jnp.zeros_like(logits)
    for k, col in enumerate(cols):
        route = jnp.where(lane == k, col, route)
    route_ref[...] = route


def _merge(x2, gm, wg, o_a, o_f, o_s, o_d, wb, wo, gf, wr, br, tm):
    n, d = x2.shape
    row = lambda w: pl.BlockSpec((tm, w), lambda i: (i, 0))
    full = lambda a: pl.BlockSpec(a.shape, lambda i: (0,) * a.ndim, pipeline_mode=pl.Buffered(1))
    ltri = (jnp.arange(tm)[:, None] > jnp.arange(tm)[None, :]).astype(BF16)
    return pl.pallas_call(
        _merge_kernel,
        grid=(n // tm,),
        in_specs=[row(d), full(gm), full(wg), row(256), row(256), row(256), row(256),
                  full(wb), full(wo), full(gf), full(wr), full(br), full(ltri)],
        out_specs=[row(d), row(d), row(LANES), pl.BlockSpec((8, LANES), lambda i: (0, 0))],
        out_shape=[jax.ShapeDtypeStruct((n, d), F32), jax.ShapeDtypeStruct((n, d), F32),
                   jax.ShapeDtypeStruct((n, LANES), F32), jax.ShapeDtypeStruct((8, LANES), F32)],
        scratch_shapes=[pltpu.VMEM((8, LANES), F32)],
        compiler_params=_cparams("arbitrary"),
        name="merge",
    )(x2, gm, wg, o_a, o_f, o_s, o_d, wb, wo, gf, wr, br, ltri)


def _slot_kernel(pad_lo_ref, pad_hi_ref, pos_ref, o_ref, *, chunk):
    i = pl.program_id(0)

    @pl.when(i == 0)
    def _():
        def clear_range(e, c):
            def clear(k, c2):
                o_ref[k] = 0
                return c2
            lax.fori_loop(pad_lo_ref[e], pad_hi_ref[e], clear, 0)
            return c
        lax.fori_loop(0, pad_lo_ref.shape[0], clear_range, 0)

    def place(a, c):
        o_ref[pos_ref[a]] = (i * chunk + a) >> 1
        return c
    lax.fori_loop(0, chunk, place, 0, unroll=8)


def _slot_table(pos_flat, pad_lo, pad_hi, n_slots):
    chunk = min(8192, pos_flat.shape[0])
    grid_spec = pltpu.PrefetchScalarGridSpec(
        num_scalar_prefetch=2,
        grid=(pos_flat.shape[0] // chunk,),
        in_specs=[pl.BlockSpec((chunk,), lambda i, lo, hi: (i,), memory_space=pltpu.SMEM)],
        out_specs=pl.BlockSpec((n_slots,), lambda i, lo, hi: (0,), memory_space=pltpu.SMEM),
    )
    return pl.pallas_call(
        functools.partial(_slot_kernel, chunk=chunk),
        grid_spec=grid_spec,
        out_shape=jax.ShapeDtypeStruct((n_slots,), I32),
        compiler_params=_cparams("arbitrary"),
        name="moe_slots",
    )(pad_lo, pad_hi, pos_flat)


def _expert_kernel(be_ref, nu_ref, tok_ref, tok1_ref, tok2_ref, h_hbm, wup_ref, wdn_ref, y_ref,
                   xbuf, sem, wup_b, wdn_b, *, te):
    b = pl.program_id(0)
    n_used = nu_ref[0]
    slot = b % 3

    def start_row(tokens_ref, dst, r):
        pltpu.make_async_copy(h_hbm.at[pl.ds(tokens_ref[0, 0, r], 1), :],
                              xbuf.at[dst, pl.ds(r, 1), :], sem.at[dst]).start()

    def start_block(tokens_ref, dst):
        def issue(r, c):
            start_row(tokens_ref, dst, r)
            return c
        lax.fori_loop(0, te, issue, 0, unroll=8)

    def block(prefetch):
        pltpu.make_async_copy(h_hbm.at[pl.ds(0, te), :], xbuf.at[slot], sem.at[slot]).wait()
        xb = xbuf[slot].astype(BF16)
        if prefetch:
            dst = (b + 2) % 3
            for r in range(te):
                start_row(tok2_ref, dst, r)
        gu = jnp.dot(xb, wup_b[...], preferred_element_type=F32)
        g = gu[:, :EXPERT_FF]
        act = g * jax.nn.sigmoid(g) * gu[:, EXPERT_FF:]
        y_ref[...] = jnp.dot(act.astype(BF16), wdn_b[...], preferred_element_type=F32)

    @pl.when((b < n_used) & ((b == 0) | (be_ref[b] != be_ref[jnp.maximum(b - 1, 0)])))
    def _():
        wup_b[...] = wup_ref[0, 0].astype(BF16)
        wdn_b[...] = wdn_ref[0, 0].astype(BF16)

    @pl.when((b == 0) & (n_used > 0))
    def _():
        start_block(tok_ref, 0)

    @pl.when((b == 0) & (n_used > 1))
    def _():
        start_block(tok1_ref, 1)

    @pl.when(b + 2 < n_used)
    def _():
        block(True)

    @pl.when((b < n_used) & (b + 2 >= n_used))
    def _():
        block(False)

    @pl.when(b >= n_used)
    def _():
        y_ref[...] = jnp.zeros_like(y_ref)


def _experts(blk_expert, n_used, slot_tok, h2, w_up, w_down, layer, te):
    n_blocks = blk_expert.shape[0]
    d = h2.shape[1]
    ahead = lambda k: pl.BlockSpec((1, 1, te), lambda b, be, nu: (jnp.minimum(b + k, n_blocks - 1), 0, 0),
                                   memory_space=pltpu.SMEM)
    grid_spec = pltpu.PrefetchScalarGridSpec(
        num_scalar_prefetch=2,
        grid=(n_blocks,),
        in_specs=[ahead(0), ahead(1), ahead(2),
                  pl.BlockSpec(memory_space=pl.ANY),
                  pl.BlockSpec((1, 1, d, 2 * EXPERT_FF), lambda b, be, nu: (layer, be[b], 0, 0)),
                  pl.BlockSpec((1, 1, EXPERT_FF, d), lambda b, be, nu: (layer, be[b], 0, 0))],
        out_specs=pl.BlockSpec((te, d), lambda b, be, nu: (b, 0)),
        scratch_shapes=[pltpu.VMEM((3, te, d), F32), pltpu.SemaphoreType.DMA((3,)),
                        pltpu.VMEM((d, 2 * EXPERT_FF), BF16), pltpu.VMEM((EXPERT_FF, d), BF16)],
    )
    slots = slot_tok.reshape(n_blocks, 1, te)
    return pl.pallas_call(
        functools.partial(_expert_kernel, te=te),
        grid_spec=grid_spec,
        out_shape=jax.ShapeDtypeStruct((n_blocks * te, d), F32),
        compiler_params=_cparams("arbitrary"),
        name="moe_experts",
    )(blk_expert, n_used, slots, slots, slots, h2, w_up, w_down)


def _combine_kernel(pos_ref, pos_next_ref, x_ref, route_ref, y_hbm, o_ref, ybuf, sem, *, tc):
    i = pl.program_id(0)
    slot = i % 2

    def gather(rows_ref, dst):
        for r in range(2 * tc):
            pltpu.make_async_copy(y_hbm.at[pl.ds(rows_ref[0, 0, r], 1), :],
                                  ybuf.at[dst, pl.ds(r, 1), :], sem.at[dst]).start()

    @pl.when(i == 0)
    def _():
        gather(pos_ref, 0)

    @pl.when(i + 1 < pl.num_programs(0))
    def _():
        gather(pos_next_ref, 1 - slot)

    pltpu.make_async_copy(y_hbm.at[pl.ds(0, 2 * tc), :], ybuf.at[slot], sem.at[slot]).wait()
    route = route_ref[...]
    o_ref[...] = (x_ref[...] + route[:, 2:3] * ybuf[slot, 0:tc, :]
                  + route[:, 3:4] * ybuf[slot, tc:2 * tc, :])


def _combine(pos, x2, route, yb, tc):
    n, d = x2.shape
    nt = n // tc
    pos_t = jnp.transpose(pos.reshape(nt, tc, 2), (0, 2, 1)).reshape(nt, 1, 2 * tc)
    return pl.pallas_call(
        functools.partial(_combine_kernel, tc=tc),
        grid=(nt,),
        in_specs=[pl.BlockSpec((1, 1, 2 * tc), lambda i: (i, 0, 0), memory_space=pltpu.SMEM),
                  pl.BlockSpec((1, 1, 2 * tc), lambda i: (jnp.minimum(i + 1, nt - 1), 0, 0),
                               memory_space=pltpu.SMEM),
                  pl.BlockSpec((tc, d), lambda i: (i, 0)),
                  pl.BlockSpec((tc, LANES), lambda i: (i, 0)),
                  pl.BlockSpec(memory_space=pl.ANY)],
        out_specs=pl.BlockSpec((tc, d), lambda i: (i, 0)),
        out_shape=jax.ShapeDtypeStruct((n, d), F32),
        scratch_shapes=[pltpu.VMEM((2, 2 * tc, d), F32), pltpu.SemaphoreType.DMA((2,))],
        compiler_params=_cparams("arbitrary"),
        name="moe_combine",
    )(pos_t, pos_t, x2, route, yb)


def _swap_mid_heads(w, axis):
    h = jnp.split(w, 4, axis=axis)
    return jnp.concatenate([h[0], h[2], h[1], h[3]], axis=axis)


def _t5_lookup(t5_table, dist):
    onehot = (_t5_bucket(dist)[..., None] == jnp.arange(T5_BUCKETS)).astype(F32)
    return jnp.einsum("...b,bh->...h", onehot, t5_table.astype(F32), precision=lax.Precision.HIGHEST)


def _t5_bucket(dist):
    n = jnp.maximum(dist, 0)
    max_exact = T5_BUCKETS // 2
    nf = jnp.maximum(n, 1).astype(F32)
    large = max_exact + (jnp.log(nf / max_exact) / math.log(T5_MAX_DISTANCE / max_exact)
                         * (T5_BUCKETS - max_exact)).astype(I32)
    large = jnp.minimum(large, T5_BUCKETS - 1)
    return jnp.where(n < max_exact, n, large)


def _swa_bias(t5_table):
    t = SWA_BLOCK
    dist = t + jnp.arange(t)[None, :] - jnp.arange(2 * t)[:, None]
    tile = jnp.where(((dist >= 0) & (dist < t))[..., None], _t5_lookup(t5_table, dist)[..., :4], NEG_INF)
    return jnp.concatenate([tile[..., h] for h in (0, 2, 1, 3)], axis=1)


def _layer_weights(w_in, qk_gain, forget_bias, w_branch):
    offs = np.concatenate([[0], np.cumsum(IN_SPLITS)]).tolist()
    part = lambda k: w_in[:, offs[k]:offs[k + 1]]
    dup = lambda w: jnp.concatenate([w, w], axis=1)
    aq = _swap_mid_heads(part(0), 1)
    cols = [aq, part(1), part(2), part(3), part(4), part(5), part(7), part(8), part(9),
            part(10), dup(part(11)), dup(part(12)), part(13), dup(part(14))]
    w1 = jnp.concatenate(cols, axis=1).astype(BF16)
    d = w_in.shape[0]
    wm = _hi_lo(jnp.concatenate([part(6), part(15), jnp.zeros((d, LANES - 8), F32)], axis=1))
    wg = part(16).astype(BF16)
    tile = lambda g, reps, scale: jnp.pad(jnp.tile(g, reps) * scale, (0, 256 - reps * HEAD_DIM))
    gains = jnp.stack([tile(qk_gain[0, 0], 4, ATTN_SCALE), tile(qk_gain[0, 1], 2, 1.0),
                       tile(qk_gain[1, 0], 4, ATTN_SCALE * LOG2E), tile(qk_gain[1, 1], 4, 1.0),
                       tile(qk_gain[2, 0], 4, ATTN_SCALE * LOG2E), tile(qk_gain[2, 1], 2, 1.0),
                       jnp.zeros((256,), F32), jnp.zeros((256,), F32)]).astype(F32)
    fb = jnp.pad(forget_bias.astype(F32), (0, LANES - 4)).reshape(1, LANES)
    wb = jnp.stack([_swap_mid_heads(w_branch[0], 0), w_branch[1], w_branch[2], w_branch[3]]).astype(BF16)
    return w1, wm, wg, gains, fb, wb


def kernel(x, norm_mix_g, w_in, forget_bias, attn_sinks, qk_gain, w_branch, w_out, t5_table, norm_ffn_g,
           w_router_group, b_router_group, w_router_expert, b_router_expert, w_expert_up, w_expert_down):
    b, s, d = x.shape
    n = b * s
    depth = w_in.shape[0]
    top_k = min(DSA_TOPK_MAX, s // 4)
    tm_proj = min(512, s)
    fox_t = min(256, s)
    sb_t = min(256, s)
    dsa_t = min(256, s)
    te = 256
    tc = 256

    gseg = (jnp.arange(256)[:, None] // HEAD_DIM == jnp.arange(256)[None, :] // HEAD_DIM).astype(BF16)
    bias_swa = _swa_bias(t5_table)
    bias_dsa = _dsa_bias(t5_table, dsa_t) * LOG2E
    n_blocks = -(-2 * n // te) + N_EXPERTS

    for layer in range(depth):
        w1, wm, wg, gains, fb, wb = _layer_weights(w_in[layer], qk_gain[layer], forget_bias[layer],
                                                   w_branch[layer])
        sink_row = jnp.repeat(attn_sinks[layer].astype(F32)[jnp.array([0, 2, 1, 3])], SWA_BLOCK)
        sinks = jnp.broadcast_to(sink_row[None, :], (8, 4 * SWA_BLOCK))
        (aq, ak, av, fq, fk, fv, sq, sk, sv, dq, dkk, dvv, iq, ikk, cm) = _proj(
            x, norm_mix_g[layer].reshape(1, d), w1, wm, gseg, gains, fb, tm_proj)

        o_swa = _swa(aq, ak, av, bias_swa, sinks, min(8, s // SWA_BLOCK))
        cmt = jnp.transpose(cm[:, :, :8], (0, 2, 1))
        o_fox = _fox(fq, fk, fv, cmt, cm, fox_t)
        o_sb = _sb(sq, sk, sv, sb_t)
        o_dsa = _dsa(dq, dkk, dvv, iq, ikk, cmt, bias_dsa, top_k, dsa_t)

        wr = _hi_lo(jnp.concatenate([w_router_group[layer], w_router_expert[layer],
                                     jnp.zeros((d, LANES - N_GROUPS - N_EXPERTS), F32)], axis=1))
        br = jnp.concatenate([b_router_group[layer], b_router_expert[layer],
                              jnp.zeros((LANES - N_GROUPS - N_EXPERTS,), F32)]).reshape(1, LANES)
        x2, h2, route, cnt = _merge(
            x.reshape(n, d), norm_mix_g[layer].reshape(1, d), wg,
            o_swa.reshape(n, 256), o_fox.reshape(n, 256), o_sb.reshape(n, 256), o_dsa.reshape(n, 256),
            wb, w_out[layer].astype(BF16), norm_ffn_g[layer].reshape(1, d), wr, br, min(512, n))

        counts = cnt[0, :N_EXPERTS].astype(I32)
        padded = (counts + te - 1) // te * te
        pend = jnp.cumsum(padded)
        pstart = pend - padded
        expert = route[:, :2].astype(I32)
        own = expert[:, :, None] == jnp.arange(N_EXPERTS, dtype=I32)
        pos = jnp.sum(jnp.where(own, pstart, 0), axis=-1) + route[:, 4:6].astype(I32)
        n_slots = n_blocks * te
        pad_lo = jnp.concatenate([pstart + counts, pend[-1:]]).astype(I32)
        pad_hi = jnp.concatenate([pend, jnp.full((1,), n_slots, I32)]).astype(I32)
        slot_tok = _slot_table(pos.reshape(-1), pad_lo, pad_hi, n_slots)
        blk_start = jnp.arange(n_blocks, dtype=I32)[:, None] * te
        blk_expert = jnp.minimum(jnp.sum((pend[None, :] <= blk_start).astype(I32), axis=1), N_EXPERTS - 1)
        n_used = (pend[-1:] // te).astype(I32)

        yb = _experts(blk_expert, n_used, slot_tok, h2, w_expert_up, w_expert_down, layer, te)
        x = _combine(pos, x2, route, yb, tc).reshape(b, s, d)
    return x
```
